```python
import math
import jax, jax.numpy as jnp
from jax import lax
import numpy as np

D_MODEL = 1024
BATCH = 16
SEQ = 2048
DEPTH = 1

D_S5 = 512
S5_GROUP = 16
S5_GROUPS = D_S5 // S5_GROUP
S5_STATE = 64
D_HY = 512
HY_ORDER = 2
HY_SHORT = 3
HY_BANDS = 16
HY_EMB = 1 + 2 * HY_BANDS
HY_HIDDEN = 64
HY_FAST_DECAY = math.log(1e-2) / 0.3
HY_SLOW_DECAY = math.log(1e-2) / 1.5
N_DIRS = 2
N_BRANCH = 2
D_FF = -(-8 * D_MODEL // (3 * 256)) * 256
D_IN = D_S5 + (HY_ORDER + 1) * D_HY + N_BRANCH * D_MODEL
EPS = 1e-6

kernel_name = "hybrid_s5_hyena_gated_encoder_block"


def _rmsnorm(x, g):
    xf = x.astype(jnp.float32)
    r = xf * lax.rsqrt(jnp.mean(xf * xf, axis=-1, keepdims=True) + EPS)
    return (r * g.astype(jnp.float32)).astype(x.dtype)


def _modulate(h, shift, scale):
    return h * (1.0 + scale[:, None, :]) + shift[:, None, :]


def _s5_scan(u, lam_re, lam_im, log_step, b_re, b_im, c_re, c_im, reverse):
    bsz, seq, _ = u.shape
    lam_re = lam_re.astype(jnp.float32); lam_im = lam_im.astype(jnp.float32)
    b_re = b_re.astype(jnp.float32); b_im = b_im.astype(jnp.float32)
    c_re = c_re.astype(jnp.float32); c_im = c_im.astype(jnp.float32)
    step = jnp.exp(log_step.astype(jnp.float32))[:, None]
    mag = jnp.exp(lam_re * step)
    abar_re = mag * jnp.cos(lam_im * step)
    abar_im = mag * jnp.sin(lam_im * step)
    num_re = abar_re - 1.0
    den = lam_re * lam_re + lam_im * lam_im
    coef_re = (num_re * lam_re + abar_im * lam_im) / den
    coef_im = (abar_im * lam_re - num_re * lam_im) / den
    bb_re = coef_re[..., None] * b_re - coef_im[..., None] * b_im
    bb_im = coef_re[..., None] * b_im + coef_im[..., None] * b_re
    ug = u.reshape(bsz, seq, S5_GROUPS, S5_GROUP)
    bu_re = jnp.einsum('blgc,gpc->lbgp', ug, bb_re)
    bu_im = jnp.einsum('blgc,gpc->lbgp', ug, bb_im)
    a_re = jnp.broadcast_to(abar_re[None, None], (seq, 1, S5_GROUPS, S5_STATE))
    a_im = jnp.broadcast_to(abar_im[None, None], (seq, 1, S5_GROUPS, S5_STATE))

    def combine(left, right):
        ar1, ai1, br1, bi1 = left
        ar2, ai2, br2, bi2 = right
        return (ar2 * ar1 - ai2 * ai1,
                ar2 * ai1 + ai2 * ar1,
                ar2 * br1 - ai2 * bi1 + br2,
                ar2 * bi1 + ai2 * br1 + bi2)

    _, _, s_re, s_im = lax.associative_scan(combine, (a_re, a_im, bu_re, bu_im), reverse=reverse, axis=0)
    y = jnp.einsum('lbgp,gcp->blgc', s_re, c_re) - jnp.einsum('lbgp,gcp->blgc', s_im, c_im)
    return y.reshape(bsz, seq, D_S5)


def _s5_branch(u, lam_re, lam_im, log_step, b_re, b_im, c_re, c_im, d, glu_w, glu_b):
    uf = u.astype(jnp.float32)
    y = uf * d.astype(jnp.float32)
    for direction in range(N_DIRS):
        y = y + _s5_scan(uf, lam_re[direction], lam_im[direction], log_step[direction],
                         b_re[direction], b_im[direction], c_re[direction], c_im[direction],
                         reverse=(direction == 1))
    z = jax.nn.gelu(y.astype(u.dtype))
    return z * jax.nn.sigmoid(z @ glu_w + glu_b)


def _short_conv(u, w, b):
    seq = u.shape[1]
    half = HY_SHORT // 2
    up = jnp.pad(u, ((0, 0), (half, HY_SHORT - 1 - half), (0, 0)))
    y = b
    for k in range(HY_SHORT):
        y = y + up[:, k:k + seq] * w[k]
    return y


def _hyena_filter_spectra(seq, w1, b1, w2, b2, w3, b3, freq, decay):
    t = jnp.arange(seq, dtype=jnp.float32)
    t01 = t / max(seq - 1, 1)
    bands = jnp.linspace(1e-4, HY_BANDS - 1, HY_BANDS, dtype=jnp.float32)
    ang = (2.0 * math.pi) * t[:, None] * bands[None, :] / seq
    feats = jnp.concatenate([t01[:, None], jnp.cos(ang), jnp.sin(ang)], axis=-1)
    f = freq.astype(jnp.float32)
    h = jnp.sin(f * (feats @ w1.astype(jnp.float32) + b1.astype(jnp.float32)))
    h = jnp.sin(f * (h @ w2.astype(jnp.float32) + b2.astype(jnp.float32)))
    h = h @ w3.astype(jnp.float32) + b3.astype(jnp.float32)
    h = h * jnp.exp(-t01[:, None] * jnp.abs(decay.astype(jnp.float32)))
    h = h.reshape(seq, HY_ORDER, N_DIRS, D_HY)
    fwd = h[:, :, 0]
    bwd = h[1:, :, 1]
    circ = jnp.concatenate([fwd, jnp.zeros((1, HY_ORDER, D_HY), jnp.float32), bwd[::-1]], axis=0)
    return jnp.fft.rfft(circ, axis=0)


def _fft_conv(z, kf, bias):
    seq = z.shape[1]
    zf32 = z.astype(jnp.float32)
    zf = jnp.fft.rfft(zf32, n=2 * seq, axis=1)
    y = jnp.fft.irfft(zf * kf[None], n=2 * seq, axis=1)[:, :seq]
    return (y + zf32 * bias.astype(jnp.float32)).astype(z.dtype)


def _hyena_branch(u, conv_w, conv_b, w1, b1, w2, b2, w3, b3, freq, decay, bias):
    seq = u.shape[1]
    u = _short_conv(u, conv_w, conv_b)
    v = u[..., :D_HY]
    gates = (u[..., D_HY:2 * D_HY], u[..., 2 * D_HY:])
    kf = _hyena_filter_spectra(seq, w1, b1, w2, b2, w3, b3, freq, decay)
    z = v
    for o in range(HY_ORDER):
        z = gates[o] * _fft_conv(z, kf[:, o], bias[o])
    return z


def setup_inputs(seed: int = 0) -> dict:
    key = jax.random.key(seed)
    ks = iter(jax.random.split(key, 48))

    def nrm(shape, std):
        return std * jax.random.normal(next(ks), shape, jnp.float32)

    G, P = S5_GROUPS, S5_STATE
    decay_init = jnp.tile(jnp.linspace(HY_FAST_DECAY, HY_SLOW_DECAY, D_HY, dtype=jnp.float32), HY_ORDER * N_DIRS)
    return {
        "x": nrm((BATCH, SEQ, D_MODEL), 1.0),
        "c": nrm((BATCH, D_MODEL), 1.0),
        "ada_w": nrm((DEPTH, D_MODEL, 6 * D_MODEL), 0.5 * D_MODEL ** -0.5),
        "ada_b": nrm((DEPTH, 6 * D_MODEL), 0.02),
        "norm1_g": 1.0 + nrm((DEPTH, D_MODEL), 0.02),
        "norm2_g": 1.0 + nrm((DEPTH, D_MODEL), 0.02),
        "w_in": nrm((DEPTH, D_MODEL, D_IN), D_MODEL ** -0.5),
        "s5_lam_re": -0.5 + nrm((DEPTH, N_DIRS, G, P), 0.01),
        "s5_lam_im": math.pi * jnp.arange(P, dtype=jnp.float32) + nrm((DEPTH, N_DIRS, G, P), 0.01),
        "s5_log_step": jax.random.uniform(next(ks), (DEPTH, N_DIRS, G), jnp.float32, math.log(1e-3), math.log(1e-1)),
        "s5_b_re": nrm((DEPTH, N_DIRS, G, P, S5_GROUP), (2 * S5_GROUP) ** -0.5),
        "s5_b_im": nrm((DEPTH, N_DIRS, G, P, S5_GROUP), (2 * S5_GROUP) ** -0.5),
        "s5_c_re": nrm((DEPTH, N_DIRS, G, S5_GROUP, P), (2 * P) ** -0.5),
        "s5_c_im": nrm((DEPTH, N_DIRS, G, S5_GROUP, P), (2 * P) ** -0.5),
        "s5_d": nrm((DEPTH, D_S5), 1.0),
        "s5_glu_w": nrm((DEPTH, D_S5, D_S5), D_S5 ** -0.5),
        "s5_glu_b": nrm((DEPTH, D_S5), 0.02),
        "hy_conv_w": nrm((DEPTH, HY_SHORT, (HY_ORDER + 1) * D_HY), HY_SHORT ** -0.5),
        "hy_conv_b": nrm((DEPTH, (HY_ORDER + 1) * D_HY), 0.02),
        "hy_ffn_w1": nrm((DEPTH, HY_EMB, HY_HIDDEN), HY_EMB ** -0.5),
        "hy_ffn_b1": nrm((DEPTH, HY_HIDDEN), 0.1),
        "hy_ffn_w2": nrm((DEPTH, HY_HIDDEN, HY_HIDDEN), HY_HIDDEN ** -0.5),
        "hy_ffn_b2": nrm((DEPTH, HY_HIDDEN), 0.1),
        "hy_ffn_w3": nrm((DEPTH, HY_HIDDEN, HY_ORDER * N_DIRS * D_HY), 0.005),
        "hy_ffn_b3": nrm((DEPTH, HY_ORDER * N_DIRS * D_HY), 0.001),
        "hy_freq": 1.0 + nrm((DEPTH, HY_HIDDEN), 0.01),
        "hy_decay": decay_init + nrm((DEPTH, HY_ORDER * N_DIRS * D_HY), 0.01),
        "hy_bias": nrm((DEPTH, HY_ORDER, D_HY), 1.0),
        "w_branch_a": nrm((DEPTH, D_S5, D_MODEL), D_S5 ** -0.5),
        "w_branch_b": nrm((DEPTH, D_HY, D_MODEL), D_HY ** -0.5),
        "w_out": nrm((DEPTH, D_MODEL, D_MODEL), D_MODEL ** -0.5),
        "ffn_w_gu": nrm((DEPTH, D_MODEL, 2 * D_FF), D_MODEL ** -0.5),
        "ffn_w_down": nrm((DEPTH, D_FF, D_MODEL), D_FF ** -0.5),
        "final_g": 1.0 + nrm((D_MODEL,), 0.02),
    }


def reference(x, c, ada_w, ada_b, norm1_g, norm2_g, w_in,
              s5_lam_re, s5_lam_im, s5_log_step, s5_b_re, s5_b_im, s5_c_re, s5_c_im,
              s5_d, s5_glu_w, s5_glu_b,
              hy_conv_w, hy_conv_b, hy_ffn_w1, hy_ffn_b1, hy_ffn_w2, hy_ffn_b2,
              hy_ffn_w3, hy_ffn_b3, hy_freq, hy_decay, hy_bias,
              w_branch_a, w_branch_b, w_out, ffn_w_gu, ffn_w_down, final_g):
    bsz, seq, _ = x.shape
    c_act = jax.nn.silu(c)
    for i in range(DEPTH):
        mod = c_act @ ada_w[i] + ada_b[i]
        sh1, sc1, g1, sh2, sc2, g2 = jnp.split(mod, 6, axis=-1)

        h = _modulate(_rmsnorm(x, norm1_g[i]), sh1, sc1)
        p = h @ w_in[i]
        u_s5 = p[..., :D_S5]
        u_hy = p[..., D_S5:D_S5 + (HY_ORDER + 1) * D_HY]
        gate = jax.nn.sigmoid(p[..., D_S5 + (HY_ORDER + 1) * D_HY:].reshape(bsz, seq, N_BRANCH, D_MODEL))
        y_a = _s5_branch(u_s5, s5_lam_re[i], s5_lam_im[i], s5_log_step[i], s5_b_re[i], s5_b_im[i],
                         s5_c_re[i], s5_c_im[i], s5_d[i], s5_glu_w[i], s5_glu_b[i]) @ w_branch_a[i]
        y_b = _hyena_branch(u_hy, hy_conv_w[i], hy_conv_b[i], hy_ffn_w1[i], hy_ffn_b1[i], hy_ffn_w2[i],
                            hy_ffn_b2[i], hy_ffn_w3[i], hy_ffn_b3[i], hy_freq[i], hy_decay[i],
                            hy_bias[i]) @ w_branch_b[i]
        merged = gate[:, :, 0] * y_a + gate[:, :, 1] * y_b
        x = x + g1[:, None, :] * (merged @ w_out[i])

        h = _modulate(_rmsnorm(x, norm2_g[i]), sh2, sc2)
        gu = h @ ffn_w_gu[i]
        x = x + g2[:, None, :] * ((jax.nn.silu(gu[..., :D_FF]) * gu[..., D_FF:]) @ ffn_w_down[i])
    return _rmsnorm(x, final_g)
```

```python
import functools
import math

import jax
import jax.numpy as jnp
from jax import lax
from jax.experimental import pallas as pl
from jax.experimental.pallas import tpu as pltpu

F32 = jnp.float32
BF16 = jnp.bfloat16
EPS = 1e-6
HIGHEST = lax.Precision.HIGHEST

V7X_VMEM_BYTES = 64 * 1024 * 1024
VMEM_LIMIT_BYTES = 56 * 1024 * 1024
LANE = 128
ROW_TILE = 512
S5_TIME_CHUNK = 32
S5_LANE_CHUNK = 256
HY_CH_TILE = 256
DFT_ROW_CHUNK = 256
HY_FAST_DECAY_T = 0.3
HY_SLOW_DECAY_T = 1.5


def _cparams(n_axes):
    return pltpu.CompilerParams(
        dimension_semantics=("arbitrary",) * n_axes,
        vmem_limit_bytes=VMEM_LIMIT_BYTES,
    )


def _const_spec(shape):
    nd = len(shape)
    return pl.BlockSpec(shape, lambda *_: (0,) * nd, pipeline_mode=pl.Buffered(1))


def _gelu_tanh(x):
    return 0.5 * x * (1.0 + jnp.tanh(math.sqrt(2.0 / math.pi) * (x + 0.044715 * (x * x * x))))


def _norm_modulate(x, g, shift, scale):
    ms = jnp.mean(x * x, axis=-1, keepdims=True)
    r = x * lax.rsqrt(ms + EPS) * g
    return r * (1.0 + scale) + shift


def _mod_kernel(c_ref, w_ref, b_ref, o_ref):
    c = c_ref[...]
    ca = c * jax.nn.sigmoid(c)
    o_ref[...] = jnp.dot(ca, w_ref[...], precision=HIGHEST, preferred_element_type=F32) + b_ref[...]


def _modulation(c, ada_w, ada_b):
    bsz, d = c.shape
    n = ada_w.shape[1]
    tn = 512
    return pl.pallas_call(
        _mod_kernel,
        out_shape=jax.ShapeDtypeStruct((bsz, n), F32),
        grid=(n // tn,),
        in_specs=[
            pl.BlockSpec((bsz, d), lambda j: (0, 0)),
            pl.BlockSpec((d, tn), lambda j: (0, j)),
            pl.BlockSpec((1, tn), lambda j: (0, j)),
        ],
        out_specs=pl.BlockSpec((bsz, tn), lambda j: (0, j)),
        compiler_params=_cparams(1),
        name="adaln_mod",
    )(c, ada_w, ada_b.reshape(1, n))


def _inproj_kernel(x_ref, mod_ref, g_ref, w_ref, us5_ref, uhy_ref, *, d_s5):
    m = mod_ref[0]
    h = _norm_modulate(x_ref[...], g_ref[...], m[0:1, :], m[1:2, :]).astype(BF16)
    p = jnp.dot(h, w_ref[...], preferred_element_type=F32)
    us5_ref[...] = p[:, :d_s5]
    uhy_ref[...] = p[:, d_s5:].astype(BF16)


def _in_projection(x2, mod3, norm_g, w_uh, bsz, seq, d_s5):
    d = x2.shape[1]
    n = w_uh.shape[1]
    tm = min(ROW_TILE, seq)
    nt = seq // tm
    return pl.pallas_call(
        functools.partial(_inproj_kernel, d_s5=d_s5),
        out_shape=(
            jax.ShapeDtypeStruct((seq, bsz * d_s5), F32),
            jax.ShapeDtypeStruct((bsz * seq, n - d_s5), BF16),
        ),
        grid=(bsz, nt),
        in_specs=[
            pl.BlockSpec((tm, d), lambda b, j: (b * nt + j, 0)),
            pl.BlockSpec((1,) + mod3.shape[1:], lambda b, j: (b, 0, 0)),
            pl.BlockSpec((1, d), lambda b, j: (0, 0)),
            _const_spec((d, n)),
        ],
        out_specs=(
            pl.BlockSpec((tm, d_s5), lambda b, j: (j, b)),
            pl.BlockSpec((tm, n - d_s5), lambda b, j: (b * nt + j, 0)),
        ),
        compiler_params=_cparams(2),
        name="in_proj",
    )(x2, mod3, norm_g, w_uh)


def _s5_prep_kernel(lre_ref, lim_ref, lstep_ref, bre_ref, bim_ref, cim_ref,
                    are_ref, aim_ref, bbre_ref, bbim_ref, ncim_ref):
    step = jnp.exp(lstep_ref[...])
    lr = lre_ref[...]
    li = lim_ref[...]
    mag = jnp.exp(lr * step)
    ar = mag * jnp.cos(li * step)
    ai = mag * jnp.sin(li * step)
    num = ar - 1.0
    den = lr * lr + li * li
    cr = (num * lr + ai * li) / den
    ci = (ai * lr - num * li) / den
    are_ref[...] = ar
    aim_ref[...] = ai
    for d in range(lre_ref.shape[0]):
        br = bre_ref[d]
        bi = bim_ref[d]
        bbre_ref[d] = cr[d:d + 1, :] * br - ci[d:d + 1, :] * bi
        bbim_ref[d] = cr[d:d + 1, :] * bi + ci[d:d + 1, :] * br
    ncim_ref[...] = -cim_ref[...]


def _s5_prepare(lam_re, lam_im, log_step, b_re, b_im, c_im):
    outs = (
        jax.ShapeDtypeStruct(lam_re.shape, F32),
        jax.ShapeDtypeStruct(lam_re.shape, F32),
        jax.ShapeDtypeStruct(b_re.shape, F32),
        jax.ShapeDtypeStruct(b_re.shape, F32),
        jax.ShapeDtypeStruct(c_im.shape, F32),
    )
    return pl.pallas_call(
        _s5_prep_kernel,
        out_shape=outs,
        compiler_params=pltpu.CompilerParams(vmem_limit_bytes=VMEM_LIMIT_BYTES),
        name="s5_discretize",
    )(lam_re, lam_im, log_step, b_re, b_im, c_im)


def _block_diag(m):
    g, r, c = m.shape
    eye = jnp.eye(g, dtype=m.dtype)
    return (m[:, :, None, :] * eye[:, None, :, None]).reshape(g * r, g * c)


def _s5_scan_kernel(*refs, reverse, final, tc, nb, ns):
    if final:
        (u_ref, bd_ref, cd_ref, are_ref, aim_ref, yprev_ref, gluw_ref, glub_ref,
         out_ref, bu_ref, s_ref) = refs
    else:
        (u_ref, bd_ref, cd_ref, are_ref, aim_ref, d_ref, out_ref, bu_ref, s_ref) = refs
    rows = tc * nb
    rb = min(256, rows)

    @pl.when(pl.program_id(0) == 0)
    def _():
        s_ref[...] = jnp.zeros_like(s_ref)

    for r in range(rows // rb):
        ub = u_ref[r * rb:(r + 1) * rb, :].astype(BF16)
        bu_ref[r * rb:(r + 1) * rb, :] = jnp.dot(ub, bd_ref[...], preferred_element_type=F32)

    lc = min(S5_LANE_CHUNK, ns)
    for c in range(ns // lc):
        re_sl = slice(c * lc, (c + 1) * lc)
        im_sl = slice(ns + c * lc, ns + (c + 1) * lc)
        a_re = are_ref[:, re_sl]
        a_im = aim_ref[:, re_sl]

        def body(k, carry, re_sl=re_sl, im_sl=im_sl, a_re=a_re, a_im=a_im):
            sr, si = carry
            t = (tc - 1 - k) if reverse else k
            row = pl.multiple_of(t * nb, nb)
            br = bu_ref[pl.ds(row, nb), re_sl]
            bi = bu_ref[pl.ds(row, nb), im_sl]
            nr = a_re * sr - a_im * si + br
            ni = a_re * si + a_im * sr + bi
            bu_ref[pl.ds(row, nb), re_sl] = nr
            bu_ref[pl.ds(row, nb), im_sl] = ni
            return nr, ni

        sr, si = lax.fori_loop(0, tc, body, (s_ref[:, re_sl], s_ref[:, im_sl]), unroll=8)
        s_ref[:, re_sl] = sr
        s_ref[:, im_sl] = si

    for r in range(rows // rb):
        rs = slice(r * rb, (r + 1) * rb)
        sb = bu_ref[rs, :].astype(BF16)
        y = jnp.dot(sb, cd_ref[...], preferred_element_type=F32)
        if final:
            tot = yprev_ref[rs, :] + y
            z = _gelu_tanh(tot)
            gate = jnp.dot(z.astype(BF16), gluw_ref[...], preferred_element_type=F32) + glub_ref[...]
            out_ref[rs, :] = (z * jax.nn.sigmoid(gate)).astype(out_ref.dtype)
        else:
            out_ref[rs, :] = y + u_ref[rs, :] * d_ref[...]


def _s5_scan(u_tm, bd, cd, a_re, a_im, extra, *, reverse, final, nb, out_dtype):
    rows_total, d_s5 = u_tm.shape
    seq = rows_total // nb
    two_ns = bd.shape[1]
    ns = two_ns // 2
    tc = min(S5_TIME_CHUNK, seq)
    nchunk = seq // tc
    rows = tc * nb
    if reverse:
        cidx = lambda i: (nchunk - 1 - i, 0)
    else:
        cidx = lambda i: (i, 0)
    in_specs = [
        pl.BlockSpec((rows, d_s5), cidx),
        _const_spec(bd.shape),
        _const_spec(cd.shape),
        _const_spec(a_re.shape),
        _const_spec(a_im.shape),
    ]
    if final:
        yprev, glu_w, glu_b = extra
        in_specs += [pl.BlockSpec((rows, d_s5), cidx), _const_spec(glu_w.shape), _const_spec(glu_b.shape)]
    else:
        in_specs += [_const_spec(extra[0].shape)]
    return pl.pallas_call(
        functools.partial(_s5_scan_kernel, reverse=reverse, final=final, tc=tc, nb=nb, ns=ns),
        out_shape=jax.ShapeDtypeStruct((rows_total, d_s5), out_dtype),
        grid=(nchunk,),
        in_specs=in_specs,
        out_specs=pl.BlockSpec((rows, d_s5), cidx),
        scratch_shapes=[pltpu.VMEM((rows, two_ns), F32), pltpu.VMEM((nb, two_ns), F32)],
        compiler_params=_cparams(1),
        name="s5_scan_bwd_glu" if final else "s5_scan_fwd",
    )(u_tm, bd, cd, a_re, a_im, *extra)


def _s5_branch(u_tm, nb, lam_re, lam_im, log_step, b_re, b_im, c_re, c_im, d, glu_w, glu_b):
    ndir, g, p = lam_re.shape
    grp = b_re.shape[-1]
    ns = g * p
    flat = lambda a: a.reshape(ndir, ns)
    to_lanes = lambda a: jnp.transpose(a, (0, 3, 1, 2)).reshape(ndir, grp, ns)
    a_re, a_im, bb_re, bb_im, ncim = _s5_prepare(
        flat(lam_re), flat(lam_im), jnp.repeat(log_step, p, axis=-1),
        to_lanes(b_re), to_lanes(b_im),
        jnp.transpose(c_im, (0, 2, 1, 3)).reshape(ndir, grp, ns))
    y = None
    for direction in range(ndir):
        blk = lambda a: jnp.transpose(a[direction].reshape(grp, g, p), (1, 0, 2))
        bd = jnp.concatenate([_block_diag(blk(bb_re)), _block_diag(blk(bb_im))], axis=1).astype(BF16)
        cblk = lambda a: jnp.transpose(a, (0, 2, 1))
        cd = jnp.concatenate([_block_diag(cblk(c_re[direction])), _block_diag(cblk(blk(ncim)))],
                             axis=0).astype(BF16)
        are = jnp.broadcast_to(a_re[direction][None, :], (nb, ns))
        aim = jnp.broadcast_to(a_im[direction][None, :], (nb, ns))
        last = direction == ndir - 1
        if direction == 0:
            extra = (d.reshape(1, -1),)
            if last:
                raise NotImplementedError("S5 branch expects forward and backward directions")
            y = _s5_scan(u_tm, bd, cd, are, aim, extra, reverse=False, final=False, nb=nb, out_dtype=F32)
        else:
            extra = (y, glu_w.astype(BF16), glu_b.reshape(1, -1))
            y = _s5_scan(u_tm, bd, cd, are, aim, extra, reverse=True, final=True, nb=nb, out_dtype=BF16)
    return y


def _dft_matrices(seq):
    n = 2 * seq
    f_lo = 64 if seq % 64 == 0 else 1
    f_hi = seq // f_lo
    t = jnp.arange(seq, dtype=jnp.int32)[None, :]
    ka = (jnp.arange(f_hi, dtype=jnp.int32)[:, None] * f_lo * t) % n
    kb = (jnp.arange(f_lo, dtype=jnp.int32)[:, None] * t) % n
    w = 2.0 * math.pi / n
    aa = ka.astype(F32) * w
    ab = kb.astype(F32) * w
    ca, sa, cb, sb = jnp.cos(aa), jnp.sin(aa), jnp.cos(ab), jnp.sin(ab)
    cm = ca[:, None, :] * cb[None, :, :] - sa[:, None, :] * sb[None, :, :]
    sm = sa[:, None, :] * cb[None, :, :] + ca[:, None, :] * sb[None, :, :]
    return cm.reshape(seq, seq).astype(BF16), sm.reshape(seq, seq).astype(BF16)


def _dft_apply(m_ref, rhs_ref, out_ref, accumulate=False):
    n = m_ref.shape[0]
    rc = min(DFT_ROW_CHUNK, n)

    def body(i, carry):
        r = pl.multiple_of(i * rc, rc)
        v = jnp.dot(m_ref[pl.ds(r, rc), :], rhs_ref[...], preferred_element_type=F32)
        if accumulate:
            out_ref[pl.ds(r, rc), :] += v
        else:
            out_ref[pl.ds(r, rc), :] = v
        return carry

    lax.fori_loop(0, n // rc, body, 0)


def _alt_sign(seq):
    row = lax.broadcasted_iota(jnp.int32, (seq, 1), 0)
    return (1 - 2 * (row & 1)).astype(F32)


def _hy_filter_kernel(feat_ref, w1_ref, b1_ref, w2_ref, b2_ref, freq_ref,
                      w3f_ref, b3f_ref, decf_ref, w3b_ref, b3b_ref, decb_ref, cm_ref, sm_ref,
                      kre_ref, kim_ref, knyq_ref, e_s, d_s, acc_s):
    seq = feat_ref.shape[0]
    n = 2 * seq
    feats = feat_ref[...]
    f = freq_ref[...]
    h = jnp.sin(f * (jnp.dot(feats, w1_ref[...], precision=HIGHEST, preferred_element_type=F32) + b1_ref[...]))
    h = jnp.sin(f * (jnp.dot(h, w2_ref[...], precision=HIGHEST, preferred_element_type=F32) + b2_ref[...]))
    t01 = feats[:, 0:1]
    row = lax.broadcasted_iota(jnp.int32, (seq, 1), 0)

    def taps(w3_ref, b3_ref, dec_ref):
        v = jnp.dot(h, w3_ref[...], precision=HIGHEST, preferred_element_type=F32) + b3_ref[...]
        return v * jnp.exp(-t01 * jnp.abs(dec_ref[...]))

    fwd = taps(w3f_ref, b3f_ref, decf_ref)
    bwd = jnp.where(row == 0, 0.0, taps(w3b_ref, b3b_ref, decb_ref))
    e = fwd + bwd
    e_s[...] = e.astype(BF16)
    d_s[...] = (bwd - fwd).astype(BF16)
    scale = jnp.where(row == 0, 1.0 / n, 2.0 / n)
    _dft_apply(cm_ref, e_s, acc_s)
    kre_ref[...] = acc_s[...] * scale
    _dft_apply(sm_ref, d_s, acc_s)
    kim_ref[...] = acc_s[...] * scale
    nyq = jnp.sum(e * _alt_sign(seq), axis=0, keepdims=True) * (1.0 / n)
    knyq_ref[...] = jnp.broadcast_to(nyq, knyq_ref.shape)


def _hyena_filters(seq, w1, b1, w2, b2, w3, b3, freq, decay, cm, sm, n_order, n_dirs, d_hy):
    emb, hid = w1.shape
    bands = (emb - 1) // 2
    t = jnp.arange(seq, dtype=F32)
    t01 = t / max(seq - 1, 1)
    band = jnp.linspace(1e-4, bands - 1, bands, dtype=F32)
    ang = (2.0 * math.pi) * t[:, None] * band[None, :] / seq
    feats = jnp.concatenate([t01[:, None], jnp.cos(ang), jnp.sin(ang)], axis=-1)
    kpad = LANE
    feats = jnp.pad(feats, ((0, 0), (0, kpad - emb)))
    w1p = jnp.pad(w1, ((0, kpad - emb), (0, 0)))
    ct = min(HY_CH_TILE, d_hy)
    nct = d_hy // ct
    ncol = n_order * d_hy
    b3r = b3.reshape(1, -1)
    decr = decay.reshape(1, -1)
    fcol = lambda o, c: (0, (o * n_dirs + 0) * nct + c)
    bcol = lambda o, c: (0, (o * n_dirs + 1) * nct + c)
    ocol = lambda o, c: (0, o * nct + c)
    full = lambda a: pl.BlockSpec(a.shape, lambda o, c: (0,) * a.ndim)
    return pl.pallas_call(
        _hy_filter_kernel,
        out_shape=(
            jax.ShapeDtypeStruct((seq, ncol), F32),
            jax.ShapeDtypeStruct((seq, ncol), F32),
            jax.ShapeDtypeStruct((8, ncol), F32),
        ),
        grid=(n_order, nct),
        in_specs=[
            full(feats), full(w1p), full(b1.reshape(1, -1)), full(w2), full(b2.reshape(1, -1)),
            full(freq.reshape(1, -1)),
            pl.BlockSpec((hid, ct), fcol), pl.BlockSpec((1, ct), fcol), pl.BlockSpec((1, ct), fcol),
            pl.BlockSpec((hid, ct), bcol), pl.BlockSpec((1, ct), bcol), pl.BlockSpec((1, ct), bcol),
            _const_spec(cm.shape), _const_spec(sm.shape),
        ],
        out_specs=(
            pl.BlockSpec((seq, ct), ocol), pl.BlockSpec((seq, ct), ocol), pl.BlockSpec((8, ct), ocol),
        ),
        scratch_shapes=[pltpu.VMEM((seq, ct), BF16), pltpu.VMEM((seq, ct), BF16), pltpu.VMEM((seq, ct), F32)],
        compiler_params=_cparams(2),
        name="hyena_filter_spectra",
    )(feats, w1p, b1.reshape(1, -1), w2, b2.reshape(1, -1), freq.reshape(1, -1),
      w3, b3r, decr, w3, b3r, decr, cm, sm)


def _hy_conv_kernel(zin_ref, gin_ref, wz_ref, bz_ref, wg_ref, bg_ref, kre_ref, kim_ref, knyq_ref,
                    bias_ref, cm_ref, sm_ref, out_ref, zb_s, ac_s, as_s, yre_s, v_s, *, conv_on_z):
    seq = zin_ref.shape[0]
    row = lax.broadcasted_iota(jnp.int32, (seq, 1), 0)

    def short_conv(u, w_ref, b_ref):
        prev = jnp.where(row == 0, 0.0, pltpu.roll(u, 1, 0))
        nxt = jnp.where(row == seq - 1, 0.0, pltpu.roll(u, seq - 1, 0))
        return b_ref[...] + prev * w_ref[0:1, :] + u * w_ref[1:2, :] + nxt * w_ref[2:3, :]

    z = zin_ref[...].astype(F32)
    if conv_on_z:
        z = short_conv(z, wz_ref, bz_ref)
    alt = _alt_sign(seq)
    nyq = jnp.sum(z * alt, axis=0, keepdims=True)
    zb_s[...] = z.astype(BF16)
    _dft_apply(cm_ref, zb_s, ac_s)
    _dft_apply(sm_ref, zb_s, as_s)
    a_c = ac_s[...]
    a_s = as_s[...]
    kre = kre_ref[...]
    kim = kim_ref[...]
    yre_s[...] = (a_c * kre + a_s * kim).astype(BF16)
    v_s[...] = (a_s * kre - a_c * kim).astype(BF16)
    _dft_apply(cm_ref, yre_s, ac_s)
    _dft_apply(sm_ref, v_s, ac_s, accumulate=True)
    y = ac_s[...] + alt * (nyq * knyq_ref[0:1, :])
    g = short_conv(gin_ref[...].astype(F32), wg_ref, bg_ref)
    z = zin_ref[...].astype(F32)
    if conv_on_z:
        z = short_conv(z, wz_ref, bz_ref)
    out_ref[...] = (g * (y + z * bias_ref[...])).astype(out_ref.dtype)


def _hy_conv(zin, zcol0, gcol0, u_hy, conv_w, conv_b, kre, kim, knyq, bias_row, cm, sm,
             *, order, conv_on_z, bsz, seq, d_hy):
    ct = min(HY_CH_TILE, d_hy)
    nct = d_hy // ct
    zc0 = zcol0 // ct
    gc0 = gcol0 // ct
    zw0 = zc0 if conv_on_z else 0
    return pl.pallas_call(
        functools.partial(_hy_conv_kernel, conv_on_z=conv_on_z),
        out_shape=jax.ShapeDtypeStruct((bsz * seq, d_hy), BF16),
        grid=(nct, bsz),
        in_specs=[
            pl.BlockSpec((seq, ct), lambda c, b: (b, zc0 + c)),
            pl.BlockSpec((seq, ct), lambda c, b: (b, gc0 + c)),
            pl.BlockSpec((conv_w.shape[0], ct), lambda c, b: (0, zw0 + c)),
            pl.BlockSpec((1, ct), lambda c, b: (0, zw0 + c)),
            pl.BlockSpec((conv_w.shape[0], ct), lambda c, b: (0, gc0 + c)),
            pl.BlockSpec((1, ct), lambda c, b: (0, gc0 + c)),
            pl.BlockSpec((seq, ct), lambda c, b: (0, order * nct + c)),
            pl.BlockSpec((seq, ct), lambda c, b: (0, order * nct + c)),
            pl.BlockSpec((8, ct), lambda c, b: (0, order * nct + c)),
            pl.BlockSpec((1, ct), lambda c, b: (0, order * nct + c)),
            _const_spec(cm.shape), _const_spec(sm.shape),
        ],
        out_specs=pl.BlockSpec((seq, ct), lambda c, b: (b, c)),
        scratch_shapes=[
            pltpu.VMEM((seq, ct), BF16), pltpu.VMEM((seq, ct), F32), pltpu.VMEM((seq, ct), F32),
            pltpu.VMEM((seq, ct), BF16), pltpu.VMEM((seq, ct), BF16),
        ],
        compiler_params=_cparams(2),
        name=f"hyena_conv_order{order}",
    )(zin, u_hy, conv_w, conv_b, conv_w, conv_b, kre, kim, knyq, bias_row, cm, sm)


def _hyena_branch(u_hy, bsz, seq, conv_w, conv_b, w1, b1, w2, b2, w3, b3, freq, decay, bias):
    n_order, d_hy = bias.shape
    n_dirs = w3.shape[1] // (n_order * d_hy)
    if n_order != 2:
        raise NotImplementedError("Hyena branch is written for two long convolutions")
    cm, sm = _dft_matrices(seq)
    kre, kim, knyq = _hyena_filters(seq, w1, b1, w2, b2, w3, b3, freq, decay, cm, sm, n_order, n_dirs, d_hy)
    cb = conv_b.reshape(1, -1)
    bias_row = bias.reshape(1, -1)
    common = dict(bsz=bsz, seq=seq, d_hy=d_hy)
    z1 = _hy_conv(u_hy, 0, d_hy, u_hy, conv_w, cb, kre, kim, knyq, bias_row, cm, sm,
                  order=0, conv_on_z=True, **common)
    return _hy_conv(z1, 0, 2 * d_hy, u_hy, conv_w, cb, kre, kim, knyq, bias_row, cm, sm,
                    order=1, conv_on_z=False, **common)


def _merge_kernel(x_ref, mod_ref, g_ref, wgate_ref, za_ref, zb_ref, wa_ref, wb_ref, wout_ref, o_ref):
    d = x_ref.shape[1]
    m = mod_ref[0]
    x = x_ref[...]
    h = _norm_modulate(x, g_ref[...], m[0:1, :], m[1:2, :]).astype(BF16)
    gate = jax.nn.sigmoid(jnp.dot(h, wgate_ref[...], preferred_element_type=F32))
    ya = jnp.dot(za_ref[...], wa_ref[...], preferred_element_type=F32)
    yb = jnp.dot(zb_ref[...], wb_ref[...], preferred_element_type=F32)
    merged = gate[:, :d] * ya + gate[:, d:] * yb
    o = jnp.dot(merged.astype(BF16), wout_ref[...], preferred_element_type=F32)
    o_ref[...] = x + m[2:3, :] * o


def _merge(x2, mod3, norm_g, w_gate, za_tm, zb, w_a, w_b, w_out, bsz, seq):
    d = x2.shape[1]
    d_s5 = w_a.shape[0]
    d_hy = w_b.shape[0]
    tm = min(ROW_TILE, seq)
    nt = seq // tm
    return pl.pallas_call(
        _merge_kernel,
        out_shape=jax.ShapeDtypeStruct(x2.shape, F32),
        grid=(bsz, nt),
        in_specs=[
            pl.BlockSpec((tm, d), lambda b, j: (b * nt + j, 0)),
            pl.BlockSpec((1,) + mod3.shape[1:], lambda b, j: (b, 0, 0)),
            pl.BlockSpec((1, d), lambda b, j: (0, 0)),
            _const_spec(w_gate.shape),
            pl.BlockSpec((tm, d_s5), lambda b, j: (j, b)),
            pl.BlockSpec((tm, d_hy), lambda b, j: (b * nt + j, 0)),
            _const_spec(w_a.shape), _const_spec(w_b.shape), _const_spec(w_out.shape),
        ],
        out_specs=pl.BlockSpec((tm, d), lambda b, j: (b * nt + j, 0)),
        compiler_params=_cparams(2),
        name="gated_merge_out_proj",
    )(x2, mod3, norm_g, w_gate, za_tm, zb, w_a, w_b, w_out)


def _ffn_kernel(x_ref, mod_ref, g_ref, wg_ref, wu_ref, wd_ref, fg_ref, o_ref, *, n_chunks):
    m = mod_ref[0]
    x = x_ref[...]
    h = _norm_modulate(x, g_ref[...], m[3:4, :], m[4:5, :]).astype(BF16)
    d_ff = wg_ref.shape[1]
    fc = d_ff // n_chunks
    acc = jnp.zeros(x.shape, F32)
    for c in range(n_chunks):
        sl = slice(c * fc, (c + 1) * fc)
        gl = jnp.dot(h, wg_ref[:, sl], preferred_element_type=F32)
        up = jnp.dot(h, wu_ref[:, sl], preferred_element_type=F32)
        act = (gl * jax.nn.sigmoid(gl) * up).astype(BF16)
        acc = acc + jnp.dot(act, wd_ref[sl, :], preferred_element_type=F32)
    x2 = x + m[5:6, :] * acc
    ms = jnp.mean(x2 * x2, axis=-1, keepdims=True)
    o_ref[...] = x2 * lax.rsqrt(ms + EPS) * fg_ref[...]


def _ffn(x2, mod3, norm_g, w_g, w_u, w_d, final_g, bsz, seq):
    d = x2.shape[1]
    d_ff = w_g.shape[1]
    tm = min(ROW_TILE, seq)
    nt = seq // tm
    n_chunks = 2 if d_ff % (2 * LANE) == 0 else 1
    return pl.pallas_call(
        functools.partial(_ffn_kernel, n_chunks=n_chunks),
        out_shape=jax.ShapeDtypeStruct(x2.shape, F32),
        grid=(bsz, nt),
        in_specs=[
            pl.BlockSpec((tm, d), lambda b, j: (b * nt + j, 0)),
            pl.BlockSpec((1,) + mod3.shape[1:], lambda b, j: (b, 0, 0)),
            pl.BlockSpec((1, d), lambda b, j: (0, 0)),
            _const_spec(w_g.shape), _const_spec(w_u.shape), _const_spec(w_d.shape),
            pl.BlockSpec((1, d), lambda b, j: (0, 0)),
        ],
        out_specs=pl.BlockSpec((tm, d), lambda b, j: (b * nt + j, 0)),
        compiler_params=_cparams(2),
        name="swiglu_final_norm",
    )(x2, mod3, norm_g, w_g, w_u, w_d, final_g)


def kernel(x, c, ada_w, ada_b, norm1_g, norm2_g, w_in, s5_lam_re, s5_lam_im, s5_log_step, s5_b_re, s5_b_im, s5_c_re, s5_c_im, s5_d, s5_glu_w, s5_glu_b, hy_conv_w, hy_conv_b, hy_ffn_w1, hy_ffn_b1, hy_ffn_w2, hy_ffn_b2, hy_ffn_w3, hy_ffn_b3, hy_freq, hy_decay, hy_bias, w_branch_a, w_branch_b, w_out, ffn_w_gu, ffn_w_down, final_g):
    bsz, seq, d = x.shape
    depth = ada_w.shape[0]
    d_s5 = s5_d.shape[-1]
    n_order, d_hy = hy_bias.shape[1:]
    d_uh = d_s5 + (n_order + 1) * d_hy
    d_ff = ffn_w_down.shape[1]
    xs = x.reshape(bsz * seq, d)
    for i in range(depth):
        mod = _modulation(c, ada_w[i], ada_b[i]).reshape(bsz, 6, d)
        w_in_b = w_in[i].astype(BF16)
        u_s5, u_hy = _in_projection(xs, mod, norm1_g[i].reshape(1, d), w_in_b[:, :d_uh], bsz, seq, d_s5)
        z_a = _s5_branch(u_s5.reshape(seq * bsz, d_s5), bsz, s5_lam_re[i], s5_lam_im[i], s5_log_step[i],
                         s5_b_re[i], s5_b_im[i], s5_c_re[i], s5_c_im[i], s5_d[i], s5_glu_w[i], s5_glu_b[i])
        z_b = _hyena_branch(u_hy, bsz, seq, hy_conv_w[i], hy_conv_b[i], hy_ffn_w1[i], hy_ffn_b1[i],
                            hy_ffn_w2[i], hy_ffn_b2[i], hy_ffn_w3[i], hy_ffn_b3[i], hy_freq[i],
                            hy_decay[i], hy_bias[i])
        xs = _merge(xs, mod, norm1_g[i].reshape(1, d), w_in_b[:, d_uh:], z_a.reshape(seq, bsz * d_s5), z_b,
                    w_branch_a[i].astype(BF16), w_branch_b[i].astype(BF16), w_out[i].astype(BF16), bsz, seq)
        w_gu = ffn_w_gu[i].astype(BF16)
        fg = final_g.reshape(1, d) if i == depth - 1 else None
        if fg is None:
            raise NotImplementedError("final RMSNorm is fused into the last layer's channel mixer")
        xs = _ffn(xs, mod, norm2_g[i].reshape(1, d), w_gu[:, :d_ff], w_gu[:, d_ff:],
                  ffn_w_down[i].astype(BF16), fg, bsz, seq)
    return xs.reshape(bsz, seq, d)
```

```python
import functools
import math

import jax
import jax.numpy as jnp
from jax import lax
from jax.experimental import pallas as pl
from jax.experimental.pallas import tpu as pltpu

F32 = jnp.float32
BF16 = jnp.bfloat16
EPS = 1e-6
HIGHEST = lax.Precision.HIGHEST

V7X_VMEM_BYTES = 64 * 1024 * 1024
VMEM_LIMIT_BYTES = 56 * 1024 * 1024
LANE = 128
ROW_TILE = 512
S5_TIME_CHUNK = 32
S5_LANE_CHUNK = 256
HY_CH_TILE = 256
DFT_ROW_CHUNK = 1024
HY_FAST_DECAY_T = 0.3
HY_SLOW_DECAY_T = 1.5


def _cparams(n_axes):
    return pltpu.CompilerParams(
        dimension_semantics=("arbitrary",) * n_axes,
        vmem_limit_bytes=VMEM_LIMIT_BYTES,
    )


def _const_spec(shape):
    nd = len(shape)
    return pl.BlockSpec(shape, lambda *_: (0,) * nd, pipeline_mode=pl.Buffered(1))


def _gelu_tanh(x):
    return 0.5 * x * (1.0 + jnp.tanh(math.sqrt(2.0 / math.pi) * (x + 0.044715 * (x * x * x))))


def _norm_modulate(x, g, shift, scale):
    ms = jnp.mean(x * x, axis=-1, keepdims=True)
    r = x * lax.rsqrt(ms + EPS) * g
    return r * (1.0 + scale) + shift


def _mod_kernel(c_ref, w_ref, b_ref, o_ref):
    c = c_ref[...]
    ca = c * jax.nn.sigmoid(c)
    o_ref[...] = jnp.dot(ca, w_ref[...], precision=HIGHEST, preferred_element_type=F32) + b_ref[...]


def _modulation(c, ada_w, ada_b):
    bsz, d = c.shape
    n = ada_w.shape[1]
    tn = 512
    return pl.pallas_call(
        _mod_kernel,
        out_shape=jax.ShapeDtypeStruct((bsz, n), F32),
        grid=(n // tn,),
        in_specs=[
            pl.BlockSpec((bsz, d), lambda j: (0, 0)),
            pl.BlockSpec((d, tn), lambda j: (0, j)),
            pl.BlockSpec((1, tn), lambda j: (0, j)),
        ],
        out_specs=pl.BlockSpec((bsz, tn), lambda j: (0, j)),
        compiler_params=_cparams(1),
        name="adaln_mod",
    )(c, ada_w, ada_b.reshape(1, n))


def _inproj_kernel(x_ref, mod_ref, g_ref, w_ref, us5_ref, uhy_ref, *, d_s5):
    m = mod_ref[0]
    h = _norm_modulate(x_ref[...], g_ref[...], m[0:1, :], m[1:2, :]).astype(BF16)
    p = jnp.dot(h, w_ref[...], preferred_element_type=F32)
    us5_ref[...] = p[:, :d_s5]
    uhy_ref[...] = p[:, d_s5:].astype(BF16)


def _in_projection(x2, mod3, norm_g, w_uh, bsz, seq, d_s5):
    d = x2.shape[1]
    n = w_uh.shape[1]
    tm = min(ROW_TILE, seq)
    nt = seq // tm
    return pl.pallas_call(
        functools.partial(_inproj_kernel, d_s5=d_s5),
        out_shape=(
            jax.ShapeDtypeStruct((seq, bsz * d_s5), F32),
            jax.ShapeDtypeStruct((bsz * seq, n - d_s5), BF16),
        ),
        grid=(bsz, nt),
        in_specs=[
            pl.BlockSpec((tm, d), lambda b, j: (b * nt + j, 0)),
            pl.BlockSpec((1,) + mod3.shape[1:], lambda b, j: (b, 0, 0)),
            pl.BlockSpec((1, d), lambda b, j: (0, 0)),
            _const_spec((d, n)),
        ],
        out_specs=(
            pl.BlockSpec((tm, d_s5), lambda b, j: (j, b)),
            pl.BlockSpec((tm, n - d_s5), lambda b, j: (b * nt + j, 0)),
        ),
        compiler_params=_cparams(2),
        name="in_proj",
    )(x2, mod3, norm_g, w_uh)


def _s5_prep_kernel(lre_ref, lim_ref, lstep_ref, bre_ref, bim_ref, cim_ref,
                    are_ref, aim_ref, bbre_ref, bbim_ref, ncim_ref):
    step = jnp.exp(lstep_ref[...])
    lr = lre_ref[...]
    li = lim_ref[...]
    mag = jnp.exp(lr * step)
    ar = mag * jnp.cos(li * step)
    ai = mag * jnp.sin(li * step)
    num = ar - 1.0
    den = lr * lr + li * li
    cr = (num * lr + ai * li) / den
    ci = (ai * lr - num * li) / den
    are_ref[...] = ar
    aim_ref[...] = ai
    for d in range(lre_ref.shape[0]):
        br = bre_ref[d]
        bi = bim_ref[d]
        bbre_ref[d] = cr[d:d + 1, :] * br - ci[d:d + 1, :] * bi
        bbim_ref[d] = cr[d:d + 1, :] * bi + ci[d:d + 1, :] * br
    ncim_ref[...] = -cim_ref[...]


def _s5_prepare(lam_re, lam_im, log_step, b_re, b_im, c_im):
    outs = (
        jax.ShapeDtypeStruct(lam_re.shape, F32),
        jax.ShapeDtypeStruct(lam_re.shape, F32),
        jax.ShapeDtypeStruct(b_re.shape, F32),
        jax.ShapeDtypeStruct(b_re.shape, F32),
        jax.ShapeDtypeStruct(c_im.shape, F32),
    )
    return pl.pallas_call(
        _s5_prep_kernel,
        out_shape=outs,
        compiler_params=pltpu.CompilerParams(vmem_limit_bytes=VMEM_LIMIT_BYTES),
        name="s5_discretize",
    )(lam_re, lam_im, log_step, b_re, b_im, c_im)


def _block_diag(m):
    g, r, c = m.shape
    eye = jnp.eye(g, dtype=m.dtype)
    return (m[:, :, None, :] * eye[:, None, :, None]).reshape(g * r, g * c)


def _s5_scan_kernel(*refs, reverse, final, tc, nb, ns):
    if final:
        (u_ref, bd_ref, cd_ref, are_ref, aim_ref, yprev_ref, gluw_ref, glub_ref,
         out_ref, bu_ref, s_ref) = refs
    else:
        (u_ref, bd_ref, cd_ref, are_ref, aim_ref, d_ref, out_ref, bu_ref, s_ref) = refs
    rows = tc * nb
    rb = min(256, rows)

    @pl.when(pl.program_id(0) == 0)
    def _():
        s_ref[...] = jnp.zeros_like(s_ref)

    for r in range(rows // rb):
        ub = u_ref[r * rb:(r + 1) * rb, :].astype(BF16)
        bu_ref[r * rb:(r + 1) * rb, :] = jnp.dot(ub, bd_ref[...], preferred_element_type=F32)

    lc = min(S5_LANE_CHUNK, ns)
    for c in range(ns // lc):
        re_sl = slice(c * lc, (c + 1) * lc)
        im_sl = slice(ns + c * lc, ns + (c + 1) * lc)
        a_re = are_ref[:, re_sl]
        a_im = aim_ref[:, re_sl]

        def body(k, carry, re_sl=re_sl, im_sl=im_sl, a_re=a_re, a_im=a_im):
            sr, si = carry
            t = (tc - 1 - k) if reverse else k
            row = pl.multiple_of(t * nb, nb)
            br = bu_ref[pl.ds(row, nb), re_sl]
            bi = bu_ref[pl.ds(row, nb), im_sl]
            nr = a_re * sr - a_im * si + br
            ni = a_re * si + a_im * sr + bi
            bu_ref[pl.ds(row, nb), re_sl] = nr
            bu_ref[pl.ds(row, nb), im_sl] = ni
            return nr, ni

        sr, si = lax.fori_loop(0, tc, body, (s_ref[:, re_sl], s_ref[:, im_sl]), unroll=8)
        s_ref[:, re_sl] = sr
        s_ref[:, im_sl] = si

    for r in range(rows // rb):
        rs = slice(r * rb, (r + 1) * rb)
        sb = bu_ref[rs, :].astype(BF16)
        y = jnp.dot(sb, cd_ref[...], preferred_element_type=F32)
        if final:
            tot = yprev_ref[rs, :] + y
            z = _gelu_tanh(tot)
            gate = jnp.dot(z.astype(BF16), gluw_ref[...], preferred_element_type=F32) + glub_ref[...]
            out_ref[rs, :] = (z * jax.nn.sigmoid(gate)).astype(out_ref.dtype)
        else:
            out_ref[rs, :] = y + u_ref[rs, :] * d_ref[...]


def _s5_scan(u_tm, bd, cd, a_re, a_im, extra, *, reverse, final, nb, out_dtype):
    rows_total, d_s5 = u_tm.shape
    seq = rows_total // nb
    two_ns = bd.shape[1]
    ns = two_ns // 2
    tc = min(S5_TIME_CHUNK, seq)
    nchunk = seq // tc
    rows = tc * nb
    if reverse:
        cidx = lambda i: (nchunk - 1 - i, 0)
    else:
        cidx = lambda i: (i, 0)
    in_specs = [
        pl.BlockSpec((rows, d_s5), cidx),
        _const_spec(bd.shape),
        _const_spec(cd.shape),
        _const_spec(a_re.shape),
        _const_spec(a_im.shape),
    ]
    if final:
        yprev, glu_w, glu_b = extra
        in_specs += [pl.BlockSpec((rows, d_s5), cidx), _const_spec(glu_w.shape), _const_spec(glu_b.shape)]
    else:
        in_specs += [_const_spec(extra[0].shape)]
    return pl.pallas_call(
        functools.partial(_s5_scan_kernel, reverse=reverse, final=final, tc=tc, nb=nb, ns=ns),
        out_shape=jax.ShapeDtypeStruct((rows_total, d_s5), out_dtype),
        grid=(nchunk,),
        in_specs=in_specs,
        out_specs=pl.BlockSpec((rows, d_s5), cidx),
        scratch_shapes=[pltpu.VMEM((rows, two_ns), F32), pltpu.VMEM((nb, two_ns), F32)],
        compiler_params=_cparams(1),
        name="s5_scan_bwd_glu" if final else "s5_scan_fwd",
    )(u_tm, bd, cd, a_re, a_im, *extra)


def _s5_branch(u_tm, nb, lam_re, lam_im, log_step, b_re, b_im, c_re, c_im, d, glu_w, glu_b):
    ndir, g, p = lam_re.shape
    grp = b_re.shape[-1]
    ns = g * p
    flat = lambda a: a.reshape(ndir, ns)
    to_lanes = lambda a: jnp.transpose(a, (0, 3, 1, 2)).reshape(ndir, grp, ns)
    a_re, a_im, bb_re, bb_im, ncim = _s5_prepare(
        flat(lam_re), flat(lam_im), jnp.repeat(log_step, p, axis=-1),
        to_lanes(b_re), to_lanes(b_im),
        jnp.transpose(c_im, (0, 2, 1, 3)).reshape(ndir, grp, ns))
    y = None
    for direction in range(ndir):
        blk = lambda a: jnp.transpose(a[direction].reshape(grp, g, p), (1, 0, 2))
        bd = jnp.concatenate([_block_diag(blk(bb_re)), _block_diag(blk(bb_im))], axis=1).astype(BF16)
        cblk = lambda a: jnp.transpose(a, (0, 2, 1))
        cd = jnp.concatenate([_block_diag(cblk(c_re[direction])), _block_diag(cblk(blk(ncim)))],
                             axis=0).astype(BF16)
        are = jnp.broadcast_to(a_re[direction][None, :], (nb, ns))
        aim = jnp.broadcast_to(a_im[direction][None, :], (nb, ns))
        last = direction == ndir - 1
        if direction == 0:
            extra = (d.reshape(1, -1),)
            if last:
                raise NotImplementedError("S5 branch expects forward and backward directions")
            y = _s5_scan(u_tm, bd, cd, are, aim, extra, reverse=False, final=False, nb=nb, out_dtype=F32)
        else:
            extra = (y, glu_w.astype(BF16), glu_b.reshape(1, -1))
            y = _s5_scan(u_tm, bd, cd, are, aim, extra, reverse=True, final=True, nb=nb, out_dtype=BF16)
    return y


def _dft_matrices(seq):
    n = 2 * seq
    f_lo = 64 if seq % 64 == 0 else 1
    f_hi = seq // f_lo
    t = jnp.arange(seq, dtype=jnp.int32)[None, :]
    ka = (jnp.arange(f_hi, dtype=jnp.int32)[:, None] * f_lo * t) % n
    kb = (jnp.arange(f_lo, dtype=jnp.int32)[:, None] * t) % n
    w = 2.0 * math.pi / n
    aa = ka.astype(F32) * w
    ab = kb.astype(F32) * w
    ca, sa, cb, sb = jnp.cos(aa), jnp.sin(aa), jnp.cos(ab), jnp.sin(ab)
    cm = ca[:, None, :] * cb[None, :, :] - sa[:, None, :] * sb[None, :, :]
    sm = sa[:, None, :] * cb[None, :, :] + ca[:, None, :] * sb[None, :, :]
    return cm.reshape(seq, seq).astype(BF16), sm.reshape(seq, seq).astype(BF16)


def _dft_apply(m_ref, rhs_ref, out_ref, accumulate=False):
    n = m_ref.shape[0]
    rc = min(DFT_ROW_CHUNK, n)

    def body(i, carry):
        r = pl.multiple_of(i * rc, rc)
        v = jnp.dot(m_ref[pl.ds(r, rc), :], rhs_ref[...], preferred_element_type=F32)
        if accumulate:
            out_ref[pl.ds(r, rc), :] += v
        else:
            out_ref[pl.ds(r, rc), :] = v
        return carry

    lax.fori_loop(0, n // rc, body, 0)


def _alt_sign(seq):
    row = lax.broadcasted_iota(jnp.int32, (seq, 1), 0)
    return (1 - 2 * (row & 1)).astype(F32)


def _hy_filter_kernel(feat_ref, w1_ref, b1_ref, w2_ref, b2_ref, freq_ref,
                      w3f_ref, b3f_ref, decf_ref, w3b_ref, b3b_ref, decb_ref, cm_ref, sm_ref,
                      kre_ref, kim_ref, knyq_ref, e_s, d_s, acc_s):
    seq = feat_ref.shape[0]
    n = 2 * seq
    feats = feat_ref[...]
    f = freq_ref[...]
    h = jnp.sin(f * (jnp.dot(feats, w1_ref[...], precision=HIGHEST, preferred_element_type=F32) + b1_ref[...]))
    h = jnp.sin(f * (jnp.dot(h, w2_ref[...], precision=HIGHEST, preferred_element_type=F32) + b2_ref[...]))
    t01 = feats[:, 0:1]
    row = lax.broadcasted_iota(jnp.int32, (seq, 1), 0)

    def taps(w3_ref, b3_ref, dec_ref):
        v = jnp.dot(h, w3_ref[...], precision=HIGHEST, preferred_element_type=F32) + b3_ref[...]
        return v * jnp.exp(-t01 * jnp.abs(dec_ref[...]))

    fwd = taps(w3f_ref, b3f_ref, decf_ref)
    bwd = jnp.where(row == 0, 0.0, taps(w3b_ref, b3b_ref, decb_ref))
    e = fwd + bwd
    e_s[...] = e.astype(BF16)
    d_s[...] = (bwd - fwd).astype(BF16)
    scale = jnp.where(row == 0, 1.0 / n, 2.0 / n)
    _dft_apply(cm_ref, e_s, acc_s)
    kre_ref[...] = acc_s[...] * scale
    _dft_apply(sm_ref, d_s, acc_s)
    kim_ref[...] = acc_s[...] * scale
    nyq = jnp.sum(e * _alt_sign(seq), axis=0, keepdims=True) * (1.0 / n)
    knyq_ref[...] = jnp.broadcast_to(nyq, knyq_ref.shape)


def _hyena_filters(seq, w1, b1, w2, b2, w3, b3, freq, decay, cm, sm, n_order, n_dirs, d_hy):
    emb, hid = w1.shape
    bands = (emb - 1) // 2
    t = jnp.arange(seq, dtype=F32)
    t01 = t / max(seq - 1, 1)
    band = jnp.linspace(1e-4, bands - 1, bands, dtype=F32)
    ang = (2.0 * math.pi) * t[:, None] * band[None, :] / seq
    feats = jnp.concatenate([t01[:, None], jnp.cos(ang), jnp.sin(ang)], axis=-1)
    kpad = LANE
    feats = jnp.pad(feats, ((0, 0), (0, kpad - emb)))
    w1p = jnp.pad(w1, ((0, kpad - emb), (0, 0)))
    ct = min(HY_CH_TILE, d_hy)
    nct = d_hy // ct
    ncol = n_order * d_hy
    b3r = b3.reshape(1, -1)
    decr = decay.reshape(1, -1)
    fcol = lambda o, c: (0, (o * n_dirs + 0) * nct + c)
    bcol = lambda o, c: (0, (o * n_dirs + 1) * nct + c)
    ocol = lambda o, c: (0, o * nct + c)
    full = lambda a: pl.BlockSpec(a.shape, lambda o, c: (0,) * a.ndim)
    return pl.pallas_call(
        _hy_filter_kernel,
        out_shape=(
            jax.ShapeDtypeStruct((seq, ncol), F32),
            jax.ShapeDtypeStruct((seq, ncol), F32),
            jax.ShapeDtypeStruct((8, ncol), F32),
        ),
        grid=(n_order, nct),
        in_specs=[
            full(feats), full(w1p), full(b1.reshape(1, -1)), full(w2), full(b2.reshape(1, -1)),
            full(freq.reshape(1, -1)),
            pl.BlockSpec((hid, ct), fcol), pl.BlockSpec((1, ct), fcol), pl.BlockSpec((1, ct), fcol),
            pl.BlockSpec((hid, ct), bcol), pl.BlockSpec((1, ct), bcol), pl.BlockSpec((1, ct), bcol),
            _const_spec(cm.shape), _const_spec(sm.shape),
        ],
        out_specs=(
            pl.BlockSpec((seq, ct), ocol), pl.BlockSpec((seq, ct), ocol), pl.BlockSpec((8, ct), ocol),
        ),
        scratch_shapes=[pltpu.VMEM((seq, ct), BF16), pltpu.VMEM((seq, ct), BF16), pltpu.VMEM((seq, ct), F32)],
        compiler_params=_cparams(2),
        name="hyena_filter_spectra",
    )(feats, w1p, b1.reshape(1, -1), w2, b2.reshape(1, -1), freq.reshape(1, -1),
      w3, b3r, decr, w3, b3r, decr, cm, sm)


def _hy_conv_kernel(zin_ref, gin_ref, wz_ref, bz_ref, wg_ref, bg_ref, kre_ref, kim_ref, knyq_ref,
                    bias_ref, cm_ref, sm_ref, out_ref, z_s, g_s, zb_s, yre_s, v_s, *, conv_on_z):
    seq = zin_ref.shape[0]
    rc = min(DFT_ROW_CHUNK, seq)
    row = lax.broadcasted_iota(jnp.int32, (seq, 1), 0)

    def short_conv(u, w_ref, b_ref):
        prev = jnp.where(row == 0, 0.0, pltpu.roll(u, 1, 0))
        nxt = jnp.where(row == seq - 1, 0.0, pltpu.roll(u, seq - 1, 0))
        return b_ref[...] + prev * w_ref[0:1, :] + u * w_ref[1:2, :] + nxt * w_ref[2:3, :]

    z = zin_ref[...].astype(F32)
    if conv_on_z:
        z = short_conv(z, wz_ref, bz_ref)
    z_s[...] = z
    zb_s[...] = z.astype(BF16)
    nyq = jnp.sum(z * _alt_sign(seq), axis=0, keepdims=True) * knyq_ref[0:1, :]
    g_s[...] = short_conv(gin_ref[...].astype(F32), wg_ref, bg_ref)

    def fwd(i, carry):
        r = pl.ds(pl.multiple_of(i * rc, rc), rc)
        a_c = jnp.dot(cm_ref[r, :], zb_s[...], preferred_element_type=F32)
        a_s = jnp.dot(sm_ref[r, :], zb_s[...], preferred_element_type=F32)
        kre = kre_ref[r, :]
        kim = kim_ref[r, :]
        yre_s[r, :] = (a_c * kre + a_s * kim).astype(BF16)
        v_s[r, :] = (a_s * kre - a_c * kim).astype(BF16)
        return carry

    lax.fori_loop(0, seq // rc, fwd, 0)

    def inv(i, carry):
        r0 = pl.multiple_of(i * rc, rc)
        r = pl.ds(r0, rc)
        y = jnp.dot(cm_ref[r, :], yre_s[...], preferred_element_type=F32)
        y = y + jnp.dot(sm_ref[r, :], v_s[...], preferred_element_type=F32)
        t = r0 + lax.broadcasted_iota(jnp.int32, (rc, 1), 0)
        y = y + (1 - 2 * (t & 1)).astype(F32) * nyq
        out_ref[r, :] = (g_s[r, :] * (y + z_s[r, :] * bias_ref[...])).astype(out_ref.dtype)
        return carry

    lax.fori_loop(0, seq // rc, inv, 0)


def _hy_conv(zin, zcol0, gcol0, u_hy, conv_w, conv_b, kre, kim, knyq, bias_row, cm, sm,
             *, order, conv_on_z, bsz, seq, d_hy):
    ct = min(HY_CH_TILE, d_hy)
    nct = d_hy // ct
    zc0 = zcol0 // ct
    gc0 = gcol0 // ct
    zw0 = zc0 if conv_on_z else 0
    return pl.pallas_call(
        functools.partial(_hy_conv_kernel, conv_on_z=conv_on_z),
        out_shape=jax.ShapeDtypeStruct((bsz * seq, d_hy), BF16),
        grid=(nct, bsz),
        in_specs=[
            pl.BlockSpec((seq, ct), lambda c, b: (b, zc0 + c)),
            pl.BlockSpec((seq, ct), lambda c, b: (b, gc0 + c)),
            pl.BlockSpec((conv_w.shape[0], ct), lambda c, b: (0, zw0 + c)),
            pl.BlockSpec((1, ct), lambda c, b: (0, zw0 + c)),
            pl.BlockSpec((conv_w.shape[0], ct), lambda c, b: (0, gc0 + c)),
            pl.BlockSpec((1, ct), lambda c, b: (0, gc0 + c)),
            pl.BlockSpec((seq, ct), lambda c, b: (0, order * nct + c)),
            pl.BlockSpec((seq, ct), lambda c, b: (0, order * nct + c)),
            pl.BlockSpec((8, ct), lambda c, b: (0, order * nct + c)),
            pl.BlockSpec((1, ct), lambda c, b: (0, order * nct + c)),
            _const_spec(cm.shape), _const_spec(sm.shape),
        ],
        out_specs=pl.BlockSpec((seq, ct), lambda c, b: (b, c)),
        scratch_shapes=[
            pltpu.VMEM((seq, ct), F32), pltpu.VMEM((seq, ct), F32),
            pltpu.VMEM((seq, ct), BF16), pltpu.VMEM((seq, ct), BF16), pltpu.VMEM((seq, ct), BF16),
        ],
        compiler_params=_cparams(2),
        name=f"hyena_conv_order{order}",
    )(zin, u_hy, conv_w, conv_b, conv_w, conv_b, kre, kim, knyq, bias_row, cm, sm)


def _hyena_branch(u_hy, bsz, seq, conv_w, conv_b, w1, b1, w2, b2, w3, b3, freq, decay, bias):
    n_order, d_hy = bias.shape
    n_dirs = w3.shape[1] // (n_order * d_hy)
    if n_order != 2:
        raise NotImplementedError("Hyena branch is written for two long convolutions")
    cm, sm = _dft_matrices(seq)
    kre, kim, knyq = _hyena_filters(seq, w1, b1, w2, b2, w3, b3, freq, decay, cm, sm, n_order, n_dirs, d_hy)
    cb = conv_b.reshape(1, -1)
    bias_row = bias.reshape(1, -1)
    common = dict(bsz=bsz, seq=seq, d_hy=d_hy)
    z1 = _hy_conv(u_hy, 0, d_hy, u_hy, conv_w, cb, kre, kim, knyq, bias_row, cm, sm,
                  order=0, conv_on_z=True, **common)
    return _hy_conv(z1, 0, 2 * d_hy, u_hy, conv_w, cb, kre, kim, knyq, bias_row, cm, sm,
                    order=1, conv_on_z=False, **common)


def _merge_kernel(x_ref, mod_ref, g_ref, wgate_ref, za_ref, zb_ref, wa_ref, wb_ref, wout_ref, o_ref):
    d = x_ref.shape[1]
    m = mod_ref[0]
    x = x_ref[...]
    h = _norm_modulate(x, g_ref[...], m[0:1, :], m[1:2, :]).astype(BF16)
    gate = jax.nn.sigmoid(jnp.dot(h, wgate_ref[...], preferred_element_type=F32))
    ya = jnp.dot(za_ref[...], wa_ref[...], preferred_element_type=F32)
    yb = jnp.dot(zb_ref[...], wb_ref[...], preferred_element_type=F32)
    merged = gate[:, :d] * ya + gate[:, d:] * yb
    o = jnp.dot(merged.astype(BF16), wout_ref[...], preferred_element_type=F32)
    o_ref[...] = x + m[2:3, :] * o


def _merge(x2, mod3, norm_g, w_gate, za_tm, zb, w_a, w_b, w_out, bsz, seq):
    d = x2.shape[1]
    d_s5 = w_a.shape[0]
    d_hy = w_b.shape[0]
    tm = min(ROW_TILE, seq)
    nt = seq // tm
    return pl.pallas_call(
        _merge_kernel,
        out_shape=jax.ShapeDtypeStruct(x2.shape, F32),
        grid=(bsz, nt),
        in_specs=[
            pl.BlockSpec((tm, d), lambda b, j: (b * nt + j, 0)),
            pl.BlockSpec((1,) + mod3.shape[1:], lambda b, j: (b, 0, 0)),
            pl.BlockSpec((1, d), lambda b, j: (0, 0)),
            _const_spec(w_gate.shape),
            pl.BlockSpec((tm, d_s5), lambda b, j: (j, b)),
            pl.BlockSpec((tm, d_hy), lambda b, j: (b * nt + j, 0)),
            _const_spec(w_a.shape), _const_spec(w_b.shape), _const_spec(w_out.shape),
        ],
        out_specs=pl.BlockSpec((tm, d), lambda b, j: (b * nt + j, 0)),
        compiler_params=_cparams(2),
        name="gated_merge_out_proj",
    )(x2, mod3, norm_g, w_gate, za_tm, zb, w_a, w_b, w_out)


def _ffn_kernel(x_ref, mod_ref, g_ref, wg_ref, wu_ref, wd_ref, fg_ref, o_ref, *, n_chunks):
    m = mod_ref[0]
    x = x_ref[...]
    h = _norm_modulate(x, g_ref[...], m[3:4, :], m[4:5, :]).astype(BF16)
    d_ff = wg_ref.shape[1]
    fc = d_ff // n_chunks
    acc = jnp.zeros(x.shape, F32)
    for c in range(n_chunks):
        sl = slice(c * fc, (c + 1) * fc)
        gl = jnp.dot(h, wg_ref[:, sl], preferred_element_type=F32)
        up = jnp.dot(h, wu_ref[:, sl], preferred_element_type=F32)
        act = (gl * jax.nn.sigmoid(gl) * up).astype(BF16)
        acc = acc + jnp.dot(act, wd_ref[sl, :], preferred_element_type=F32)
    x2 = x + m[5:6, :] * acc
    ms = jnp.mean(x2 * x2, axis=-1, keepdims=True)
    o_ref[...] = x2 * lax.rsqrt(ms + EPS) * fg_ref[...]


def _ffn(x2, mod3, norm_g, w_g, w_u, w_d, final_g, bsz, seq):
    d = x2.shape[1]
    d_ff = w_g.shape[1]
    tm = min(ROW_TILE, seq)
    nt = seq // tm
    n_chunks = 2 if d_ff % (2 * LANE) == 0 else 1
    return pl.pallas_call(
        functools.partial(_ffn_kernel, n_chunks=n_chunks),
        out_shape=jax.ShapeDtypeStruct(x2.shape, F32),
        grid=(bsz, nt),
        in_specs=[
            pl.BlockSpec((tm, d), lambda b, j: (b * nt + j, 0)),
            pl.BlockSpec((1,) + mod3.shape[1:], lambda b, j: (b, 0, 0)),
            pl.BlockSpec((1, d), lambda b, j: (0, 0)),
            _const_spec(w_g.shape), _const_spec(w_u.shape), _const_spec(w_d.shape),
            pl.BlockSpec((1, d), lambda b, j: (0, 0)),
        ],
        out_specs=pl.BlockSpec((tm, d), lambda b, j: (b * nt + j, 0)),
        compiler_params=_cparams(2),
        name="swiglu_final_norm",
    )(x2, mod3, norm_g, w_g, w_u, w_d, final_g)


def kernel(x, c, ada_w, ada_b, norm1_g, norm2_g, w_in, s5_lam_re, s5_lam_im, s5_log_step, s5_b_re, s5_b_im, s5_c_re, s5_c_im, s5_d, s5_glu_w, s5_glu_b, hy_conv_w, hy_conv_b, hy_ffn_w1, hy_ffn_b1, hy_ffn_w2, hy_ffn_b2, hy_ffn_w3, hy_ffn_b3, hy_freq, hy_decay, hy_bias, w_branch_a, w_branch_b, w_out, ffn_w_gu, ffn_w_down, final_g):
    bsz, seq, d = x.shape
    depth = ada_w.shape[0]
    d_s5 = s5_d.shape[-1]
    n_order, d_hy = hy_bias.shape[1:]
    d_uh = d_s5 + (n_order + 1) * d_hy
    d_ff = ffn_w_down.shape[1]
    xs = x.reshape(bsz * seq, d)
    for i in range(depth):
        mod = _modulation(c, ada_w[i], ada_b[i]).reshape(bsz, 6, d)
        w_in_b = w_in[i].astype(BF16)
        u_s5, u_hy = _in_projection(xs, mod, norm1_g[i].reshape(1, d), w_in_b[:, :d_uh], bsz, seq, d_s5)
        z_a = _s5_branch(u_s5.reshape(seq * bsz, d_s5), bsz, s5_lam_re[i], s5_lam_im[i], s5_log_step[i],
                         s5_b_re[i], s5_b_im[i], s5_c_re[i], s5_c_im[i], s5_d[i], s5_glu_w[i], s5_glu_b[i])
        z_b = _hyena_branch(u_hy, bsz, seq, hy_conv_w[i], hy_conv_b[i], hy_ffn_w1[i], hy_ffn_b1[i],
                            hy_ffn_w2[i], hy_ffn_b2[i], hy_ffn_w3[i], hy_ffn_b3[i], hy_freq[i],
                            hy_decay[i], hy_bias[i])
        xs = _merge(xs, mod, norm1_g[i].reshape(1, d), w_in_b[:, d_uh:], z_a.reshape(seq, bsz * d_s5), z_b,
                    w_branch_a[i].astype(BF16), w_branch_b[i].astype(BF16), w_out[i].astype(BF16), bsz, seq)
        w_gu = ffn_w_gu[i].astype(BF16)
        fg = final_g.reshape(1, d) if i == depth - 1 else None
        if fg is None:
            raise NotImplementedError("final RMSNorm is fused into the last layer's channel mixer")
        xs = _ffn(xs, mod, norm2_g[i].reshape(1, d), w_gu[:, :d_ff], w_gu[:, d_ff:],
                  ffn_w_down[i].astype(BF16), fg, bsz, seq)
    return xs.reshape(bsz, seq, d)
```

```python
import functools
import math

import jax
import jax.numpy as jnp
from jax import lax
from jax.experimental import pallas as pl
from jax.experimental.pallas import tpu as pltpu

F32 = jnp.float32
BF16 = jnp.bfloat16
EPS = 1e-6
HIGHEST = lax.Precision.HIGHEST

V7X_VMEM_BYTES = 64 * 1024 * 1024
VMEM_LIMIT_BYTES = 56 * 1024 * 1024
LANE = 128
ROW_TILE = 512
S5_TIME_CHUNK = 64
S5_LANE_CHUNK = 512
S5_CH_SPLIT = 256
HY_CH_TILE = 256
DFT_ROW_CHUNK = 1024
HY_FAST_DECAY_T = 0.3
HY_SLOW_DECAY_T = 1.5


def _cparams(n_axes):
    return pltpu.CompilerParams(
        dimension_semantics=("arbitrary",) * n_axes,
        vmem_limit_bytes=VMEM_LIMIT_BYTES,
    )


def _const_spec(shape):
    nd = len(shape)
    return pl.BlockSpec(shape, lambda *_: (0,) * nd, pipeline_mode=pl.Buffered(1))


def _gelu_tanh(x):
    return 0.5 * x * (1.0 + jnp.tanh(math.sqrt(2.0 / math.pi) * (x + 0.044715 * (x * x * x))))


def _norm_modulate(x, g, shift, scale):
    ms = jnp.mean(x * x, axis=-1, keepdims=True)
    r = x * lax.rsqrt(ms + EPS) * g
    return r * (1.0 + scale) + shift


def _mod_kernel(c_ref, w_ref, b_ref, o_ref):
    c = c_ref[...]
    ca = c * jax.nn.sigmoid(c)
    o_ref[...] = jnp.dot(ca, w_ref[...], precision=HIGHEST, preferred_element_type=F32) + b_ref[...]


def _modulation(c, ada_w, ada_b):
    bsz, d = c.shape
    n = ada_w.shape[1]
    tn = 512
    return pl.pallas_call(
        _mod_kernel,
        out_shape=jax.ShapeDtypeStruct((bsz, n), F32),
        grid=(n // tn,),
        in_specs=[
            pl.BlockSpec((bsz, d), lambda j: (0, 0)),
            pl.BlockSpec((d, tn), lambda j: (0, j)),
            pl.BlockSpec((1, tn), lambda j: (0, j)),
        ],
        out_specs=pl.BlockSpec((bsz, tn), lambda j: (0, j)),
        compiler_params=_cparams(1),
        name="adaln_mod",
    )(c, ada_w, ada_b.reshape(1, n))


def _inproj_kernel(x_ref, mod_ref, g_ref, w_ref, us5_ref, uhy_ref, *, d_s5):
    m = mod_ref[0]
    h = _norm_modulate(x_ref[...], g_ref[...], m[0:1, :], m[1:2, :]).astype(BF16)
    p = jnp.dot(h, w_ref[...], preferred_element_type=F32)
    us5_ref[...] = p[:, :d_s5]
    uhy_ref[...] = p[:, d_s5:].astype(BF16)


def _in_projection(x2, mod3, norm_g, w_uh, bsz, seq, d_s5):
    d = x2.shape[1]
    n = w_uh.shape[1]
    tm = min(ROW_TILE, seq)
    nt = seq // tm
    return pl.pallas_call(
        functools.partial(_inproj_kernel, d_s5=d_s5),
        out_shape=(
            jax.ShapeDtypeStruct((seq, bsz * d_s5), F32),
            jax.ShapeDtypeStruct((bsz * seq, n - d_s5), BF16),
        ),
        grid=(bsz, nt),
        in_specs=[
            pl.BlockSpec((tm, d), lambda b, j: (b * nt + j, 0)),
            pl.BlockSpec((1,) + mod3.shape[1:], lambda b, j: (b, 0, 0)),
            pl.BlockSpec((1, d), lambda b, j: (0, 0)),
            _const_spec((d, n)),
        ],
        out_specs=(
            pl.BlockSpec((tm, d_s5), lambda b, j: (j, b)),
            pl.BlockSpec((tm, n - d_s5), lambda b, j: (b * nt + j, 0)),
        ),
        compiler_params=_cparams(2),
        name="in_proj",
    )(x2, mod3, norm_g, w_uh)


def _s5_prep_kernel(lre_ref, lim_ref, lstep_ref, bre_ref, bim_ref, cim_ref,
                    are_ref, aim_ref, bbre_ref, bbim_ref, ncim_ref):
    step = jnp.exp(lstep_ref[...])
    lr = lre_ref[...]
    li = lim_ref[...]
    mag = jnp.exp(lr * step)
    ar = mag * jnp.cos(li * step)
    ai = mag * jnp.sin(li * step)
    num = ar - 1.0
    den = lr * lr + li * li
    cr = (num * lr + ai * li) / den
    ci = (ai * lr - num * li) / den
    are_ref[...] = ar
    aim_ref[...] = ai
    for d in range(lre_ref.shape[0]):
        br = bre_ref[d]
        bi = bim_ref[d]
        bbre_ref[d] = cr[d:d + 1, :] * br - ci[d:d + 1, :] * bi
        bbim_ref[d] = cr[d:d + 1, :] * bi + ci[d:d + 1, :] * br
    ncim_ref[...] = -cim_ref[...]


def _s5_prepare(lam_re, lam_im, log_step, b_re, b_im, c_im):
    outs = (
        jax.ShapeDtypeStruct(lam_re.shape, F32),
        jax.ShapeDtypeStruct(lam_re.shape, F32),
        jax.ShapeDtypeStruct(b_re.shape, F32),
        jax.ShapeDtypeStruct(b_re.shape, F32),
        jax.ShapeDtypeStruct(c_im.shape, F32),
    )
    return pl.pallas_call(
        _s5_prep_kernel,
        out_shape=outs,
        compiler_params=pltpu.CompilerParams(vmem_limit_bytes=VMEM_LIMIT_BYTES),
        name="s5_discretize",
    )(lam_re, lam_im, log_step, b_re, b_im, c_im)


def _block_diag(m):
    g, r, c = m.shape
    eye = jnp.eye(g, dtype=m.dtype)
    return (m[:, :, None, :] * eye[:, None, :, None]).reshape(g * r, g * c)


def _s5_scan_kernel(*refs, reverse, final, tc, nb):
    if final:
        (u_ref, bd_ref, cd_ref, are_ref, aim_ref, yprev_ref, gluw_ref, glub_ref,
         out_ref, bu_ref, sb_ref, s_ref) = refs
    else:
        (u_ref, bd_ref, cd_ref, are_ref, aim_ref, d_ref, out_ref, bu_ref, sb_ref, s_ref) = refs
    rows = tc * nb
    nsplit, ks, two_nss = bd_ref.shape
    nss = two_nss // 2

    @pl.when(pl.program_id(0) == 0)
    def _():
        s_ref[...] = jnp.zeros_like(s_ref)

    for h in range(nsplit):
        ub = u_ref[:, h * ks:(h + 1) * ks].astype(BF16)
        bu_ref[:, h * two_nss:(h + 1) * two_nss] = jnp.dot(ub, bd_ref[h], preferred_element_type=F32)

    lc = min(S5_LANE_CHUNK, nss)
    for h in range(nsplit):
        for c in range(nss // lc):
            a_sl = slice(h * nss + c * lc, h * nss + (c + 1) * lc)
            re_sl = slice(h * two_nss + c * lc, h * two_nss + (c + 1) * lc)
            im_sl = slice(h * two_nss + nss + c * lc, h * two_nss + nss + (c + 1) * lc)
            a_re = are_ref[:, a_sl]
            a_im = aim_ref[:, a_sl]

            def body(k, carry, re_sl=re_sl, im_sl=im_sl, a_re=a_re, a_im=a_im):
                sr, si = carry
                t = (tc - 1 - k) if reverse else k
                row = pl.ds(pl.multiple_of(t * nb, nb), nb)
                nr = a_re * sr - a_im * si + bu_ref[row, re_sl]
                ni = a_re * si + a_im * sr + bu_ref[row, im_sl]
                sb_ref[row, re_sl] = nr.astype(BF16)
                sb_ref[row, im_sl] = ni.astype(BF16)
                return nr, ni

            sr, si = lax.fori_loop(0, tc, body, (s_ref[:, re_sl], s_ref[:, im_sl]), unroll=8)
            s_ref[:, re_sl] = sr
            s_ref[:, im_sl] = si

    rb = min(256, rows)
    for h in range(nsplit):
        y = jnp.dot(sb_ref[:, h * two_nss:(h + 1) * two_nss], cd_ref[h], preferred_element_type=F32)
        if final:
            bu_ref[:, h * ks:(h + 1) * ks] = y
        else:
            out_ref[:, h * ks:(h + 1) * ks] = y + u_ref[:, h * ks:(h + 1) * ks] * d_ref[:, h * ks:(h + 1) * ks]
    if final:
        d_s5 = nsplit * ks
        for r in range(rows // rb):
            rs = slice(r * rb, (r + 1) * rb)
            z = _gelu_tanh(yprev_ref[rs, :] + bu_ref[rs, :d_s5])
            gate = jnp.dot(z.astype(BF16), gluw_ref[...], preferred_element_type=F32) + glub_ref[...]
            out_ref[rs, :] = (z * jax.nn.sigmoid(gate)).astype(out_ref.dtype)


def _s5_scan(u_tm, bd, cd, a_re, a_im, extra, *, reverse, final, nb, out_dtype):
    rows_total, d_s5 = u_tm.shape
    seq = rows_total // nb
    two_ns = bd.shape[0] * bd.shape[2]
    tc = min(S5_TIME_CHUNK, seq)
    nchunk = seq // tc
    rows = tc * nb
    if reverse:
        cidx = lambda i: (nchunk - 1 - i, 0)
    else:
        cidx = lambda i: (i, 0)
    in_specs = [
        pl.BlockSpec((rows, d_s5), cidx),
        _const_spec(bd.shape),
        _const_spec(cd.shape),
        _const_spec(a_re.shape),
        _const_spec(a_im.shape),
    ]
    if final:
        yprev, glu_w, glu_b = extra
        in_specs += [pl.BlockSpec((rows, d_s5), cidx), _const_spec(glu_w.shape), _const_spec(glu_b.shape)]
    else:
        in_specs += [_const_spec(extra[0].shape)]
    return pl.pallas_call(
        functools.partial(_s5_scan_kernel, reverse=reverse, final=final, tc=tc, nb=nb),
        out_shape=jax.ShapeDtypeStruct((rows_total, d_s5), out_dtype),
        grid=(nchunk,),
        in_specs=in_specs,
        out_specs=pl.BlockSpec((rows, d_s5), cidx),
        scratch_shapes=[pltpu.VMEM((rows, two_ns), F32), pltpu.VMEM((rows, two_ns), BF16),
                        pltpu.VMEM((nb, two_ns), F32)],
        compiler_params=_cparams(1),
        name="s5_scan_bwd_glu" if final else "s5_scan_fwd",
    )(u_tm, bd, cd, a_re, a_im, *extra)


def _s5_branch(u_tm, nb, lam_re, lam_im, log_step, b_re, b_im, c_re, c_im, d, glu_w, glu_b):
    ndir, g, p = lam_re.shape
    grp = b_re.shape[-1]
    ns = g * p
    d_s5 = g * grp
    ks = S5_CH_SPLIT if d_s5 % S5_CH_SPLIT == 0 else d_s5
    nsplit = d_s5 // ks
    gs = g // nsplit
    flat = lambda a: a.reshape(ndir, ns)
    to_lanes = lambda a: jnp.transpose(a, (0, 3, 1, 2)).reshape(ndir, grp, ns)
    a_re, a_im, bb_re, bb_im, ncim = _s5_prepare(
        flat(lam_re), flat(lam_im), jnp.repeat(log_step, p, axis=-1),
        to_lanes(b_re), to_lanes(b_im),
        jnp.transpose(c_im, (0, 2, 1, 3)).reshape(ndir, grp, ns))

    def in_blocks(a):
        a = jnp.transpose(a.reshape(grp, nsplit, gs, p), (1, 2, 0, 3))
        return jnp.stack([_block_diag(a[h]) for h in range(nsplit)])

    def out_blocks(a):
        a = jnp.transpose(a.reshape(nsplit, gs, grp, p), (0, 1, 3, 2))
        return jnp.stack([_block_diag(a[h]) for h in range(nsplit)])

    y = None
    for direction in range(ndir):
        bd = jnp.concatenate([in_blocks(bb_re[direction]), in_blocks(bb_im[direction])], axis=2).astype(BF16)
        ncim_g = jnp.transpose(ncim[direction].reshape(grp, g, p), (1, 0, 2))
        cd = jnp.concatenate([out_blocks(c_re[direction]), out_blocks(ncim_g)], axis=1).astype(BF16)
        are = jnp.broadcast_to(a_re[direction][None, :], (nb, ns))
        aim = jnp.broadcast_to(a_im[direction][None, :], (nb, ns))
        if direction == 0:
            if ndir != 2:
                raise NotImplementedError("S5 branch expects forward and backward directions")
            y = _s5_scan(u_tm, bd, cd, are, aim, (d.reshape(1, -1),), reverse=False, final=False,
                         nb=nb, out_dtype=F32)
        else:
            extra = (y, glu_w.astype(BF16), glu_b.reshape(1, -1))
            y = _s5_scan(u_tm, bd, cd, are, aim, extra, reverse=True, final=True, nb=nb, out_dtype=BF16)
    return y


def _dft_matrices(seq):
    n = 2 * seq
    f_lo = 64 if seq % 64 == 0 else 1
    f_hi = seq // f_lo
    t = jnp.arange(seq, dtype=jnp.int32)[None, :]
    ka = (jnp.arange(f_hi, dtype=jnp.int32)[:, None] * f_lo * t) % n
    kb = (jnp.arange(f_lo, dtype=jnp.int32)[:, None] * t) % n
    w = 2.0 * math.pi / n
    aa = ka.astype(F32) * w
    ab = kb.astype(F32) * w
    ca, sa, cb, sb = jnp.cos(aa), jnp.sin(aa), jnp.cos(ab), jnp.sin(ab)
    cm = ca[:, None, :] * cb[None, :, :] - sa[:, None, :] * sb[None, :, :]
    sm = sa[:, None, :] * cb[None, :, :] + ca[:, None, :] * sb[None, :, :]
    return cm.reshape(seq, seq).astype(BF16), sm.reshape(seq, seq).astype(BF16)


def _dft_apply(m_ref, rhs_ref, out_ref, accumulate=False):
    n = m_ref.shape[0]
    rc = min(DFT_ROW_CHUNK, n)

    def body(i, carry):
        r = pl.multiple_of(i * rc, rc)
        v = jnp.dot(m_ref[pl.ds(r, rc), :], rhs_ref[...], preferred_element_type=F32)
        if accumulate:
            out_ref[pl.ds(r, rc), :] += v
        else:
            out_ref[pl.ds(r, rc), :] = v
        return carry

    lax.fori_loop(0, n // rc, body, 0)


def _alt_sign(seq):
    row = lax.broadcasted_iota(jnp.int32, (seq, 1), 0)
    return (1 - 2 * (row & 1)).astype(F32)


def _hy_filter_kernel(feat_ref, w1_ref, b1_ref, w2_ref, b2_ref, freq_ref,
                      w3f_ref, b3f_ref, decf_ref, w3b_ref, b3b_ref, decb_ref, cm_ref, sm_ref,
                      kre_ref, kim_ref, knyq_ref, e_s, d_s, acc_s):
    seq = feat_ref.shape[0]
    n = 2 * seq
    feats = feat_ref[...]
    f = freq_ref[...]
    h = jnp.sin(f * (jnp.dot(feats, w1_ref[...], precision=HIGHEST, preferred_element_type=F32) + b1_ref[...]))
    h = jnp.sin(f * (jnp.dot(h, w2_ref[...], precision=HIGHEST, preferred_element_type=F32) + b2_ref[...]))
    t01 = feats[:, 0:1]
    row = lax.broadcasted_iota(jnp.int32, (seq, 1), 0)

    def taps(w3_ref, b3_ref, dec_ref):
        v = jnp.dot(h, w3_ref[...], precision=HIGHEST, preferred_element_type=F32) + b3_ref[...]
        return v * jnp.exp(-t01 * jnp.abs(dec_ref[...]))

    fwd = taps(w3f_ref, b3f_ref, decf_ref)
    bwd = jnp.where(row == 0, 0.0, taps(w3b_ref, b3b_ref, decb_ref))
    e = fwd + bwd
    e_s[...] = e.astype(BF16)
    d_s[...] = (bwd - fwd).astype(BF16)
    scale = jnp.where(row == 0, 1.0 / n, 2.0 / n)
    _dft_apply(cm_ref, e_s, acc_s)
    kre_ref[...] = acc_s[...] * scale
    _dft_apply(sm_ref, d_s, acc_s)
    kim_ref[...] = acc_s[...] * scale
    nyq = jnp.sum(e * _alt_sign(seq), axis=0, keepdims=True) * (1.0 / n)
    knyq_ref[...] = jnp.broadcast_to(nyq, knyq_ref.shape)


def _hyena_filters(seq, w1, b1, w2, b2, w3, b3, freq, decay, cm, sm, n_order, n_dirs, d_hy):
    emb, hid = w1.shape
    bands = (emb - 1) // 2
    t = jnp.arange(seq, dtype=F32)
    t01 = t / max(seq - 1, 1)
    band = jnp.linspace(1e-4, bands - 1, bands, dtype=F32)
    ang = (2.0 * math.pi) * t[:, None] * band[None, :] / seq
    feats = jnp.concatenate([t01[:, None], jnp.cos(ang), jnp.sin(ang)], axis=-1)
    kpad = LANE
    feats = jnp.pad(feats, ((0, 0), (0, kpad - emb)))
    w1p = jnp.pad(w1, ((0, kpad - emb), (0, 0)))
    ct = min(HY_CH_TILE, d_hy)
    nct = d_hy // ct
    ncol = n_order * d_hy
    b3r = b3.reshape(1, -1)
    decr = decay.reshape(1, -1)
    fcol = lambda o, c: (0, (o * n_dirs + 0) * nct + c)
    bcol = lambda o, c: (0, (o * n_dirs + 1) * nct + c)
    ocol = lambda o, c: (0, o * nct + c)
    full = lambda a: pl.BlockSpec(a.shape, lambda o, c: (0,) * a.ndim)
    return pl.pallas_call(
        _hy_filter_kernel,
        out_shape=(
            jax.ShapeDtypeStruct((seq, ncol), F32),
            jax.ShapeDtypeStruct((seq, ncol), F32),
            jax.ShapeDtypeStruct((8, ncol), F32),
        ),
        grid=(n_order, nct),
        in_specs=[
            full(feats), full(w1p), full(b1.reshape(1, -1)), full(w2), full(b2.reshape(1, -1)),
            full(freq.reshape(1, -1)),
            pl.BlockSpec((hid, ct), fcol), pl.BlockSpec((1, ct), fcol), pl.BlockSpec((1, ct), fcol),
            pl.BlockSpec((hid, ct), bcol), pl.BlockSpec((1, ct), bcol), pl.BlockSpec((1, ct), bcol),
            _const_spec(cm.shape), _const_spec(sm.shape),
        ],
        out_specs=(
            pl.BlockSpec((seq, ct), ocol), pl.BlockSpec((seq, ct), ocol), pl.BlockSpec((8, ct), ocol),
        ),
        scratch_shapes=[pltpu.VMEM((seq, ct), BF16), pltpu.VMEM((seq, ct), BF16), pltpu.VMEM((seq, ct), F32)],
        compiler_params=_cparams(2),
        name="hyena_filter_spectra",
    )(feats, w1p, b1.reshape(1, -1), w2, b2.reshape(1, -1), freq.reshape(1, -1),
      w3, b3r, decr, w3, b3r, decr, cm, sm)


def _hy_conv_kernel(zin_ref, gin_ref, wz_ref, bz_ref, wg_ref, bg_ref, kre_ref, kim_ref, knyq_ref,
                    bias_ref, cm_ref, sm_ref, out_ref, z_s, g_s, zb_s, yre_s, v_s, *, conv_on_z):
    seq = zin_ref.shape[0]
    rc = min(DFT_ROW_CHUNK, seq)
    row = lax.broadcasted_iota(jnp.int32, (seq, 1), 0)

    def short_conv(u, w_ref, b_ref):
        prev = jnp.where(row == 0, 0.0, pltpu.roll(u, 1, 0))
        nxt = jnp.where(row == seq - 1, 0.0, pltpu.roll(u, seq - 1, 0))
        return b_ref[...] + prev * w_ref[0:1, :] + u * w_ref[1:2, :] + nxt * w_ref[2:3, :]

    z = zin_ref[...].astype(F32)
    if conv_on_z:
        z = short_conv(z, wz_ref, bz_ref)
    z_s[...] = z
    zb_s[...] = z.astype(BF16)
    nyq = jnp.sum(z * _alt_sign(seq), axis=0, keepdims=True) * knyq_ref[0:1, :]
    g_s[...] = short_conv(gin_ref[...].astype(F32), wg_ref, bg_ref)

    def fwd(i, carry):
        r = pl.ds(pl.multiple_of(i * rc, rc), rc)
        a_c = jnp.dot(cm_ref[r, :], zb_s[...], preferred_element_type=F32)
        a_s = jnp.dot(sm_ref[r, :], zb_s[...], preferred_element_type=F32)
        kre = kre_ref[r, :]
        kim = kim_ref[r, :]
        yre_s[r, :] = (a_c * kre + a_s * kim).astype(BF16)
        v_s[r, :] = (a_s * kre - a_c * kim).astype(BF16)
        return carry

    lax.fori_loop(0, seq // rc, fwd, 0)

    def inv(i, carry):
        r0 = pl.multiple_of(i * rc, rc)
        r = pl.ds(r0, rc)
        y = jnp.dot(cm_ref[r, :], yre_s[...], preferred_element_type=F32)
        y = y + jnp.dot(sm_ref[r, :], v_s[...], preferred_element_type=F32)
        t = r0 + lax.broadcasted_iota(jnp.int32, (rc, 1), 0)
        y = y + (1 - 2 * (t & 1)).astype(F32) * nyq
        out_ref[r, :] = (g_s[r, :] * (y + z_s[r, :] * bias_ref[...])).astype(out_ref.dtype)
        return carry

    lax.fori_loop(0, seq // rc, inv, 0)


def _hy_conv(zin, zcol0, gcol0, u_hy, conv_w, conv_b, kre, kim, knyq, bias_row, cm, sm,
             *, order, conv_on_z, bsz, seq, d_hy):
    ct = min(HY_CH_TILE, d_hy)
    nct = d_hy // ct
    zc0 = zcol0 // ct
    gc0 = gcol0 // ct
    zw0 = zc0 if conv_on_z else 0
    return pl.pallas_call(
        functools.partial(_hy_conv_kernel, conv_on_z=conv_on_z),
        out_shape=jax.ShapeDtypeStruct((bsz * seq, d_hy), BF16),
        grid=(nct, bsz),
        in_specs=[
            pl.BlockSpec((seq, ct), lambda c, b: (b, zc0 + c)),
            pl.BlockSpec((seq, ct), lambda c, b: (b, gc0 + c)),
            pl.BlockSpec((conv_w.shape[0], ct), lambda c, b: (0, zw0 + c)),
            pl.BlockSpec((1, ct), lambda c, b: (0, zw0 + c)),
            pl.BlockSpec((conv_w.shape[0], ct), lambda c, b: (0, gc0 + c)),
            pl.BlockSpec((1, ct), lambda c, b: (0, gc0 + c)),
            pl.BlockSpec((seq, ct), lambda c, b: (0, order * nct + c)),
            pl.BlockSpec((seq, ct), lambda c, b: (0, order * nct + c)),
            pl.BlockSpec((8, ct), lambda c, b: (0, order * nct + c)),
            pl.BlockSpec((1, ct), lambda c, b: (0, order * nct + c)),
            _const_spec(cm.shape), _const_spec(sm.shape),
        ],
        out_specs=pl.BlockSpec((seq, ct), lambda c, b: (b, c)),
        scratch_shapes=[
            pltpu.VMEM((seq, ct), F32), pltpu.VMEM((seq, ct), F32),
            pltpu.VMEM((seq, ct), BF16), pltpu.VMEM((seq, ct), BF16), pltpu.VMEM((seq, ct), BF16),
        ],
        compiler_params=_cparams(2),
        name=f"hyena_conv_order{order}",
    )(zin, u_hy, conv_w, conv_b, conv_w, conv_b, kre, kim, knyq, bias_row, cm, sm)


def _hyena_branch(u_hy, bsz, seq, conv_w, conv_b, w1, b1, w2, b2, w3, b3, freq, decay, bias):
    n_order, d_hy = bias.shape
    n_dirs = w3.shape[1] // (n_order * d_hy)
    if n_order != 2:
        raise NotImplementedError("Hyena branch is written for two long convolutions")
    cm, sm = _dft_matrices(seq)
    kre, kim, knyq = _hyena_filters(seq, w1, b1, w2, b2, w3, b3, freq, decay, cm, sm, n_order, n_dirs, d_hy)
    cb = conv_b.reshape(1, -1)
    bias_row = bias.reshape(1, -1)
    common = dict(bsz=bsz, seq=seq, d_hy=d_hy)
    z1 = _hy_conv(u_hy, 0, d_hy, u_hy, conv_w, cb, kre, kim, knyq, bias_row, cm, sm,
                  order=0, conv_on_z=True, **common)
    return _hy_conv(z1, 0, 2 * d_hy, u_hy, conv_w, cb, kre, kim, knyq, bias_row, cm, sm,
                    order=1, conv_on_z=False, **common)


def _merge_kernel(x_ref, mod_ref, g_ref, wgate_ref, za_ref, zb_ref, wa_ref, wb_ref, wout_ref, o_ref):
    d = x_ref.shape[1]
    m = mod_ref[0]
    x = x_ref[...]
    h = _norm_modulate(x, g_ref[...], m[0:1, :], m[1:2, :]).astype(BF16)
    gate = jax.nn.sigmoid(jnp.dot(h, wgate_ref[...], preferred_element_type=F32))
    ya = jnp.dot(za_ref[...], wa_ref[...], preferred_element_type=F32)
    yb = jnp.dot(zb_ref[...], wb_ref[...], preferred_element_type=F32)
    merged = gate[:, :d] * ya + gate[:, d:] * yb
    o = jnp.dot(merged.astype(BF16), wout_ref[...], preferred_element_type=F32)
    o_ref[...] = x + m[2:3, :] * o


def _merge(x2, mod3, norm_g, w_gate, za_tm, zb, w_a, w_b, w_out, bsz, seq):
    d = x2.shape[1]
    d_s5 = w_a.shape[0]
    d_hy = w_b.shape[0]
    tm = min(ROW_TILE, seq)
    nt = seq // tm
    return pl.pallas_call(
        _merge_kernel,
        out_shape=jax.ShapeDtypeStruct(x2.shape, F32),
        grid=(bsz, nt),
        in_specs=[
            pl.BlockSpec((tm, d), lambda b, j: (b * nt + j, 0)),
            pl.BlockSpec((1,) + mod3.shape[1:], lambda b, j: (b, 0, 0)),
            pl.BlockSpec((1, d), lambda b, j: (0, 0)),
            _const_spec(w_gate.shape),
            pl.BlockSpec((tm, d_s5), lambda b, j: (j, b)),
            pl.BlockSpec((tm, d_hy), lambda b, j: (b * nt + j, 0)),
            _const_spec(w_a.shape), _const_spec(w_b.shape), _const_spec(w_out.shape),
        ],
        out_specs=pl.BlockSpec((tm, d), lambda b, j: (b * nt + j, 0)),
        compiler_params=_cparams(2),
        name="gated_merge_out_proj",
    )(x2, mod3, norm_g, w_gate, za_tm, zb, w_a, w_b, w_out)


def _ffn_kernel(x_ref, mod_ref, g_ref, wg_ref, wu_ref, wd_ref, fg_ref, o_ref, *, n_chunks):
    m = mod_ref[0]
    x = x_ref[...]
    h = _norm_modulate(x, g_ref[...], m[3:4, :], m[4:5, :]).astype(BF16)
    d_ff = wg_ref.shape[1]
    fc = d_ff // n_chunks
    acc = jnp.zeros(x.shape, F32)
    for c in range(n_chunks):
        sl = slice(c * fc, (c + 1) * fc)
        gl = jnp.dot(h, wg_ref[:, sl], preferred_element_type=F32)
        up = jnp.dot(h, wu_ref[:, sl], preferred_element_type=F32)
        act = (gl * jax.nn.sigmoid(gl) * up).astype(BF16)
        acc = acc + jnp.dot(act, wd_ref[sl, :], preferred_element_type=F32)
    x2 = x + m[5:6, :] * acc
    ms = jnp.mean(x2 * x2, axis=-1, keepdims=True)
    o_ref[...] = x2 * lax.rsqrt(ms + EPS) * fg_ref[...]


def _ffn(x2, mod3, norm_g, w_g, w_u, w_d, final_g, bsz, seq):
    d = x2.shape[1]
    d_ff = w_g.shape[1]
    tm = min(ROW_TILE, seq)
    nt = seq // tm
    n_chunks = 2 if d_ff % (2 * LANE) == 0 else 1
    return pl.pallas_call(
        functools.partial(_ffn_kernel, n_chunks=n_chunks),
        out_shape=jax.ShapeDtypeStruct(x2.shape, F32),
        grid=(bsz, nt),
        in_specs=[
            pl.BlockSpec((tm, d), lambda b, j: (b * nt + j, 0)),
            pl.BlockSpec((1,) + mod3.shape[1:], lambda b, j: (b, 0, 0)),
            pl.BlockSpec((1, d), lambda b, j: (0, 0)),
            _const_spec(w_g.shape), _const_spec(w_u.shape), _const_spec(w_d.shape),
            pl.BlockSpec((1, d), lambda b, j: (0, 0)),
        ],
        out_specs=pl.BlockSpec((tm, d), lambda b, j: (b * nt + j, 0)),
        compiler_params=_cparams(2),
        name="swiglu_final_norm",
    )(x2, mod3, norm_g, w_g, w_u, w_d, final_g)


def kernel(x, c, ada_w, ada_b, norm1_g, norm2_g, w_in, s5_lam_re, s5_lam_im, s5_log_step, s5_b_re, s5_b_im, s5_c_re, s5_c_im, s5_d, s5_glu_w, s5_glu_b, hy_conv_w, hy_conv_b, hy_ffn_w1, hy_ffn_b1, hy_ffn_w2, hy_ffn_b2, hy_ffn_w3, hy_ffn_b3, hy_freq, hy_decay, hy_bias, w_branch_a, w_branch_b, w_out, ffn_w_gu, ffn_w_down, final_g):
    bsz, seq, d = x.shape
    depth = ada_w.shape[0]
    d_s5 = s5_d.shape[-1]
    n_order, d_hy = hy_bias.shape[1:]
    d_uh = d_s5 + (n_order + 1) * d_hy
    d_ff = ffn_w_down.shape[1]
    xs = x.reshape(bsz * seq, d)
    for i in range(depth):
        mod = _modulation(c, ada_w[i], ada_b[i]).reshape(bsz, 6, d)
        w_in_b = w_in[i].astype(BF16)
        u_s5, u_hy = _in_projection(xs, mod, norm1_g[i].reshape(1, d), w_in_b[:, :d_uh], bsz, seq, d_s5)
        z_a = _s5_branch(u_s5.reshape(seq * bsz, d_s5), bsz, s5_lam_re[i], s5_lam_im[i], s5_log_step[i],
                         s5_b_re[i], s5_b_im[i], s5_c_re[i], s5_c_im[i], s5_d[i], s5_glu_w[i], s5_glu_b[i])
        z_b = _hyena_branch(u_hy, bsz, seq, hy_conv_w[i], hy_conv_b[i], hy_ffn_w1[i], hy_ffn_b1[i],
                            hy_ffn_w2[i], hy_ffn_b2[i], hy_ffn_w3[i], hy_ffn_b3[i], hy_freq[i],
                            hy_decay[i], hy_bias[i])
        xs = _merge(xs, mod, norm1_g[i].reshape(1, d), w_in_b[:, d_uh:], z_a.reshape(seq, bsz * d_s5), z_b,
                    w_branch_a[i].astype(BF16), w_branch_b[i].astype(BF16), w_out[i].astype(BF16), bsz, seq)
        w_gu = ffn_w_gu[i].astype(BF16)
        fg = final_g.reshape(1, d) if i == depth - 1 else None
        if fg is None:
            raise NotImplementedError("final RMSNorm is fused into the last layer's channel mixer")
        xs = _ffn(xs, mod, norm2_g[i].reshape(1, d), w_gu[:, :d_ff], w_gu[:, d_ff:],
                  ffn_w_down[i].astype(BF16), fg, bsz, seq)
    return xs.reshape(bsz, seq, d)
```

```python
import functools
import math

import jax
import jax.numpy as jnp
from jax import lax
from jax.experimental import pallas as pl
from jax.experimental.pallas import tpu as pltpu

F32 = jnp.float32
BF16 = jnp.bfloat16
EPS = 1e-6
HIGHEST = lax.Precision.HIGHEST

V7X_VMEM_BYTES = 64 * 1024 * 1024
VMEM_LIMIT_BYTES = 56 * 1024 * 1024
LANE = 128
ROW_TILE = 512
S5_TIME_CHUNK = 64
S5_LANE_CHUNK = 512
S5_CH_SPLIT = 256
HY_CH_TILE = 256
DFT_ROW_CHUNK = 1024
HY_FAST_DECAY_T = 0.3
HY_SLOW_DECAY_T = 1.5


def _cparams(n_axes):
    return pltpu.CompilerParams(
        dimension_semantics=("arbitrary",) * n_axes,
        vmem_limit_bytes=VMEM_LIMIT_BYTES,
    )


def _const_spec(shape):
    nd = len(shape)
    return pl.BlockSpec(shape, lambda *_: (0,) * nd, pipeline_mode=pl.Buffered(1))


def _gelu_tanh(x):
    return 0.5 * x * (1.0 + jnp.tanh(math.sqrt(2.0 / math.pi) * (x + 0.044715 * (x * x * x))))


def _norm_modulate(x, g, shift, scale):
    ms = jnp.mean(x * x, axis=-1, keepdims=True)
    r = x * lax.rsqrt(ms + EPS) * g
    return r * (1.0 + scale) + shift


def _mod_kernel(c_ref, w_ref, b_ref, o_ref):
    c = c_ref[...]
    ca = c * jax.nn.sigmoid(c)
    o_ref[...] = jnp.dot(ca, w_ref[...], precision=HIGHEST, preferred_element_type=F32) + b_ref[...]


def _modulation(c, ada_w, ada_b):
    bsz, d = c.shape
    n = ada_w.shape[1]
    tn = 512
    return pl.pallas_call(
        _mod_kernel,
        out_shape=jax.ShapeDtypeStruct((bsz, n), F32),
        grid=(n // tn,),
        in_specs=[
            pl.BlockSpec((bsz, d), lambda j: (0, 0)),
            pl.BlockSpec((d, tn), lambda j: (0, j)),
            pl.BlockSpec((1, tn), lambda j: (0, j)),
        ],
        out_specs=pl.BlockSpec((bsz, tn), lambda j: (0, j)),
        compiler_params=_cparams(1),
        name="adaln_mod",
    )(c, ada_w, ada_b.reshape(1, n))


def _time_tile(bsz, seq):
    return max(min(ROW_TILE // bsz, seq), 1)


def _row_permutation(bsz, tt):
    r = jnp.arange(bsz * tt, dtype=jnp.int32)
    src = (r % bsz) * tt + r // bsz
    return (src[:, None] == r[None, :]).astype(BF16)


def _inproj_kernel(x_ref, mod_ref, g_ref, w_ref, perm_ref, us5_ref, uhy_ref, *, d_s5):
    nb, tt, d = x_ref.shape
    h = _norm_modulate(x_ref[...], g_ref[...], mod_ref[:, 0:1, :], mod_ref[:, 1:2, :])
    hb = h.reshape(nb * tt, d).astype(BF16)
    p = jnp.dot(hb, w_ref[...], preferred_element_type=F32)
    us5_ref[...] = jnp.dot(perm_ref[...], p[:, :d_s5].astype(BF16),
                           preferred_element_type=F32).astype(us5_ref.dtype)
    uhy_ref[...] = p[:, d_s5:].astype(BF16).reshape(uhy_ref.shape)


def _in_projection(x, mod3, norm_g, w_uh, perm, d_s5):
    bsz, seq, d = x.shape
    n = w_uh.shape[1]
    tt = _time_tile(bsz, seq)
    return pl.pallas_call(
        functools.partial(_inproj_kernel, d_s5=d_s5),
        out_shape=(
            jax.ShapeDtypeStruct((seq * bsz, d_s5), BF16),
            jax.ShapeDtypeStruct((bsz, seq, n - d_s5), BF16),
        ),
        grid=(seq // tt,),
        in_specs=[
            pl.BlockSpec((bsz, tt, d), lambda j: (0, j, 0)),
            _const_spec(mod3.shape),
            _const_spec((1, d)),
            _const_spec((d, n)),
            _const_spec(perm.shape),
        ],
        out_specs=(
            pl.BlockSpec((tt * bsz, d_s5), lambda j: (j, 0)),
            pl.BlockSpec((bsz, tt, n - d_s5), lambda j: (0, j, 0)),
        ),
        compiler_params=_cparams(1),
        name="in_proj",
    )(x, mod3, norm_g, w_uh, perm)


def _s5_prep_kernel(lre_ref, lim_ref, lstep_ref, bre_ref, bim_ref, cim_ref,
                    are_ref, aim_ref, bbre_ref, bbim_ref, ncim_ref):
    step = jnp.exp(lstep_ref[...])
    lr = lre_ref[...]
    li = lim_ref[...]
    mag = jnp.exp(lr * step)
    ar = mag * jnp.cos(li * step)
    ai = mag * jnp.sin(li * step)
    num = ar - 1.0
    den = lr * lr + li * li
    cr = (num * lr + ai * li) / den
    ci = (ai * lr - num * li) / den
    are_ref[...] = ar
    aim_ref[...] = ai
    for d in range(lre_ref.shape[0]):
        br = bre_ref[d]
        bi = bim_ref[d]
        bbre_ref[d] = cr[d:d + 1, :] * br - ci[d:d + 1, :] * bi
        bbim_ref[d] = cr[d:d + 1, :] * bi + ci[d:d + 1, :] * br
    ncim_ref[...] = -cim_ref[...]


def _s5_prepare(lam_re, lam_im, log_step, b_re, b_im, c_im):
    outs = (
        jax.ShapeDtypeStruct(lam_re.shape, F32),
        jax.ShapeDtypeStruct(lam_re.shape, F32),
        jax.ShapeDtypeStruct(b_re.shape, F32),
        jax.ShapeDtypeStruct(b_re.shape, F32),
        jax.ShapeDtypeStruct(c_im.shape, F32),
    )
    return pl.pallas_call(
        _s5_prep_kernel,
        out_shape=outs,
        compiler_params=pltpu.CompilerParams(vmem_limit_bytes=VMEM_LIMIT_BYTES),
        name="s5_discretize",
    )(lam_re, lam_im, log_step, b_re, b_im, c_im)


def _block_diag(m):
    g, r, c = m.shape
    eye = jnp.eye(g, dtype=m.dtype)
    return (m[:, :, None, :] * eye[:, None, :, None]).reshape(g * r, g * c)


def _s5_scan_kernel(*refs, reverse, final, tc, nb):
    if final:
        (u_ref, bd_ref, cd_ref, are_ref, aim_ref, yprev_ref, gluw_ref, glub_ref,
         out_ref, bu_ref, sb_ref, s_ref) = refs
    else:
        (u_ref, bd_ref, cd_ref, are_ref, aim_ref, d_ref, out_ref, bu_ref, sb_ref, s_ref) = refs
    rows = tc * nb
    nsplit, ks, two_nss = bd_ref.shape
    nss = two_nss // 2

    @pl.when(pl.program_id(0) == 0)
    def _():
        s_ref[...] = jnp.zeros_like(s_ref)

    for h in range(nsplit):
        ub = u_ref[:, h * ks:(h + 1) * ks].astype(BF16)
        bu_ref[:, h * two_nss:(h + 1) * two_nss] = jnp.dot(ub, bd_ref[h], preferred_element_type=F32)

    lc = min(S5_LANE_CHUNK, nss)
    for h in range(nsplit):
        for c in range(nss // lc):
            a_sl = slice(h * nss + c * lc, h * nss + (c + 1) * lc)
            re_sl = slice(h * two_nss + c * lc, h * two_nss + (c + 1) * lc)
            im_sl = slice(h * two_nss + nss + c * lc, h * two_nss + nss + (c + 1) * lc)
            a_re = are_ref[:, a_sl]
            a_im = aim_ref[:, a_sl]

            def body(k, carry, re_sl=re_sl, im_sl=im_sl, a_re=a_re, a_im=a_im):
                sr, si = carry
                t = (tc - 1 - k) if reverse else k
                row = pl.ds(pl.multiple_of(t * nb, nb), nb)
                nr = a_re * sr - a_im * si + bu_ref[row, re_sl]
                ni = a_re * si + a_im * sr + bu_ref[row, im_sl]
                sb_ref[row, re_sl] = nr.astype(BF16)
                sb_ref[row, im_sl] = ni.astype(BF16)
                return nr, ni

            sr, si = lax.fori_loop(0, tc, body, (s_ref[:, re_sl], s_ref[:, im_sl]), unroll=8)
            s_ref[:, re_sl] = sr
            s_ref[:, im_sl] = si

    rb = min(256, rows)
    for h in range(nsplit):
        y = jnp.dot(sb_ref[:, h * two_nss:(h + 1) * two_nss], cd_ref[h], preferred_element_type=F32)
        if final:
            bu_ref[:, h * ks:(h + 1) * ks] = y
        else:
            out_ref[:, h * ks:(h + 1) * ks] = y + u_ref[:, h * ks:(h + 1) * ks] * d_ref[:, h * ks:(h + 1) * ks]
    if final:
        d_s5 = nsplit * ks
        for r in range(rows // rb):
            rs = slice(r * rb, (r + 1) * rb)
            z = _gelu_tanh(yprev_ref[rs, :] + bu_ref[rs, :d_s5])
            gate = jnp.dot(z.astype(BF16), gluw_ref[...], preferred_element_type=F32) + glub_ref[...]
            out_ref[rs, :] = (z * jax.nn.sigmoid(gate)).astype(out_ref.dtype)


def _s5_scan(u_tm, bd, cd, a_re, a_im, extra, *, reverse, final, nb, out_dtype):
    rows_total, d_s5 = u_tm.shape
    seq = rows_total // nb
    two_ns = bd.shape[0] * bd.shape[2]
    tc = min(S5_TIME_CHUNK, seq)
    nchunk = seq // tc
    rows = tc * nb
    if reverse:
        cidx = lambda i: (nchunk - 1 - i, 0)
    else:
        cidx = lambda i: (i, 0)
    in_specs = [
        pl.BlockSpec((rows, d_s5), cidx),
        _const_spec(bd.shape),
        _const_spec(cd.shape),
        _const_spec(a_re.shape),
        _const_spec(a_im.shape),
    ]
    if final:
        yprev, glu_w, glu_b = extra
        in_specs += [pl.BlockSpec((rows, d_s5), cidx), _const_spec(glu_w.shape), _const_spec(glu_b.shape)]
    else:
        in_specs += [_const_spec(extra[0].shape)]
    return pl.pallas_call(
        functools.partial(_s5_scan_kernel, reverse=reverse, final=final, tc=tc, nb=nb),
        out_shape=jax.ShapeDtypeStruct((rows_total, d_s5), out_dtype),
        grid=(nchunk,),
        in_specs=in_specs,
        out_specs=pl.BlockSpec((rows, d_s5), cidx),
        scratch_shapes=[pltpu.VMEM((rows, two_ns), F32), pltpu.VMEM((rows, two_ns), BF16),
                        pltpu.VMEM((nb, two_ns), F32)],
        compiler_params=_cparams(1),
        name="s5_scan_bwd_glu" if final else "s5_scan_fwd",
    )(u_tm, bd, cd, a_re, a_im, *extra)


def _s5_branch(u_tm, nb, lam_re, lam_im, log_step, b_re, b_im, c_re, c_im, d, glu_w, glu_b):
    ndir, g, p = lam_re.shape
    grp = b_re.shape[-1]
    ns = g * p
    d_s5 = g * grp
    ks = S5_CH_SPLIT if d_s5 % S5_CH_SPLIT == 0 else d_s5
    nsplit = d_s5 // ks
    gs = g // nsplit
    flat = lambda a: a.reshape(ndir, ns)
    to_lanes = lambda a: jnp.transpose(a, (0, 3, 1, 2)).reshape(ndir, grp, ns)
    a_re, a_im, bb_re, bb_im, ncim = _s5_prepare(
        flat(lam_re), flat(lam_im), jnp.repeat(log_step, p, axis=-1),
        to_lanes(b_re), to_lanes(b_im),
        jnp.transpose(c_im, (0, 2, 1, 3)).reshape(ndir, grp, ns))

    def in_blocks(a):
        a = jnp.transpose(a.reshape(grp, nsplit, gs, p), (1, 2, 0, 3))
        return jnp.stack([_block_diag(a[h]) for h in range(nsplit)])

    def out_blocks(a):
        a = jnp.transpose(a.reshape(nsplit, gs, grp, p), (0, 1, 3, 2))
        return jnp.stack([_block_diag(a[h]) for h in range(nsplit)])

    y = None
    for direction in range(ndir):
        bd = jnp.concatenate([in_blocks(bb_re[direction]), in_blocks(bb_im[direction])], axis=2).astype(BF16)
        ncim_g = jnp.transpose(ncim[direction].reshape(grp, g, p), (1, 0, 2))
        cd = jnp.concatenate([out_blocks(c_re[direction]), out_blocks(ncim_g)], axis=1).astype(BF16)
        are = jnp.broadcast_to(a_re[direction][None, :], (nb, ns))
        aim = jnp.broadcast_to(a_im[direction][None, :], (nb, ns))
        if direction == 0:
            if ndir != 2:
                raise NotImplementedError("S5 branch expects forward and backward directions")
            y = _s5_scan(u_tm, bd, cd, are, aim, (d.reshape(1, -1),), reverse=False, final=False,
                         nb=nb, out_dtype=F32)
        else:
            extra = (y, glu_w.astype(BF16), glu_b.reshape(1, -1))
            y = _s5_scan(u_tm, bd, cd, are, aim, extra, reverse=True, final=True, nb=nb, out_dtype=BF16)
    return y


def _dft_matrices(seq):
    n = 2 * seq
    f_lo = 64 if seq % 64 == 0 else 1
    f_hi = seq // f_lo
    t = jnp.arange(seq, dtype=jnp.int32)[None, :]
    ka = (jnp.arange(f_hi, dtype=jnp.int32)[:, None] * f_lo * t) % n
    kb = (jnp.arange(f_lo, dtype=jnp.int32)[:, None] * t) % n
    w = 2.0 * math.pi / n
    aa = ka.astype(F32) * w
    ab = kb.astype(F32) * w
    ca, sa, cb, sb = jnp.cos(aa), jnp.sin(aa), jnp.cos(ab), jnp.sin(ab)
    cm = ca[:, None, :] * cb[None, :, :] - sa[:, None, :] * sb[None, :, :]
    sm = sa[:, None, :] * cb[None, :, :] + ca[:, None, :] * sb[None, :, :]
    return cm.reshape(seq, seq).astype(BF16), sm.reshape(seq, seq).astype(BF16)


def _dft_apply(m_ref, rhs_ref, out_ref, accumulate=False):
    n = m_ref.shape[0]
    rc = min(DFT_ROW_CHUNK, n)

    def body(i, carry):
        r = pl.multiple_of(i * rc, rc)
        v = jnp.dot(m_ref[pl.ds(r, rc), :], rhs_ref[...], preferred_element_type=F32)
        if accumulate:
            out_ref[pl.ds(r, rc), :] += v
        else:
            out_ref[pl.ds(r, rc), :] = v
        return carry

    lax.fori_loop(0, n // rc, body, 0)


def _alt_sign(seq):
    row = lax.broadcasted_iota(jnp.int32, (seq, 1), 0)
    return (1 - 2 * (row & 1)).astype(F32)


def _hy_filter_kernel(feat_ref, w1_ref, b1_ref, w2_ref, b2_ref, freq_ref,
                      w3f_ref, b3f_ref, decf_ref, w3b_ref, b3b_ref, decb_ref, cm_ref, sm_ref,
                      kre_ref, kim_ref, knyq_ref, e_s, d_s, acc_s):
    seq = feat_ref.shape[0]
    n = 2 * seq
    feats = feat_ref[...]
    f = freq_ref[...]
    h = jnp.sin(f * (jnp.dot(feats, w1_ref[...], precision=HIGHEST, preferred_element_type=F32) + b1_ref[...]))
    h = jnp.sin(f * (jnp.dot(h, w2_ref[...], precision=HIGHEST, preferred_element_type=F32) + b2_ref[...]))
    t01 = feats[:, 0:1]
    row = lax.broadcasted_iota(jnp.int32, (seq, 1), 0)

    def taps(w3_ref, b3_ref, dec_ref):
        v = jnp.dot(h, w3_ref[...], precision=HIGHEST, preferred_element_type=F32) + b3_ref[...]
        return v * jnp.exp(-t01 * jnp.abs(dec_ref[...]))

    fwd = taps(w3f_ref, b3f_ref, decf_ref)
    bwd = jnp.where(row == 0, 0.0, taps(w3b_ref, b3b_ref, decb_ref))
    e = fwd + bwd
    e_s[...] = e.astype(BF16)
    d_s[...] = (bwd - fwd).astype(BF16)
    scale = jnp.where(row == 0, 1.0 / n, 2.0 / n)
    _dft_apply(cm_ref, e_s, acc_s)
    kre_ref[...] = acc_s[...] * scale
    _dft_apply(sm_ref, d_s, acc_s)
    kim_ref[...] = acc_s[...] * scale
    nyq = jnp.sum(e * _alt_sign(seq), axis=0, keepdims=True) * (1.0 / n)
    knyq_ref[...] = jnp.broadcast_to(nyq, knyq_ref.shape)


def _hyena_filters(seq, w1, b1, w2, b2, w3, b3, freq, decay, cm, sm, n_order, n_dirs, d_hy):
    emb, hid = w1.shape
    bands = (emb - 1) // 2
    t = jnp.arange(seq, dtype=F32)
    t01 = t / max(seq - 1, 1)
    band = jnp.linspace(1e-4, bands - 1, bands, dtype=F32)
    ang = (2.0 * math.pi) * t[:, None] * band[None, :] / seq
    feats = jnp.concatenate([t01[:, None], jnp.cos(ang), jnp.sin(ang)], axis=-1)
    kpad = LANE
    feats = jnp.pad(feats, ((0, 0), (0, kpad - emb)))
    w1p = jnp.pad(w1, ((0, kpad - emb), (0, 0)))
    ct = min(HY_CH_TILE, d_hy)
    nct = d_hy // ct
    ncol = n_order * d_hy
    b3r = b3.reshape(1, -1)
    decr = decay.reshape(1, -1)
    fcol = lambda o, c: (0, (o * n_dirs + 0) * nct + c)
    bcol = lambda o, c: (0, (o * n_dirs + 1) * nct + c)
    ocol = lambda o, c: (0, o * nct + c)
    full = lambda a: pl.BlockSpec(a.shape, lambda o, c: (0,) * a.ndim)
    return pl.pallas_call(
        _hy_filter_kernel,
        out_shape=(
            jax.ShapeDtypeStruct((seq, ncol), F32),
            jax.ShapeDtypeStruct((seq, ncol), F32),
            jax.ShapeDtypeStruct((8, ncol), F32),
        ),
        grid=(n_order, nct),
        in_specs=[
            full(feats), full(w1p), full(b1.reshape(1, -1)), full(w2), full(b2.reshape(1, -1)),
            full(freq.reshape(1, -1)),
            pl.BlockSpec((hid, ct), fcol), pl.BlockSpec((1, ct), fcol), pl.BlockSpec((1, ct), fcol),
            pl.BlockSpec((hid, ct), bcol), pl.BlockSpec((1, ct), bcol), pl.BlockSpec((1, ct), bcol),
            _const_spec(cm.shape), _const_spec(sm.shape),
        ],
        out_specs=(
            pl.BlockSpec((seq, ct), ocol), pl.BlockSpec((seq, ct), ocol), pl.BlockSpec((8, ct), ocol),
        ),
        scratch_shapes=[pltpu.VMEM((seq, ct), BF16), pltpu.VMEM((seq, ct), BF16), pltpu.VMEM((seq, ct), F32)],
        compiler_params=_cparams(2),
        name="hyena_filter_spectra",
    )(feats, w1p, b1.reshape(1, -1), w2, b2.reshape(1, -1), freq.reshape(1, -1),
      w3, b3r, decr, w3, b3r, decr, cm, sm)


def _hy_conv_kernel(zin_ref, gin_ref, wz_ref, bz_ref, wg_ref, bg_ref, kre_ref, kim_ref, knyq_ref,
                    bias_ref, cm_ref, sm_ref, out_ref, z_s, g_s, zb_s, yre_s, v_s, *, conv_on_z):
    seq = zin_ref.shape[0]
    rc = min(DFT_ROW_CHUNK, seq)
    row = lax.broadcasted_iota(jnp.int32, (seq, 1), 0)

    def short_conv(u, w_ref, b_ref):
        prev = jnp.where(row == 0, 0.0, pltpu.roll(u, 1, 0))
        nxt = jnp.where(row == seq - 1, 0.0, pltpu.roll(u, seq - 1, 0))
        return b_ref[...] + prev * w_ref[0:1, :] + u * w_ref[1:2, :] + nxt * w_ref[2:3, :]

    z = zin_ref[...].astype(F32)
    if conv_on_z:
        z = short_conv(z, wz_ref, bz_ref)
    z_s[...] = z
    zb_s[...] = z.astype(BF16)
    nyq = jnp.sum(z * _alt_sign(seq), axis=0, keepdims=True) * knyq_ref[0:1, :]
    g_s[...] = short_conv(gin_ref[...].astype(F32), wg_ref, bg_ref)

    def fwd(i, carry):
        r = pl.ds(pl.multiple_of(i * rc, rc), rc)
        a_c = jnp.dot(cm_ref[r, :], zb_s[...], preferred_element_type=F32)
        a_s = jnp.dot(sm_ref[r, :], zb_s[...], preferred_element_type=F32)
        kre = kre_ref[r, :]
        kim = kim_ref[r, :]
        yre_s[r, :] = (a_c * kre + a_s * kim).astype(BF16)
        v_s[r, :] = (a_s * kre - a_c * kim).astype(BF16)
        return carry

    lax.fori_loop(0, seq // rc, fwd, 0)

    def inv(i, carry):
        r0 = pl.multiple_of(i * rc, rc)
        r = pl.ds(r0, rc)
        y = jnp.dot(cm_ref[r, :], yre_s[...], preferred_element_type=F32)
        y = y + jnp.dot(sm_ref[r, :], v_s[...], preferred_element_type=F32)
        t = r0 + lax.broadcasted_iota(jnp.int32, (rc, 1), 0)
        y = y + (1 - 2 * (t & 1)).astype(F32) * nyq
        out_ref[r, :] = (g_s[r, :] * (y + z_s[r, :] * bias_ref[...])).astype(out_ref.dtype)
        return carry

    lax.fori_loop(0, seq // rc, inv, 0)


def _hy_conv(zin, zcol0, gcol0, u_hy, conv_w, conv_b, kre, kim, knyq, bias_row, cm, sm,
             *, order, conv_on_z, bsz, seq, d_hy):
    ct = min(HY_CH_TILE, d_hy)
    nct = d_hy // ct
    zc0 = zcol0 // ct
    gc0 = gcol0 // ct
    zw0 = zc0 if conv_on_z else 0
    return pl.pallas_call(
        functools.partial(_hy_conv_kernel, conv_on_z=conv_on_z),
        out_shape=jax.ShapeDtypeStruct((bsz * seq, d_hy), BF16),
        grid=(nct, bsz),
        in_specs=[
            pl.BlockSpec((seq, ct), lambda c, b: (b, zc0 + c)),
            pl.BlockSpec((seq, ct), lambda c, b: (b, gc0 + c)),
            pl.BlockSpec((conv_w.shape[0], ct), lambda c, b: (0, zw0 + c)),
            pl.BlockSpec((1, ct), lambda c, b: (0, zw0 + c)),
            pl.BlockSpec((conv_w.shape[0], ct), lambda c, b: (0, gc0 + c)),
            pl.BlockSpec((1, ct), lambda c, b: (0, gc0 + c)),
            pl.BlockSpec((seq, ct), lambda c, b: (0, order * nct + c)),
            pl.BlockSpec((seq, ct), lambda c, b: (0, order * nct + c)),
            pl.BlockSpec((8, ct), lambda c, b: (0, order * nct + c)),
            pl.BlockSpec((1, ct), lambda c, b: (0, order * nct + c)),
            _const_spec(cm.shape), _const_spec(sm.shape),
        ],
        out_specs=pl.BlockSpec((seq, ct), lambda c, b: (b, c)),
        scratch_shapes=[
            pltpu.VMEM((seq, ct), F32), pltpu.VMEM((seq, ct), F32),
            pltpu.VMEM((seq, ct), BF16), pltpu.VMEM((seq, ct), BF16), pltpu.VMEM((seq, ct), BF16),
        ],
        compiler_params=_cparams(2),
        name=f"hyena_conv_order{order}",
    )(zin, u_hy, conv_w, conv_b, conv_w, conv_b, kre, kim, knyq, bias_row, cm, sm)


def _hyena_branch(u_hy, bsz, seq, conv_w, conv_b, w1, b1, w2, b2, w3, b3, freq, decay, bias):
    n_order, d_hy = bias.shape
    n_dirs = w3.shape[1] // (n_order * d_hy)
    if n_order != 2:
        raise NotImplementedError("Hyena branch is written for two long convolutions")
    cm, sm = _dft_matrices(seq)
    kre, kim, knyq = _hyena_filters(seq, w1, b1, w2, b2, w3, b3, freq, decay, cm, sm, n_order, n_dirs, d_hy)
    cb = conv_b.reshape(1, -1)
    bias_row = bias.reshape(1, -1)
    common = dict(bsz=bsz, seq=seq, d_hy=d_hy)
    z1 = _hy_conv(u_hy, 0, d_hy, u_hy, conv_w, cb, kre, kim, knyq, bias_row, cm, sm,
                  order=0, conv_on_z=True, **common)
    return _hy_conv(z1, 0, 2 * d_hy, u_hy, conv_w, cb, kre, kim, knyq, bias_row, cm, sm,
                    order=1, conv_on_z=False, **common)


def _mixer_kernel(x_ref, mod_ref, g1_ref, g2_ref, fg_ref, permt_ref, za_ref, zb_ref,
                  wgate_ref, wa_ref, wb_ref, wout_ref, wg_ref, wu_ref, wd_ref, o_ref, *, n_chunks):
    nb, tt, d = x_ref.shape
    rows = nb * tt
    mod = lambda k: mod_ref[:, k:k + 1, :]
    x = x_ref[...]
    h = _norm_modulate(x, g1_ref[...], mod(0), mod(1)).reshape(rows, d).astype(BF16)
    gate = jax.nn.sigmoid(jnp.dot(h, wgate_ref[...], preferred_element_type=F32))
    za = jnp.dot(permt_ref[...], za_ref[...], preferred_element_type=F32).astype(BF16)
    ya = jnp.dot(za, wa_ref[...], preferred_element_type=F32)
    yb = jnp.dot(zb_ref[...].reshape(rows, zb_ref.shape[-1]), wb_ref[...], preferred_element_type=F32)
    merged = gate[:, :d] * ya + gate[:, d:] * yb
    o = jnp.dot(merged.astype(BF16), wout_ref[...], preferred_element_type=F32)
    x1 = x + mod(2) * o.reshape(nb, tt, d)

    h2 = _norm_modulate(x1, g2_ref[...], mod(3), mod(4)).reshape(rows, d).astype(BF16)
    fc = wg_ref.shape[1] // n_chunks
    acc = jnp.zeros((rows, d), F32)
    for c in range(n_chunks):
        sl = slice(c * fc, (c + 1) * fc)
        gl = jnp.dot(h2, wg_ref[:, sl], preferred_element_type=F32)
        up = jnp.dot(h2, wu_ref[:, sl], preferred_element_type=F32)
        act = (gl * jax.nn.sigmoid(gl) * up).astype(BF16)
        acc = acc + jnp.dot(act, wd_ref[sl, :], preferred_element_type=F32)
    x2 = x1 + mod(5) * acc.reshape(nb, tt, d)
    ms = jnp.mean(x2 * x2, axis=-1, keepdims=True)
    o_ref[...] = x2 * lax.rsqrt(ms + EPS) * fg_ref[...]


def _mixer(x, mod3, norm1_g, norm2_g, final_g, perm_t, za_tm, zb, w_gate, w_a, w_b, w_out, w_g, w_u, w_d):
    bsz, seq, d = x.shape
    d_s5 = w_a.shape[0]
    d_hy = w_b.shape[0]
    d_ff = w_g.shape[1]
    tt = _time_tile(bsz, seq)
    n_chunks = 2 if d_ff % (2 * LANE) == 0 else 1
    consts = (mod3, norm1_g, norm2_g, final_g, perm_t)
    weights = (w_gate, w_a, w_b, w_out, w_g, w_u, w_d)
    return pl.pallas_call(
        functools.partial(_mixer_kernel, n_chunks=n_chunks),
        out_shape=jax.ShapeDtypeStruct(x.shape, F32),
        grid=(seq // tt,),
        in_specs=[pl.BlockSpec((bsz, tt, d), lambda j: (0, j, 0))]
        + [_const_spec(a.shape) for a in consts]
        + [pl.BlockSpec((tt * bsz, d_s5), lambda j: (j, 0)),
           pl.BlockSpec((bsz, tt, d_hy), lambda j: (0, j, 0))]
        + [_const_spec(a.shape) for a in weights],
        out_specs=pl.BlockSpec((bsz, tt, d), lambda j: (0, j, 0)),
        compiler_params=_cparams(1),
        name="merge_swiglu_final_norm",
    )(x, *consts, za_tm, zb, *weights)


def kernel(x, c, ada_w, ada_b, norm1_g, norm2_g, w_in, s5_lam_re, s5_lam_im, s5_log_step, s5_b_re, s5_b_im, s5_c_re, s5_c_im, s5_d, s5_glu_w, s5_glu_b, hy_conv_w, hy_conv_b, hy_ffn_w1, hy_ffn_b1, hy_ffn_w2, hy_ffn_b2, hy_ffn_w3, hy_ffn_b3, hy_freq, hy_decay, hy_bias, w_branch_a, w_branch_b, w_out, ffn_w_gu, ffn_w_down, final_g):
    bsz, seq, d = x.shape
    depth = ada_w.shape[0]
    if depth != 1:
        raise NotImplementedError("the final RMSNorm is fused into the (single) layer's channel mixer")
    d_s5 = s5_d.shape[-1]
    n_order, d_hy = hy_bias.shape[1:]
    d_uh = d_s5 + (n_order + 1) * d_hy
    d_ff = ffn_w_down.shape[1]
    i = 0
    perm = _row_permutation(bsz, _time_tile(bsz, seq))
    mod = _modulation(c, ada_w[i], ada_b[i]).reshape(bsz, 6, d)
    w_in_b = w_in[i].astype(BF16)
    u_s5, u_hy = _in_projection(x, mod, norm1_g[i].reshape(1, d), w_in_b[:, :d_uh], perm, d_s5)
    z_a = _s5_branch(u_s5, bsz, s5_lam_re[i], s5_lam_im[i], s5_log_step[i], s5_b_re[i], s5_b_im[i],
                     s5_c_re[i], s5_c_im[i], s5_d[i], s5_glu_w[i], s5_glu_b[i])
    z_b = _hyena_branch(u_hy.reshape(bsz * seq, -1), bsz, seq, hy_conv_w[i], hy_conv_b[i], hy_ffn_w1[i],
                        hy_ffn_b1[i], hy_ffn_w2[i], hy_ffn_b2[i], hy_ffn_w3[i], hy_ffn_b3[i], hy_freq[i],
                        hy_decay[i], hy_bias[i])
    w_gu = ffn_w_gu[i].astype(BF16)
    return _mixer(x, mod, norm1_g[i].reshape(1, d), norm2_g[i].reshape(1, d), final_g.reshape(1, d),
                  perm.T, z_a, z_b.reshape(bsz, seq, d_hy), w_in_b[:, d_uh:],
                  w_branch_a[i].astype(BF16), w_branch_b[i].astype(BF16), w_out[i].astype(BF16),
                  w_gu[:, :d_ff], w_gu[:, d_ff:], ffn_w_down[i].astype(BF16))
```

```python
import functools
import math

import jax
import jax.numpy as jnp
from jax import lax
from jax.experimental import pallas as pl
from jax.experimental.pallas import tpu as pltpu

F32 = jnp.float32
BF16 = jnp.bfloat16
EPS = 1e-6
HIGHEST = lax.Precision.HIGHEST

V7X_VMEM_BYTES = 64 * 1024 * 1024
VMEM_LIMIT_BYTES = 56 * 1024 * 1024
LANE = 128
ROW_TILE = 512
S5_TIME_CHUNK = 64
S5_LANE_CHUNK = 512
S5_CH_SPLIT = 256
HY_CH_TILE = 256
DFT_ROW_CHUNK = 512


def _cparams(n_axes):
    return pltpu.CompilerParams(
        dimension_semantics=("arbitrary",) * n_axes,
        vmem_limit_bytes=VMEM_LIMIT_BYTES,
    )


def _const_spec(shape):
    nd = len(shape)
    return pl.BlockSpec(shape, lambda *_: (0,) * nd, pipeline_mode=pl.Buffered(1))


def _gelu_tanh(x):
    return 0.5 * x * (1.0 + jnp.tanh(math.sqrt(2.0 / math.pi) * (x + 0.044715 * (x * x * x))))


def _norm_modulate(x, g, shift, scale):
    ms = jnp.mean(x * x, axis=-1, keepdims=True)
    r = x * lax.rsqrt(ms + EPS) * g
    return r * (1.0 + scale) + shift


def _mod_kernel(c_ref, w_ref, b_ref, o_ref):
    c = c_ref[...]
    ca = c * jax.nn.sigmoid(c)
    o_ref[...] = jnp.dot(ca, w_ref[...], precision=HIGHEST, preferred_element_type=F32) + b_ref[...]


def _modulation(c, ada_w, ada_b):
    bsz, d = c.shape
    n = ada_w.shape[1]
    tn = 512
    return pl.pallas_call(
        _mod_kernel,
        out_shape=jax.ShapeDtypeStruct((bsz, n), F32),
        grid=(n // tn,),
        in_specs=[
            pl.BlockSpec((bsz, d), lambda j: (0, 0)),
            pl.BlockSpec((d, tn), lambda j: (0, j)),
            pl.BlockSpec((1, tn), lambda j: (0, j)),
        ],
        out_specs=pl.BlockSpec((bsz, tn), lambda j: (0, j)),
        compiler_params=_cparams(1),
        name="adaln_mod",
    )(c, ada_w, ada_b.reshape(1, n))


def _time_tile(bsz, seq):
    return max(min(ROW_TILE // bsz, seq), 1)


def _row_permutation(bsz, tt):
    r = jnp.arange(bsz * tt, dtype=jnp.int32)
    src = (r % bsz) * tt + r // bsz
    return (src[:, None] == r[None, :]).astype(BF16)


def _inproj_kernel(x_ref, mod_ref, g_ref, w_ref, perm_ref, us5_ref, uhy_ref, *, d_s5):
    nb, tt, d = x_ref.shape
    h = _norm_modulate(x_ref[...], g_ref[...], mod_ref[:, 0:1, :], mod_ref[:, 1:2, :])
    hb = h.reshape(nb * tt, d).astype(BF16)
    p = jnp.dot(hb, w_ref[...], preferred_element_type=F32)
    us5_ref[...] = jnp.dot(perm_ref[...], p[:, :d_s5].astype(BF16),
                           preferred_element_type=F32).astype(us5_ref.dtype)
    uhy_ref[...] = p[:, d_s5:].astype(BF16).reshape(uhy_ref.shape)


def _in_projection(x, mod3, norm_g, w_uh, perm, d_s5):
    bsz, seq, d = x.shape
    n = w_uh.shape[1]
    tt = _time_tile(bsz, seq)
    return pl.pallas_call(
        functools.partial(_inproj_kernel, d_s5=d_s5),
        out_shape=(
            jax.ShapeDtypeStruct((seq * bsz, d_s5), BF16),
            jax.ShapeDtypeStruct((bsz, seq, n - d_s5), BF16),
        ),
        grid=(seq // tt,),
        in_specs=[
            pl.BlockSpec((bsz, tt, d), lambda j: (0, j, 0)),
            _const_spec(mod3.shape),
            _const_spec((1, d)),
            _const_spec((d, n)),
            _const_spec(perm.shape),
        ],
        out_specs=(
            pl.BlockSpec((tt * bsz, d_s5), lambda j: (j, 0)),
            pl.BlockSpec((bsz, tt, n - d_s5), lambda j: (0, j, 0)),
        ),
        compiler_params=_cparams(1),
        name="in_proj",
    )(x, mod3, norm_g, w_uh, perm)


def _s5_prep_kernel(lre_ref, lim_ref, lstep_ref, bre_ref, bim_ref, cim_ref,
                    are_ref, aim_ref, bbre_ref, bbim_ref, ncim_ref):
    step = jnp.exp(lstep_ref[...])
    lr = lre_ref[...]
    li = lim_ref[...]
    mag = jnp.exp(lr * step)
    ar = mag * jnp.cos(li * step)
    ai = mag * jnp.sin(li * step)
    num = ar - 1.0
    den = lr * lr + li * li
    cr = (num * lr + ai * li) / den
    ci = (ai * lr - num * li) / den
    are_ref[...] = ar
    aim_ref[...] = ai
    for d in range(lre_ref.shape[0]):
        br = bre_ref[d]
        bi = bim_ref[d]
        bbre_ref[d] = cr[d:d + 1, :] * br - ci[d:d + 1, :] * bi
        bbim_ref[d] = cr[d:d + 1, :] * bi + ci[d:d + 1, :] * br
    ncim_ref[...] = -cim_ref[...]


def _s5_prepare(lam_re, lam_im, log_step, b_re, b_im, c_im):
    outs = (
        jax.ShapeDtypeStruct(lam_re.shape, F32),
        jax.ShapeDtypeStruct(lam_re.shape, F32),
        jax.ShapeDtypeStruct(b_re.shape, F32),
        jax.ShapeDtypeStruct(b_re.shape, F32),
        jax.ShapeDtypeStruct(c_im.shape, F32),
    )
    return pl.pallas_call(
        _s5_prep_kernel,
        out_shape=outs,
        compiler_params=pltpu.CompilerParams(vmem_limit_bytes=VMEM_LIMIT_BYTES),
        name="s5_discretize",
    )(lam_re, lam_im, log_step, b_re, b_im, c_im)


def _block_diag(m):
    g, r, c = m.shape
    eye = jnp.eye(g, dtype=m.dtype)
    return (m[:, :, None, :] * eye[:, None, :, None]).reshape(g * r, g * c)


def _s5_scan_kernel(*refs, reverse, final, tc, nb):
    if final:
        (u_ref, bd_ref, cd_ref, are_ref, aim_ref, yprev_ref, gluw_ref, glub_ref,
         out_ref, bu_ref, sb_ref, s_ref) = refs
    else:
        (u_ref, bd_ref, cd_ref, are_ref, aim_ref, d_ref, out_ref, bu_ref, sb_ref, s_ref) = refs
    rows = tc * nb
    nsplit, ks, two_nss = bd_ref.shape
    nss = two_nss // 2

    @pl.when(pl.program_id(0) == 0)
    def _():
        s_ref[...] = jnp.zeros_like(s_ref)

    for h in range(nsplit):
        ub = u_ref[:, h * ks:(h + 1) * ks].astype(BF16)
        bu_ref[:, h * two_nss:(h + 1) * two_nss] = jnp.dot(ub, bd_ref[h], preferred_element_type=F32)

    lc = min(S5_LANE_CHUNK, nss)
    for h in range(nsplit):
        for c in range(nss // lc):
            a_sl = slice(h * nss + c * lc, h * nss + (c + 1) * lc)
            re_sl = slice(h * two_nss + c * lc, h * two_nss + (c + 1) * lc)
            im_sl = slice(h * two_nss + nss + c * lc, h * two_nss + nss + (c + 1) * lc)
            a_re = are_ref[:, a_sl]
            a_im = aim_ref[:, a_sl]

            def body(k, carry, re_sl=re_sl, im_sl=im_sl, a_re=a_re, a_im=a_im):
                sr, si = carry
                t = (tc - 1 - k) if reverse else k
                row = pl.ds(pl.multiple_of(t * nb, nb), nb)
                nr = a_re * sr - a_im * si + bu_ref[row, re_sl]
                ni = a_re * si + a_im * sr + bu_ref[row, im_sl]
                sb_ref[row, re_sl] = nr.astype(BF16)
                sb_ref[row, im_sl] = ni.astype(BF16)
                return nr, ni

            sr, si = lax.fori_loop(0, tc, body, (s_ref[:, re_sl], s_ref[:, im_sl]), unroll=8)
            s_ref[:, re_sl] = sr
            s_ref[:, im_sl] = si

    rb = min(256, rows)
    for h in range(nsplit):
        y = jnp.dot(sb_ref[:, h * two_nss:(h + 1) * two_nss], cd_ref[h], preferred_element_type=F32)
        if final:
            bu_ref[:, h * ks:(h + 1) * ks] = y
        else:
            out_ref[:, h * ks:(h + 1) * ks] = y + u_ref[:, h * ks:(h + 1) * ks] * d_ref[:, h * ks:(h + 1) * ks]
    if final:
        d_s5 = nsplit * ks
        for r in range(rows // rb):
            rs = slice(r * rb, (r + 1) * rb)
            z = _gelu_tanh(yprev_ref[rs, :] + bu_ref[rs, :d_s5])
            gate = jnp.dot(z.astype(BF16), gluw_ref[...], preferred_element_type=F32) + glub_ref[...]
            out_ref[rs, :] = (z * jax.nn.sigmoid(gate)).astype(out_ref.dtype)


def _s5_scan(u_tm, bd, cd, a_re, a_im, extra, *, reverse, final, nb, out_dtype):
    rows_total, d_s5 = u_tm.shape
    seq = rows_total // nb
    two_ns = bd.shape[0] * bd.shape[2]
    tc = min(S5_TIME_CHUNK, seq)
    nchunk = seq // tc
    rows = tc * nb
    if reverse:
        cidx = lambda i: (nchunk - 1 - i, 0)
    else:
        cidx = lambda i: (i, 0)
    in_specs = [
        pl.BlockSpec((rows, d_s5), cidx),
        _const_spec(bd.shape),
        _const_spec(cd.shape),
        _const_spec(a_re.shape),
        _const_spec(a_im.shape),
    ]
    if final:
        yprev, glu_w, glu_b = extra
        in_specs += [pl.BlockSpec((rows, d_s5), cidx), _const_spec(glu_w.shape), _const_spec(glu_b.shape)]
    else:
        in_specs += [_const_spec(extra[0].shape)]
    return pl.pallas_call(
        functools.partial(_s5_scan_kernel, reverse=reverse, final=final, tc=tc, nb=nb),
        out_shape=jax.ShapeDtypeStruct((rows_total, d_s5), out_dtype),
        grid=(nchunk,),
        in_specs=in_specs,
        out_specs=pl.BlockSpec((rows, d_s5), cidx),
        scratch_shapes=[pltpu.VMEM((rows, two_ns), F32), pltpu.VMEM((rows, two_ns), BF16),
                        pltpu.VMEM((nb, two_ns), F32)],
        compiler_params=_cparams(1),
        name="s5_scan_bwd_glu" if final else "s5_scan_fwd",
    )(u_tm, bd, cd, a_re, a_im, *extra)


def _s5_branch(u_tm, nb, lam_re, lam_im, log_step, b_re, b_im, c_re, c_im, d, glu_w, glu_b):
    ndir, g, p = lam_re.shape
    grp = b_re.shape[-1]
    ns = g * p
    d_s5 = g * grp
    ks = S5_CH_SPLIT if d_s5 % S5_CH_SPLIT == 0 else d_s5
    nsplit = d_s5 // ks
    gs = g // nsplit
    flat = lambda a: a.reshape(ndir, ns)
    to_lanes = lambda a: jnp.transpose(a, (0, 3, 1, 2)).reshape(ndir, grp, ns)
    a_re, a_im, bb_re, bb_im, ncim = _s5_prepare(
        flat(lam_re), flat(lam_im), jnp.repeat(log_step, p, axis=-1),
        to_lanes(b_re), to_lanes(b_im),
        jnp.transpose(c_im, (0, 2, 1, 3)).reshape(ndir, grp, ns))

    def in_blocks(a):
        a = jnp.transpose(a.reshape(grp, nsplit, gs, p), (1, 2, 0, 3))
        return jnp.stack([_block_diag(a[h]) for h in range(nsplit)])

    def out_blocks(a):
        a = jnp.transpose(a.reshape(nsplit, gs, grp, p), (0, 1, 3, 2))
        return jnp.stack([_block_diag(a[h]) for h in range(nsplit)])

    y = None
    for direction in range(ndir):
        bd = jnp.concatenate([in_blocks(bb_re[direction]), in_blocks(bb_im[direction])], axis=2).astype(BF16)
        ncim_g = jnp.transpose(ncim[direction].reshape(grp, g, p), (1, 0, 2))
        cd = jnp.concatenate([out_blocks(c_re[direction]), out_blocks(ncim_g)], axis=1).astype(BF16)
        are = jnp.broadcast_to(a_re[direction][None, :], (nb, ns))
        aim = jnp.broadcast_to(a_im[direction][None, :], (nb, ns))
        if direction == 0:
            if ndir != 2:
                raise NotImplementedError("S5 branch expects forward and backward directions")
            y = _s5_scan(u_tm, bd, cd, are, aim, (d.reshape(1, -1),), reverse=False, final=False,
                         nb=nb, out_dtype=F32)
        else:
            extra = (y, glu_w.astype(BF16), glu_b.reshape(1, -1))
            y = _s5_scan(u_tm, bd, cd, are, aim, extra, reverse=True, final=True, nb=nb, out_dtype=BF16)
    return y


def _dft_matrices(seq):
    n = 2 * seq
    f_lo = 64 if seq % 64 == 0 else 1
    f_hi = seq // f_lo
    t = jnp.arange(seq, dtype=jnp.int32)[None, :]
    ka = (jnp.arange(f_hi, dtype=jnp.int32)[:, None] * f_lo * t) % n
    kb = (jnp.arange(f_lo, dtype=jnp.int32)[:, None] * t) % n
    w = 2.0 * math.pi / n
    aa = ka.astype(F32) * w
    ab = kb.astype(F32) * w
    ca, sa, cb, sb = jnp.cos(aa), jnp.sin(aa), jnp.cos(ab), jnp.sin(ab)
    cm = ca[:, None, :] * cb[None, :, :] - sa[:, None, :] * sb[None, :, :]
    sm = sa[:, None, :] * cb[None, :, :] + ca[:, None, :] * sb[None, :, :]
    return cm.reshape(seq, seq).astype(BF16), sm.reshape(seq, seq).astype(BF16)


def _hyena_tables(seq, ct):
    half = seq // 2
    cm, sm = _dft_matrices(half)
    ang = jnp.arange(half, dtype=F32)[:, None] * (math.pi / seq)
    return cm, sm, jnp.broadcast_to(jnp.cos(ang), (half, ct)), jnp.broadcast_to(jnp.sin(ang), (half, ct))


def _store_slabs(dst_ref, slab0, x):
    for j in range(x.shape[1] // LANE):
        dst_ref[slab0 + j] = x[:, j * LANE:(j + 1) * LANE]


def _deinterleave_rows(src_ref, slab0, nsl, dst_ref, col0):
    half = src_ref.shape[1] // 2
    ct = nsl * LANE
    for j in range(nsl):
        for par in range(2):
            c = col0 + par * ct + j * LANE
            dst_ref[:, c:c + LANE] = src_ref[slab0 + j, pl.ds(par, half, stride=2), :].astype(BF16)


def _mid_frequency(x):
    r = lax.broadcasted_iota(jnp.int32, (x.shape[0], 1), 0) & 3
    me = jnp.where(r == 0, 1.0, jnp.where(r == 2, -1.0, 0.0))
    mo = jnp.where(r == 1, 1.0, jnp.where(r == 3, -1.0, 0.0))
    return jnp.sum(x * me, axis=0, keepdims=True), jnp.sum(x * mo, axis=0, keepdims=True)


def _split_spectrum(c2, s2, cw, sw):
    ct = cw.shape[1]
    ec, oc = c2[:, :ct], c2[:, ct:]
    es, os_ = s2[:, :ct], s2[:, ct:]
    wc = cw * oc - sw * os_
    ws = cw * os_ + sw * oc
    return ec + wc, es + ws, ec - wc, ws - es


def _hy_filter_kernel(feat_ref, w1_ref, b1_ref, w2_ref, b2_ref, freq_ref,
                      w3f_ref, b3f_ref, decf_ref, w3b_ref, b3b_ref, decb_ref,
                      cm_ref, sm_ref, cw_ref, sw_ref,
                      kpc_ref, kps_ref, kqc_ref, kqs_ref, km_ref, taps_s, rhs_s):
    seq = feat_ref.shape[0]
    half = seq // 2
    n = 2 * seq
    ct = cw_ref.shape[1]
    nsl = ct // LANE
    feats = feat_ref[...]
    f = freq_ref[...]
    h = jnp.sin(f * (jnp.dot(feats, w1_ref[...], precision=HIGHEST, preferred_element_type=F32) + b1_ref[...]))
    h = jnp.sin(f * (jnp.dot(h, w2_ref[...], precision=HIGHEST, preferred_element_type=F32) + b2_ref[...]))
    t01 = feats[:, 0:1]
    row = lax.broadcasted_iota(jnp.int32, (seq, 1), 0)

    def taps(w3_ref, b3_ref, dec_ref):
        v = jnp.dot(h, w3_ref[...], precision=HIGHEST, preferred_element_type=F32) + b3_ref[...]
        return v * jnp.exp(-t01 * jnp.abs(dec_ref[...]))

    fwd = taps(w3f_ref, b3f_ref, decf_ref)
    bwd = jnp.where(row == 0, 0.0, taps(w3b_ref, b3b_ref, decb_ref))
    _store_slabs(taps_s, 0, fwd)
    _store_slabs(taps_s, nsl, bwd)
    _deinterleave_rows(taps_s, 0, nsl, rhs_s, 0)
    _deinterleave_rows(taps_s, nsl, nsl, rhs_s, 2 * ct)
    c4 = jnp.dot(cm_ref[...], rhs_s[...], preferred_element_type=F32)
    s4 = jnp.dot(sm_ref[...], rhs_s[...], preferred_element_type=F32)
    cw = cw_ref[...]
    sw = sw_ref[...]
    fpc, fps, fqc, fqs = _split_spectrum(c4[:, :2 * ct], s4[:, :2 * ct], cw, sw)
    bpc, bps, bqc, bqs = _split_spectrum(c4[:, 2 * ct:], s4[:, 2 * ct:], cw, sw)
    scale = jnp.where(lax.broadcasted_iota(jnp.int32, (half, 1), 0) == 0, 1.0 / n, 2.0 / n)
    kpc_ref[...] = (fpc + bpc) * scale
    kps_ref[...] = (fps - bps) * scale
    kqc_ref[...] = (fqc + bqc) * scale
    kqs_ref[...] = (fqs - bqs) * scale
    fmc, fms = _mid_frequency(fwd)
    bmc, bms = _mid_frequency(bwd)
    km_ref[...] = jnp.zeros_like(km_ref)
    km_ref[0:1, :] = (fmc + bmc) * (2.0 / n)
    km_ref[1:2, :] = (fms - bms) * (2.0 / n)


def _hyena_filters(seq, w1, b1, w2, b2, w3, b3, freq, decay, tables, n_order, n_dirs, d_hy):
    cm, sm, cw, sw = tables
    half = seq // 2
    emb, hid = w1.shape
    bands = (emb - 1) // 2
    t = jnp.arange(seq, dtype=F32)
    t01 = t / max(seq - 1, 1)
    band = jnp.linspace(1e-4, bands - 1, bands, dtype=F32)
    ang = (2.0 * math.pi) * t[:, None] * band[None, :] / seq
    feats = jnp.concatenate([t01[:, None], jnp.cos(ang), jnp.sin(ang)], axis=-1)
    kpad = LANE
    feats = jnp.pad(feats, ((0, 0), (0, kpad - emb)))
    w1p = jnp.pad(w1, ((0, kpad - emb), (0, 0)))
    ct = cw.shape[1]
    nct = d_hy // ct
    ncol = n_order * d_hy
    b3r = b3.reshape(1, -1)
    decr = decay.reshape(1, -1)
    fcol = lambda o, c: (0, (o * n_dirs + 0) * nct + c)
    bcol = lambda o, c: (0, (o * n_dirs + 1) * nct + c)
    ocol = lambda o, c: (0, o * nct + c)
    full = lambda a: pl.BlockSpec(a.shape, lambda o, c: (0,) * a.ndim)
    spec = jax.ShapeDtypeStruct((half, ncol), F32)
    return pl.pallas_call(
        _hy_filter_kernel,
        out_shape=(spec, spec, spec, spec, jax.ShapeDtypeStruct((8, ncol), F32)),
        grid=(n_order, nct),
        in_specs=[
            full(feats), full(w1p), full(b1.reshape(1, -1)), full(w2), full(b2.reshape(1, -1)),
            full(freq.reshape(1, -1)),
            pl.BlockSpec((hid, ct), fcol), pl.BlockSpec((1, ct), fcol), pl.BlockSpec((1, ct), fcol),
            pl.BlockSpec((hid, ct), bcol), pl.BlockSpec((1, ct), bcol), pl.BlockSpec((1, ct), bcol),
            _const_spec(cm.shape), _const_spec(sm.shape), _const_spec(cw.shape), _const_spec(sw.shape),
        ],
        out_specs=tuple(pl.BlockSpec((half, ct), ocol) for _ in range(4)) + (pl.BlockSpec((8, ct), ocol),),
        scratch_shapes=[pltpu.VMEM((2 * ct // LANE, seq, LANE), F32), pltpu.VMEM((half, 4 * ct), BF16)],
        compiler_params=_cparams(2),
        name="hyena_filter_spectra",
    )(feats, w1p, b1.reshape(1, -1), w2, b2.reshape(1, -1), freq.reshape(1, -1),
      w3, b3r, decr, w3, b3r, decr, cm, sm, cw, sw)


def _hy_conv_kernel(zin_ref, gin_ref, wz_ref, bz_ref, wg_ref, bg_ref,
                    kpc_ref, kps_ref, kqc_ref, kqs_ref, km_ref, bias_ref,
                    cm_ref, sm_ref, cw_ref, sw_ref, out_ref,
                    z_s, g_s, o_s, rhs_s, ac_s, as_s, *, conv_on_z):
    seq, ct = zin_ref.shape
    half = seq // 2
    nsl = ct // LANE
    rc = min(DFT_ROW_CHUNK, half)
    row = lax.broadcasted_iota(jnp.int32, (seq, 1), 0)

    def short_conv(u, w_ref, b_ref):
        prev = jnp.where(row == 0, 0.0, pltpu.roll(u, 1, 0))
        nxt = jnp.where(row == seq - 1, 0.0, pltpu.roll(u, seq - 1, 0))
        return b_ref[...] + prev * w_ref[0:1, :] + u * w_ref[1:2, :] + nxt * w_ref[2:3, :]

    z = zin_ref[...].astype(F32)
    if conv_on_z:
        z = short_conv(z, wz_ref, bz_ref)
    _store_slabs(z_s, 0, z)
    zmc, zms = _mid_frequency(z)
    ymc = zmc * km_ref[0:1, :] - zms * km_ref[1:2, :]
    yms = zmc * km_ref[1:2, :] + zms * km_ref[0:1, :]
    _store_slabs(g_s, 0, short_conv(gin_ref[...].astype(F32), wg_ref, bg_ref))
    _deinterleave_rows(z_s, 0, nsl, rhs_s, 0)

    for i in range(half // rc):
        r = slice(i * rc, (i + 1) * rc)
        c2 = jnp.dot(cm_ref[r, :], rhs_s[...], preferred_element_type=F32)
        s2 = jnp.dot(sm_ref[r, :], rhs_s[...], preferred_element_type=F32)
        cw = cw_ref[r, :]
        sw = sw_ref[r, :]
        pc, ps, qc, qs = _split_spectrum(c2, s2, cw, sw)
        kpc, kps, kqc, kqs = kpc_ref[r, :], kps_ref[r, :], kqc_ref[r, :], kqs_ref[r, :]
        ypc = pc * kpc - ps * kps
        yps = pc * kps + ps * kpc
        yqc = qc * kqc - qs * kqs
        yqs = qc * kqs + qs * kqc
        dc = ypc - yqc
        ds = yps + yqs
        ac_s[r, :ct] = (ypc + yqc).astype(BF16)
        as_s[r, :ct] = (yps - yqs).astype(BF16)
        ac_s[r, ct:] = (cw * dc + sw * ds).astype(BF16)
        as_s[r, ct:] = (cw * ds - sw * dc).astype(BF16)

    for i in range(half // rc):
        r = slice(i * rc, (i + 1) * rc)
        y2 = jnp.dot(cm_ref[r, :], ac_s[...], preferred_element_type=F32)
        y2 = y2 + jnp.dot(sm_ref[r, :], as_s[...], preferred_element_type=F32)
        m = i * rc + lax.broadcasted_iota(jnp.int32, (rc, 1), 0)
        alt = (1 - 2 * (m & 1)).astype(F32)
        for par, ym in enumerate((ymc, yms)):
            y = y2[:, par * ct:(par + 1) * ct] + alt * ym
            rows = pl.ds(2 * i * rc + par, rc, stride=2)
            for j in range(nsl):
                ls = slice(j * LANE, (j + 1) * LANE)
                o_s[j, rows, :] = g_s[j, rows, :] * (y[:, ls] + z_s[j, rows, :] * bias_ref[:, ls])
    for j in range(nsl):
        out_ref[:, j * LANE:(j + 1) * LANE] = o_s[j].astype(out_ref.dtype)


def _hy_conv(zin, zcol0, gcol0, u_hy, conv_w, conv_b, spectra, bias_row, tables,
             *, order, conv_on_z, bsz, seq, d_hy):
    cm, sm, cw, sw = tables
    kpc, kps, kqc, kqs, km = spectra
    half = seq // 2
    ct = cw.shape[1]
    nct = d_hy // ct
    zc0 = zcol0 // ct
    gc0 = gcol0 // ct
    zw0 = zc0 if conv_on_z else 0
    kcol = lambda c, b: (0, order * nct + c)
    return pl.pallas_call(
        functools.partial(_hy_conv_kernel, conv_on_z=conv_on_z),
        out_shape=jax.ShapeDtypeStruct((bsz * seq, d_hy), BF16),
        grid=(nct, bsz),
        in_specs=[
            pl.BlockSpec((seq, ct), lambda c, b: (b, zc0 + c)),
            pl.BlockSpec((seq, ct), lambda c, b: (b, gc0 + c)),
            pl.BlockSpec((conv_w.shape[0], ct), lambda c, b: (0, zw0 + c)),
            pl.BlockSpec((1, ct), lambda c, b: (0, zw0 + c)),
            pl.BlockSpec((conv_w.shape[0], ct), lambda c, b: (0, gc0 + c)),
            pl.BlockSpec((1, ct), lambda c, b: (0, gc0 + c)),
            pl.BlockSpec((half, ct), kcol), pl.BlockSpec((half, ct), kcol),
            pl.BlockSpec((half, ct), kcol), pl.BlockSpec((half, ct), kcol),
            pl.BlockSpec((8, ct), kcol), pl.BlockSpec((1, ct), kcol),
            _const_spec(cm.shape), _const_spec(sm.shape), _const_spec(cw.shape), _const_spec(sw.shape),
        ],
        out_specs=pl.BlockSpec((seq, ct), lambda c, b: (b, c)),
        scratch_shapes=[
            pltpu.VMEM((ct // LANE, seq, LANE), F32), pltpu.VMEM((ct // LANE, seq, LANE), F32),
            pltpu.VMEM((ct // LANE, seq, LANE), F32),
            pltpu.VMEM((half, 2 * ct), BF16), pltpu.VMEM((half, 2 * ct), BF16), pltpu.VMEM((half, 2 * ct), BF16),
        ],
        compiler_params=_cparams(2),
        name=f"hyena_conv_order{order}",
    )(zin, u_hy, conv_w, conv_b, conv_w, conv_b, kpc, kps, kqc, kqs, km, bias_row, cm, sm, cw, sw)


def _hyena_branch(u_hy, bsz, seq, conv_w, conv_b, w1, b1, w2, b2, w3, b3, freq, decay, bias):
    n_order, d_hy = bias.shape
    n_dirs = w3.shape[1] // (n_order * d_hy)
    if n_order != 2 or seq % 4 != 0 or d_hy % LANE != 0:
        raise NotImplementedError("Hyena branch: two long convolutions, L % 4 == 0, 128-lane channel tiles")
    tables = _hyena_tables(seq, min(HY_CH_TILE, d_hy))
    spectra = _hyena_filters(seq, w1, b1, w2, b2, w3, b3, freq, decay, tables, n_order, n_dirs, d_hy)
    cb = conv_b.reshape(1, -1)
    bias_row = bias.reshape(1, -1)
    common = dict(bsz=bsz, seq=seq, d_hy=d_hy)
    z1 = _hy_conv(u_hy, 0, d_hy, u_hy, conv_w, cb, spectra, bias_row, tables,
                  order=0, conv_on_z=True, **common)
    return _hy_conv(z1, 0, 2 * d_hy, u_hy, conv_w, cb, spectra, bias_row, tables,
                    order=1, conv_on_z=False, **common)


def _mixer_kernel(x_ref, mod_ref, g1_ref, g2_ref, fg_ref, permt_ref, za_ref, zb_ref,
                  wgate_ref, wa_ref, wb_ref, wout_ref, wg_ref, wu_ref, wd_ref, o_ref, *, n_chunks):
    nb, tt, d = x_ref.shape
    rows = nb * tt
    mod = lambda k: mod_ref[:, k:k + 1, :]
    x = x_ref[...]
    h = _norm_modulate(x, g1_ref[...], mod(0), mod(1)).reshape(rows, d).astype(BF16)
    gate = jax.nn.sigmoid(jnp.dot(h, wgate_ref[...], preferred_element_type=F32))
    za = jnp.dot(permt_ref[...], za_ref[...], preferred_element_type=F32).astype(BF16)
    ya = jnp.dot(za, wa_ref[...], preferred_element_type=F32)
    yb = jnp.dot(zb_ref[...].reshape(rows, zb_ref.shape[-1]), wb_ref[...], preferred_element_type=F32)
    merged = gate[:, :d] * ya + gate[:, d:] * yb
    o = jnp.dot(merged.astype(BF16), wout_ref[...], preferred_element_type=F32)
    x1 = x + mod(2) * o.reshape(nb, tt, d)

    h2 = _norm_modulate(x1, g2_ref[...], mod(3), mod(4)).reshape(rows, d).astype(BF16)
    fc = wg_ref.shape[1] // n_chunks
    acc = jnp.zeros((rows, d), F32)
    for c in range(n_chunks):
        sl = slice(c * fc, (c + 1) * fc)
        gl = jnp.dot(h2, wg_ref[:, sl], preferred_element_type=F32)
        up = jnp.dot(h2, wu_ref[:, sl], preferred_element_type=F32)
        act = (gl * jax.nn.sigmoid(gl) * up).astype(BF16)
        acc = acc + jnp.dot(act, wd_ref[sl, :], preferred_element_type=F32)
    x2 = x1 + mod(5) * acc.reshape(nb, tt, d)
    ms = jnp.mean(x2 * x2, axis=-1, keepdims=True)
    o_ref[...] = x2 * lax.rsqrt(ms + EPS) * fg_ref[...]


def _mixer(x, mod3, norm1_g, norm2_g, final_g, perm_t, za_tm, zb, w_gate, w_a, w_b, w_out, w_g, w_u, w_d):
    bsz, seq, d = x.shape
    d_s5 = w_a.shape[0]
    d_hy = w_b.shape[0]
    d_ff = w_g.shape[1]
    tt = _time_tile(bsz, seq)
    n_chunks = 2 if d_ff % (2 * LANE) == 0 else 1
    consts = (mod3, norm1_g, norm2_g, final_g, perm_t)
    weights = (w_gate, w_a, w_b, w_out, w_g, w_u, w_d)
    return pl.pallas_call(
        functools.partial(_mixer_kernel, n_chunks=n_chunks),
        out_shape=jax.ShapeDtypeStruct(x.shape, F32),
        grid=(seq // tt,),
        in_specs=[pl.BlockSpec((bsz, tt, d), lambda j: (0, j, 0))]
        + [_const_spec(a.shape) for a in consts]
        + [pl.BlockSpec((tt * bsz, d_s5), lambda j: (j, 0)),
           pl.BlockSpec((bsz, tt, d_hy), lambda j: (0, j, 0))]
        + [_const_spec(a.shape) for a in weights],
        out_specs=pl.BlockSpec((bsz, tt, d), lambda j: (0, j, 0)),
        compiler_params=_cparams(1),
        name="merge_swiglu_final_norm",
    )(x, *consts, za_tm, zb, *weights)


def kernel(x, c, ada_w, ada_b, norm1_g, norm2_g, w_in, s5_lam_re, s5_lam_im, s5_log_step, s5_b_re, s5_b_im, s5_c_re, s5_c_im, s5_d, s5_glu_w, s5_glu_b, hy_conv_w, hy_conv_b, hy_ffn_w1, hy_ffn_b1, hy_ffn_w2, hy_ffn_b2, hy_ffn_w3, hy_ffn_b3, hy_freq, hy_decay, hy_bias, w_branch_a, w_branch_b, w_out, ffn_w_gu, ffn_w_down, final_g):
    bsz, seq, d = x.shape
    depth = ada_w.shape[0]
    if depth != 1:
        raise NotImplementedError("the final RMSNorm is fused into the (single) layer's channel mixer")
    d_s5 = s5_d.shape[-1]
    n_order, d_hy = hy_bias.shape[1:]
    d_uh = d_s5 + (n_order + 1) * d_hy
    d_ff = ffn_w_down.shape[1]
    i = 0
    perm = _row_permutation(bsz, _time_tile(bsz, seq))
    mod = _modulation(c, ada_w[i], ada_b[i]).reshape(bsz, 6, d)
    w_in_b = w_in[i].astype(BF16)
    u_s5, u_hy = _in_projection(x, mod, norm1_g[i].reshape(1, d), w_in_b[:, :d_uh], perm, d_s5)
    z_a = _s5_branch(u_s5, bsz, s5_lam_re[i], s5_lam_im[i], s5_log_step[i], s5_b_re[i], s5_b_im[i],
                     s5_c_re[i], s5_c_im[i], s5_d[i], s5_glu_w[i], s5_glu_b[i])
    z_b = _hyena_branch(u_hy.reshape(bsz * seq, -1), bsz, seq, hy_conv_w[i], hy_conv_b[i], hy_ffn_w1[i],
                        hy_ffn_b1[i], hy_ffn_w2[i], hy_ffn_b2[i], hy_ffn_w3[i], hy_ffn_b3[i], hy_freq[i],
                        hy_decay[i], hy_bias[i])
    w_gu = ffn_w_gu[i].astype(BF16)
    return _mixer(x, mod, norm1_g[i].reshape(1, d), norm2_g[i].reshape(1, d), final_g.reshape(1, d),
                  perm.T, z_a, z_b.reshape(bsz, seq, d_hy), w_in_b[:, d_uh:],
                  w_branch_a[i].astype(BF16), w_branch_b[i].astype(BF16), w_out[i].astype(BF16),
                  w_gu[:, :d_ff], w_gu[:, d_ff:], ffn_w_down[i].astype(BF16))
```

```python
import functools
import math

import jax
import jax.numpy as jnp
from jax import lax
from jax.experimental import pallas as pl
from jax.experimental.pallas import tpu as pltpu

F32 = jnp.float32
BF16 = jnp.bfloat16
EPS = 1e-6
HIGHEST = lax.Precision.HIGHEST

V7X_VMEM_BYTES = 64 * 1024 * 1024
VMEM_LIMIT_BYTES = 56 * 1024 * 1024
LANE = 128
SUBLANE = 8
ROW_TILE = 512
S5_TIME_CHUNK = 64
S5_LANE_CHUNK = 512
S5_CH_SPLIT = 256
HY_CH_TILE = 256
DFT_ROW_CHUNK = 512


def _cparams(n_axes):
    return pltpu.CompilerParams(
        dimension_semantics=("arbitrary",) * n_axes,
        vmem_limit_bytes=VMEM_LIMIT_BYTES,
    )


def _const_spec(shape):
    nd = len(shape)
    return pl.BlockSpec(shape, lambda *_: (0,) * nd, pipeline_mode=pl.Buffered(1))


def _gelu_tanh(x):
    return 0.5 * x * (1.0 + jnp.tanh(math.sqrt(2.0 / math.pi) * (x + 0.044715 * (x * x * x))))


def _norm_modulate(x, g, shift, scale):
    ms = jnp.mean(x * x, axis=-1, keepdims=True)
    r = x * lax.rsqrt(ms + EPS) * g
    return r * (1.0 + scale) + shift


def _mod_kernel(c_ref, w_ref, b_ref, o_ref):
    c = c_ref[...]
    ca = c * jax.nn.sigmoid(c)
    o_ref[...] = jnp.dot(ca, w_ref[...], precision=HIGHEST, preferred_element_type=F32) + b_ref[...]


def _modulation(c, ada_w, ada_b):
    bsz, d = c.shape
    n = ada_w.shape[1]
    tn = 512
    return pl.pallas_call(
        _mod_kernel,
        out_shape=jax.ShapeDtypeStruct((bsz, n), F32),
        grid=(n // tn,),
        in_specs=[
            pl.BlockSpec((bsz, d), lambda j: (0, 0)),
            pl.BlockSpec((d, tn), lambda j: (0, j)),
            pl.BlockSpec((1, tn), lambda j: (0, j)),
        ],
        out_specs=pl.BlockSpec((bsz, tn), lambda j: (0, j)),
        compiler_params=_cparams(1),
        name="adaln_mod",
    )(c, ada_w, ada_b.reshape(1, n))


def _time_tile(bsz, seq):
    return max(min(ROW_TILE // bsz, seq), 1)


def _row_permutation(bsz, tt):
    r = jnp.arange(bsz * tt, dtype=jnp.int32)
    src = (r % bsz) * tt + r // bsz
    return (src[:, None] == r[None, :]).astype(BF16)


def _inproj_kernel(x_ref, mod_ref, g_ref, w_ref, perm_ref, us5_ref, uhy_ref, *, d_s5):
    nb, tt, d = x_ref.shape
    h = _norm_modulate(x_ref[...], g_ref[...], mod_ref[:, 0:1, :], mod_ref[:, 1:2, :])
    hb = h.reshape(nb * tt, d).astype(BF16)
    p = jnp.dot(hb, w_ref[...], preferred_element_type=F32)
    us5_ref[...] = jnp.dot(perm_ref[...], p[:, :d_s5].astype(BF16),
                           preferred_element_type=F32).astype(us5_ref.dtype)
    uhy_ref[...] = p[:, d_s5:].astype(BF16).reshape(uhy_ref.shape)


def _in_projection(x, mod3, norm_g, w_uh, perm, d_s5):
    bsz, seq, d = x.shape
    n = w_uh.shape[1]
    tt = _time_tile(bsz, seq)
    return pl.pallas_call(
        functools.partial(_inproj_kernel, d_s5=d_s5),
        out_shape=(
            jax.ShapeDtypeStruct((seq * bsz, d_s5), BF16),
            jax.ShapeDtypeStruct((bsz, seq, n - d_s5), BF16),
        ),
        grid=(seq // tt,),
        in_specs=[
            pl.BlockSpec((bsz, tt, d), lambda j: (0, j, 0)),
            _const_spec(mod3.shape),
            _const_spec((1, d)),
            _const_spec((d, n)),
            _const_spec(perm.shape),
        ],
        out_specs=(
            pl.BlockSpec((tt * bsz, d_s5), lambda j: (j, 0)),
            pl.BlockSpec((bsz, tt, n - d_s5), lambda j: (0, j, 0)),
        ),
        compiler_params=_cparams(1),
        name="in_proj",
    )(x, mod3, norm_g, w_uh, perm)


def _s5_prep_kernel(lre_ref, lim_ref, lstep_ref, bre_ref, bim_ref, cim_ref,
                    are_ref, aim_ref, bbre_ref, bbim_ref, ncim_ref):
    step = jnp.exp(lstep_ref[...])
    lr = lre_ref[...]
    li = lim_ref[...]
    mag = jnp.exp(lr * step)
    ar = mag * jnp.cos(li * step)
    ai = mag * jnp.sin(li * step)
    num = ar - 1.0
    den = lr * lr + li * li
    cr = (num * lr + ai * li) / den
    ci = (ai * lr - num * li) / den
    are_ref[...] = ar
    aim_ref[...] = ai
    for d in range(lre_ref.shape[0]):
        br = bre_ref[d]
        bi = bim_ref[d]
        bbre_ref[d] = cr[d:d + 1, :] * br - ci[d:d + 1, :] * bi
        bbim_ref[d] = cr[d:d + 1, :] * bi + ci[d:d + 1, :] * br
    ncim_ref[...] = -cim_ref[...]


def _s5_prepare(lam_re, lam_im, log_step, b_re, b_im, c_im):
    outs = (
        jax.ShapeDtypeStruct(lam_re.shape, F32),
        jax.ShapeDtypeStruct(lam_re.shape, F32),
        jax.ShapeDtypeStruct(b_re.shape, F32),
        jax.ShapeDtypeStruct(b_re.shape, F32),
        jax.ShapeDtypeStruct(c_im.shape, F32),
    )
    return pl.pallas_call(
        _s5_prep_kernel,
        out_shape=outs,
        compiler_params=pltpu.CompilerParams(vmem_limit_bytes=VMEM_LIMIT_BYTES),
        name="s5_discretize",
    )(lam_re, lam_im, log_step, b_re, b_im, c_im)


def _block_diag(m):
    g, r, c = m.shape
    eye = jnp.eye(g, dtype=m.dtype)
    return (m[:, :, None, :] * eye[:, None, :, None]).reshape(g * r, g * c)


def _s5_scan_kernel(*refs, reverse, final, tc, nb):
    if final:
        (u_ref, bd_ref, cd_ref, are_ref, aim_ref, yprev_ref, gluw_ref, glub_ref,
         out_ref, bu_ref, sb_ref, s_ref) = refs
    else:
        (u_ref, bd_ref, cd_ref, are_ref, aim_ref, d_ref, out_ref, bu_ref, sb_ref, s_ref) = refs
    rows = tc * nb
    nsplit, ks, two_nss = bd_ref.shape
    nss = two_nss // 2

    @pl.when(pl.program_id(0) == 0)
    def _():
        s_ref[...] = jnp.zeros_like(s_ref)

    for h in range(nsplit):
        ub = u_ref[:, h * ks:(h + 1) * ks].astype(BF16)
        bu_ref[:, h * two_nss:(h + 1) * two_nss] = jnp.dot(ub, bd_ref[h], preferred_element_type=F32)

    lc = min(S5_LANE_CHUNK, nss)
    for h in range(nsplit):
        for c in range(nss // lc):
            a_sl = slice(h * nss + c * lc, h * nss + (c + 1) * lc)
            re_sl = slice(h * two_nss + c * lc, h * two_nss + (c + 1) * lc)
            im_sl = slice(h * two_nss + nss + c * lc, h * two_nss + nss + (c + 1) * lc)
            a_re = are_ref[:, a_sl]
            a_im = aim_ref[:, a_sl]

            def body(k, carry, re_sl=re_sl, im_sl=im_sl, a_re=a_re, a_im=a_im):
                sr, si = carry
                t = (tc - 1 - k) if reverse else k
                row = pl.ds(pl.multiple_of(t * nb, nb), nb)
                nr = a_re * sr - a_im * si + bu_ref[row, re_sl]
                ni = a_re * si + a_im * sr + bu_ref[row, im_sl]
                sb_ref[row, re_sl] = nr.astype(BF16)
                sb_ref[row, im_sl] = ni.astype(BF16)
                return nr, ni

            sr, si = lax.fori_loop(0, tc, body, (s_ref[:, re_sl], s_ref[:, im_sl]), unroll=8)
            s_ref[:, re_sl] = sr
            s_ref[:, im_sl] = si

    rb = min(256, rows)
    for h in range(nsplit):
        y = jnp.dot(sb_ref[:, h * two_nss:(h + 1) * two_nss], cd_ref[h], preferred_element_type=F32)
        if final:
            bu_ref[:, h * ks:(h + 1) * ks] = y
        else:
            out_ref[:, h * ks:(h + 1) * ks] = y + u_ref[:, h * ks:(h + 1) * ks] * d_ref[:, h * ks:(h + 1) * ks]
    if final:
        d_s5 = nsplit * ks
        for r in range(rows // rb):
            rs = slice(r * rb, (r + 1) * rb)
            z = _gelu_tanh(yprev_ref[rs, :] + bu_ref[rs, :d_s5])
            gate = jnp.dot(z.astype(BF16), gluw_ref[...], preferred_element_type=F32) + glub_ref[...]
            out_ref[rs, :] = (z * jax.nn.sigmoid(gate)).astype(out_ref.dtype)


def _s5_scan(u_tm, bd, cd, a_re, a_im, extra, *, reverse, final, nb, out_dtype):
    rows_total, d_s5 = u_tm.shape
    seq = rows_total // nb
    two_ns = bd.shape[0] * bd.shape[2]
    tc = min(S5_TIME_CHUNK, seq)
    nchunk = seq // tc
    rows = tc * nb
    if reverse:
        cidx = lambda i: (nchunk - 1 - i, 0)
    else:
        cidx = lambda i: (i, 0)
    in_specs = [
        pl.BlockSpec((rows, d_s5), cidx),
        _const_spec(bd.shape),
        _const_spec(cd.shape),
        _const_spec(a_re.shape),
        _const_spec(a_im.shape),
    ]
    if final:
        yprev, glu_w, glu_b = extra
        in_specs += [pl.BlockSpec((rows, d_s5), cidx), _const_spec(glu_w.shape), _const_spec(glu_b.shape)]
    else:
        in_specs += [_const_spec(extra[0].shape)]
    return pl.pallas_call(
        functools.partial(_s5_scan_kernel, reverse=reverse, final=final, tc=tc, nb=nb),
        out_shape=jax.ShapeDtypeStruct((rows_total, d_s5), out_dtype),
        grid=(nchunk,),
        in_specs=in_specs,
        out_specs=pl.BlockSpec((rows, d_s5), cidx),
        scratch_shapes=[pltpu.VMEM((rows, two_ns), F32), pltpu.VMEM((rows, two_ns), BF16),
                        pltpu.VMEM((nb, two_ns), F32)],
        compiler_params=_cparams(1),
        name="s5_scan_bwd_glu" if final else "s5_scan_fwd",
    )(u_tm, bd, cd, a_re, a_im, *extra)


def _s5_branch(u_tm, nb, lam_re, lam_im, log_step, b_re, b_im, c_re, c_im, d, glu_w, glu_b):
    ndir, g, p = lam_re.shape
    grp = b_re.shape[-1]
    ns = g * p
    d_s5 = g * grp
    ks = S5_CH_SPLIT if d_s5 % S5_CH_SPLIT == 0 else d_s5
    nsplit = d_s5 // ks
    gs = g // nsplit
    flat = lambda a: a.reshape(ndir, ns)
    to_lanes = lambda a: jnp.transpose(a, (0, 3, 1, 2)).reshape(ndir, grp, ns)
    a_re, a_im, bb_re, bb_im, ncim = _s5_prepare(
        flat(lam_re), flat(lam_im), jnp.repeat(log_step, p, axis=-1),
        to_lanes(b_re), to_lanes(b_im),
        jnp.transpose(c_im, (0, 2, 1, 3)).reshape(ndir, grp, ns))

    def in_blocks(a):
        a = jnp.transpose(a.reshape(grp, nsplit, gs, p), (1, 2, 0, 3))
        return jnp.stack([_block_diag(a[h]) for h in range(nsplit)])

    def out_blocks(a):
        a = jnp.transpose(a.reshape(nsplit, gs, grp, p), (0, 1, 3, 2))
        return jnp.stack([_block_diag(a[h]) for h in range(nsplit)])

    y = None
    for direction in range(ndir):
        bd = jnp.concatenate([in_blocks(bb_re[direction]), in_blocks(bb_im[direction])], axis=2).astype(BF16)
        ncim_g = jnp.transpose(ncim[direction].reshape(grp, g, p), (1, 0, 2))
        cd = jnp.concatenate([out_blocks(c_re[direction]), out_blocks(ncim_g)], axis=1).astype(BF16)
        are = jnp.broadcast_to(a_re[direction][None, :], (nb, ns))
        aim = jnp.broadcast_to(a_im[direction][None, :], (nb, ns))
        if direction == 0:
            if ndir != 2:
                raise NotImplementedError("S5 branch expects forward and backward directions")
            y = _s5_scan(u_tm, bd, cd, are, aim, (d.reshape(1, -1),), reverse=False, final=False,
                         nb=nb, out_dtype=F32)
        else:
            extra = (y, glu_w.astype(BF16), glu_b.reshape(1, -1))
            y = _s5_scan(u_tm, bd, cd, are, aim, extra, reverse=True, final=True, nb=nb, out_dtype=BF16)
    return y


def _dft_matrices(seq):
    n = 2 * seq
    f_lo = 64 if seq % 64 == 0 else 1
    f_hi = seq // f_lo
    t = jnp.arange(seq, dtype=jnp.int32)[None, :]
    ka = (jnp.arange(f_hi, dtype=jnp.int32)[:, None] * f_lo * t) % n
    kb = (jnp.arange(f_lo, dtype=jnp.int32)[:, None] * t) % n
    w = 2.0 * math.pi / n
    aa = ka.astype(F32) * w
    ab = kb.astype(F32) * w
    ca, sa, cb, sb = jnp.cos(aa), jnp.sin(aa), jnp.cos(ab), jnp.sin(ab)
    cm = ca[:, None, :] * cb[None, :, :] - sa[:, None, :] * sb[None, :, :]
    sm = sa[:, None, :] * cb[None, :, :] + ca[:, None, :] * sb[None, :, :]
    return cm.reshape(seq, seq).astype(BF16), sm.reshape(seq, seq).astype(BF16)


def _hyena_tables(seq, ct):
    half = seq // 2
    cm, sm = _dft_matrices(half)
    alt = (1 - 2 * (jnp.arange(half, dtype=jnp.int32) & 1)).astype(BF16)
    ang = jnp.arange(half, dtype=F32)[:, None] * (math.pi / seq)
    return dict(cm=cm, sm=sm, sm_fwd=sm.at[0, :].set(alt), sm_inv=sm.at[:, 0].set(alt),
                cw=jnp.broadcast_to(jnp.cos(ang), (half, ct)), sw=jnp.broadcast_to(jnp.sin(ang), (half, ct)))


def _store_slabs(dst_ref, slab0, x):
    for j in range(x.shape[1] // LANE):
        dst_ref[slab0 + j] = x[:, j * LANE:(j + 1) * LANE]


def _deinterleave_rows(src_ref, slab0, nsl, dst_ref, col0):
    half = src_ref.shape[1] // 2
    ct = nsl * LANE
    for j in range(nsl):
        for par in range(2):
            c = col0 + par * ct + j * LANE
            dst_ref[:, c:c + LANE] = src_ref[slab0 + j, pl.ds(par, half, stride=2), :].astype(BF16)


def _mid_frequency(x):
    r = lax.broadcasted_iota(jnp.int32, (x.shape[0], 1), 0) & 3
    me = jnp.where(r == 0, 1.0, jnp.where(r == 2, -1.0, 0.0))
    mo = jnp.where(r == 1, 1.0, jnp.where(r == 3, -1.0, 0.0))
    return jnp.sum(x * me, axis=0, keepdims=True), jnp.sum(x * mo, axis=0, keepdims=True)


def _split_spectrum(c2, s2, cw, sw):
    ct = cw.shape[1]
    ec, oc = c2[:, :ct], c2[:, ct:]
    es, os_ = s2[:, :ct], s2[:, ct:]
    wc = cw * oc - sw * os_
    ws = cw * os_ + sw * oc
    return ec + wc, es + ws, ec - wc, ws - es


def _hy_filter_kernel(feat_ref, w1_ref, b1_ref, w2_ref, b2_ref, freq_ref,
                      w3f_ref, b3f_ref, decf_ref, w3b_ref, b3b_ref, decb_ref,
                      cm_ref, sm_ref, cw_ref, sw_ref,
                      kpc_ref, kps_ref, kqc_ref, kqs_ref, km_ref, taps_s, rhs_s):
    seq = feat_ref.shape[0]
    half = seq // 2
    n = 2 * seq
    ct = cw_ref.shape[1]
    nsl = ct // LANE
    feats = feat_ref[...]
    f = freq_ref[...]
    h = jnp.sin(f * (jnp.dot(feats, w1_ref[...], precision=HIGHEST, preferred_element_type=F32) + b1_ref[...]))
    h = jnp.sin(f * (jnp.dot(h, w2_ref[...], precision=HIGHEST, preferred_element_type=F32) + b2_ref[...]))
    t01 = feats[:, 0:1]
    row = lax.broadcasted_iota(jnp.int32, (seq, 1), 0)

    def taps(w3_ref, b3_ref, dec_ref):
        v = jnp.dot(h, w3_ref[...], precision=HIGHEST, preferred_element_type=F32) + b3_ref[...]
        return v * jnp.exp(-t01 * jnp.abs(dec_ref[...]))

    fwd = taps(w3f_ref, b3f_ref, decf_ref)
    bwd = jnp.where(row == 0, 0.0, taps(w3b_ref, b3b_ref, decb_ref))
    _store_slabs(taps_s, 0, fwd)
    _store_slabs(taps_s, nsl, bwd)
    _deinterleave_rows(taps_s, 0, nsl, rhs_s, 0)
    _deinterleave_rows(taps_s, nsl, nsl, rhs_s, 2 * ct)
    c4 = jnp.dot(cm_ref[...], rhs_s[...], preferred_element_type=F32)
    s4 = jnp.dot(sm_ref[...], rhs_s[...], preferred_element_type=F32)
    cw = cw_ref[...]
    sw = sw_ref[...]
    fpc, fps, fqc, fqs = _split_spectrum(c4[:, :2 * ct], s4[:, :2 * ct], cw, sw)
    bpc, bps, bqc, bqs = _split_spectrum(c4[:, 2 * ct:], s4[:, 2 * ct:], cw, sw)
    scale = jnp.where(lax.broadcasted_iota(jnp.int32, (half, 1), 0) == 0, 1.0 / n, 2.0 / n)
    kpc_ref[...] = (fpc + bpc) * scale
    kps_ref[...] = (fps - bps) * scale
    kqc_ref[...] = (fqc + bqc) * scale
    kqs_ref[...] = (fqs - bqs) * scale
    fmc, fms = _mid_frequency(fwd)
    bmc, bms = _mid_frequency(bwd)
    km_ref[...] = jnp.zeros_like(km_ref)
    km_ref[0:1, :] = (fmc + bmc) * (2.0 / n)
    km_ref[1:2, :] = (fms - bms) * (2.0 / n)


def _hyena_filters(seq, w1, b1, w2, b2, w3, b3, freq, decay, tables, n_order, n_dirs, d_hy):
    cm, sm, cw, sw = tables["cm"], tables["sm"], tables["cw"], tables["sw"]
    half = seq // 2
    emb, hid = w1.shape
    bands = (emb - 1) // 2
    t = jnp.arange(seq, dtype=F32)
    t01 = t / max(seq - 1, 1)
    band = jnp.linspace(1e-4, bands - 1, bands, dtype=F32)
    ang = (2.0 * math.pi) * t[:, None] * band[None, :] / seq
    feats = jnp.concatenate([t01[:, None], jnp.cos(ang), jnp.sin(ang)], axis=-1)
    kpad = LANE
    feats = jnp.pad(feats, ((0, 0), (0, kpad - emb)))
    w1p = jnp.pad(w1, ((0, kpad - emb), (0, 0)))
    ct = cw.shape[1]
    nct = d_hy // ct
    ncol = n_order * d_hy
    b3r = b3.reshape(1, -1)
    decr = decay.reshape(1, -1)
    fcol = lambda o, c: (0, (o * n_dirs + 0) * nct + c)
    bcol = lambda o, c: (0, (o * n_dirs + 1) * nct + c)
    ocol = lambda o, c: (0, o * nct + c)
    full = lambda a: pl.BlockSpec(a.shape, lambda o, c: (0,) * a.ndim)
    spec = jax.ShapeDtypeStruct((half, ncol), F32)
    return pl.pallas_call(
        _hy_filter_kernel,
        out_shape=(spec, spec, spec, spec, jax.ShapeDtypeStruct((8, ncol), F32)),
        grid=(n_order, nct),
        in_specs=[
            full(feats), full(w1p), full(b1.reshape(1, -1)), full(w2), full(b2.reshape(1, -1)),
            full(freq.reshape(1, -1)),
            pl.BlockSpec((hid, ct), fcol), pl.BlockSpec((1, ct), fcol), pl.BlockSpec((1, ct), fcol),
            pl.BlockSpec((hid, ct), bcol), pl.BlockSpec((1, ct), bcol), pl.BlockSpec((1, ct), bcol),
            _const_spec(cm.shape), _const_spec(sm.shape), _const_spec(cw.shape), _const_spec(sw.shape),
        ],
        out_specs=tuple(pl.BlockSpec((half, ct), ocol) for _ in range(4)) + (pl.BlockSpec((8, ct), ocol),),
        scratch_shapes=[pltpu.VMEM((2 * ct // LANE, seq, LANE), F32), pltpu.VMEM((half, 4 * ct), BF16)],
        compiler_params=_cparams(2),
        name="hyena_filter_spectra",
    )(feats, w1p, b1.reshape(1, -1), w2, b2.reshape(1, -1), freq.reshape(1, -1),
      w3, b3r, decr, w3, b3r, decr, cm, sm, cw, sw)


def _hy_conv_kernel(zin_ref, gin_ref, wz_ref, bz_ref, wg_ref, bg_ref,
                    kpc_ref, kps_ref, kqc_ref, kqs_ref, km_ref, bias_ref,
                    cm_ref, smf_ref, smi_ref, cw_ref, sw_ref, out_ref,
                    zraw_s, graw_s, o_s, z_s, g_s, rhs_s, ac_s, as_s, *, conv_on_z):
    seq, ct = zin_ref.shape
    half = seq // 2
    nsl = ct // LANE
    pad = zraw_s.shape[1] - seq
    top = pad // 2
    rc = min(DFT_ROW_CHUNK, half)

    def stage(raw_s, src_ref):
        for j in range(nsl):
            raw_s[j, 0:top, :] = jnp.zeros((top, LANE), F32)
            raw_s[j, top + seq:, :] = jnp.zeros((pad - top, LANE), F32)
            raw_s[j, top:top + seq, :] = src_ref[:, j * LANE:(j + 1) * LANE].astype(F32)

    def split_rows(raw_s, dst_ref, w_ref, b_ref):
        for par in range(2):
            for j in range(nsl):
                ls = slice(j * LANE, (j + 1) * LANE)
                tap = lambda k: raw_s[j, pl.ds(top + par + k, half, stride=2), :]
                if w_ref is None:
                    v = tap(0)
                else:
                    v = (b_ref[:, ls] + tap(-1) * w_ref[0:1, ls] + tap(0) * w_ref[1:2, ls]
                         + tap(1) * w_ref[2:3, ls])
                dst_ref[:, par * ct + j * LANE:par * ct + (j + 1) * LANE] = v

    stage(zraw_s, zin_ref)
    stage(graw_s, gin_ref)
    split_rows(zraw_s, z_s, wz_ref if conv_on_z else None, bz_ref)
    split_rows(graw_s, g_s, wg_ref, bg_ref)
    rhs_s[...] = z_s[...].astype(BF16)

    for i in range(half // rc):
        r = slice(i * rc, (i + 1) * rc)
        c2 = jnp.dot(cm_ref[r, :], rhs_s[...], preferred_element_type=F32)
        s2 = jnp.dot(smf_ref[r, :], rhs_s[...], preferred_element_type=F32)
        if i == 0:
            row0 = lax.broadcasted_iota(jnp.int32, (rc, 1), 0) == 0
            zmc, zms = s2[0:1, :ct], s2[0:1, ct:]
            ymc = zmc * km_ref[0:1, :] - zms * km_ref[1:2, :]
            yms = zmc * km_ref[1:2, :] + zms * km_ref[0:1, :]
            s2 = jnp.where(row0, 0.0, s2)
        cw = cw_ref[r, :]
        sw = sw_ref[r, :]
        pc, ps, qc, qs = _split_spectrum(c2, s2, cw, sw)
        kpc, kps, kqc, kqs = kpc_ref[r, :], kps_ref[r, :], kqc_ref[r, :], kqs_ref[r, :]
        ypc = pc * kpc - ps * kps
        yps = pc * kps + ps * kpc
        yqc = qc * kqc - qs * kqs
        yqs = qc * kqs + qs * kqc
        dc = ypc - yqc
        ds = yps + yqs
        a0s = yps - yqs
        a1s = cw * ds - sw * dc
        if i == 0:
            a0s = jnp.where(row0, ymc, a0s)
            a1s = jnp.where(row0, yms, a1s)
        ac_s[r, :ct] = (ypc + yqc).astype(BF16)
        as_s[r, :ct] = a0s.astype(BF16)
        ac_s[r, ct:] = (cw * dc + sw * ds).astype(BF16)
        as_s[r, ct:] = a1s.astype(BF16)

    for i in range(half // rc):
        r = slice(i * rc, (i + 1) * rc)
        y2 = jnp.dot(cm_ref[r, :], ac_s[...], preferred_element_type=F32)
        y2 = y2 + jnp.dot(smi_ref[r, :], as_s[...], preferred_element_type=F32)
        for par in range(2):
            rows = pl.ds(2 * i * rc + par, rc, stride=2)
            for j in range(nsl):
                ls = slice(par * ct + j * LANE, par * ct + (j + 1) * LANE)
                o_s[j, rows, :] = g_s[r, ls] * (y2[:, ls] + z_s[r, ls] * bias_ref[:, j * LANE:(j + 1) * LANE])
    for j in range(nsl):
        out_ref[:, j * LANE:(j + 1) * LANE] = o_s[j].astype(out_ref.dtype)


def _hy_conv(zin, zcol0, gcol0, u_hy, conv_w, conv_b, spectra, bias_row, tables,
             *, order, conv_on_z, bsz, seq, d_hy):
    cm, smf, smi, cw, sw = (tables[k] for k in ("cm", "sm_fwd", "sm_inv", "cw", "sw"))
    kpc, kps, kqc, kqs, km = spectra
    half = seq // 2
    ct = cw.shape[1]
    nsl = ct // LANE
    nct = d_hy // ct
    zc0 = zcol0 // ct
    gc0 = gcol0 // ct
    zw0 = zc0 if conv_on_z else 0
    kcol = lambda c, b: (0, order * nct + c)
    return pl.pallas_call(
        functools.partial(_hy_conv_kernel, conv_on_z=conv_on_z),
        out_shape=jax.ShapeDtypeStruct((bsz * seq, d_hy), BF16),
        grid=(nct, bsz),
        in_specs=[
            pl.BlockSpec((seq, ct), lambda c, b: (b, zc0 + c)),
            pl.BlockSpec((seq, ct), lambda c, b: (b, gc0 + c)),
            pl.BlockSpec((conv_w.shape[0], ct), lambda c, b: (0, zw0 + c)),
            pl.BlockSpec((1, ct), lambda c, b: (0, zw0 + c)),
            pl.BlockSpec((conv_w.shape[0], ct), lambda c, b: (0, gc0 + c)),
            pl.BlockSpec((1, ct), lambda c, b: (0, gc0 + c)),
            pl.BlockSpec((half, ct), kcol), pl.BlockSpec((half, ct), kcol),
            pl.BlockSpec((half, ct), kcol), pl.BlockSpec((half, ct), kcol),
            pl.BlockSpec((8, ct), kcol), pl.BlockSpec((1, ct), kcol),
            _const_spec(cm.shape), _const_spec(smf.shape), _const_spec(smi.shape),
            _const_spec(cw.shape), _const_spec(sw.shape),
        ],
        out_specs=pl.BlockSpec((seq, ct), lambda c, b: (b, c)),
        scratch_shapes=[
            pltpu.VMEM((nsl, seq + 2 * SUBLANE, LANE), F32), pltpu.VMEM((nsl, seq + 2 * SUBLANE, LANE), F32),
            pltpu.VMEM((nsl, seq, LANE), F32),
            pltpu.VMEM((half, 2 * ct), F32), pltpu.VMEM((half, 2 * ct), F32),
            pltpu.VMEM((half, 2 * ct), BF16), pltpu.VMEM((half, 2 * ct), BF16), pltpu.VMEM((half, 2 * ct), BF16),
        ],
        compiler_params=_cparams(2),
        name=f"hyena_conv_order{order}",
    )(zin, u_hy, conv_w, conv_b, conv_w, conv_b, kpc, kps, kqc, kqs, km, bias_row, cm, smf, smi, cw, sw)


def _hyena_branch(u_hy, bsz, seq, conv_w, conv_b, w1, b1, w2, b2, w3, b3, freq, decay, bias):
    n_order, d_hy = bias.shape
    n_dirs = w3.shape[1] // (n_order * d_hy)
    if n_order != 2 or seq % 4 != 0 or d_hy % LANE != 0:
        raise NotImplementedError("Hyena branch: two long convolutions, L % 4 == 0, 128-lane channel tiles")
    tables = _hyena_tables(seq, min(HY_CH_TILE, d_hy))
    spectra = _hyena_filters(seq, w1, b1, w2, b2, w3, b3, freq, decay, tables, n_order, n_dirs, d_hy)
    cb = conv_b.reshape(1, -1)
    bias_row = bias.reshape(1, -1)
    common = dict(bsz=bsz, seq=seq, d_hy=d_hy)
    z1 = _hy_conv(u_hy, 0, d_hy, u_hy, conv_w, cb, spectra, bias_row, tables,
                  order=0, conv_on_z=True, **common)
    return _hy_conv(z1, 0, 2 * d_hy, u_hy, conv_w, cb, spectra, bias_row, tables,
                    order=1, conv_on_z=False, **common)


def _mixer_kernel(x_ref, mod_ref, g1_ref, g2_ref, fg_ref, permt_ref, za_ref, zb_ref,
                  wgate_ref, wa_ref, wb_ref, wout_ref, wg_ref, wu_ref, wd_ref, o_ref, *, n_chunks):
    nb, tt, d = x_ref.shape
    rows = nb * tt
    mod = lambda k: mod_ref[:, k:k + 1, :]
    x = x_ref[...]
    h = _norm_modulate(x, g1_ref[...], mod(0), mod(1)).reshape(rows, d).astype(BF16)
    gate = jax.nn.sigmoid(jnp.dot(h, wgate_ref[...], preferred_element_type=F32))
    za = jnp.dot(permt_ref[...], za_ref[...], preferred_element_type=F32).astype(BF16)
    ya = jnp.dot(za, wa_ref[...], preferred_element_type=F32)
    yb = jnp.dot(zb_ref[...].reshape(rows, zb_ref.shape[-1]), wb_ref[...], preferred_element_type=F32)
    merged = gate[:, :d] * ya + gate[:, d:] * yb
    o = jnp.dot(merged.astype(BF16), wout_ref[...], preferred_element_type=F32)
    x1 = x + mod(2) * o.reshape(nb, tt, d)

    h2 = _norm_modulate(x1, g2_ref[...], mod(3), mod(4)).reshape(rows, d).astype(BF16)
    fc = wg_ref.shape[1] // n_chunks
    acc = jnp.zeros((rows, d), F32)
    for c in range(n_chunks):
        sl = slice(c * fc, (c + 1) * fc)
        gl = jnp.dot(h2, wg_ref[:, sl], preferred_element_type=F32)
        up = jnp.dot(h2, wu_ref[:, sl], preferred_element_type=F32)
        act = (gl * jax.nn.sigmoid(gl) * up).astype(BF16)
        acc = acc + jnp.dot(act, wd_ref[sl, :], preferred_element_type=F32)
    x2 = x1 + mod(5) * acc.reshape(nb, tt, d)
    ms = jnp.mean(x2 * x2, axis=-1, keepdims=True)
    o_ref[...] = x2 * lax.rsqrt(ms + EPS) * fg_ref[...]


def _mixer(x, mod3, norm1_g, norm2_g, final_g, perm_t, za_tm, zb, w_gate, w_a, w_b, w_out, w_g, w_u, w_d):
    bsz, seq, d = x.shape
    d_s5 = w_a.shape[0]
    d_hy = w_b.shape[0]
    d_ff = w_g.shape[1]
    tt = _time_tile(bsz, seq)
    n_chunks = 2 if d_ff % (2 * LANE) == 0 else 1
    consts = (mod3, norm1_g, norm2_g, final_g, perm_t)
    weights = (w_gate, w_a, w_b, w_out, w_g, w_u, w_d)
    return pl.pallas_call(
        functools.partial(_mixer_kernel, n_chunks=n_chunks),
        out_shape=jax.ShapeDtypeStruct(x.shape, F32),
        grid=(seq // tt,),
        in_specs=[pl.BlockSpec((bsz, tt, d), lambda j: (0, j, 0))]
        + [_const_spec(a.shape) for a in consts]
        + [pl.BlockSpec((tt * bsz, d_s5), lambda j: (j, 0)),
           pl.BlockSpec((bsz, tt, d_hy), lambda j: (0, j, 0))]
        + [_const_spec(a.shape) for a in weights],
        out_specs=pl.BlockSpec((bsz, tt, d), lambda j: (0, j, 0)),
        compiler_params=_cparams(1),
        name="merge_swiglu_final_norm",
    )(x, *consts, za_tm, zb, *weights)


def kernel(x, c, ada_w, ada_b, norm1_g, norm2_g, w_in, s5_lam_re, s5_lam_im, s5_log_step, s5_b_re, s5_b_im, s5_c_re, s5_c_im, s5_d, s5_glu_w, s5_glu_b, hy_conv_w, hy_conv_b, hy_ffn_w1, hy_ffn_b1, hy_ffn_w2, hy_ffn_b2, hy_ffn_w3, hy_ffn_b3, hy_freq, hy_decay, hy_bias, w_branch_a, w_branch_b, w_out, ffn_w_gu, ffn_w_down, final_g):
    bsz, seq, d = x.shape
    depth = ada_w.shape[0]
    if depth != 1:
        raise NotImplementedError("the final RMSNorm is fused into the (single) layer's channel mixer")
    d_s5 = s5_d.shape[-1]
    n_order, d_hy = hy_bias.shape[1:]
    d_uh = d_s5 + (n_order + 1) * d_hy
    d_ff = ffn_w_down.shape[1]
    i = 0
    perm = _row_permutation(bsz, _time_tile(bsz, seq))
    mod = _modulation(c, ada_w[i], ada_b[i]).reshape(bsz, 6, d)
    w_in_b = w_in[i].astype(BF16)
    u_s5, u_hy = _in_projection(x, mod, norm1_g[i].reshape(1, d), w_in_b[:, :d_uh], perm, d_s5)
    z_a = _s5_branch(u_s5, bsz, s5_lam_re[i], s5_lam_im[i], s5_log_step[i], s5_b_re[i], s5_b_im[i],
                     s5_c_re[i], s5_c_im[i], s5_d[i], s5_glu_w[i], s5_glu_b[i])
    z_b = _hyena_branch(u_hy.reshape(bsz * seq, -1), bsz, seq, hy_conv_w[i], hy_conv_b[i], hy_ffn_w1[i],
                        hy_ffn_b1[i], hy_ffn_w2[i], hy_ffn_b2[i], hy_ffn_w3[i], hy_ffn_b3[i], hy_freq[i],
                        hy_decay[i], hy_bias[i])
    w_gu = ffn_w_gu[i].astype(BF16)
    return _mixer(x, mod, norm1_g[i].reshape(1, d), norm2_g[i].reshape(1, d), final_g.reshape(1, d),
                  perm.T, z_a, z_b.reshape(bsz, seq, d_hy), w_in_b[:, d_uh:],
                  w_branch_a[i].astype(BF16), w_branch_b[i].astype(BF16), w_out[i].astype(BF16),
                  w_gu[:, :d_ff], w_gu[:, d_ff:], ffn_w_down[i].astype(BF16))
```

```python
import functools
import math

import jax
import jax.numpy as jnp
from jax import lax
from jax.experimental import pallas as pl
from jax.experimental.pallas import tpu as pltpu

F32 = jnp.float32
BF16 = jnp.bfloat16
EPS = 1e-6
HIGHEST = lax.Precision.HIGHEST

V7X_VMEM_BYTES = 64 * 1024 * 1024
VMEM_LIMIT_BYTES = 56 * 1024 * 1024
LANE = 128
SUBLANE = 8
ROW_TILE = 512
S5_TIME_CHUNK = 64
S5_LANE_CHUNK = 512
S5_CH_SPLIT = 256
S5_STEPS_PER_GROUP = 8
MXU_TILE = 256
HY_CH_TILE = 256
DFT_ROW_CHUNK = 512


def _cparams(n_axes):
    return pltpu.CompilerParams(
        dimension_semantics=("arbitrary",) * n_axes,
        vmem_limit_bytes=VMEM_LIMIT_BYTES,
    )


def _const_spec(shape):
    nd = len(shape)
    return pl.BlockSpec(shape, lambda *_: (0,) * nd, pipeline_mode=pl.Buffered(1))


def _gelu_tanh(x):
    return 0.5 * x * (1.0 + jnp.tanh(math.sqrt(2.0 / math.pi) * (x + 0.044715 * (x * x * x))))


def _norm_modulate(x, g, shift, scale):
    ms = jnp.mean(x * x, axis=-1, keepdims=True)
    r = x * lax.rsqrt(ms + EPS) * g
    return r * (1.0 + scale) + shift


def _mod_kernel(c_ref, w_ref, b_ref, o_ref):
    c = c_ref[...]
    ca = c * jax.nn.sigmoid(c)
    o_ref[...] = jnp.dot(ca, w_ref[...], precision=HIGHEST, preferred_element_type=F32) + b_ref[...]


def _modulation(c, ada_w, ada_b):
    bsz, d = c.shape
    n = ada_w.shape[1]
    tn = 512
    return pl.pallas_call(
        _mod_kernel,
        out_shape=jax.ShapeDtypeStruct((bsz, n), F32),
        grid=(n // tn,),
        in_specs=[
            pl.BlockSpec((bsz, d), lambda j: (0, 0)),
            pl.BlockSpec((d, tn), lambda j: (0, j)),
            pl.BlockSpec((1, tn), lambda j: (0, j)),
        ],
        out_specs=pl.BlockSpec((bsz, tn), lambda j: (0, j)),
        compiler_params=_cparams(1),
        name="adaln_mod",
    )(c, ada_w, ada_b.reshape(1, n))


def _time_tile(bsz, seq):
    return max(min(ROW_TILE // bsz, seq), 1)


def _row_permutation(bsz, tt):
    r = jnp.arange(bsz * tt, dtype=jnp.int32)
    src = (r % bsz) * tt + r // bsz
    return (src[:, None] == r[None, :]).astype(BF16)


def _inproj_kernel(x_ref, mod_ref, g_ref, w_ref, perm_ref, us5_ref, uhy_ref, *, d_s5):
    nb, tt, d = x_ref.shape
    h = _norm_modulate(x_ref[...], g_ref[...], mod_ref[:, 0:1, :], mod_ref[:, 1:2, :])
    hb = h.reshape(nb * tt, d).astype(BF16)
    p = jnp.dot(hb, w_ref[...], preferred_element_type=F32)
    us5_ref[...] = jnp.dot(perm_ref[...], p[:, :d_s5].astype(BF16),
                           preferred_element_type=F32).astype(us5_ref.dtype)
    uhy_ref[...] = p[:, d_s5:].astype(BF16).reshape(uhy_ref.shape)


def _in_projection(x, mod3, norm_g, w_uh, perm, d_s5):
    bsz, seq, d = x.shape
    n = w_uh.shape[1]
    tt = _time_tile(bsz, seq)
    return pl.pallas_call(
        functools.partial(_inproj_kernel, d_s5=d_s5),
        out_shape=(
            jax.ShapeDtypeStruct((seq * bsz, d_s5), BF16),
            jax.ShapeDtypeStruct((bsz, seq, n - d_s5), BF16),
        ),
        grid=(seq // tt,),
        in_specs=[
            pl.BlockSpec((bsz, tt, d), lambda j: (0, j, 0)),
            _const_spec(mod3.shape),
            _const_spec((1, d)),
            _const_spec((d, n)),
            _const_spec(perm.shape),
        ],
        out_specs=(
            pl.BlockSpec((tt * bsz, d_s5), lambda j: (j, 0)),
            pl.BlockSpec((bsz, tt, n - d_s5), lambda j: (0, j, 0)),
        ),
        compiler_params=_cparams(1),
        name="in_proj",
    )(x, mod3, norm_g, w_uh, perm)


def _s5_prep_kernel(lre_ref, lim_ref, lstep_ref, bre_ref, bim_ref, cim_ref,
                    are_ref, aim_ref, bbre_ref, bbim_ref, ncim_ref):
    step = jnp.exp(lstep_ref[...])
    lr = lre_ref[...]
    li = lim_ref[...]
    mag = jnp.exp(lr * step)
    ar = mag * jnp.cos(li * step)
    ai = mag * jnp.sin(li * step)
    num = ar - 1.0
    den = lr * lr + li * li
    cr = (num * lr + ai * li) / den
    ci = (ai * lr - num * li) / den
    are_ref[...] = ar
    aim_ref[...] = ai
    for d in range(lre_ref.shape[0]):
        br = bre_ref[d]
        bi = bim_ref[d]
        bbre_ref[d] = cr[d:d + 1, :] * br - ci[d:d + 1, :] * bi
        bbim_ref[d] = cr[d:d + 1, :] * bi + ci[d:d + 1, :] * br
    ncim_ref[...] = -cim_ref[...]


def _s5_prepare(lam_re, lam_im, log_step, b_re, b_im, c_im):
    outs = (
        jax.ShapeDtypeStruct(lam_re.shape, F32),
        jax.ShapeDtypeStruct(lam_re.shape, F32),
        jax.ShapeDtypeStruct(b_re.shape, F32),
        jax.ShapeDtypeStruct(b_re.shape, F32),
        jax.ShapeDtypeStruct(c_im.shape, F32),
    )
    return pl.pallas_call(
        _s5_prep_kernel,
        out_shape=outs,
        compiler_params=pltpu.CompilerParams(vmem_limit_bytes=VMEM_LIMIT_BYTES),
        name="s5_discretize",
    )(lam_re, lam_im, log_step, b_re, b_im, c_im)


def _block_diag(m):
    g, r, c = m.shape
    eye = jnp.eye(g, dtype=m.dtype)
    return (m[:, :, None, :] * eye[:, None, :, None]).reshape(g * r, g * c)


def _spread(items, n):
    return [items[(len(items) * i) // n:(len(items) * (i + 1)) // n] for i in range(n)]


def _s5_scan_kernel(*refs, reverse, final, tc, nb):
    if final:
        (u_ref, bd_ref, cd_ref, are_ref, aim_ref, yprev_ref, gluw_ref, glub_ref,
         out_ref, bu_ref, sb_ref, y_ref, s_ref) = refs
    else:
        (u_ref, bd_ref, cd_ref, are_ref, aim_ref, d_ref, out_ref, bu_ref, sb_ref, y_ref, s_ref) = refs
    rows = tc * nb
    nsplit, ks, two_nss = bd_ref.shape
    nss = two_nss // 2
    lc = min(S5_LANE_CHUNK, nss)
    tw = min(MXU_TILE, lc)
    rh = rows // 2 if rows % (2 * SUBLANE * 2) == 0 else rows
    row_halves = [slice(r, r + rh) for r in range(0, rows, rh)]

    @pl.when(pl.program_id(0) == 0)
    def _():
        s_ref[...] = jnp.zeros_like(s_ref)

    units = [(h, q) for h in range(nsplit) for q in range(nss // lc)]

    def lanes(unit, part, off=0, width=None):
        h, q = unit
        start = h * two_nss + part * nss + q * lc + off
        return slice(start, start + (lc if width is None else width))

    def expand_pieces(unit):
        h, q = unit
        out = []
        for part in range(2):
            for off in range(0, lc, tw):
                for rs in row_halves:
                    def piece(part=part, off=off, rs=rs):
                        col = part * nss + q * lc + off
                        bu_ref[rs, lanes(unit, part, off, tw)] = jnp.dot(
                            u_ref[rs, h * ks:(h + 1) * ks], bd_ref[h, :, col:col + tw],
                            preferred_element_type=F32)
                    out.append(piece)
        return out

    def readout_pieces(unit):
        h, q = unit
        out = []
        for part in range(2):
            for off in range(0, lc, tw):
                for rs in row_halves:
                    first = q == 0 and part == 0 and off == 0
                    def piece(part=part, off=off, rs=rs, first=first):
                        k0 = part * nss + q * lc + off
                        v = jnp.dot(sb_ref[rs, lanes(unit, part, off, tw)], cd_ref[h, k0:k0 + tw, :],
                                    preferred_element_type=F32)
                        if first:
                            y_ref[rs, h * ks:(h + 1) * ks] = v
                        else:
                            y_ref[rs, h * ks:(h + 1) * ks] += v
                    out.append(piece)
        return out

    step_groups = _spread(list(range(tc)), max(tc // S5_STEPS_PER_GROUP, 1))
    for p in expand_pieces(units[0]):
        p()
    for c, unit in enumerate(units):
        mxu_work = (expand_pieces(units[c + 1]) if c + 1 < len(units) else []) + \
                   (readout_pieces(units[c - 1]) if c > 0 else [])
        mxu_groups = _spread(mxu_work, len(step_groups))
        a_sl = slice(unit[0] * nss + unit[1] * lc, unit[0] * nss + (unit[1] + 1) * lc)
        a_re = are_ref[:, a_sl]
        a_im = aim_ref[:, a_sl]
        re_sl, im_sl = lanes(unit, 0), lanes(unit, 1)
        sr = s_ref[:, re_sl]
        si = s_ref[:, im_sl]
        for steps, pieces in zip(step_groups, mxu_groups):
            for p in pieces:
                p()
            for k in steps:
                t = (tc - 1 - k) if reverse else k
                row = slice(t * nb, (t + 1) * nb)
                nr = a_re * sr - a_im * si + bu_ref[row, re_sl]
                ni = a_re * si + a_im * sr + bu_ref[row, im_sl]
                sb_ref[row, re_sl] = nr.astype(BF16)
                sb_ref[row, im_sl] = ni.astype(BF16)
                sr, si = nr, ni
        s_ref[:, re_sl] = sr
        s_ref[:, im_sl] = si
    for p in readout_pieces(units[-1]):
        p()

    rb = min(256, rows)
    for r in range(rows // rb):
        rs = slice(r * rb, (r + 1) * rb)
        if final:
            z = _gelu_tanh(yprev_ref[rs, :] + y_ref[rs, :])
            gate = jnp.dot(z.astype(BF16), gluw_ref[...], preferred_element_type=F32) + glub_ref[...]
            out_ref[rs, :] = (z * jax.nn.sigmoid(gate)).astype(out_ref.dtype)
        else:
            out_ref[rs, :] = y_ref[rs, :] + u_ref[rs, :] * d_ref[...]


def _s5_scan(u_tm, bd, cd, a_re, a_im, extra, *, reverse, final, nb, out_dtype):
    rows_total, d_s5 = u_tm.shape
    seq = rows_total // nb
    two_ns = bd.shape[0] * bd.shape[2]
    tc = min(S5_TIME_CHUNK, seq)
    nchunk = seq // tc
    rows = tc * nb
    if reverse:
        cidx = lambda i: (nchunk - 1 - i, 0)
    else:
        cidx = lambda i: (i, 0)
    in_specs = [
        pl.BlockSpec((rows, d_s5), cidx),
        _const_spec(bd.shape),
        _const_spec(cd.shape),
        _const_spec(a_re.shape),
        _const_spec(a_im.shape),
    ]
    if final:
        yprev, glu_w, glu_b = extra
        in_specs += [pl.BlockSpec((rows, d_s5), cidx), _const_spec(glu_w.shape), _const_spec(glu_b.shape)]
    else:
        in_specs += [_const_spec(extra[0].shape)]
    return pl.pallas_call(
        functools.partial(_s5_scan_kernel, reverse=reverse, final=final, tc=tc, nb=nb),
        out_shape=jax.ShapeDtypeStruct((rows_total, d_s5), out_dtype),
        grid=(nchunk,),
        in_specs=in_specs,
        out_specs=pl.BlockSpec((rows, d_s5), cidx),
        scratch_shapes=[pltpu.VMEM((rows, two_ns), F32), pltpu.VMEM((rows, two_ns), BF16),
                        pltpu.VMEM((rows, d_s5), F32), pltpu.VMEM((nb, two_ns), F32)],
        compiler_params=_cparams(1),
        name="s5_scan_bwd_glu" if final else "s5_scan_fwd",
    )(u_tm, bd, cd, a_re, a_im, *extra)


def _s5_branch(u_tm, nb, lam_re, lam_im, log_step, b_re, b_im, c_re, c_im, d, glu_w, glu_b):
    ndir, g, p = lam_re.shape
    grp = b_re.shape[-1]
    ns = g * p
    d_s5 = g * grp
    ks = S5_CH_SPLIT if d_s5 % S5_CH_SPLIT == 0 else d_s5
    nsplit = d_s5 // ks
    gs = g // nsplit
    flat = lambda a: a.reshape(ndir, ns)
    to_lanes = lambda a: jnp.transpose(a, (0, 3, 1, 2)).reshape(ndir, grp, ns)
    a_re, a_im, bb_re, bb_im, ncim = _s5_prepare(
        flat(lam_re), flat(lam_im), jnp.repeat(log_step, p, axis=-1),
        to_lanes(b_re), to_lanes(b_im),
        jnp.transpose(c_im, (0, 2, 1, 3)).reshape(ndir, grp, ns))

    def in_blocks(a):
        a = jnp.transpose(a.reshape(grp, nsplit, gs, p), (1, 2, 0, 3))
        return jnp.stack([_block_diag(a[h]) for h in range(nsplit)])

    def out_blocks(a):
        a = jnp.transpose(a.reshape(nsplit, gs, grp, p), (0, 1, 3, 2))
        return jnp.stack([_block_diag(a[h]) for h in range(nsplit)])

    y = None
    for direction in range(ndir):
        bd = jnp.concatenate([in_blocks(bb_re[direction]), in_blocks(bb_im[direction])], axis=2).astype(BF16)
        ncim_g = jnp.transpose(ncim[direction].reshape(grp, g, p), (1, 0, 2))
        cd = jnp.concatenate([out_blocks(c_re[direction]), out_blocks(ncim_g)], axis=1).astype(BF16)
        are = jnp.broadcast_to(a_re[direction][None, :], (nb, ns))
        aim = jnp.broadcast_to(a_im[direction][None, :], (nb, ns))
        if direction == 0:
            if ndir != 2:
                raise NotImplementedError("S5 branch expects forward and backward directions")
            y = _s5_scan(u_tm, bd, cd, are, aim, (d.reshape(1, -1),), reverse=False, final=False,
                         nb=nb, out_dtype=F32)
        else:
            extra = (y, glu_w.astype(BF16), glu_b.reshape(1, -1))
            y = _s5_scan(u_tm, bd, cd, are, aim, extra, reverse=True, final=True, nb=nb, out_dtype=BF16)
    return y


def _dft_matrices(seq):
    n = 2 * seq
    f_lo = 64 if seq % 64 == 0 else 1
    f_hi = seq // f_lo
    t = jnp.arange(seq, dtype=jnp.int32)[None, :]
    ka = (jnp.arange(f_hi, dtype=jnp.int32)[:, None] * f_lo * t) % n
    kb = (jnp.arange(f_lo, dtype=jnp.int32)[:, None] * t) % n
    w = 2.0 * math.pi / n
    aa = ka.astype(F32) * w
    ab = kb.astype(F32) * w
    ca, sa, cb, sb = jnp.cos(aa), jnp.sin(aa), jnp.cos(ab), jnp.sin(ab)
    cm = ca[:, None, :] * cb[None, :, :] - sa[:, None, :] * sb[None, :, :]
    sm = sa[:, None, :] * cb[None, :, :] + ca[:, None, :] * sb[None, :, :]
    return cm.reshape(seq, seq).astype(BF16), sm.reshape(seq, seq).astype(BF16)


def _hyena_tables(seq, ct):
    half = seq // 2
    cm, sm = _dft_matrices(half)
    alt = (1 - 2 * (jnp.arange(half, dtype=jnp.int32) & 1)).astype(BF16)
    ang = jnp.arange(half, dtype=F32)[:, None] * (math.pi / seq)
    return dict(cm=cm, sm=sm, sm_fwd=sm.at[0, :].set(alt), sm_inv=sm.at[:, 0].set(alt),
                cw=jnp.broadcast_to(jnp.cos(ang), (half, ct)), sw=jnp.broadcast_to(jnp.sin(ang), (half, ct)))


def _store_slabs(dst_ref, slab0, x):
    for j in range(x.shape[1] // LANE):
        dst_ref[slab0 + j] = x[:, j * LANE:(j + 1) * LANE]


def _deinterleave_rows(src_ref, slab0, nsl, dst_ref, col0):
    half = src_ref.shape[1] // 2
    ct = nsl * LANE
    for j in range(nsl):
        for par in range(2):
            c = col0 + par * ct + j * LANE
            dst_ref[:, c:c + LANE] = src_ref[slab0 + j, pl.ds(par, half, stride=2), :].astype(BF16)


def _mid_frequency(x):
    r = lax.broadcasted_iota(jnp.int32, (x.shape[0], 1), 0) & 3
    me = jnp.where(r == 0, 1.0, jnp.where(r == 2, -1.0, 0.0))
    mo = jnp.where(r == 1, 1.0, jnp.where(r == 3, -1.0, 0.0))
    return jnp.sum(x * me, axis=0, keepdims=True), jnp.sum(x * mo, axis=0, keepdims=True)


def _split_spectrum(c2, s2, cw, sw):
    ct = cw.shape[1]
    ec, oc = c2[:, :ct], c2[:, ct:]
    es, os_ = s2[:, :ct], s2[:, ct:]
    wc = cw * oc - sw * os_
    ws = cw * os_ + sw * oc
    return ec + wc, es + ws, ec - wc, ws - es


def _hy_filter_kernel(feat_ref, w1_ref, b1_ref, w2_ref, b2_ref, freq_ref,
                      w3f_ref, b3f_ref, decf_ref, w3b_ref, b3b_ref, decb_ref,
                      cm_ref, sm_ref, cw_ref, sw_ref,
                      kpc_ref, kps_ref, kqc_ref, kqs_ref, km_ref, taps_s, rhs_s):
    seq = feat_ref.shape[0]
    half = seq // 2
    n = 2 * seq
    ct = cw_ref.shape[1]
    nsl = ct // LANE
    feats = feat_ref[...]
    f = freq_ref[...]
    h = jnp.sin(f * (jnp.dot(feats, w1_ref[...], precision=HIGHEST, preferred_element_type=F32) + b1_ref[...]))
    h = jnp.sin(f * (jnp.dot(h, w2_ref[...], precision=HIGHEST, preferred_element_type=F32) + b2_ref[...]))
    t01 = feats[:, 0:1]
    row = lax.broadcasted_iota(jnp.int32, (seq, 1), 0)

    def taps(w3_ref, b3_ref, dec_ref):
        v = jnp.dot(h, w3_ref[...], precision=HIGHEST, preferred_element_type=F32) + b3_ref[...]
        return v * jnp.exp(-t01 * jnp.abs(dec_ref[...]))

    fwd = taps(w3f_ref, b3f_ref, decf_ref)
    bwd = jnp.where(row == 0, 0.0, taps(w3b_ref, b3b_ref, decb_ref))
    _store_slabs(taps_s, 0, fwd)
    _store_slabs(taps_s, nsl, bwd)
    _deinterleave_rows(taps_s, 0, nsl, rhs_s, 0)
    _deinterleave_rows(taps_s, nsl, nsl, rhs_s, 2 * ct)
    c4 = jnp.dot(cm_ref[...], rhs_s[...], preferred_element_type=F32)
    s4 = jnp.dot(sm_ref[...], rhs_s[...], preferred_element_type=F32)
    cw = cw_ref[...]
    sw = sw_ref[...]
    fpc, fps, fqc, fqs = _split_spectrum(c4[:, :2 * ct], s4[:, :2 * ct], cw, sw)
    bpc, bps, bqc, bqs = _split_spectrum(c4[:, 2 * ct:], s4[:, 2 * ct:], cw, sw)
    scale = jnp.where(lax.broadcasted_iota(jnp.int32, (half, 1), 0) == 0, 1.0 / n, 2.0 / n)
    kpc_ref[...] = (fpc + bpc) * scale
    kps_ref[...] = (fps - bps) * scale
    kqc_ref[...] = (fqc + bqc) * scale
    kqs_ref[...] = (fqs - bqs) * scale
    fmc, fms = _mid_frequency(fwd)
    bmc, bms = _mid_frequency(bwd)
    km_ref[...] = jnp.zeros_like(km_ref)
    km_ref[0:1, :] = (fmc + bmc) * (2.0 / n)
    km_ref[1:2, :] = (fms - bms) * (2.0 / n)


def _hyena_filters(seq, w1, b1, w2, b2, w3, b3, freq, decay, tables, n_order, n_dirs, d_hy):
    cm, sm, cw, sw = tables["cm"], tables["sm"], tables["cw"], tables["sw"]
    half = seq // 2
    emb, hid = w1.shape
    bands = (emb - 1) // 2
    t = jnp.arange(seq, dtype=F32)
    t01 = t / max(seq - 1, 1)
    band = jnp.linspace(1e-4, bands - 1, bands, dtype=F32)
    ang = (2.0 * math.pi) * t[:, None] * band[None, :] / seq
    feats = jnp.concatenate([t01[:, None], jnp.cos(ang), jnp.sin(ang)], axis=-1)
    kpad = LANE
    feats = jnp.pad(feats, ((0, 0), (0, kpad - emb)))
    w1p = jnp.pad(w1, ((0, kpad - emb), (0, 0)))
    ct = cw.shape[1]
    nct = d_hy // ct
    ncol = n_order * d_hy
    b3r = b3.reshape(1, -1)
    decr = decay.reshape(1, -1)
    fcol = lambda o, c: (0, (o * n_dirs + 0) * nct + c)
    bcol = lambda o, c: (0, (o * n_dirs + 1) * nct + c)
    ocol = lambda o, c: (0, o * nct + c)
    full = lambda a: pl.BlockSpec(a.shape, lambda o, c: (0,) * a.ndim)
    spec = jax.ShapeDtypeStruct((half, ncol), F32)
    return pl.pallas_call(
        _hy_filter_kernel,
        out_shape=(spec, spec, spec, spec, jax.ShapeDtypeStruct((8, ncol), F32)),
        grid=(n_order, nct),
        in_specs=[
            full(feats), full(w1p), full(b1.reshape(1, -1)), full(w2), full(b2.reshape(1, -1)),
            full(freq.reshape(1, -1)),
            pl.BlockSpec((hid, ct), fcol), pl.BlockSpec((1, ct), fcol), pl.BlockSpec((1, ct), fcol),
            pl.BlockSpec((hid, ct), bcol), pl.BlockSpec((1, ct), bcol), pl.BlockSpec((1, ct), bcol),
            _const_spec(cm.shape), _const_spec(sm.shape), _const_spec(cw.shape), _const_spec(sw.shape),
        ],
        out_specs=tuple(pl.BlockSpec((half, ct), ocol) for _ in range(4)) + (pl.BlockSpec((8, ct), ocol),),
        scratch_shapes=[pltpu.VMEM((2 * ct // LANE, seq, LANE), F32), pltpu.VMEM((half, 4 * ct), BF16)],
        compiler_params=_cparams(2),
        name="hyena_filter_spectra",
    )(feats, w1p, b1.reshape(1, -1), w2, b2.reshape(1, -1), freq.reshape(1, -1),
      w3, b3r, decr, w3, b3r, decr, cm, sm, cw, sw)


def _hy_conv_kernel(zin_ref, gin_ref, wz_ref, bz_ref, wg_ref, bg_ref,
                    kpc_ref, kps_ref, kqc_ref, kqs_ref, km_ref, bias_ref,
                    cm_ref, smf_ref, smi_ref, cw_ref, sw_ref, out_ref,
                    zraw_s, graw_s, o_s, z_s, g_s, rhs_s, ac_s, as_s, *, conv_on_z):
    seq, ct = zin_ref.shape
    half = seq // 2
    nsl = ct // LANE
    pad = zraw_s.shape[1] - seq
    top = pad // 2
    rc = min(DFT_ROW_CHUNK, half)

    def stage(raw_s, src_ref):
        for j in range(nsl):
            raw_s[j, 0:top, :] = jnp.zeros((top, LANE), F32)
            raw_s[j, top + seq:, :] = jnp.zeros((pad - top, LANE), F32)
            raw_s[j, top:top + seq, :] = src_ref[:, j * LANE:(j + 1) * LANE].astype(F32)

    def split_rows(raw_s, dst_ref, w_ref, b_ref):
        for par in range(2):
            for j in range(nsl):
                ls = slice(j * LANE, (j + 1) * LANE)
                tap = lambda k: raw_s[j, pl.ds(top + par + k, half, stride=2), :]
                if w_ref is None:
                    v = tap(0)
                else:
                    v = (b_ref[:, ls] + tap(-1) * w_ref[0:1, ls] + tap(0) * w_ref[1:2, ls]
                         + tap(1) * w_ref[2:3, ls])
                dst_ref[:, par * ct + j * LANE:par * ct + (j + 1) * LANE] = v

    stage(zraw_s, zin_ref)
    stage(graw_s, gin_ref)
    split_rows(zraw_s, z_s, wz_ref if conv_on_z else None, bz_ref)
    split_rows(graw_s, g_s, wg_ref, bg_ref)
    rhs_s[...] = z_s[...].astype(BF16)

    for i in range(half // rc):
        r = slice(i * rc, (i + 1) * rc)
        c2 = jnp.dot(cm_ref[r, :], rhs_s[...], preferred_element_type=F32)
        s2 = jnp.dot(smf_ref[r, :], rhs_s[...], preferred_element_type=F32)
        if i == 0:
            row0 = lax.broadcasted_iota(jnp.int32, (rc, 1), 0) == 0
            zmc, zms = s2[0:1, :ct], s2[0:1, ct:]
            ymc = zmc * km_ref[0:1, :] - zms * km_ref[1:2, :]
            yms = zmc * km_ref[1:2, :] + zms * km_ref[0:1, :]
            s2 = jnp.where(row0, 0.0, s2)
        cw = cw_ref[r, :]
        sw = sw_ref[r, :]
        pc, ps, qc, qs = _split_spectrum(c2, s2, cw, sw)
        kpc, kps, kqc, kqs = kpc_ref[r, :], kps_ref[r, :], kqc_ref[r, :], kqs_ref[r, :]
        ypc = pc * kpc - ps * kps
        yps = pc * kps + ps * kpc
        yqc = qc * kqc - qs * kqs
        yqs = qc * kqs + qs * kqc
        dc = ypc - yqc
        ds = yps + yqs
        a0s = yps - yqs
        a1s = cw * ds - sw * dc
        if i == 0:
            a0s = jnp.where(row0, ymc, a0s)
            a1s = jnp.where(row0, yms, a1s)
        ac_s[r, :ct] = (ypc + yqc).astype(BF16)
        as_s[r, :ct] = a0s.astype(BF16)
        ac_s[r, ct:] = (cw * dc + sw * ds).astype(BF16)
        as_s[r, ct:] = a1s.astype(BF16)

    for i in range(half // rc):
        r = slice(i * rc, (i + 1) * rc)
        y2 = jnp.dot(cm_ref[r, :], ac_s[...], preferred_element_type=F32)
        y2 = y2 + jnp.dot(smi_ref[r, :], as_s[...], preferred_element_type=F32)
        for par in range(2):
            rows = pl.ds(2 * i * rc + par, rc, stride=2)
            for j in range(nsl):
                ls = slice(par * ct + j * LANE, par * ct + (j + 1) * LANE)
                o_s[j, rows, :] = g_s[r, ls] * (y2[:, ls] + z_s[r, ls] * bias_ref[:, j * LANE:(j + 1) * LANE])
    for j in range(nsl):
        out_ref[:, j * LANE:(j + 1) * LANE] = o_s[j].astype(out_ref.dtype)


def _hy_conv(zin, zcol0, gcol0, u_hy, conv_w, conv_b, spectra, bias_row, tables,
             *, order, conv_on_z, bsz, seq, d_hy):
    cm, smf, smi, cw, sw = (tables[k] for k in ("cm", "sm_fwd", "sm_inv", "cw", "sw"))
    kpc, kps, kqc, kqs, km = spectra
    half = seq // 2
    ct = cw.shape[1]
    nsl = ct // LANE
    nct = d_hy // ct
    zc0 = zcol0 // ct
    gc0 = gcol0 // ct
    zw0 = zc0 if conv_on_z else 0
    kcol = lambda c, b: (0, order * nct + c)
    return pl.pallas_call(
        functools.partial(_hy_conv_kernel, conv_on_z=conv_on_z),
        out_shape=jax.ShapeDtypeStruct((bsz * seq, d_hy), BF16),
        grid=(nct, bsz),
        in_specs=[
            pl.BlockSpec((seq, ct), lambda c, b: (b, zc0 + c)),
            pl.BlockSpec((seq, ct), lambda c, b: (b, gc0 + c)),
            pl.BlockSpec((conv_w.shape[0], ct), lambda c, b: (0, zw0 + c)),
            pl.BlockSpec((1, ct), lambda c, b: (0, zw0 + c)),
            pl.BlockSpec((conv_w.shape[0], ct), lambda c, b: (0, gc0 + c)),
            pl.BlockSpec((1, ct), lambda c, b: (0, gc0 + c)),
            pl.BlockSpec((half, ct), kcol), pl.BlockSpec((half, ct), kcol),
            pl.BlockSpec((half, ct), kcol), pl.BlockSpec((half, ct), kcol),
            pl.BlockSpec((8, ct), kcol), pl.BlockSpec((1, ct), kcol),
            _const_spec(cm.shape), _const_spec(smf.shape), _const_spec(smi.shape),
            _const_spec(cw.shape), _const_spec(sw.shape),
        ],
        out_specs=pl.BlockSpec((seq, ct), lambda c, b: (b, c)),
        scratch_shapes=[
            pltpu.VMEM((nsl, seq + 2 * SUBLANE, LANE), F32), pltpu.VMEM((nsl, seq + 2 * SUBLANE, LANE), F32),
            pltpu.VMEM((nsl, seq, LANE), F32),
            pltpu.VMEM((half, 2 * ct), F32), pltpu.VMEM((half, 2 * ct), F32),
            pltpu.VMEM((half, 2 * ct), BF16), pltpu.VMEM((half, 2 * ct), BF16), pltpu.VMEM((half, 2 * ct), BF16),
        ],
        compiler_params=_cparams(2),
        name=f"hyena_conv_order{order}",
    )(zin, u_hy, conv_w, conv_b, conv_w, conv_b, kpc, kps, kqc, kqs, km, bias_row, cm, smf, smi, cw, sw)


def _hyena_branch(u_hy, bsz, seq, conv_w, conv_b, w1, b1, w2, b2, w3, b3, freq, decay, bias):
    n_order, d_hy = bias.shape
    n_dirs = w3.shape[1] // (n_order * d_hy)
    if n_order != 2 or seq % 4 != 0 or d_hy % LANE != 0:
        raise NotImplementedError("Hyena branch: two long convolutions, L % 4 == 0, 128-lane channel tiles")
    tables = _hyena_tables(seq, min(HY_CH_TILE, d_hy))
    spectra = _hyena_filters(seq, w1, b1, w2, b2, w3, b3, freq, decay, tables, n_order, n_dirs, d_hy)
    cb = conv_b.reshape(1, -1)
    bias_row = bias.reshape(1, -1)
    common = dict(bsz=bsz, seq=seq, d_hy=d_hy)
    z1 = _hy_conv(u_hy, 0, d_hy, u_hy, conv_w, cb, spectra, bias_row, tables,
                  order=0, conv_on_z=True, **common)
    return _hy_conv(z1, 0, 2 * d_hy, u_hy, conv_w, cb, spectra, bias_row, tables,
                    order=1, conv_on_z=False, **common)


def _mixer_kernel(x_ref, mod_ref, g1_ref, g2_ref, fg_ref, permt_ref, za_ref, zb_ref,
                  wgate_ref, wa_ref, wb_ref, wout_ref, wg_ref, wu_ref, wd_ref, o_ref, *, n_chunks):
    nb, tt, d = x_ref.shape
    rows = nb * tt
    mod = lambda k: mod_ref[:, k:k + 1, :]
    x = x_ref[...]
    h = _norm_modulate(x, g1_ref[...], mod(0), mod(1)).reshape(rows, d).astype(BF16)
    gate = jax.nn.sigmoid(jnp.dot(h, wgate_ref[...], preferred_element_type=F32))
    za = jnp.dot(permt_ref[...], za_ref[...], preferred_element_type=F32).astype(BF16)
    ya = jnp.dot(za, wa_ref[...], preferred_element_type=F32)
    yb = jnp.dot(zb_ref[...].reshape(rows, zb_ref.shape[-1]), wb_ref[...], preferred_element_type=F32)
    merged = gate[:, :d] * ya + gate[:, d:] * yb
    o = jnp.dot(merged.astype(BF16), wout_ref[...], preferred_element_type=F32)
    x1 = x + mod(2) * o.reshape(nb, tt, d)

    h2 = _norm_modulate(x1, g2_ref[...], mod(3), mod(4)).reshape(rows, d).astype(BF16)
    fc = wg_ref.shape[1] // n_chunks
    acc = jnp.zeros((rows, d), F32)
    for c in range(n_chunks):
        sl = slice(c * fc, (c + 1) * fc)
        gl = jnp.dot(h2, wg_ref[:, sl], preferred_element_type=F32)
        up = jnp.dot(h2, wu_ref[:, sl], preferred_element_type=F32)
        act = (gl * jax.nn.sigmoid(gl) * up).astype(BF16)
        acc = acc + jnp.dot(act, wd_ref[sl, :], preferred_element_type=F32)
    x2 = x1 + mod(5) * acc.reshape(nb, tt, d)
    ms = jnp.mean(x2 * x2, axis=-1, keepdims=True)
    o_ref[...] = x2 * lax.rsqrt(ms + EPS) * fg_ref[...]


def _mixer(x, mod3, norm1_g, norm2_g, final_g, perm_t, za_tm, zb, w_gate, w_a, w_b, w_out, w_g, w_u, w_d):
    bsz, seq, d = x.shape
    d_s5 = w_a.shape[0]
    d_hy = w_b.shape[0]
    d_ff = w_g.shape[1]
    tt = _time_tile(bsz, seq)
    n_chunks = 2 if d_ff % (2 * LANE) == 0 else 1
    consts = (mod3, norm1_g, norm2_g, final_g, perm_t)
    weights = (w_gate, w_a, w_b, w_out, w_g, w_u, w_d)
    return pl.pallas_call(
        functools.partial(_mixer_kernel, n_chunks=n_chunks),
        out_shape=jax.ShapeDtypeStruct(x.shape, F32),
        grid=(seq // tt,),
        in_specs=[pl.BlockSpec((bsz, tt, d), lambda j: (0, j, 0))]
        + [_const_spec(a.shape) for a in consts]
        + [pl.BlockSpec((tt * bsz, d_s5), lambda j: (j, 0)),
           pl.BlockSpec((bsz, tt, d_hy), lambda j: (0, j, 0))]
        + [_const_spec(a.shape) for a in weights],
        out_specs=pl.BlockSpec((bsz, tt, d), lambda j: (0, j, 0)),
        compiler_params=_cparams(1),
        name="merge_swiglu_final_norm",
    )(x, *consts, za_tm, zb, *weights)


def kernel(x, c, ada_w, ada_b, norm1_g, norm2_g, w_in, s5_lam_re, s5_lam_im, s5_log_step, s5_b_re, s5_b_im, s5_c_re, s5_c_im, s5_d, s5_glu_w, s5_glu_b, hy_conv_w, hy_conv_b, hy_ffn_w1, hy_ffn_b1, hy_ffn_w2, hy_ffn_b2, hy_ffn_w3, hy_ffn_b3, hy_freq, hy_decay, hy_bias, w_branch_a, w_branch_b, w_out, ffn_w_gu, ffn_w_down, final_g):
    bsz, seq, d = x.shape
    depth = ada_w.shape[0]
    if depth != 1:
        raise NotImplementedError("the final RMSNorm is fused into the (single) layer's channel mixer")
    d_s5 = s5_d.shape[-1]
    n_order, d_hy = hy_bias.shape[1:]
    d_uh = d_s5 + (n_order + 1) * d_hy
    d_ff = ffn_w_down.shape[1]
    i = 0
    perm = _row_permutation(bsz, _time_tile(bsz, seq))
    mod = _modulation(c, ada_w[i], ada_b[i]).reshape(bsz, 6, d)
    w_in_b = w_in[i].astype(BF16)
    u_s5, u_hy = _in_projection(x, mod, norm1_g[i].reshape(1, d), w_in_b[:, :d_uh], perm, d_s5)
    z_a = _s5_branch(u_s5, bsz, s5_lam_re[i], s5_lam_im[i], s5_log_step[i], s5_b_re[i], s5_b_im[i],
                     s5_c_re[i], s5_c_im[i], s5_d[i], s5_glu_w[i], s5_glu_b[i])
    z_b = _hyena_branch(u_hy.reshape(bsz * seq, -1), bsz, seq, hy_conv_w[i], hy_conv_b[i], hy_ffn_w1[i],
                        hy_ffn_b1[i], hy_ffn_w2[i], hy_ffn_b2[i], hy_ffn_w3[i], hy_ffn_b3[i], hy_freq[i],
                        hy_decay[i], hy_bias[i])
    w_gu = ffn_w_gu[i].astype(BF16)
    return _mixer(x, mod, norm1_g[i].reshape(1, d), norm2_g[i].reshape(1, d), final_g.reshape(1, d),
                  perm.T, z_a, z_b.reshape(bsz, seq, d_hy), w_in_b[:, d_uh:],
                  w_branch_a[i].astype(BF16), w_branch_b[i].astype(BF16), w_out[i].astype(BF16),
                  w_gu[:, :d_ff], w_gu[:, d_ff:], ffn_w_down[i].astype(BF16))
```

```python
import functools
import math

import jax
import jax.numpy as jnp
from jax import lax
from jax.experimental import pallas as pl
from jax.experimental.pallas import tpu as pltpu

F32 = jnp.float32
BF16 = jnp.bfloat16
EPS = 1e-6
HIGHEST = lax.Precision.HIGHEST

V7X_VMEM_BYTES = 64 * 1024 * 1024
VMEM_LIMIT_BYTES = 56 * 1024 * 1024
LANE = 128
SUBLANE = 8
ROW_TILE = 512
S5_TIME_CHUNK = 64
S5_LANE_CHUNK = 512
S5_CH_SPLIT = 256
S5_STEPS_PER_GROUP = 8
MXU_TILE = 256
HY_CH_TILE = 256
HY_BATCH_GROUP = 2
DFT_ROW_CHUNK = 512


def _cparams(n_axes):
    return pltpu.CompilerParams(
        dimension_semantics=("arbitrary",) * n_axes,
        vmem_limit_bytes=VMEM_LIMIT_BYTES,
    )


def _const_spec(shape):
    nd = len(shape)
    return pl.BlockSpec(shape, lambda *_: (0,) * nd, pipeline_mode=pl.Buffered(1))


def _gelu_tanh(x):
    return 0.5 * x * (1.0 + jnp.tanh(math.sqrt(2.0 / math.pi) * (x + 0.044715 * (x * x * x))))


def _norm_modulate(x, g, shift, scale):
    ms = jnp.mean(x * x, axis=-1, keepdims=True)
    r = x * lax.rsqrt(ms + EPS) * g
    return r * (1.0 + scale) + shift


def _mod_kernel(c_ref, w_ref, b_ref, o_ref):
    c = c_ref[...]
    ca = c * jax.nn.sigmoid(c)
    o_ref[...] = jnp.dot(ca, w_ref[...], precision=HIGHEST, preferred_element_type=F32) + b_ref[...]


def _modulation(c, ada_w, ada_b):
    bsz, d = c.shape
    n = ada_w.shape[1]
    tn = 512
    return pl.pallas_call(
        _mod_kernel,
        out_shape=jax.ShapeDtypeStruct((bsz, n), F32),
        grid=(n // tn,),
        in_specs=[
            pl.BlockSpec((bsz, d), lambda j: (0, 0)),
            pl.BlockSpec((d, tn), lambda j: (0, j)),
            pl.BlockSpec((1, tn), lambda j: (0, j)),
        ],
        out_specs=pl.BlockSpec((bsz, tn), lambda j: (0, j)),
        compiler_params=_cparams(1),
        name="adaln_mod",
    )(c, ada_w, ada_b.reshape(1, n))


def _time_tile(bsz, seq):
    return max(min(ROW_TILE // bsz, seq), 1)


def _row_permutation(bsz, tt):
    r = jnp.arange(bsz * tt, dtype=jnp.int32)
    src = (r % bsz) * tt + r // bsz
    return (src[:, None] == r[None, :]).astype(BF16)


def _inproj_kernel(x_ref, mod_ref, g_ref, w_ref, perm_ref, us5_ref, uhy_ref, *, d_s5):
    nb, tt, d = x_ref.shape
    h = _norm_modulate(x_ref[...], g_ref[...], mod_ref[:, 0:1, :], mod_ref[:, 1:2, :])
    hb = h.reshape(nb * tt, d).astype(BF16)
    p = jnp.dot(hb, w_ref[...], preferred_element_type=F32)
    us5_ref[...] = jnp.dot(perm_ref[...], p[:, :d_s5].astype(BF16),
                           preferred_element_type=F32).astype(us5_ref.dtype)
    uhy_ref[...] = p[:, d_s5:].astype(BF16).reshape(uhy_ref.shape)


def _in_projection(x, mod3, norm_g, w_uh, perm, d_s5):
    bsz, seq, d = x.shape
    n = w_uh.shape[1]
    tt = _time_tile(bsz, seq)
    return pl.pallas_call(
        functools.partial(_inproj_kernel, d_s5=d_s5),
        out_shape=(
            jax.ShapeDtypeStruct((seq * bsz, d_s5), BF16),
            jax.ShapeDtypeStruct((bsz, seq, n - d_s5), BF16),
        ),
        grid=(seq // tt,),
        in_specs=[
            pl.BlockSpec((bsz, tt, d), lambda j: (0, j, 0)),
            _const_spec(mod3.shape),
            _const_spec((1, d)),
            _const_spec((d, n)),
            _const_spec(perm.shape),
        ],
        out_specs=(
            pl.BlockSpec((tt * bsz, d_s5), lambda j: (j, 0)),
            pl.BlockSpec((bsz, tt, n - d_s5), lambda j: (0, j, 0)),
        ),
        compiler_params=_cparams(1),
        name="in_proj",
    )(x, mod3, norm_g, w_uh, perm)


def _s5_prep_kernel(lre_ref, lim_ref, lstep_ref, bre_ref, bim_ref, cim_ref,
                    are_ref, aim_ref, bbre_ref, bbim_ref, ncim_ref):
    step = jnp.exp(lstep_ref[...])
    lr = lre_ref[...]
    li = lim_ref[...]
    mag = jnp.exp(lr * step)
    ar = mag * jnp.cos(li * step)
    ai = mag * jnp.sin(li * step)
    num = ar - 1.0
    den = lr * lr + li * li
    cr = (num * lr + ai * li) / den
    ci = (ai * lr - num * li) / den
    are_ref[...] = ar
    aim_ref[...] = ai
    for d in range(lre_ref.shape[0]):
        br = bre_ref[d]
        bi = bim_ref[d]
        bbre_ref[d] = cr[d:d + 1, :] * br - ci[d:d + 1, :] * bi
        bbim_ref[d] = cr[d:d + 1, :] * bi + ci[d:d + 1, :] * br
    ncim_ref[...] = -cim_ref[...]


def _s5_prepare(lam_re, lam_im, log_step, b_re, b_im, c_im):
    outs = (
        jax.ShapeDtypeStruct(lam_re.shape, F32),
        jax.ShapeDtypeStruct(lam_re.shape, F32),
        jax.ShapeDtypeStruct(b_re.shape, F32),
        jax.ShapeDtypeStruct(b_re.shape, F32),
        jax.ShapeDtypeStruct(c_im.shape, F32),
    )
    return pl.pallas_call(
        _s5_prep_kernel,
        out_shape=outs,
        compiler_params=pltpu.CompilerParams(vmem_limit_bytes=VMEM_LIMIT_BYTES),
        name="s5_discretize",
    )(lam_re, lam_im, log_step, b_re, b_im, c_im)


def _block_diag(m):
    g, r, c = m.shape
    eye = jnp.eye(g, dtype=m.dtype)
    return (m[:, :, None, :] * eye[:, None, :, None]).reshape(g * r, g * c)


def _spread(items, n):
    return [items[(len(items) * i) // n:(len(items) * (i + 1)) // n] for i in range(n)]


def _s5_scan_kernel(*refs, reverse, final, tc, nb):
    if final:
        (u_ref, bd_ref, cd_ref, are_ref, aim_ref, yprev_ref, gluw_ref, glub_ref,
         out_ref, bu_ref, sb_ref, y_ref, s_ref) = refs
    else:
        (u_ref, bd_ref, cd_ref, are_ref, aim_ref, d_ref, out_ref, bu_ref, sb_ref, y_ref, s_ref) = refs
    rows = tc * nb
    nsplit, ks, two_nss = bd_ref.shape
    nss = two_nss // 2
    lc = min(S5_LANE_CHUNK, nss)
    tw = min(MXU_TILE, lc)
    rh = rows // 2 if rows % (2 * SUBLANE * 2) == 0 else rows
    row_halves = [slice(r, r + rh) for r in range(0, rows, rh)]

    @pl.when(pl.program_id(0) == 0)
    def _():
        s_ref[...] = jnp.zeros_like(s_ref)

    units = [(h, q) for h in range(nsplit) for q in range(nss // lc)]

    def lanes(unit, part, off=0, width=None):
        h, q = unit
        start = h * two_nss + part * nss + q * lc + off
        return slice(start, start + (lc if width is None else width))

    def expand_pieces(unit):
        h, q = unit
        out = []
        for part in range(2):
            for off in range(0, lc, tw):
                for rs in row_halves:
                    def piece(part=part, off=off, rs=rs):
                        col = part * nss + q * lc + off
                        bu_ref[rs, lanes(unit, part, off, tw)] = jnp.dot(
                            u_ref[rs, h * ks:(h + 1) * ks], bd_ref[h, :, col:col + tw],
                            preferred_element_type=F32)
                    out.append(piece)
        return out

    def readout_pieces(unit):
        h, q = unit
        out = []
        for part in range(2):
            for off in range(0, lc, tw):
                for rs in row_halves:
                    first = q == 0 and part == 0 and off == 0
                    def piece(part=part, off=off, rs=rs, first=first):
                        k0 = part * nss + q * lc + off
                        v = jnp.dot(sb_ref[rs, lanes(unit, part, off, tw)], cd_ref[h, k0:k0 + tw, :],
                                    preferred_element_type=F32)
                        if first:
                            y_ref[rs, h * ks:(h + 1) * ks] = v
                        else:
                            y_ref[rs, h * ks:(h + 1) * ks] += v
                    out.append(piece)
        return out

    step_groups = _spread(list(range(tc)), max(tc // S5_STEPS_PER_GROUP, 1))
    for p in expand_pieces(units[0]):
        p()
    for c, unit in enumerate(units):
        mxu_work = (expand_pieces(units[c + 1]) if c + 1 < len(units) else []) + \
                   (readout_pieces(units[c - 1]) if c > 0 else [])
        mxu_groups = _spread(mxu_work, len(step_groups))
        a_sl = slice(unit[0] * nss + unit[1] * lc, unit[0] * nss + (unit[1] + 1) * lc)
        a_re = are_ref[:, a_sl]
        a_im = aim_ref[:, a_sl]
        re_sl, im_sl = lanes(unit, 0), lanes(unit, 1)
        sr = s_ref[:, re_sl]
        si = s_ref[:, im_sl]
        for steps, pieces in zip(step_groups, mxu_groups):
            for p in pieces:
                p()
            for k in steps:
                t = (tc - 1 - k) if reverse else k
                row = slice(t * nb, (t + 1) * nb)
                nr = a_re * sr - a_im * si + bu_ref[row, re_sl]
                ni = a_re * si + a_im * sr + bu_ref[row, im_sl]
                sb_ref[row, re_sl] = nr.astype(BF16)
                sb_ref[row, im_sl] = ni.astype(BF16)
                sr, si = nr, ni
        s_ref[:, re_sl] = sr
        s_ref[:, im_sl] = si
    for p in readout_pieces(units[-1]):
        p()

    rb = min(256, rows)
    for r in range(rows // rb):
        rs = slice(r * rb, (r + 1) * rb)
        if final:
            z = _gelu_tanh(yprev_ref[rs, :] + y_ref[rs, :])
            gate = jnp.dot(z.astype(BF16), gluw_ref[...], preferred_element_type=F32) + glub_ref[...]
            out_ref[rs, :] = (z * jax.nn.sigmoid(gate)).astype(out_ref.dtype)
        else:
            out_ref[rs, :] = y_ref[rs, :] + u_ref[rs, :] * d_ref[...]


def _s5_scan(u_tm, bd, cd, a_re, a_im, extra, *, reverse, final, nb, out_dtype):
    rows_total, d_s5 = u_tm.shape
    seq = rows_total // nb
    two_ns = bd.shape[0] * bd.shape[2]
    tc = min(S5_TIME_CHUNK, seq)
    nchunk = seq // tc
    rows = tc * nb
    if reverse:
        cidx = lambda i: (nchunk - 1 - i, 0)
    else:
        cidx = lambda i: (i, 0)
    in_specs = [
        pl.BlockSpec((rows, d_s5), cidx),
        _const_spec(bd.shape),
        _const_spec(cd.shape),
        _const_spec(a_re.shape),
        _const_spec(a_im.shape),
    ]
    if final:
        yprev, glu_w, glu_b = extra
        in_specs += [pl.BlockSpec((rows, d_s5), cidx), _const_spec(glu_w.shape), _const_spec(glu_b.shape)]
    else:
        in_specs += [_const_spec(extra[0].shape)]
    return pl.pallas_call(
        functools.partial(_s5_scan_kernel, reverse=reverse, final=final, tc=tc, nb=nb),
        out_shape=jax.ShapeDtypeStruct((rows_total, d_s5), out_dtype),
        grid=(nchunk,),
        in_specs=in_specs,
        out_specs=pl.BlockSpec((rows, d_s5), cidx),
        scratch_shapes=[pltpu.VMEM((rows, two_ns), F32), pltpu.VMEM((rows, two_ns), BF16),
                        pltpu.VMEM((rows, d_s5), F32), pltpu.VMEM((nb, two_ns), F32)],
        compiler_params=_cparams(1),
        name="s5_scan_bwd_glu" if final else "s5_scan_fwd",
    )(u_tm, bd, cd, a_re, a_im, *extra)


def _s5_branch(u_tm, nb, lam_re, lam_im, log_step, b_re, b_im, c_re, c_im, d, glu_w, glu_b):
    ndir, g, p = lam_re.shape
    grp = b_re.shape[-1]
    ns = g * p
    d_s5 = g * grp
    ks = S5_CH_SPLIT if d_s5 % S5_CH_SPLIT == 0 else d_s5
    nsplit = d_s5 // ks
    gs = g // nsplit
    flat = lambda a: a.reshape(ndir, ns)
    to_lanes = lambda a: jnp.transpose(a, (0, 3, 1, 2)).reshape(ndir, grp, ns)
    a_re, a_im, bb_re, bb_im, ncim = _s5_prepare(
        flat(lam_re), flat(lam_im), jnp.repeat(log_step, p, axis=-1),
        to_lanes(b_re), to_lanes(b_im),
        jnp.transpose(c_im, (0, 2, 1, 3)).reshape(ndir, grp, ns))

    def in_blocks(a):
        a = jnp.transpose(a.reshape(grp, nsplit, gs, p), (1, 2, 0, 3))
        return jnp.stack([_block_diag(a[h]) for h in range(nsplit)])

    def out_blocks(a):
        a = jnp.transpose(a.reshape(nsplit, gs, grp, p), (0, 1, 3, 2))
        return jnp.stack([_block_diag(a[h]) for h in range(nsplit)])

    y = None
    for direction in range(ndir):
        bd = jnp.concatenate([in_blocks(bb_re[direction]), in_blocks(bb_im[direction])], axis=2).astype(BF16)
        ncim_g = jnp.transpose(ncim[direction].reshape(grp, g, p), (1, 0, 2))
        cd = jnp.concatenate([out_blocks(c_re[direction]), out_blocks(ncim_g)], axis=1).astype(BF16)
        are = jnp.broadcast_to(a_re[direction][None, :], (nb, ns))
        aim = jnp.broadcast_to(a_im[direction][None, :], (nb, ns))
        if direction == 0:
            if ndir != 2:
                raise NotImplementedError("S5 branch expects forward and backward directions")
            y = _s5_scan(u_tm, bd, cd, are, aim, (d.reshape(1, -1),), reverse=False, final=False,
                         nb=nb, out_dtype=F32)
        else:
            extra = (y, glu_w.astype(BF16), glu_b.reshape(1, -1))
            y = _s5_scan(u_tm, bd, cd, are, aim, extra, reverse=True, final=True, nb=nb, out_dtype=BF16)
    return y


def _dft_matrices(seq):
    n = 2 * seq
    f_lo = 64 if seq % 64 == 0 else 1
    f_hi = seq // f_lo
    t = jnp.arange(seq, dtype=jnp.int32)[None, :]
    ka = (jnp.arange(f_hi, dtype=jnp.int32)[:, None] * f_lo * t) % n
    kb = (jnp.arange(f_lo, dtype=jnp.int32)[:, None] * t) % n
    w = 2.0 * math.pi / n
    aa = ka.astype(F32) * w
    ab = kb.astype(F32) * w
    ca, sa, cb, sb = jnp.cos(aa), jnp.sin(aa), jnp.cos(ab), jnp.sin(ab)
    cm = ca[:, None, :] * cb[None, :, :] - sa[:, None, :] * sb[None, :, :]
    sm = sa[:, None, :] * cb[None, :, :] + ca[:, None, :] * sb[None, :, :]
    return cm.reshape(seq, seq).astype(BF16), sm.reshape(seq, seq).astype(BF16)


def _hyena_tables(seq, ct):
    half = seq // 2
    cm, sm = _dft_matrices(half)
    alt = (1 - 2 * (jnp.arange(half, dtype=jnp.int32) & 1)).astype(BF16)
    ang = jnp.arange(half, dtype=F32)[:, None] * (math.pi / seq)
    return dict(cm=cm, sm=sm, sm_fwd=sm.at[0, :].set(alt), sm_inv=sm.at[:, 0].set(alt),
                cw=jnp.broadcast_to(jnp.cos(ang), (half, ct)), sw=jnp.broadcast_to(jnp.sin(ang), (half, ct)))


def _store_slabs(dst_ref, slab0, x):
    for j in range(x.shape[1] // LANE):
        dst_ref[slab0 + j] = x[:, j * LANE:(j + 1) * LANE]


def _deinterleave_rows(src_ref, slab0, nsl, dst_ref, col0):
    half = src_ref.shape[1] // 2
    ct = nsl * LANE
    for j in range(nsl):
        for par in range(2):
            c = col0 + par * ct + j * LANE
            dst_ref[:, c:c + LANE] = src_ref[slab0 + j, pl.ds(par, half, stride=2), :].astype(BF16)


def _mid_frequency(x):
    r = lax.broadcasted_iota(jnp.int32, (x.shape[0], 1), 0) & 3
    me = jnp.where(r == 0, 1.0, jnp.where(r == 2, -1.0, 0.0))
    mo = jnp.where(r == 1, 1.0, jnp.where(r == 3, -1.0, 0.0))
    return jnp.sum(x * me, axis=0, keepdims=True), jnp.sum(x * mo, axis=0, keepdims=True)


def _split_spectrum(c2, s2, cw, sw):
    ct = cw.shape[1]
    ec, oc = c2[:, :ct], c2[:, ct:]
    es, os_ = s2[:, :ct], s2[:, ct:]
    wc = cw * oc - sw * os_
    ws = cw * os_ + sw * oc
    return ec + wc, es + ws, ec - wc, ws - es


def _hy_filter_kernel(feat_ref, w1_ref, b1_ref, w2_ref, b2_ref, freq_ref,
                      w3f_ref, b3f_ref, decf_ref, w3b_ref, b3b_ref, decb_ref,
                      cm_ref, sm_ref, cw_ref, sw_ref,
                      kpc_ref, kps_ref, kqc_ref, kqs_ref, km_ref, taps_s, rhs_s, h_s):
    seq = feat_ref.shape[0]
    half = seq // 2
    n = 2 * seq
    ct = cw_ref.shape[1]
    nsl = ct // LANE

    @pl.when((pl.program_id(0) == 0) & (pl.program_id(1) == 0))
    def _():
        f = freq_ref[...]
        h1 = jnp.sin(f * (jnp.dot(feat_ref[...], w1_ref[...], precision=HIGHEST,
                                  preferred_element_type=F32) + b1_ref[...]))
        h_s[...] = jnp.sin(f * (jnp.dot(h1, w2_ref[...], precision=HIGHEST,
                                        preferred_element_type=F32) + b2_ref[...]))

    h = h_s[...]
    t01 = feat_ref[:, 0:1]
    row = lax.broadcasted_iota(jnp.int32, (seq, 1), 0)

    def taps(w3_ref, b3_ref, dec_ref):
        v = jnp.dot(h, w3_ref[...], precision=HIGHEST, preferred_element_type=F32) + b3_ref[...]
        return v * jnp.exp(-t01 * jnp.abs(dec_ref[...]))

    fwd = taps(w3f_ref, b3f_ref, decf_ref)
    bwd = jnp.where(row == 0, 0.0, taps(w3b_ref, b3b_ref, decb_ref))
    _store_slabs(taps_s, 0, fwd)
    _store_slabs(taps_s, nsl, bwd)
    _deinterleave_rows(taps_s, 0, nsl, rhs_s, 0)
    _deinterleave_rows(taps_s, nsl, nsl, rhs_s, 2 * ct)
    c4 = jnp.dot(cm_ref[...], rhs_s[...], preferred_element_type=F32)
    s4 = jnp.dot(sm_ref[...], rhs_s[...], preferred_element_type=F32)
    cw = cw_ref[...]
    sw = sw_ref[...]
    fpc, fps, fqc, fqs = _split_spectrum(c4[:, :2 * ct], s4[:, :2 * ct], cw, sw)
    bpc, bps, bqc, bqs = _split_spectrum(c4[:, 2 * ct:], s4[:, 2 * ct:], cw, sw)
    scale = jnp.where(lax.broadcasted_iota(jnp.int32, (half, 1), 0) == 0, 1.0 / n, 2.0 / n)
    kpc_ref[...] = (fpc + bpc) * scale
    kps_ref[...] = (fps - bps) * scale
    kqc_ref[...] = (fqc + bqc) * scale
    kqs_ref[...] = (fqs - bqs) * scale
    fmc, fms = _mid_frequency(fwd)
    bmc, bms = _mid_frequency(bwd)
    km_ref[...] = jnp.zeros_like(km_ref)
    km_ref[0:1, :] = (fmc + bmc) * (2.0 / n)
    km_ref[1:2, :] = (fms - bms) * (2.0 / n)


def _hyena_filters(seq, w1, b1, w2, b2, w3, b3, freq, decay, tables, n_order, n_dirs, d_hy):
    cm, sm, cw, sw = tables["cm"], tables["sm"], tables["cw"], tables["sw"]
    half = seq // 2
    emb, hid = w1.shape
    bands = (emb - 1) // 2
    t = jnp.arange(seq, dtype=F32)
    t01 = t / max(seq - 1, 1)
    band = jnp.linspace(1e-4, bands - 1, bands, dtype=F32)
    ang = (2.0 * math.pi) * t[:, None] * band[None, :] / seq
    feats = jnp.concatenate([t01[:, None], jnp.cos(ang), jnp.sin(ang)], axis=-1)
    kpad = LANE
    feats = jnp.pad(feats, ((0, 0), (0, kpad - emb)))
    w1p = jnp.pad(w1, ((0, kpad - emb), (0, 0)))
    ct = cw.shape[1]
    nct = d_hy // ct
    ncol = n_order * d_hy
    b3r = b3.reshape(1, -1)
    decr = decay.reshape(1, -1)
    fcol = lambda o, c: (0, (o * n_dirs + 0) * nct + c)
    bcol = lambda o, c: (0, (o * n_dirs + 1) * nct + c)
    ocol = lambda o, c: (0, o * nct + c)
    full = lambda a: pl.BlockSpec(a.shape, lambda o, c: (0,) * a.ndim)
    spec = jax.ShapeDtypeStruct((half, ncol), F32)
    return pl.pallas_call(
        _hy_filter_kernel,
        out_shape=(spec, spec, spec, spec, jax.ShapeDtypeStruct((8, ncol), F32)),
        grid=(n_order, nct),
        in_specs=[
            full(feats), full(w1p), full(b1.reshape(1, -1)), full(w2), full(b2.reshape(1, -1)),
            full(freq.reshape(1, -1)),
            pl.BlockSpec((hid, ct), fcol), pl.BlockSpec((1, ct), fcol), pl.BlockSpec((1, ct), fcol),
            pl.BlockSpec((hid, ct), bcol), pl.BlockSpec((1, ct), bcol), pl.BlockSpec((1, ct), bcol),
            _const_spec(cm.shape), _const_spec(sm.shape), _const_spec(cw.shape), _const_spec(sw.shape),
        ],
        out_specs=tuple(pl.BlockSpec((half, ct), ocol) for _ in range(4)) + (pl.BlockSpec((8, ct), ocol),),
        scratch_shapes=[pltpu.VMEM((2 * ct // LANE, seq, LANE), F32), pltpu.VMEM((half, 4 * ct), BF16),
                        pltpu.VMEM((seq, hid), F32)],
        compiler_params=_cparams(2),
        name="hyena_filter_spectra",
    )(feats, w1p, b1.reshape(1, -1), w2, b2.reshape(1, -1), freq.reshape(1, -1),
      w3, b3r, decr, w3, b3r, decr, cm, sm, cw, sw)


def _hy_conv_kernel(zin_ref, gin_ref, wz_ref, bz_ref, wg_ref, bg_ref,
                    kpc_ref, kps_ref, kqc_ref, kqs_ref, km_ref, bias_ref,
                    cm_ref, smf_ref, smi_ref, cw_ref, sw_ref, out_ref,
                    zraw_s, graw_s, o_s, z_s, g_s, rhs_s, ac_s, as_s, *, conv_on_z, seq):
    ct = zin_ref.shape[1]
    nbat = zin_ref.shape[0] // seq
    half = seq // 2
    nsl = ct // LANE
    pad = zraw_s.shape[2] - seq
    top = pad // 2
    rc = min(DFT_ROW_CHUNK, half)

    def stage(raw_s, src_ref, k):
        for j in range(nsl):
            raw_s[k, j, 0:top, :] = jnp.zeros((top, LANE), F32)
            raw_s[k, j, top + seq:, :] = jnp.zeros((pad - top, LANE), F32)
            raw_s[k, j, top:top + seq, :] = src_ref[k * seq:(k + 1) * seq, j * LANE:(j + 1) * LANE].astype(F32)

    def split_rows(raw_s, dst_ref, k, w_ref, b_ref):
        for par in range(2):
            for j in range(nsl):
                ls = slice(j * LANE, (j + 1) * LANE)
                tap = lambda d: raw_s[k, j, pl.ds(top + par + d, half, stride=2), :]
                if w_ref is None:
                    v = tap(0)
                else:
                    v = (b_ref[:, ls] + tap(-1) * w_ref[0:1, ls] + tap(0) * w_ref[1:2, ls]
                         + tap(1) * w_ref[2:3, ls])
                dst_ref[k, :, par * ct + j * LANE:par * ct + (j + 1) * LANE] = v

    def spectrum_chunk(k, i):
        r = slice(i * rc, (i + 1) * rc)
        c2 = jnp.dot(cm_ref[r, :], rhs_s[k], preferred_element_type=F32)
        s2 = jnp.dot(smf_ref[r, :], rhs_s[k], preferred_element_type=F32)
        if i == 0:
            row0 = lax.broadcasted_iota(jnp.int32, (rc, 1), 0) == 0
            zmc, zms = s2[0:1, :ct], s2[0:1, ct:]
            ymc = zmc * km_ref[0:1, :] - zms * km_ref[1:2, :]
            yms = zmc * km_ref[1:2, :] + zms * km_ref[0:1, :]
            s2 = jnp.where(row0, 0.0, s2)
        cw = cw_ref[r, :]
        sw = sw_ref[r, :]
        pc, ps, qc, qs = _split_spectrum(c2, s2, cw, sw)
        kpc, kps, kqc, kqs = kpc_ref[r, :], kps_ref[r, :], kqc_ref[r, :], kqs_ref[r, :]
        ypc = pc * kpc - ps * kps
        yps = pc * kps + ps * kpc
        yqc = qc * kqc - qs * kqs
        yqs = qc * kqs + qs * kqc
        dc = ypc - yqc
        ds = yps + yqs
        a0s = yps - yqs
        a1s = cw * ds - sw * dc
        if i == 0:
            a0s = jnp.where(row0, ymc, a0s)
            a1s = jnp.where(row0, yms, a1s)
        ac_s[k, r, :ct] = (ypc + yqc).astype(BF16)
        as_s[k, r, :ct] = a0s.astype(BF16)
        ac_s[k, r, ct:] = (cw * dc + sw * ds).astype(BF16)
        as_s[k, r, ct:] = a1s.astype(BF16)

    def output_chunk(k, i):
        r = slice(i * rc, (i + 1) * rc)
        y2 = jnp.dot(cm_ref[r, :], ac_s[k], preferred_element_type=F32)
        y2 = y2 + jnp.dot(smi_ref[r, :], as_s[k], preferred_element_type=F32)
        for par in range(2):
            rows = pl.ds(2 * i * rc + par, rc, stride=2)
            for j in range(nsl):
                ls = slice(par * ct + j * LANE, par * ct + (j + 1) * LANE)
                o_s[k, j, rows, :] = g_s[k, r, ls] * (y2[:, ls] + z_s[k, r, ls] * bias_ref[:, j * LANE:(j + 1) * LANE])

    for k in range(nbat):
        stage(zraw_s, zin_ref, k)
        stage(graw_s, gin_ref, k)
        split_rows(zraw_s, z_s, k, wz_ref if conv_on_z else None, bz_ref)
        split_rows(graw_s, g_s, k, wg_ref, bg_ref)
        rhs_s[k] = z_s[k].astype(BF16)
    for i in range(half // rc):
        for k in range(nbat):
            spectrum_chunk(k, i)
    for i in range(half // rc):
        for k in range(nbat):
            output_chunk(k, i)
    for k in range(nbat):
        for j in range(nsl):
            out_ref[k * seq:(k + 1) * seq, j * LANE:(j + 1) * LANE] = o_s[k, j].astype(out_ref.dtype)


def _hy_conv(zin, zcol0, gcol0, u_hy, conv_w, conv_b, spectra, bias_row, tables,
             *, order, conv_on_z, bsz, seq, d_hy):
    cm, smf, smi, cw, sw = (tables[k] for k in ("cm", "sm_fwd", "sm_inv", "cw", "sw"))
    kpc, kps, kqc, kqs, km = spectra
    half = seq // 2
    ct = cw.shape[1]
    nsl = ct // LANE
    nct = d_hy // ct
    nbat = HY_BATCH_GROUP if bsz % HY_BATCH_GROUP == 0 else 1
    zc0 = zcol0 // ct
    gc0 = gcol0 // ct
    zw0 = zc0 if conv_on_z else 0
    kspec = lambda rows: pl.BlockSpec((rows, ct), lambda c, b: (0, order * nct + c),
                                      pipeline_mode=pl.Buffered(1))
    scratch = lambda rows, lanes, dt: pltpu.VMEM((nbat, rows, lanes), dt)
    slabs = lambda rows: pltpu.VMEM((nbat, nsl, rows, LANE), F32)
    return pl.pallas_call(
        functools.partial(_hy_conv_kernel, conv_on_z=conv_on_z, seq=seq),
        out_shape=jax.ShapeDtypeStruct((bsz * seq, d_hy), BF16),
        grid=(nct, bsz // nbat),
        in_specs=[
            pl.BlockSpec((nbat * seq, ct), lambda c, b: (b, zc0 + c)),
            pl.BlockSpec((nbat * seq, ct), lambda c, b: (b, gc0 + c)),
            pl.BlockSpec((conv_w.shape[0], ct), lambda c, b: (0, zw0 + c)),
            pl.BlockSpec((1, ct), lambda c, b: (0, zw0 + c)),
            pl.BlockSpec((conv_w.shape[0], ct), lambda c, b: (0, gc0 + c)),
            pl.BlockSpec((1, ct), lambda c, b: (0, gc0 + c)),
            kspec(half), kspec(half), kspec(half), kspec(half), kspec(8), kspec(1),
            _const_spec(cm.shape), _const_spec(smf.shape), _const_spec(smi.shape),
            _const_spec(cw.shape), _const_spec(sw.shape),
        ],
        out_specs=pl.BlockSpec((nbat * seq, ct), lambda c, b: (b, c)),
        scratch_shapes=[
            slabs(seq + 2 * SUBLANE), slabs(seq + 2 * SUBLANE), slabs(seq),
            scratch(half, 2 * ct, F32), scratch(half, 2 * ct, F32),
            scratch(half, 2 * ct, BF16), scratch(half, 2 * ct, BF16), scratch(half, 2 * ct, BF16),
        ],
        compiler_params=_cparams(2),
        name=f"hyena_conv_order{order}",
    )(zin, u_hy, conv_w, conv_b, conv_w, conv_b, kpc, kps, kqc, kqs, km, bias_row, cm, smf, smi, cw, sw)


def _hyena_branch(u_hy, bsz, seq, conv_w, conv_b, w1, b1, w2, b2, w3, b3, freq, decay, bias):
    n_order, d_hy = bias.shape
    n_dirs = w3.shape[1] // (n_order * d_hy)
    if n_order != 2 or seq % 4 != 0 or d_hy % LANE != 0:
        raise NotImplementedError("Hyena branch: two long convolutions, L % 4 == 0, 128-lane channel tiles")
    tables = _hyena_tables(seq, min(HY_CH_TILE, d_hy))
    spectra = _hyena_filters(seq, w1, b1, w2, b2, w3, b3, freq, decay, tables, n_order, n_dirs, d_hy)
    cb = conv_b.reshape(1, -1)
    bias_row = bias.reshape(1, -1)
    common = dict(bsz=bsz, seq=seq, d_hy=d_hy)
    z1 = _hy_conv(u_hy, 0, d_hy, u_hy, conv_w, cb, spectra, bias_row, tables,
                  order=0, conv_on_z=True, **common)
    return _hy_conv(z1, 0, 2 * d_hy, u_hy, conv_w, cb, spectra, bias_row, tables,
                    order=1, conv_on_z=False, **common)


def _mixer_kernel(x_ref, mod_ref, g1_ref, g2_ref, fg_ref, permt_ref, za_ref, zb_ref,
                  wgate_ref, wa_ref, wb_ref, wout_ref, wg_ref, wu_ref, wd_ref, o_ref, *, n_chunks):
    nb, tt, d = x_ref.shape
    rows = nb * tt
    mod = lambda k: mod_ref[:, k:k + 1, :]
    x = x_ref[...]
    h = _norm_modulate(x, g1_ref[...], mod(0), mod(1)).reshape(rows, d).astype(BF16)
    gate = jax.nn.sigmoid(jnp.dot(h, wgate_ref[...], preferred_element_type=F32))
    za = jnp.dot(permt_ref[...], za_ref[...], preferred_element_type=F32).astype(BF16)
    ya = jnp.dot(za, wa_ref[...], preferred_element_type=F32)
    yb = jnp.dot(zb_ref[...].reshape(rows, zb_ref.shape[-1]), wb_ref[...], preferred_element_type=F32)
    merged = gate[:, :d] * ya + gate[:, d:] * yb
    o = jnp.dot(merged.astype(BF16), wout_ref[...], preferred_element_type=F32)
    x1 = x + mod(2) * o.reshape(nb, tt, d)

    h2 = _norm_modulate(x1, g2_ref[...], mod(3), mod(4)).reshape(rows, d).astype(BF16)
    d_ff = wg_ref.shape[1]
    unit = MXU_TILE if d_ff % MXU_TILE == 0 else d_ff // n_chunks
    edges = [len(g) for g in _spread(list(range(d_ff // unit)), n_chunks)]
    acc = jnp.zeros((rows, d), F32)
    start = 0
    for width in edges:
        sl = slice(start * unit, (start + width) * unit)
        start += width
        gl = jnp.dot(h2, wg_ref[:, sl], preferred_element_type=F32)
        up = jnp.dot(h2, wu_ref[:, sl], preferred_element_type=F32)
        act = (gl * jax.nn.sigmoid(gl) * up).astype(BF16)
        acc = acc + jnp.dot(act, wd_ref[sl, :], preferred_element_type=F32)
    x2 = x1 + mod(5) * acc.reshape(nb, tt, d)
    ms = jnp.mean(x2 * x2, axis=-1, keepdims=True)
    o_ref[...] = x2 * lax.rsqrt(ms + EPS) * fg_ref[...]


def _mixer(x, mod3, norm1_g, norm2_g, final_g, perm_t, za_tm, zb, w_gate, w_a, w_b, w_out, w_g, w_u, w_d):
    bsz, seq, d = x.shape
    d_s5 = w_a.shape[0]
    d_hy = w_b.shape[0]
    d_ff = w_g.shape[1]
    tt = _time_tile(bsz, seq)
    n_chunks = 2 if d_ff % (2 * LANE) == 0 else 1
    consts = (mod3, norm1_g, norm2_g, final_g, perm_t)
    weights = (w_gate, w_a, w_b, w_out, w_g, w_u, w_d)
    return pl.pallas_call(
        functools.partial(_mixer_kernel, n_chunks=n_chunks),
        out_shape=jax.ShapeDtypeStruct(x.shape, F32),
        grid=(seq // tt,),
        in_specs=[pl.BlockSpec((bsz, tt, d), lambda j: (0, j, 0))]
        + [_const_spec(a.shape) for a in consts]
        + [pl.BlockSpec((tt * bsz, d_s5), lambda j: (j, 0)),
           pl.BlockSpec((bsz, tt, d_hy), lambda j: (0, j, 0))]
        + [_const_spec(a.shape) for a in weights],
        out_specs=pl.BlockSpec((bsz, tt, d), lambda j: (0, j, 0)),
        compiler_params=_cparams(1),
        name="merge_swiglu_final_norm",
    )(x, *consts, za_tm, zb, *weights)


def kernel(x, c, ada_w, ada_b, norm1_g, norm2_g, w_in, s5_lam_re, s5_lam_im, s5_log_step, s5_b_re, s5_b_im, s5_c_re, s5_c_im, s5_d, s5_glu_w, s5_glu_b, hy_conv_w, hy_conv_b, hy_ffn_w1, hy_ffn_b1, hy_ffn_w2, hy_ffn_b2, hy_ffn_w3, hy_ffn_b3, hy_freq, hy_decay, hy_bias, w_branch_a, w_branch_b, w_out, ffn_w_gu, ffn_w_down, final_g):
    bsz, seq, d = x.shape
    depth = ada_w.shape[0]
    if depth != 1:
        raise NotImplementedError("the final RMSNorm is fused into the (single) layer's channel mixer")
    d_s5 = s5_d.shape[-1]
    n_order, d_hy = hy_bias.shape[1:]
    d_uh = d_s5 + (n_order + 1) * d_hy
    d_ff = ffn_w_down.shape[1]
    i = 0
    perm = _row_permutation(bsz, _time_tile(bsz, seq))
    mod = _modulation(c, ada_w[i], ada_b[i]).reshape(bsz, 6, d)
    w_in_b = w_in[i].astype(BF16)
    u_s5, u_hy = _in_projection(x, mod, norm1_g[i].reshape(1, d), w_in_b[:, :d_uh], perm, d_s5)
    z_a = _s5_branch(u_s5, bsz, s5_lam_re[i], s5_lam_im[i], s5_log_step[i], s5_b_re[i], s5_b_im[i],
                     s5_c_re[i], s5_c_im[i], s5_d[i], s5_glu_w[i], s5_glu_b[i])
    z_b = _hyena_branch(u_hy.reshape(bsz * seq, -1), bsz, seq, hy_conv_w[i], hy_conv_b[i], hy_ffn_w1[i],
                        hy_ffn_b1[i], hy_ffn_w2[i], hy_ffn_b2[i], hy_ffn_w3[i], hy_ffn_b3[i], hy_freq[i],
                        hy_decay[i], hy_bias[i])
    w_gu = ffn_w_gu[i].astype(BF16)
    return _mixer(x, mod, norm1_g[i].reshape(1, d), norm2_g[i].reshape(1, d), final_g.reshape(1, d),
                  perm.T, z_a, z_b.reshape(bsz, seq, d_hy), w_in_b[:, d_uh:],
                  w_branch_a[i].astype(BF16), w_branch_b[i].astype(BF16), w_out[i].astype(BF16),
                  w_gu[:, :d_ff], w_gu[:, d_ff:], ffn_w_down[i].astype(BF16))
```

```python
import functools
import math

import jax
import jax.numpy as jnp
from jax import lax
from jax.experimental import pallas as pl
from jax.experimental.pallas import tpu as pltpu

F32 = jnp.float32
BF16 = jnp.bfloat16
EPS = 1e-6
HIGHEST = lax.Precision.HIGHEST

V7X_VMEM_BYTES = 64 * 1024 * 1024
VMEM_LIMIT_BYTES = 56 * 1024 * 1024
LANE = 128
SUBLANE = 8
ROW_TILE = 512
S5_TIME_CHUNK = 64
S5_LANE_CHUNK = 512
S5_CH_SPLIT = 256
S5_STEPS_PER_GROUP = 8
MXU_TILE = 256
HY_CH_TILE = 256
HY_BATCH_GROUP = 2
DFT_ROW_CHUNK = 512


def _cparams(n_axes):
    return pltpu.CompilerParams(
        dimension_semantics=("arbitrary",) * n_axes,
        vmem_limit_bytes=VMEM_LIMIT_BYTES,
    )


def _const_spec(shape):
    nd = len(shape)
    return pl.BlockSpec(shape, lambda *_: (0,) * nd, pipeline_mode=pl.Buffered(1))


def _gelu_tanh(x):
    return 0.5 * x * (1.0 + jnp.tanh(math.sqrt(2.0 / math.pi) * (x + 0.044715 * (x * x * x))))


def _norm_modulate(x, g, shift, scale):
    ms = jnp.mean(x * x, axis=-1, keepdims=True)
    r = x * lax.rsqrt(ms + EPS) * g
    return r * (1.0 + scale) + shift


def _mod_kernel(c_ref, w_ref, b_ref, o_ref):
    c = c_ref[...]
    ca = c * jax.nn.sigmoid(c)
    o_ref[...] = jnp.dot(ca, w_ref[...], precision=HIGHEST, preferred_element_type=F32) + b_ref[...]


def _modulation(c, ada_w, ada_b):
    bsz, d = c.shape
    n = ada_w.shape[1]
    tn = 512
    return pl.pallas_call(
        _mod_kernel,
        out_shape=jax.ShapeDtypeStruct((bsz, n), F32),
        grid=(n // tn,),
        in_specs=[
            pl.BlockSpec((bsz, d), lambda j: (0, 0)),
            pl.BlockSpec((d, tn), lambda j: (0, j)),
            pl.BlockSpec((1, tn), lambda j: (0, j)),
        ],
        out_specs=pl.BlockSpec((bsz, tn), lambda j: (0, j)),
        compiler_params=_cparams(1),
        name="adaln_mod",
    )(c, ada_w, ada_b.reshape(1, n))


def _time_tile(bsz, seq):
    return max(min(ROW_TILE // bsz, seq), 1)


def _row_permutation(bsz, tt):
    r = jnp.arange(bsz * tt, dtype=jnp.int32)
    src = (r % bsz) * tt + r // bsz
    return (src[:, None] == r[None, :]).astype(BF16)


def _inproj_kernel(x_ref, mod_ref, g_ref, w_ref, perm_ref, us5_ref, uhy_ref, *, d_s5):
    nb, tt, d = x_ref.shape
    h = _norm_modulate(x_ref[...], g_ref[...], mod_ref[:, 0:1, :], mod_ref[:, 1:2, :])
    hb = h.reshape(nb * tt, d).astype(BF16)
    p = jnp.dot(hb, w_ref[...], preferred_element_type=F32)
    us5_ref[...] = jnp.dot(perm_ref[...], p[:, :d_s5].astype(BF16),
                           preferred_element_type=F32).astype(us5_ref.dtype)
    uhy_ref[...] = p[:, d_s5:].astype(BF16).reshape(uhy_ref.shape)


def _in_projection(x, mod3, norm_g, w_uh, perm, d_s5):
    bsz, seq, d = x.shape
    n = w_uh.shape[1]
    tt = _time_tile(bsz, seq)
    return pl.pallas_call(
        functools.partial(_inproj_kernel, d_s5=d_s5),
        out_shape=(
            jax.ShapeDtypeStruct((seq * bsz, d_s5), BF16),
            jax.ShapeDtypeStruct((bsz, seq, n - d_s5), BF16),
        ),
        grid=(seq // tt,),
        in_specs=[
            pl.BlockSpec((bsz, tt, d), lambda j: (0, j, 0)),
            _const_spec(mod3.shape),
            _const_spec((1, d)),
            _const_spec((d, n)),
            _const_spec(perm.shape),
        ],
        out_specs=(
            pl.BlockSpec((tt * bsz, d_s5), lambda j: (j, 0)),
            pl.BlockSpec((bsz, tt, n - d_s5), lambda j: (0, j, 0)),
        ),
        compiler_params=_cparams(1),
        name="in_proj",
    )(x, mod3, norm_g, w_uh, perm)


def _s5_prep_kernel(lre_ref, lim_ref, lstep_ref, bre_ref, bim_ref, cim_ref,
                    are_ref, aim_ref, bbre_ref, bbim_ref, ncim_ref):
    step = jnp.exp(lstep_ref[...])
    lr = lre_ref[...]
    li = lim_ref[...]
    mag = jnp.exp(lr * step)
    ar = mag * jnp.cos(li * step)
    ai = mag * jnp.sin(li * step)
    num = ar - 1.0
    den = lr * lr + li * li
    cr = (num * lr + ai * li) / den
    ci = (ai * lr - num * li) / den
    are_ref[...] = ar
    aim_ref[...] = ai
    for d in range(lre_ref.shape[0]):
        br = bre_ref[d]
        bi = bim_ref[d]
        bbre_ref[d] = cr[d:d + 1, :] * br - ci[d:d + 1, :] * bi
        bbim_ref[d] = cr[d:d + 1, :] * bi + ci[d:d + 1, :] * br
    ncim_ref[...] = -cim_ref[...]


def _s5_prepare(lam_re, lam_im, log_step, b_re, b_im, c_im):
    outs = (
        jax.ShapeDtypeStruct(lam_re.shape, F32),
        jax.ShapeDtypeStruct(lam_re.shape, F32),
        jax.ShapeDtypeStruct(b_re.shape, F32),
        jax.ShapeDtypeStruct(b_re.shape, F32),
        jax.ShapeDtypeStruct(c_im.shape, F32),
    )
    return pl.pallas_call(
        _s5_prep_kernel,
        out_shape=outs,
        compiler_params=pltpu.CompilerParams(vmem_limit_bytes=VMEM_LIMIT_BYTES),
        name="s5_discretize",
    )(lam_re, lam_im, log_step, b_re, b_im, c_im)


def _block_diag(m):
    g, r, c = m.shape
    eye = jnp.eye(g, dtype=m.dtype)
    return (m[:, :, None, :] * eye[:, None, :, None]).reshape(g * r, g * c)


def _spread(items, n):
    return [items[(len(items) * i) // n:(len(items) * (i + 1)) // n] for i in range(n)]


def _s5_scan_kernel(*refs, reverse, final, tc, nb):
    if final:
        (u_ref, bd_ref, cd_ref, are_ref, aim_ref, yprev_ref, gluw_ref, glub_ref,
         out_ref, bu_ref, sb_ref, y_ref, s_ref) = refs
    else:
        (u_ref, bd_ref, cd_ref, are_ref, aim_ref, d_ref, out_ref, bu_ref, sb_ref, y_ref, s_ref) = refs
    rows = tc * nb
    nsplit, ks, two_nss = bd_ref.shape
    nss = two_nss // 2
    lc = min(S5_LANE_CHUNK, nss)
    tw = min(MXU_TILE, lc)
    rh = rows // 2 if rows % (2 * SUBLANE * 2) == 0 else rows
    row_halves = [slice(r, r + rh) for r in range(0, rows, rh)]

    @pl.when(pl.program_id(0) == 0)
    def _():
        s_ref[...] = jnp.zeros_like(s_ref)

    units = [(h, q) for h in range(nsplit) for q in range(nss // lc)]

    def lanes(unit, part, off=0, width=None):
        h, q = unit
        start = h * two_nss + part * nss + q * lc + off
        return slice(start, start + (lc if width is None else width))

    def expand_pieces(unit):
        h, q = unit
        out = []
        for part in range(2):
            for off in range(0, lc, tw):
                for rs in row_halves:
                    def piece(part=part, off=off, rs=rs):
                        col = part * nss + q * lc + off
                        bu_ref[rs, lanes(unit, part, off, tw)] = jnp.dot(
                            u_ref[rs, h * ks:(h + 1) * ks], bd_ref[h, :, col:col + tw],
                            preferred_element_type=F32)
                    out.append(piece)
        return out

    def readout_pieces(unit):
        h, q = unit
        out = []
        for part in range(2):
            for off in range(0, lc, tw):
                for rs in row_halves:
                    first = q == 0 and part == 0 and off == 0
                    def piece(part=part, off=off, rs=rs, first=first):
                        k0 = part * nss + q * lc + off
                        v = jnp.dot(sb_ref[rs, lanes(unit, part, off, tw)], cd_ref[h, k0:k0 + tw, :],
                                    preferred_element_type=F32)
                        if first:
                            y_ref[rs, h * ks:(h + 1) * ks] = v
                        else:
                            y_ref[rs, h * ks:(h + 1) * ks] += v
                    out.append(piece)
        return out

    step_groups = _spread(list(range(tc)), max(tc // S5_STEPS_PER_GROUP, 1))
    for p in expand_pieces(units[0]):
        p()
    for c, unit in enumerate(units):
        mxu_work = (expand_pieces(units[c + 1]) if c + 1 < len(units) else []) + \
                   (readout_pieces(units[c - 1]) if c > 0 else [])
        mxu_groups = _spread(mxu_work, len(step_groups))
        a_sl = slice(unit[0] * nss + unit[1] * lc, unit[0] * nss + (unit[1] + 1) * lc)
        a_re = are_ref[:, a_sl]
        a_im = aim_ref[:, a_sl]
        re_sl, im_sl = lanes(unit, 0), lanes(unit, 1)
        sr = s_ref[:, re_sl]
        si = s_ref[:, im_sl]
        for steps, pieces in zip(step_groups, mxu_groups):
            for p in pieces:
                p()
            for k in steps:
                t = (tc - 1 - k) if reverse else k
                row = slice(t * nb, (t + 1) * nb)
                nr = a_re * sr - a_im * si + bu_ref[row, re_sl]
                ni = a_re * si + a_im * sr + bu_ref[row, im_sl]
                sb_ref[row, re_sl] = nr.astype(BF16)
                sb_ref[row, im_sl] = ni.astype(BF16)
                sr, si = nr, ni
        s_ref[:, re_sl] = sr
        s_ref[:, im_sl] = si
    for p in readout_pieces(units[-1]):
        p()

    rb = min(256, rows)
    for r in range(rows // rb):
        rs = slice(r * rb, (r + 1) * rb)
        if final:
            z = _gelu_tanh(yprev_ref[rs, :] + y_ref[rs, :])
            gate = jnp.dot(z.astype(BF16), gluw_ref[...], preferred_element_type=F32) + glub_ref[...]
            out_ref[rs, :] = (z * jax.nn.sigmoid(gate)).astype(out_ref.dtype)
        else:
            out_ref[rs, :] = y_ref[rs, :] + u_ref[rs, :] * d_ref[...]


def _s5_scan(u_tm, bd, cd, a_re, a_im, extra, *, reverse, final, nb, out_dtype):
    rows_total, d_s5 = u_tm.shape
    seq = rows_total // nb
    two_ns = bd.shape[0] * bd.shape[2]
    tc = min(S5_TIME_CHUNK, seq)
    nchunk = seq // tc
    rows = tc * nb
    if reverse:
        cidx = lambda i: (nchunk - 1 - i, 0)
    else:
        cidx = lambda i: (i, 0)
    in_specs = [
        pl.BlockSpec((rows, d_s5), cidx),
        _const_spec(bd.shape),
        _const_spec(cd.shape),
        _const_spec(a_re.shape),
        _const_spec(a_im.shape),
    ]
    if final:
        yprev, glu_w, glu_b = extra
        in_specs += [pl.BlockSpec((rows, d_s5), cidx), _const_spec(glu_w.shape), _const_spec(glu_b.shape)]
    else:
        in_specs += [_const_spec(extra[0].shape)]
    return pl.pallas_call(
        functools.partial(_s5_scan_kernel, reverse=reverse, final=final, tc=tc, nb=nb),
        out_shape=jax.ShapeDtypeStruct((rows_total, d_s5), out_dtype),
        grid=(nchunk,),
        in_specs=in_specs,
        out_specs=pl.BlockSpec((rows, d_s5), cidx),
        scratch_shapes=[pltpu.VMEM((rows, two_ns), F32), pltpu.VMEM((rows, two_ns), BF16),
                        pltpu.VMEM((rows, d_s5), F32), pltpu.VMEM((nb, two_ns), F32)],
        compiler_params=_cparams(1),
        name="s5_scan_bwd_glu" if final else "s5_scan_fwd",
    )(u_tm, bd, cd, a_re, a_im, *extra)


def _s5_branch(u_tm, nb, lam_re, lam_im, log_step, b_re, b_im, c_re, c_im, d, glu_w, glu_b):
    ndir, g, p = lam_re.shape
    grp = b_re.shape[-1]
    ns = g * p
    d_s5 = g * grp
    ks = S5_CH_SPLIT if d_s5 % S5_CH_SPLIT == 0 else d_s5
    nsplit = d_s5 // ks
    gs = g // nsplit
    flat = lambda a: a.reshape(ndir, ns)
    to_lanes = lambda a: jnp.transpose(a, (0, 3, 1, 2)).reshape(ndir, grp, ns)
    a_re, a_im, bb_re, bb_im, ncim = _s5_prepare(
        flat(lam_re), flat(lam_im), jnp.repeat(log_step, p, axis=-1),
        to_lanes(b_re), to_lanes(b_im),
        jnp.transpose(c_im, (0, 2, 1, 3)).reshape(ndir, grp, ns))

    def in_blocks(a):
        a = jnp.transpose(a.reshape(grp, nsplit, gs, p), (1, 2, 0, 3))
        return jnp.stack([_block_diag(a[h]) for h in range(nsplit)])

    def out_blocks(a):
        a = jnp.transpose(a.reshape(nsplit, gs, grp, p), (0, 1, 3, 2))
        return jnp.stack([_block_diag(a[h]) for h in range(nsplit)])

    y = None
    for direction in range(ndir):
        bd = jnp.concatenate([in_blocks(bb_re[direction]), in_blocks(bb_im[direction])], axis=2).astype(BF16)
        ncim_g = jnp.transpose(ncim[direction].reshape(grp, g, p), (1, 0, 2))
        cd = jnp.concatenate([out_blocks(c_re[direction]), out_blocks(ncim_g)], axis=1).astype(BF16)
        are = jnp.broadcast_to(a_re[direction][None, :], (nb, ns))
        aim = jnp.broadcast_to(a_im[direction][None, :], (nb, ns))
        if direction == 0:
            if ndir != 2:
                raise NotImplementedError("S5 branch expects forward and backward directions")
            y = _s5_scan(u_tm, bd, cd, are, aim, (d.reshape(1, -1),), reverse=False, final=False,
                         nb=nb, out_dtype=F32)
        else:
            extra = (y, glu_w.astype(BF16), glu_b.reshape(1, -1))
            y = _s5_scan(u_tm, bd, cd, are, aim, extra, reverse=True, final=True, nb=nb, out_dtype=BF16)
    return y


RSQRT2 = math.sqrt(0.5)


def _dft_matrices(seq):
    n = 2 * seq
    f_lo = 64 if seq % 64 == 0 else 1
    f_hi = seq // f_lo
    t = jnp.arange(seq, dtype=jnp.int32)[None, :]
    ka = (jnp.arange(f_hi, dtype=jnp.int32)[:, None] * f_lo * t) % n
    kb = (jnp.arange(f_lo, dtype=jnp.int32)[:, None] * t) % n
    w = 2.0 * math.pi / n
    aa = ka.astype(F32) * w
    ab = kb.astype(F32) * w
    ca, sa, cb, sb = jnp.cos(aa), jnp.sin(aa), jnp.cos(ab), jnp.sin(ab)
    cm = ca[:, None, :] * cb[None, :, :] - sa[:, None, :] * sb[None, :, :]
    sm = sa[:, None, :] * cb[None, :, :] + ca[:, None, :] * sb[None, :, :]
    return cm.reshape(seq, seq).astype(BF16), sm.reshape(seq, seq).astype(BF16)


def _hyena_tables(seq, ct):
    q = seq // 4
    cm, sm = _dft_matrices(q)
    alt = (1 - 2 * (jnp.arange(q, dtype=jnp.int32) & 1)).astype(BF16)
    f = jnp.arange(q, dtype=F32)[:, None]
    tab = lambda a: jnp.broadcast_to(a, (q, ct))
    return dict(cm=cm, sm_fwd=sm.at[0, :].set(alt), sm_inv=sm.at[:, 0].set(alt),
                cw=tab(jnp.cos(f * (math.pi / seq))), sw=tab(jnp.sin(f * (math.pi / seq))),
                cv=tab(jnp.cos(f * (2.0 * math.pi / seq))), sv=tab(jnp.sin(f * (2.0 * math.pi / seq))))


HY_TABLE_KEYS = ("cm", "sm_fwd", "sm_inv", "cw", "sw", "cv", "sv")


def _bf(a, b, t):
    (ac, as_), (bc, bs), (tc, ts) = a, b, t
    wc = tc * bc - ts * bs
    ws = tc * bs + ts * bc
    return (ac + wc, as_ + ws), (ac - wc, ws - as_)


def _ibf(y1, y2, t):
    (y1c, y1s), (y2c, y2s), (tc, ts) = y1, y2, t
    dc = y1c - y2c
    ds = y1s + y2s
    return (y1c + y2c, y1s - y2s), (tc * dc + ts * ds, tc * ds - ts * dc)


def _cmul(z, k):
    (zc, zs), (kc, ks) = z, k
    return zc * kc - zs * ks, zc * ks + zs * kc


def _twiddles(cw_ref, sw_ref, cv_ref, sv_ref, r):
    cw, sw = cw_ref[r, :], sw_ref[r, :]
    return dict(w=(cw, sw), wq=(sw, cw), v=(cv_ref[r, :], sv_ref[r, :]))


def _forward_spectrum(c4, s4, xq, tw, ct):
    lane = lambda a, r: a[:, r * ct:(r + 1) * ct]
    x = [(lane(c4, r), lane(s4, r)) for r in range(4)]
    pe, qe = _bf(x[0], x[2], tw["v"])
    po, qo = _bf(x[1], x[3], tw["v"])
    za, zb = _bf(pe, po, tw["w"])
    zc, zd = _bf(qe, qo, tw["wq"])
    mid = _bf((lane(xq, 0), lane(xq, 2)), (lane(xq, 1), lane(xq, 3)), (RSQRT2, RSQRT2))
    return [za, zb, zc, zd], list(mid)


def _inverse_spectrum(y, ymid, tw):
    p0, p1 = _ibf(y[0], y[1], tw["w"])
    q0, q1 = _ibf(y[2], y[3], tw["wq"])
    m0, m1 = _ibf(ymid[0], ymid[1], (RSQRT2, RSQRT2))
    t0, t2 = _ibf(p0, q0, tw["v"])
    t1, t3 = _ibf(p1, q1, tw["v"])
    return [t0, t1, t2, t3], [m0[0], m1[0], m0[1], m1[1]]


def _hy_filter_kernel(feat_ref, w1_ref, b1_ref, w2_ref, b2_ref, freq_ref,
                      w3f_ref, b3f_ref, decf_ref, w3b_ref, b3b_ref, decb_ref,
                      cm_ref, smf_ref, cw_ref, sw_ref, cv_ref, sv_ref,
                      kac_ref, kas_ref, kbc_ref, kbs_ref, kcc_ref, kcs_ref, kdc_ref, kds_ref, km_ref,
                      taps_s, rhs_s, h_s):
    seq = feat_ref.shape[0]
    q = seq // 4
    n = 2 * seq
    ct = cw_ref.shape[1]
    nsl = ct // LANE

    @pl.when((pl.program_id(0) == 0) & (pl.program_id(1) == 0))
    def _():
        f = freq_ref[...]
        h1 = jnp.sin(f * (jnp.dot(feat_ref[...], w1_ref[...], precision=HIGHEST,
                                  preferred_element_type=F32) + b1_ref[...]))
        h_s[...] = jnp.sin(f * (jnp.dot(h1, w2_ref[...], precision=HIGHEST,
                                        preferred_element_type=F32) + b2_ref[...]))

    h = h_s[...]
    t01 = feat_ref[:, 0:1]
    row = lax.broadcasted_iota(jnp.int32, (seq, 1), 0)

    def taps(w3_ref, b3_ref, dec_ref):
        v = jnp.dot(h, w3_ref[...], precision=HIGHEST, preferred_element_type=F32) + b3_ref[...]
        return v * jnp.exp(-t01 * jnp.abs(dec_ref[...]))

    fwd = taps(w3f_ref, b3f_ref, decf_ref)
    bwd = jnp.where(row == 0, 0.0, taps(w3b_ref, b3b_ref, decb_ref))
    for d, x in enumerate((fwd, bwd)):
        for j in range(nsl):
            taps_s[d * nsl + j] = x[:, j * LANE:(j + 1) * LANE]
        for r in range(4):
            for j in range(nsl):
                c0 = (4 * d + r) * ct + j * LANE
                rhs_s[:, c0:c0 + LANE] = taps_s[d * nsl + j, pl.ds(r, q, stride=4), :].astype(BF16)
    c8 = jnp.dot(cm_ref[...], rhs_s[...], preferred_element_type=F32)
    s8 = jnp.dot(smf_ref[...], rhs_s[...], preferred_element_type=F32)
    row0 = lax.broadcasted_iota(jnp.int32, (q, 1), 0) == 0
    xq = s8[0:1, :]
    s8 = jnp.where(row0, 0.0, s8)
    tw = _twiddles(cw_ref, sw_ref, cv_ref, sv_ref, slice(None))
    fz, fm = _forward_spectrum(c8[:, :4 * ct], s8[:, :4 * ct], xq[:, :4 * ct], tw, ct)
    bz, bm = _forward_spectrum(c8[:, 4 * ct:], s8[:, 4 * ct:], xq[:, 4 * ct:], tw, ct)
    scale = jnp.where(row0, 1.0 / n, 2.0 / n)
    outs = ((kac_ref, kas_ref), (kbc_ref, kbs_ref), (kcc_ref, kcs_ref), (kdc_ref, kds_ref))
    for (oc, os_), (fc, fs), (bc, bs) in zip(outs, fz, bz):
        oc[...] = (fc + bc) * scale
        os_[...] = (fs - bs) * scale
    km_ref[...] = jnp.zeros_like(km_ref)
    for i, ((fc, fs), (bc, bs)) in enumerate(zip(fm, bm)):
        km_ref[2 * i:2 * i + 1, :] = (fc + bc) * (2.0 / n)
        km_ref[2 * i + 1:2 * i + 2, :] = (fs - bs) * (2.0 / n)


def _hyena_filters(seq, w1, b1, w2, b2, w3, b3, freq, decay, tables, n_order, n_dirs, d_hy):
    q = seq // 4
    emb, hid = w1.shape
    bands = (emb - 1) // 2
    t = jnp.arange(seq, dtype=F32)
    t01 = t / max(seq - 1, 1)
    band = jnp.linspace(1e-4, bands - 1, bands, dtype=F32)
    ang = (2.0 * math.pi) * t[:, None] * band[None, :] / seq
    feats = jnp.concatenate([t01[:, None], jnp.cos(ang), jnp.sin(ang)], axis=-1)
    kpad = LANE
    feats = jnp.pad(feats, ((0, 0), (0, kpad - emb)))
    w1p = jnp.pad(w1, ((0, kpad - emb), (0, 0)))
    tabs = [tables[k] for k in ("cm", "sm_fwd", "cw", "sw", "cv", "sv")]
    ct = tables["cw"].shape[1]
    nct = d_hy // ct
    ncol = n_order * d_hy
    b3r = b3.reshape(1, -1)
    decr = decay.reshape(1, -1)
    fcol = lambda o, c: (0, (o * n_dirs + 0) * nct + c)
    bcol = lambda o, c: (0, (o * n_dirs + 1) * nct + c)
    ocol = lambda o, c: (0, o * nct + c)
    full = lambda a: pl.BlockSpec(a.shape, lambda o, c: (0,) * a.ndim)
    spec = jax.ShapeDtypeStruct((q, ncol), F32)
    return pl.pallas_call(
        _hy_filter_kernel,
        out_shape=(spec,) * 8 + (jax.ShapeDtypeStruct((8, ncol), F32),),
        grid=(n_order, nct),
        in_specs=[
            full(feats), full(w1p), full(b1.reshape(1, -1)), full(w2), full(b2.reshape(1, -1)),
            full(freq.reshape(1, -1)),
            pl.BlockSpec((hid, ct), fcol), pl.BlockSpec((1, ct), fcol), pl.BlockSpec((1, ct), fcol),
            pl.BlockSpec((hid, ct), bcol), pl.BlockSpec((1, ct), bcol), pl.BlockSpec((1, ct), bcol),
        ] + [_const_spec(a.shape) for a in tabs],
        out_specs=tuple(pl.BlockSpec((q, ct), ocol) for _ in range(8)) + (pl.BlockSpec((8, ct), ocol),),
        scratch_shapes=[pltpu.VMEM((2 * ct // LANE, seq, LANE), F32), pltpu.VMEM((q, 8 * ct), BF16),
                        pltpu.VMEM((seq, hid), F32)],
        compiler_params=_cparams(2),
        name="hyena_filter_spectra",
    )(feats, w1p, b1.reshape(1, -1), w2, b2.reshape(1, -1), freq.reshape(1, -1),
      w3, b3r, decr, w3, b3r, decr, *tabs)


def _hy_conv_kernel(zin_ref, gin_ref, wz_ref, bz_ref, wg_ref, bg_ref,
                    kac_ref, kas_ref, kbc_ref, kbs_ref, kcc_ref, kcs_ref, kdc_ref, kds_ref, km_ref, bias_ref,
                    cm_ref, smf_ref, smi_ref, cw_ref, sw_ref, cv_ref, sv_ref, out_ref,
                    zraw_s, graw_s, o_s, z_s, g_s, rhs_s, ac_s, as_s, *, conv_on_z, seq):
    ct = zin_ref.shape[1]
    nbat = zin_ref.shape[0] // seq
    q = seq // 4
    nsl = ct // LANE
    pad = zraw_s.shape[2] - seq
    top = pad // 2
    rc = min(DFT_ROW_CHUNK, q)
    k_refs = ((kac_ref, kas_ref), (kbc_ref, kbs_ref), (kcc_ref, kcs_ref), (kdc_ref, kds_ref))

    def stage(raw_s, src_ref, k):
        for j in range(nsl):
            raw_s[k, j, 0:top, :] = jnp.zeros((top, LANE), F32)
            raw_s[k, j, top + seq:, :] = jnp.zeros((pad - top, LANE), F32)
            raw_s[k, j, top:top + seq, :] = src_ref[k * seq:(k + 1) * seq, j * LANE:(j + 1) * LANE].astype(F32)

    def split_rows(raw_s, dst_ref, k, w_ref, b_ref):
        for r in range(4):
            for j in range(nsl):
                ls = slice(j * LANE, (j + 1) * LANE)
                tap = lambda d: raw_s[k, j, pl.ds(top + r + d, q, stride=4), :]
                if w_ref is None:
                    v = tap(0)
                else:
                    v = (b_ref[:, ls] + tap(-1) * w_ref[0:1, ls] + tap(0) * w_ref[1:2, ls]
                         + tap(1) * w_ref[2:3, ls])
                dst_ref[k, :, r * ct + j * LANE:r * ct + (j + 1) * LANE] = v

    def spectrum_chunk(k, i):
        r = slice(i * rc, (i + 1) * rc)
        c4 = jnp.dot(cm_ref[r, :], rhs_s[k], preferred_element_type=F32)
        s4 = jnp.dot(smf_ref[r, :], rhs_s[k], preferred_element_type=F32)
        tw = _twiddles(cw_ref, sw_ref, cv_ref, sv_ref, r)
        if i == 0:
            row0 = lax.broadcasted_iota(jnp.int32, (rc, 1), 0) == 0
            xq = s4[0:1, :]
            s4 = jnp.where(row0, 0.0, s4)
        else:
            xq = jnp.zeros((1, 4 * ct), F32)
        z, zmid = _forward_spectrum(c4, s4, xq, tw, ct)
        y = [_cmul(zx, (kc[r, :], ks[r, :])) for zx, (kc, ks) in zip(z, k_refs)]
        ymid = [_cmul(zmid[m], (km_ref[2 * m:2 * m + 1, :], km_ref[2 * m + 1:2 * m + 2, :])) for m in range(2)]
        t, tmid = _inverse_spectrum(y, ymid, tw)
        for x, ((tc, ts), tm) in enumerate(zip(t, tmid)):
            if i == 0:
                ts = jnp.where(row0, tm, ts)
            ac_s[k, r, x * ct:(x + 1) * ct] = tc.astype(BF16)
            as_s[k, r, x * ct:(x + 1) * ct] = ts.astype(BF16)

    def output_chunk(k, i):
        r = slice(i * rc, (i + 1) * rc)
        y4 = jnp.dot(cm_ref[r, :], ac_s[k], preferred_element_type=F32)
        y4 = y4 + jnp.dot(smi_ref[r, :], as_s[k], preferred_element_type=F32)
        for x in range(4):
            rows = pl.ds(4 * i * rc + x, rc, stride=4)
            for j in range(nsl):
                ls = slice(x * ct + j * LANE, x * ct + (j + 1) * LANE)
                o_s[k, j, rows, :] = g_s[k, r, ls] * (y4[:, ls] + z_s[k, r, ls] * bias_ref[:, j * LANE:(j + 1) * LANE])

    for k in range(nbat):
        stage(zraw_s, zin_ref, k)
        stage(graw_s, gin_ref, k)
        split_rows(zraw_s, z_s, k, wz_ref if conv_on_z else None, bz_ref)
        split_rows(graw_s, g_s, k, wg_ref, bg_ref)
        rhs_s[k] = z_s[k].astype(BF16)
    for i in range(q // rc):
        for k in range(nbat):
            spectrum_chunk(k, i)
    for i in range(q // rc):
        for k in range(nbat):
            output_chunk(k, i)
    for k in range(nbat):
        for j in range(nsl):
            out_ref[k * seq:(k + 1) * seq, j * LANE:(j + 1) * LANE] = o_s[k, j].astype(out_ref.dtype)


def _hy_conv(zin, zcol0, gcol0, u_hy, conv_w, conv_b, spectra, bias_row, tables,
             *, order, conv_on_z, bsz, seq, d_hy):
    tabs = [tables[k] for k in HY_TABLE_KEYS]
    q = seq // 4
    ct = tables["cw"].shape[1]
    nsl = ct // LANE
    nct = d_hy // ct
    nbat = HY_BATCH_GROUP if bsz % HY_BATCH_GROUP == 0 else 1
    zc0 = zcol0 // ct
    gc0 = gcol0 // ct
    zw0 = zc0 if conv_on_z else 0
    kspec = lambda rows: pl.BlockSpec((rows, ct), lambda c, b: (0, order * nct + c),
                                      pipeline_mode=pl.Buffered(1))
    scratch = lambda dt: pltpu.VMEM((nbat, q, 4 * ct), dt)
    slabs = lambda rows: pltpu.VMEM((nbat, nsl, rows, LANE), F32)
    return pl.pallas_call(
        functools.partial(_hy_conv_kernel, conv_on_z=conv_on_z, seq=seq),
        out_shape=jax.ShapeDtypeStruct((bsz * seq, d_hy), BF16),
        grid=(nct, bsz // nbat),
        in_specs=[
            pl.BlockSpec((nbat * seq, ct), lambda c, b: (b, zc0 + c)),
            pl.BlockSpec((nbat * seq, ct), lambda c, b: (b, gc0 + c)),
            pl.BlockSpec((conv_w.shape[0], ct), lambda c, b: (0, zw0 + c)),
            pl.BlockSpec((1, ct), lambda c, b: (0, zw0 + c)),
            pl.BlockSpec((conv_w.shape[0], ct), lambda c, b: (0, gc0 + c)),
            pl.BlockSpec((1, ct), lambda c, b: (0, gc0 + c)),
        ] + [kspec(q)] * 8 + [kspec(8), kspec(1)] + [_const_spec(a.shape) for a in tabs],
        out_specs=pl.BlockSpec((nbat * seq, ct), lambda c, b: (b, c)),
        scratch_shapes=[
            slabs(seq + 2 * SUBLANE), slabs(seq + 2 * SUBLANE), slabs(seq),
            scratch(F32), scratch(F32), scratch(BF16), scratch(BF16), scratch(BF16),
        ],
        compiler_params=_cparams(2),
        name=f"hyena_conv_order{order}",
    )(zin, u_hy, conv_w, conv_b, conv_w, conv_b, *spectra, bias_row, *tabs)


def _hyena_branch(u_hy, bsz, seq, conv_w, conv_b, w1, b1, w2, b2, w3, b3, freq, decay, bias):
    n_order, d_hy = bias.shape
    n_dirs = w3.shape[1] // (n_order * d_hy)
    if n_order != 2 or seq % 8 != 0 or d_hy % LANE != 0:
        raise NotImplementedError("Hyena branch: two long convolutions, L % 8 == 0, 128-lane channel tiles")
    tables = _hyena_tables(seq, min(HY_CH_TILE, d_hy))
    spectra = _hyena_filters(seq, w1, b1, w2, b2, w3, b3, freq, decay, tables, n_order, n_dirs, d_hy)
    cb = conv_b.reshape(1, -1)
    bias_row = bias.reshape(1, -1)
    common = dict(bsz=bsz, seq=seq, d_hy=d_hy)
    z1 = _hy_conv(u_hy, 0, d_hy, u_hy, conv_w, cb, spectra, bias_row, tables,
                  order=0, conv_on_z=True, **common)
    return _hy_conv(z1, 0, 2 * d_hy, u_hy, conv_w, cb, spectra, bias_row, tables,
                    order=1, conv_on_z=False, **common)


def _mixer_kernel(x_ref, mod_ref, g1_ref, g2_ref, fg_ref, permt_ref, za_ref, zb_ref,
                  wgate_ref, wa_ref, wb_ref, wout_ref, wg_ref, wu_ref, wd_ref, o_ref, *, n_chunks):
    nb, tt, d = x_ref.shape
    rows = nb * tt
    mod = lambda k: mod_ref[:, k:k + 1, :]
    x = x_ref[...]
    h = _norm_modulate(x, g1_ref[...], mod(0), mod(1)).reshape(rows, d).astype(BF16)
    gate = jax.nn.sigmoid(jnp.dot(h, wgate_ref[...], preferred_element_type=F32))
    za = jnp.dot(permt_ref[...], za_ref[...], preferred_element_type=F32).astype(BF16)
    ya = jnp.dot(za, wa_ref[...], preferred_element_type=F32)
    yb = jnp.dot(zb_ref[...].reshape(rows, zb_ref.shape[-1]), wb_ref[...], preferred_element_type=F32)
    merged = gate[:, :d] * ya + gate[:, d:] * yb
    o = jnp.dot(merged.astype(BF16), wout_ref[...], preferred_element_type=F32)
    x1 = x + mod(2) * o.reshape(nb, tt, d)

    h2 = _norm_modulate(x1, g2_ref[...], mod(3), mod(4)).reshape(rows, d).astype(BF16)
    d_ff = wg_ref.shape[1]
    unit = MXU_TILE if d_ff % MXU_TILE == 0 else d_ff // n_chunks
    edges = [len(g) for g in _spread(list(range(d_ff // unit)), n_chunks)]
    acc = jnp.zeros((rows, d), F32)
    start = 0
    for width in edges:
        sl = slice(start * unit, (start + width) * unit)
        start += width
        gl = jnp.dot(h2, wg_ref[:, sl], preferred_element_type=F32)
        up = jnp.dot(h2, wu_ref[:, sl], preferred_element_type=F32)
        act = (gl * jax.nn.sigmoid(gl) * up).astype(BF16)
        acc = acc + jnp.dot(act, wd_ref[sl, :], preferred_element_type=F32)
    x2 = x1 + mod(5) * acc.reshape(nb, tt, d)
    ms = jnp.mean(x2 * x2, axis=-1, keepdims=True)
    o_ref[...] = x2 * lax.rsqrt(ms + EPS) * fg_ref[...]


def _mixer(x, mod3, norm1_g, norm2_g, final_g, perm_t, za_tm, zb, w_gate, w_a, w_b, w_out, w_g, w_u, w_d):
    bsz, seq, d = x.shape
    d_s5 = w_a.shape[0]
    d_hy = w_b.shape[0]
    d_ff = w_g.shape[1]
    tt = _time_tile(bsz, seq)
    n_chunks = 2 if d_ff % (2 * LANE) == 0 else 1
    consts = (mod3, norm1_g, norm2_g, final_g, perm_t)
    weights = (w_gate, w_a, w_b, w_out, w_g, w_u, w_d)
    return pl.pallas_call(
        functools.partial(_mixer_kernel, n_chunks=n_chunks),
        out_shape=jax.ShapeDtypeStruct(x.shape, F32),
        grid=(seq // tt,),
        in_specs=[pl.BlockSpec((bsz, tt, d), lambda j: (0, j, 0))]
        + [_const_spec(a.shape) for a in consts]
        + [pl.BlockSpec((tt * bsz, d_s5), lambda j: (j, 0)),
           pl.BlockSpec((bsz, tt, d_hy), lambda j: (0, j, 0))]
        + [_const_spec(a.shape) for a in weights],
        out_specs=pl.BlockSpec((bsz, tt, d), lambda j: (0, j, 0)),
        compiler_params=_cparams(1),
        name="merge_swiglu_final_norm",
    )(x, *consts, za_tm, zb, *weights)


def kernel(x, c, ada_w, ada_b, norm1_g, norm2_g, w_in, s5_lam_re, s5_lam_im, s5_log_step, s5_b_re, s5_b_im, s5_c_re, s5_c_im, s5_d, s5_glu_w, s5_glu_b, hy_conv_w, hy_conv_b, hy_ffn_w1, hy_ffn_b1, hy_ffn_w2, hy_ffn_b2, hy_ffn_w3, hy_ffn_b3, hy_freq, hy_decay, hy_bias, w_branch_a, w_branch_b, w_out, ffn_w_gu, ffn_w_down, final_g):
    bsz, seq, d = x.shape
    depth = ada_w.shape[0]
    if depth != 1:
        raise NotImplementedError("the final RMSNorm is fused into the (single) layer's channel mixer")
    d_s5 = s5_d.shape[-1]
    n_order, d_hy = hy_bias.shape[1:]
    d_uh = d_s5 + (n_order + 1) * d_hy
    d_ff = ffn_w_down.shape[1]
    i = 0
    perm = _row_permutation(bsz, _time_tile(bsz, seq))
    mod = _modulation(c, ada_w[i], ada_b[i]).reshape(bsz, 6, d)
    w_in_b = w_in[i].astype(BF16)
    u_s5, u_hy = _in_projection(x, mod, norm1_g[i].reshape(1, d), w_in_b[:, :d_uh], perm, d_s5)
    z_a = _s5_branch(u_s5, bsz, s5_lam_re[i], s5_lam_im[i], s5_log_step[i], s5_b_re[i], s5_b_im[i],
                     s5_c_re[i], s5_c_im[i], s5_d[i], s5_glu_w[i], s5_glu_b[i])
    z_b = _hyena_branch(u_hy.reshape(bsz * seq, -1), bsz, seq, hy_conv_w[i], hy_conv_b[i], hy_ffn_w1[i],
                        hy_ffn_b1[i], hy_ffn_w2[i], hy_ffn_b2[i], hy_ffn_w3[i], hy_ffn_b3[i], hy_freq[i],
                        hy_decay[i], hy_bias[i])
    w_gu = ffn_w_gu[i].astype(BF16)
    return _mixer(x, mod, norm1_g[i].reshape(1, d), norm2_g[i].reshape(1, d), final_g.reshape(1, d),
                  perm.T, z_a, z_b.reshape(bsz, seq, d_hy), w_in_b[:, d_uh:],
                  w_branch_a[i].astype(BF16), w_branch_b[i].astype(BF16), w_out[i].astype(BF16),
                  w_gu[:, :d_ff], w_gu[:, d_ff:], ffn_w_down[i].astype(BF16))
```

```python
import functools
import math

import jax
import jax.numpy as jnp
from jax import lax
from jax.experimental import pallas as pl
from jax.experimental.pallas import tpu as pltpu

F32 = jnp.float32
BF16 = jnp.bfloat16
EPS = 1e-6
HIGHEST = lax.Precision.HIGHEST

V7X_VMEM_BYTES = 64 * 1024 * 1024
VMEM_LIMIT_BYTES = 56 * 1024 * 1024
LANE = 128
SUBLANE = 8
ROW_TILE = 512
S5_TIME_CHUNK = 64
S5_LANE_CHUNK = 512
S5_CH_SPLIT = 256
S5_STEPS_PER_GROUP = 8
MXU_TILE = 256
HY_CH_TILE = 256
HY_BATCH_GROUP = 2
DFT_ROW_CHUNK = 512


def _cparams(n_axes):
    return pltpu.CompilerParams(
        dimension_semantics=("arbitrary",) * n_axes,
        vmem_limit_bytes=VMEM_LIMIT_BYTES,
    )


def _const_spec(shape):
    nd = len(shape)
    return pl.BlockSpec(shape, lambda *_: (0,) * nd, pipeline_mode=pl.Buffered(1))


def _gelu_tanh(x):
    return 0.5 * x * (1.0 + jnp.tanh(math.sqrt(2.0 / math.pi) * (x + 0.044715 * (x * x * x))))


def _norm_modulate(x, g, shift, scale):
    ms = jnp.mean(x * x, axis=-1, keepdims=True)
    r = x * lax.rsqrt(ms + EPS) * g
    return r * (1.0 + scale) + shift


def _mod_kernel(c_ref, w_ref, b_ref, o_ref):
    c = c_ref[...]
    ca = c * jax.nn.sigmoid(c)
    o_ref[...] = jnp.dot(ca, w_ref[...], precision=HIGHEST, preferred_element_type=F32) + b_ref[...]


def _modulation(c, ada_w, ada_b):
    bsz, d = c.shape
    n = ada_w.shape[1]
    tn = 512
    return pl.pallas_call(
        _mod_kernel,
        out_shape=jax.ShapeDtypeStruct((bsz, n), F32),
        grid=(n // tn,),
        in_specs=[
            pl.BlockSpec((bsz, d), lambda j: (0, 0)),
            pl.BlockSpec((d, tn), lambda j: (0, j)),
            pl.BlockSpec((1, tn), lambda j: (0, j)),
        ],
        out_specs=pl.BlockSpec((bsz, tn), lambda j: (0, j)),
        compiler_params=_cparams(1),
        name="adaln_mod",
    )(c, ada_w, ada_b.reshape(1, n))


def _time_tile(bsz, seq):
    return max(min(ROW_TILE // bsz, seq), 1)


def _row_permutation(bsz, tt):
    r = jnp.arange(bsz * tt, dtype=jnp.int32)
    src = (r % bsz) * tt + r // bsz
    return (src[:, None] == r[None, :]).astype(BF16)


def _inproj_kernel(x_ref, mod_ref, g_ref, w_ref, perm_ref, us5_ref, uhy_ref, *, d_s5):
    nb, tt, d = x_ref.shape
    h = _norm_modulate(x_ref[...], g_ref[...], mod_ref[:, 0:1, :], mod_ref[:, 1:2, :])
    hb = h.reshape(nb * tt, d).astype(BF16)
    p = jnp.dot(hb, w_ref[...], preferred_element_type=F32)
    us5_ref[...] = jnp.dot(perm_ref[...], p[:, :d_s5].astype(BF16),
                           preferred_element_type=F32).astype(us5_ref.dtype)
    uhy_ref[...] = p[:, d_s5:].astype(BF16).reshape(uhy_ref.shape)


def _in_projection(x, mod3, norm_g, w_uh, perm, d_s5):
    bsz, seq, d = x.shape
    n = w_uh.shape[1]
    tt = _time_tile(bsz, seq)
    return pl.pallas_call(
        functools.partial(_inproj_kernel, d_s5=d_s5),
        out_shape=(
            jax.ShapeDtypeStruct((seq * bsz, d_s5), BF16),
            jax.ShapeDtypeStruct((bsz, seq, n - d_s5), BF16),
        ),
        grid=(seq // tt,),
        in_specs=[
            pl.BlockSpec((bsz, tt, d), lambda j: (0, j, 0)),
            _const_spec(mod3.shape),
            _const_spec((1, d)),
            _const_spec((d, n)),
            _const_spec(perm.shape),
        ],
        out_specs=(
            pl.BlockSpec((tt * bsz, d_s5), lambda j: (j, 0)),
            pl.BlockSpec((bsz, tt, n - d_s5), lambda j: (0, j, 0)),
        ),
        compiler_params=_cparams(1),
        name="in_proj",
    )(x, mod3, norm_g, w_uh, perm)


def _s5_prep_kernel(lre_ref, lim_ref, lstep_ref, bre_ref, bim_ref, cim_ref,
                    are_ref, aim_ref, bbre_ref, bbim_ref, ncim_ref):
    step = jnp.exp(lstep_ref[...])
    lr = lre_ref[...]
    li = lim_ref[...]
    mag = jnp.exp(lr * step)
    ar = mag * jnp.cos(li * step)
    ai = mag * jnp.sin(li * step)
    num = ar - 1.0
    den = lr * lr + li * li
    cr = (num * lr + ai * li) / den
    ci = (ai * lr - num * li) / den
    are_ref[...] = ar
    aim_ref[...] = ai
    for d in range(lre_ref.shape[0]):
        br = bre_ref[d]
        bi = bim_ref[d]
        bbre_ref[d] = cr[d:d + 1, :] * br - ci[d:d + 1, :] * bi
        bbim_ref[d] = cr[d:d + 1, :] * bi + ci[d:d + 1, :] * br
    ncim_ref[...] = -cim_ref[...]


def _s5_prepare(lam_re, lam_im, log_step, b_re, b_im, c_im):
    outs = (
        jax.ShapeDtypeStruct(lam_re.shape, F32),
        jax.ShapeDtypeStruct(lam_re.shape, F32),
        jax.ShapeDtypeStruct(b_re.shape, F32),
        jax.ShapeDtypeStruct(b_re.shape, F32),
        jax.ShapeDtypeStruct(c_im.shape, F32),
    )
    return pl.pallas_call(
        _s5_prep_kernel,
        out_shape=outs,
        compiler_params=pltpu.CompilerParams(vmem_limit_bytes=VMEM_LIMIT_BYTES),
        name="s5_discretize",
    )(lam_re, lam_im, log_step, b_re, b_im, c_im)


def _block_diag(m):
    g, r, c = m.shape
    eye = jnp.eye(g, dtype=m.dtype)
    return (m[:, :, None, :] * eye[:, None, :, None]).reshape(g * r, g * c)


def _spread(items, n):
    return [items[(len(items) * i) // n:(len(items) * (i + 1)) // n] for i in range(n)]


def _s5_scan_kernel(*refs, reverse, final, tc, nb):
    if final:
        (u_ref, bd_ref, cd_ref, are_ref, aim_ref, yprev_ref, gluw_ref, glub_ref,
         out_ref, bu_ref, sb_ref, y_ref, s_ref) = refs
    else:
        (u_ref, bd_ref, cd_ref, are_ref, aim_ref, d_ref, out_ref, bu_ref, sb_ref, y_ref, s_ref) = refs
    rows = tc * nb
    nsplit, ks, two_nss = bd_ref.shape
    nss = two_nss // 2
    lc = min(S5_LANE_CHUNK, nss)
    tw = min(MXU_TILE, lc)
    rh = rows // 2 if rows % (2 * SUBLANE * 2) == 0 else rows
    row_halves = [slice(r, r + rh) for r in range(0, rows, rh)]

    @pl.when(pl.program_id(0) == 0)
    def _():
        s_ref[...] = jnp.zeros_like(s_ref)

    units = [(h, q) for h in range(nsplit) for q in range(nss // lc)]

    def lanes(unit, part, off=0, width=None):
        h, q = unit
        start = h * two_nss + part * nss + q * lc + off
        return slice(start, start + (lc if width is None else width))

    def expand_pieces(unit):
        h, q = unit
        out = []
        for part in range(2):
            for off in range(0, lc, tw):
                for rs in row_halves:
                    def piece(part=part, off=off, rs=rs):
                        col = part * nss + q * lc + off
                        bu_ref[rs, lanes(unit, part, off, tw)] = jnp.dot(
                            u_ref[rs, h * ks:(h + 1) * ks], bd_ref[h, :, col:col + tw],
                            preferred_element_type=F32)
                    out.append(piece)
        return out

    def readout_pieces(unit):
        h, q = unit
        out = []
        for part in range(2):
            for off in range(0, lc, tw):
                for rs in row_halves:
                    first = q == 0 and part == 0 and off == 0
                    def piece(part=part, off=off, rs=rs, first=first):
                        k0 = part * nss + q * lc + off
                        v = jnp.dot(sb_ref[rs, lanes(unit, part, off, tw)], cd_ref[h, k0:k0 + tw, :],
                                    preferred_element_type=F32)
                        if first:
                            y_ref[rs, h * ks:(h + 1) * ks] = v
                        else:
                            y_ref[rs, h * ks:(h + 1) * ks] += v
                    out.append(piece)
        return out

    step_groups = _spread(list(range(tc)), max(tc // S5_STEPS_PER_GROUP, 1))
    for p in expand_pieces(units[0]):
        p()
    for c, unit in enumerate(units):
        mxu_work = (expand_pieces(units[c + 1]) if c + 1 < len(units) else []) + \
                   (readout_pieces(units[c - 1]) if c > 0 else [])
        mxu_groups = _spread(mxu_work, len(step_groups))
        a_sl = slice(unit[0] * nss + unit[1] * lc, unit[0] * nss + (unit[1] + 1) * lc)
        a_re = are_ref[:, a_sl]
        a_im = aim_ref[:, a_sl]
        re_sl, im_sl = lanes(unit, 0), lanes(unit, 1)
        sr = s_ref[:, re_sl]
        si = s_ref[:, im_sl]
        for steps, pieces in zip(step_groups, mxu_groups):
            for p in pieces:
                p()
            for k in steps:
                t = (tc - 1 - k) if reverse else k
                row = slice(t * nb, (t + 1) * nb)
                nr = a_re * sr - a_im * si + bu_ref[row, re_sl]
                ni = a_re * si + a_im * sr + bu_ref[row, im_sl]
                sb_ref[row, re_sl] = nr.astype(BF16)
                sb_ref[row, im_sl] = ni.astype(BF16)
                sr, si = nr, ni
        s_ref[:, re_sl] = sr
        s_ref[:, im_sl] = si
    for p in readout_pieces(units[-1]):
        p()

    rb = min(256, rows)
    for r in range(rows // rb):
        rs = slice(r * rb, (r + 1) * rb)
        if final:
            z = _gelu_tanh(yprev_ref[rs, :] + y_ref[rs, :])
            gate = jnp.dot(z.astype(BF16), gluw_ref[...], preferred_element_type=F32) + glub_ref[...]
            out_ref[rs, :] = (z * jax.nn.sigmoid(gate)).astype(out_ref.dtype)
        else:
            out_ref[rs, :] = y_ref[rs, :] + u_ref[rs, :] * d_ref[...]


def _s5_scan(u_tm, bd, cd, a_re, a_im, extra, *, reverse, final, nb, out_dtype):
    rows_total, d_s5 = u_tm.shape
    seq = rows_total // nb
    two_ns = bd.shape[0] * bd.shape[2]
    tc = min(S5_TIME_CHUNK, seq)
    nchunk = seq // tc
    rows = tc * nb
    if reverse:
        cidx = lambda i: (nchunk - 1 - i, 0)
    else:
        cidx = lambda i: (i, 0)
    in_specs = [
        pl.BlockSpec((rows, d_s5), cidx),
        _const_spec(bd.shape),
        _const_spec(cd.shape),
        _const_spec(a_re.shape),
        _const_spec(a_im.shape),
    ]
    if final:
        yprev, glu_w, glu_b = extra
        in_specs += [pl.BlockSpec((rows, d_s5), cidx), _const_spec(glu_w.shape), _const_spec(glu_b.shape)]
    else:
        in_specs += [_const_spec(extra[0].shape)]
    return pl.pallas_call(
        functools.partial(_s5_scan_kernel, reverse=reverse, final=final, tc=tc, nb=nb),
        out_shape=jax.ShapeDtypeStruct((rows_total, d_s5), out_dtype),
        grid=(nchunk,),
        in_specs=in_specs,
        out_specs=pl.BlockSpec((rows, d_s5), cidx),
        scratch_shapes=[pltpu.VMEM((rows, two_ns), F32), pltpu.VMEM((rows, two_ns), BF16),
                        pltpu.VMEM((rows, d_s5), F32), pltpu.VMEM((nb, two_ns), F32)],
        compiler_params=_cparams(1),
        name="s5_scan_bwd_glu" if final else "s5_scan_fwd",
    )(u_tm, bd, cd, a_re, a_im, *extra)


def _s5_branch(u_tm, nb, lam_re, lam_im, log_step, b_re, b_im, c_re, c_im, d, glu_w, glu_b):
    ndir, g, p = lam_re.shape
    grp = b_re.shape[-1]
    ns = g * p
    d_s5 = g * grp
    ks = S5_CH_SPLIT if d_s5 % S5_CH_SPLIT == 0 else d_s5
    nsplit = d_s5 // ks
    gs = g // nsplit
    flat = lambda a: a.reshape(ndir, ns)
    to_lanes = lambda a: jnp.transpose(a, (0, 3, 1, 2)).reshape(ndir, grp, ns)
    a_re, a_im, bb_re, bb_im, ncim = _s5_prepare(
        flat(lam_re), flat(lam_im), jnp.repeat(log_step, p, axis=-1),
        to_lanes(b_re), to_lanes(b_im),
        jnp.transpose(c_im, (0, 2, 1, 3)).reshape(ndir, grp, ns))

    def in_blocks(a):
        a = jnp.transpose(a.reshape(grp, nsplit, gs, p), (1, 2, 0, 3))
        return jnp.stack([_block_diag(a[h]) for h in range(nsplit)])

    def out_blocks(a):
        a = jnp.transpose(a.reshape(nsplit, gs, grp, p), (0, 1, 3, 2))
        return jnp.stack([_block_diag(a[h]) for h in range(nsplit)])

    y = None
    for direction in range(ndir):
        bd = jnp.concatenate([in_blocks(bb_re[direction]), in_blocks(bb_im[direction])], axis=2).astype(BF16)
        ncim_g = jnp.transpose(ncim[direction].reshape(grp, g, p), (1, 0, 2))
        cd = jnp.concatenate([out_blocks(c_re[direction]), out_blocks(ncim_g)], axis=1).astype(BF16)
        are = jnp.broadcast_to(a_re[direction][None, :], (nb, ns))
        aim = jnp.broadcast_to(a_im[direction][None, :], (nb, ns))
        if direction == 0:
            if ndir != 2:
                raise NotImplementedError("S5 branch expects forward and backward directions")
            y = _s5_scan(u_tm, bd, cd, are, aim, (d.reshape(1, -1),), reverse=False, final=False,
                         nb=nb, out_dtype=F32)
        else:
            extra = (y, glu_w.astype(BF16), glu_b.reshape(1, -1))
            y = _s5_scan(u_tm, bd, cd, are, aim, extra, reverse=True, final=True, nb=nb, out_dtype=BF16)
    return y


RSQRT2 = math.sqrt(0.5)


def _dft_table(nf, nt, n):
    f_lo = 64 if nf % 64 == 0 else 1
    f_hi = nf // f_lo
    t = jnp.arange(nt, dtype=jnp.int32)[None, :]
    ka = (jnp.arange(f_hi, dtype=jnp.int32)[:, None] * f_lo * t) % n
    kb = (jnp.arange(f_lo, dtype=jnp.int32)[:, None] * t) % n
    w = 2.0 * math.pi / n
    aa = ka.astype(F32) * w
    ab = kb.astype(F32) * w
    ca, sa, cb, sb = jnp.cos(aa), jnp.sin(aa), jnp.cos(ab), jnp.sin(ab)
    cm = ca[:, None, :] * cb[None, :, :] - sa[:, None, :] * sb[None, :, :]
    sm = sa[:, None, :] * cb[None, :, :] + ca[:, None, :] * sb[None, :, :]
    return cm.reshape(nf, nt), sm.reshape(nf, nt)


def _hyena_tables(seq):
    q = seq // 4
    cfull, sfull = _dft_table(q, seq, 2 * seq)
    alt = (1 - 2 * (jnp.arange(q, dtype=jnp.int32) & 1)).astype(F32)
    cf = jnp.stack([cfull[:, r::4] for r in range(4)])
    sf = jnp.stack([sfull[:, r::4] for r in range(4)])
    ci = jnp.transpose(cf, (0, 2, 1))
    si = jnp.transpose(sf, (0, 2, 1))
    return dict(c_fwd=cf.astype(BF16), s_fwd=sf.at[:, 0, :].set(alt).astype(BF16),
                c_inv=ci.astype(BF16), s_inv=si.at[:, :, 0].set(alt).astype(BF16))


def _cmul(z, k):
    (zc, zs), (kc, ks) = z, k
    return zc * kc - zs * ks, zc * ks + zs * kc


def _forward_spectrum(x, a):
    (c0, s0), (c1, s1), (c2, s2), (c3, s3) = x
    pc, ps, mc, ms = c0 + c2, s0 + s2, c0 - c2, s0 - s2
    qc, qs, nc, ns = c1 + c3, s1 + s3, c1 - c3, s1 - s3
    za = (pc + qc, ps + qs)
    zb = (pc - qc, qs - ps)
    zc = (mc + ns, nc - ms)
    zd = (mc - ns, ms + nc)
    e = (a[1] - a[3]) * RSQRT2
    o = (a[1] + a[3]) * RSQRT2
    return [za, zb, zc, zd], [(a[0] + e, a[2] + o), (a[0] - e, o - a[2])]


def _inverse_spectrum(y, ymid):
    (ac, as_), (bc, bs), (cc, cs), (dc, ds) = y
    upc, ups, umc, ums = ac + bc, as_ - bs, ac - bc, as_ + bs
    vpc, vps, vmc, vms = dc + cc, ds - cs, dc - cc, ds + cs
    t = [(upc + vpc, ups + vps), (umc + vms, ums - vmc), (upc - vpc, ups - vps), (umc - vms, ums + vmc)]
    (y1c, y1s), (y3c, y3s) = ymid
    sp = [y1c + y3c, ((y1c + y1s) + (y3s - y3c)) * RSQRT2, y1s - y3s, ((y1s - y1c) + (y3c + y3s)) * RSQRT2]
    return t, sp


def _hy_filter_kernel(feat_ref, w1_ref, b1_ref, w2_ref, b2_ref, freq_ref,
                      w3f_ref, b3f_ref, decf_ref, w3b_ref, b3b_ref, decb_ref,
                      cf_ref, sf_ref,
                      kac_ref, kas_ref, kbc_ref, kbs_ref, kcc_ref, kcs_ref, kdc_ref, kds_ref, km_ref,
                      taps_s, rhs_s, h_s):
    seq = feat_ref.shape[0]
    q = seq // 4
    n = 2 * seq
    ct = kac_ref.shape[1]
    nsl = ct // LANE

    @pl.when((pl.program_id(0) == 0) & (pl.program_id(1) == 0))
    def _():
        f = freq_ref[...]
        h1 = jnp.sin(f * (jnp.dot(feat_ref[...], w1_ref[...], precision=HIGHEST,
                                  preferred_element_type=F32) + b1_ref[...]))
        h_s[...] = jnp.sin(f * (jnp.dot(h1, w2_ref[...], precision=HIGHEST,
                                        preferred_element_type=F32) + b2_ref[...]))

    h = h_s[...]
    t01 = feat_ref[:, 0:1]
    row = lax.broadcasted_iota(jnp.int32, (seq, 1), 0)

    def taps(w3_ref, b3_ref, dec_ref):
        v = jnp.dot(h, w3_ref[...], precision=HIGHEST, preferred_element_type=F32) + b3_ref[...]
        return v * jnp.exp(-t01 * jnp.abs(dec_ref[...]))

    fwd = taps(w3f_ref, b3f_ref, decf_ref)
    bwd = jnp.where(row == 0, 0.0, taps(w3b_ref, b3b_ref, decb_ref))
    for d, x in enumerate((fwd, bwd)):
        for j in range(nsl):
            taps_s[d * nsl + j] = x[:, j * LANE:(j + 1) * LANE]
        for r in range(4):
            for j in range(nsl):
                c0 = (4 * d + r) * ct + j * LANE
                rhs_s[:, c0:c0 + LANE] = taps_s[d * nsl + j, pl.ds(r, q, stride=4), :].astype(BF16)
    row0 = lax.broadcasted_iota(jnp.int32, (q, 1), 0) == 0

    def spectrum(d):
        x, a = [], []
        for r in range(4):
            rhs = rhs_s[:, (4 * d + r) * ct:(4 * d + r + 1) * ct]
            c = jnp.dot(cf_ref[r], rhs, preferred_element_type=F32)
            sn = jnp.dot(sf_ref[r], rhs, preferred_element_type=F32)
            a.append(sn[0:1, :])
            x.append((c, jnp.where(row0, 0.0, sn)))
        return _forward_spectrum(x, a)

    fz, fm = spectrum(0)
    bz, bm = spectrum(1)
    scale = jnp.where(row0, 1.0 / n, 2.0 / n)
    outs = ((kac_ref, kas_ref), (kbc_ref, kbs_ref), (kcc_ref, kcs_ref), (kdc_ref, kds_ref))
    for (oc, os_), (fc, fs), (bc, bs) in zip(outs, fz, bz):
        oc[...] = (fc + bc) * scale
        os_[...] = (fs - bs) * scale
    km_ref[...] = jnp.zeros_like(km_ref)
    for i, ((fc, fs), (bc, bs)) in enumerate(zip(fm, bm)):
        km_ref[2 * i:2 * i + 1, :] = (fc + bc) * (2.0 / n)
        km_ref[2 * i + 1:2 * i + 2, :] = (fs - bs) * (2.0 / n)


def _hyena_filters(seq, w1, b1, w2, b2, w3, b3, freq, decay, tables, n_order, n_dirs, d_hy):
    q = seq // 4
    emb, hid = w1.shape
    bands = (emb - 1) // 2
    t = jnp.arange(seq, dtype=F32)
    t01 = t / max(seq - 1, 1)
    band = jnp.linspace(1e-4, bands - 1, bands, dtype=F32)
    ang = (2.0 * math.pi) * t[:, None] * band[None, :] / seq
    feats = jnp.concatenate([t01[:, None], jnp.cos(ang), jnp.sin(ang)], axis=-1)
    kpad = LANE
    feats = jnp.pad(feats, ((0, 0), (0, kpad - emb)))
    w1p = jnp.pad(w1, ((0, kpad - emb), (0, 0)))
    tabs = [tables["c_fwd"], tables["s_fwd"]]
    ct = min(HY_CH_TILE, d_hy)
    nct = d_hy // ct
    ncol = n_order * d_hy
    b3r = b3.reshape(1, -1)
    decr = decay.reshape(1, -1)
    fcol = lambda o, c: (0, (o * n_dirs + 0) * nct + c)
    bcol = lambda o, c: (0, (o * n_dirs + 1) * nct + c)
    ocol = lambda o, c: (0, o * nct + c)
    full = lambda a: pl.BlockSpec(a.shape, lambda o, c: (0,) * a.ndim)
    spec = jax.ShapeDtypeStruct((q, ncol), F32)
    return pl.pallas_call(
        _hy_filter_kernel,
        out_shape=(spec,) * 8 + (jax.ShapeDtypeStruct((8, ncol), F32),),
        grid=(n_order, nct),
        in_specs=[
            full(feats), full(w1p), full(b1.reshape(1, -1)), full(w2), full(b2.reshape(1, -1)),
            full(freq.reshape(1, -1)),
            pl.BlockSpec((hid, ct), fcol), pl.BlockSpec((1, ct), fcol), pl.BlockSpec((1, ct), fcol),
            pl.BlockSpec((hid, ct), bcol), pl.BlockSpec((1, ct), bcol), pl.BlockSpec((1, ct), bcol),
        ] + [_const_spec(a.shape) for a in tabs],
        out_specs=tuple(pl.BlockSpec((q, ct), ocol) for _ in range(8)) + (pl.BlockSpec((8, ct), ocol),),
        scratch_shapes=[pltpu.VMEM((2 * ct // LANE, seq, LANE), F32), pltpu.VMEM((q, 8 * ct), BF16),
                        pltpu.VMEM((seq, hid), F32)],
        compiler_params=_cparams(2),
        name="hyena_filter_spectra",
    )(feats, w1p, b1.reshape(1, -1), w2, b2.reshape(1, -1), freq.reshape(1, -1),
      w3, b3r, decr, w3, b3r, decr, *tabs)


def _hy_conv_kernel(zin_ref, gin_ref, wz_ref, bz_ref, wg_ref, bg_ref,
                    kac_ref, kas_ref, kbc_ref, kbs_ref, kcc_ref, kcs_ref, kdc_ref, kds_ref, km_ref, bias_ref,
                    cf_ref, sf_ref, ci_ref, si_ref, out_ref,
                    zraw_s, graw_s, o_s, z_s, g_s, rhs_s, ac_s, as_s, *, conv_on_z, seq):
    ct = zin_ref.shape[1]
    nbat = zin_ref.shape[0] // seq
    q = seq // 4
    nsl = ct // LANE
    pad = zraw_s.shape[2] - seq
    top = pad // 2
    rc = min(DFT_ROW_CHUNK, q)
    k_refs = ((kac_ref, kas_ref), (kbc_ref, kbs_ref), (kcc_ref, kcs_ref), (kdc_ref, kds_ref))

    def stage(raw_s, src_ref, k):
        for j in range(nsl):
            raw_s[k, j, 0:top, :] = jnp.zeros((top, LANE), F32)
            raw_s[k, j, top + seq:, :] = jnp.zeros((pad - top, LANE), F32)
            raw_s[k, j, top:top + seq, :] = src_ref[k * seq:(k + 1) * seq, j * LANE:(j + 1) * LANE].astype(F32)

    def split_rows(raw_s, dst_ref, k, w_ref, b_ref):
        for r in range(4):
            for j in range(nsl):
                ls = slice(j * LANE, (j + 1) * LANE)
                tap = lambda d: raw_s[k, j, pl.ds(top + r + d, q, stride=4), :]
                if w_ref is None:
                    v = tap(0)
                else:
                    v = (b_ref[:, ls] + tap(-1) * w_ref[0:1, ls] + tap(0) * w_ref[1:2, ls]
                         + tap(1) * w_ref[2:3, ls])
                dst_ref[k, :, r * ct + j * LANE:r * ct + (j + 1) * LANE] = v

    def spectrum_chunk(k, i):
        r = slice(i * rc, (i + 1) * rc)
        row0 = lax.broadcasted_iota(jnp.int32, (rc, 1), 0) == 0
        x, a = [], []
        for p in range(4):
            rhs = rhs_s[k, :, p * ct:(p + 1) * ct]
            c = jnp.dot(cf_ref[p, r, :], rhs, preferred_element_type=F32)
            sn = jnp.dot(sf_ref[p, r, :], rhs, preferred_element_type=F32)
            if i == 0:
                a.append(sn[0:1, :])
                sn = jnp.where(row0, 0.0, sn)
            else:
                a.append(jnp.zeros((1, ct), F32))
            x.append((c, sn))
        z, zmid = _forward_spectrum(x, a)
        y = [_cmul(zx, (kc[r, :], ks[r, :])) for zx, (kc, ks) in zip(z, k_refs)]
        ymid = [_cmul(zmid[m], (km_ref[2 * m:2 * m + 1, :], km_ref[2 * m + 1:2 * m + 2, :])) for m in range(2)]
        t, sp = _inverse_spectrum(y, ymid)
        for p, ((tc, ts), spp) in enumerate(zip(t, sp)):
            if i == 0:
                ts = jnp.where(row0, spp, ts)
            ac_s[k, r, p * ct:(p + 1) * ct] = tc.astype(BF16)
            as_s[k, r, p * ct:(p + 1) * ct] = ts.astype(BF16)

    def output_chunk(k, i):
        r = slice(i * rc, (i + 1) * rc)
        for p in range(4):
            ps = slice(p * ct, (p + 1) * ct)
            y = jnp.dot(ci_ref[p, r, :], ac_s[k, :, ps], preferred_element_type=F32)
            y = y + jnp.dot(si_ref[p, r, :], as_s[k, :, ps], preferred_element_type=F32)
            rows = pl.ds(4 * i * rc + p, rc, stride=4)
            for j in range(nsl):
                ls = slice(p * ct + j * LANE, p * ct + (j + 1) * LANE)
                jl = slice(j * LANE, (j + 1) * LANE)
                o_s[k, j, rows, :] = g_s[k, r, ls] * (y[:, jl] + z_s[k, r, ls] * bias_ref[:, jl])

    for k in range(nbat):
        stage(zraw_s, zin_ref, k)
        stage(graw_s, gin_ref, k)
        split_rows(zraw_s, z_s, k, wz_ref if conv_on_z else None, bz_ref)
        split_rows(graw_s, g_s, k, wg_ref, bg_ref)
        rhs_s[k] = z_s[k].astype(BF16)
    for i in range(q // rc):
        for k in range(nbat):
            spectrum_chunk(k, i)
    for i in range(q // rc):
        for k in range(nbat):
            output_chunk(k, i)
    for k in range(nbat):
        for j in range(nsl):
            out_ref[k * seq:(k + 1) * seq, j * LANE:(j + 1) * LANE] = o_s[k, j].astype(out_ref.dtype)


def _hy_conv(zin, zcol0, gcol0, u_hy, conv_w, conv_b, spectra, bias_row, tables,
             *, order, conv_on_z, bsz, seq, d_hy):
    tabs = [tables[k] for k in ("c_fwd", "s_fwd", "c_inv", "s_inv")]
    q = seq // 4
    ct = min(HY_CH_TILE, d_hy)
    nsl = ct // LANE
    nct = d_hy // ct
    nbat = HY_BATCH_GROUP if bsz % HY_BATCH_GROUP == 0 else 1
    zc0 = zcol0 // ct
    gc0 = gcol0 // ct
    zw0 = zc0 if conv_on_z else 0
    kspec = lambda rows: pl.BlockSpec((rows, ct), lambda c, b: (0, order * nct + c),
                                      pipeline_mode=pl.Buffered(1))
    scratch = lambda dt: pltpu.VMEM((nbat, q, 4 * ct), dt)
    slabs = lambda rows: pltpu.VMEM((nbat, nsl, rows, LANE), F32)
    return pl.pallas_call(
        functools.partial(_hy_conv_kernel, conv_on_z=conv_on_z, seq=seq),
        out_shape=jax.ShapeDtypeStruct((bsz * seq, d_hy), BF16),
        grid=(nct, bsz // nbat),
        in_specs=[
            pl.BlockSpec((nbat * seq, ct), lambda c, b: (b, zc0 + c)),
            pl.BlockSpec((nbat * seq, ct), lambda c, b: (b, gc0 + c)),
            pl.BlockSpec((conv_w.shape[0], ct), lambda c, b: (0, zw0 + c)),
            pl.BlockSpec((1, ct), lambda c, b: (0, zw0 + c)),
            pl.BlockSpec((conv_w.shape[0], ct), lambda c, b: (0, gc0 + c)),
            pl.BlockSpec((1, ct), lambda c, b: (0, gc0 + c)),
        ] + [kspec(q)] * 8 + [kspec(8), kspec(1)] + [_const_spec(a.shape) for a in tabs],
        out_specs=pl.BlockSpec((nbat * seq, ct), lambda c, b: (b, c)),
        scratch_shapes=[
            slabs(seq + 2 * SUBLANE), slabs(seq + 2 * SUBLANE), slabs(seq),
            scratch(F32), scratch(F32), scratch(BF16), scratch(BF16), scratch(BF16),
        ],
        compiler_params=_cparams(2),
        name=f"hyena_conv_order{order}",
    )(zin, u_hy, conv_w, conv_b, conv_w, conv_b, *spectra, bias_row, *tabs)


def _hyena_branch(u_hy, bsz, seq, conv_w, conv_b, w1, b1, w2, b2, w3, b3, freq, decay, bias):
    n_order, d_hy = bias.shape
    n_dirs = w3.shape[1] // (n_order * d_hy)
    if n_order != 2 or seq % 8 != 0 or d_hy % LANE != 0:
        raise NotImplementedError("Hyena branch: two long convolutions, L % 8 == 0, 128-lane channel tiles")
    tables = _hyena_tables(seq)
    spectra = _hyena_filters(seq, w1, b1, w2, b2, w3, b3, freq, decay, tables, n_order, n_dirs, d_hy)
    cb = conv_b.reshape(1, -1)
    bias_row = bias.reshape(1, -1)
    common = dict(bsz=bsz, seq=seq, d_hy=d_hy)
    z1 = _hy_conv(u_hy, 0, d_hy, u_hy, conv_w, cb, spectra, bias_row, tables,
                  order=0, conv_on_z=True, **common)
    return _hy_conv(z1, 0, 2 * d_hy, u_hy, conv_w, cb, spectra, bias_row, tables,
                    order=1, conv_on_z=False, **common)


def _mixer_kernel(x_ref, mod_ref, g1_ref, g2_ref, fg_ref, permt_ref, za_ref, zb_ref,
                  wgate_ref, wa_ref, wb_ref, wout_ref, wg_ref, wu_ref, wd_ref, o_ref, *, n_chunks):
    nb, tt, d = x_ref.shape
    rows = nb * tt
    mod = lambda k: mod_ref[:, k:k + 1, :]
    x = x_ref[...]
    h = _norm_modulate(x, g1_ref[...], mod(0), mod(1)).reshape(rows, d).astype(BF16)
    gate = jax.nn.sigmoid(jnp.dot(h, wgate_ref[...], preferred_element_type=F32))
    za = jnp.dot(permt_ref[...], za_ref[...], preferred_element_type=F32).astype(BF16)
    ya = jnp.dot(za, wa_ref[...], preferred_element_type=F32)
    yb = jnp.dot(zb_ref[...].reshape(rows, zb_ref.shape[-1]), wb_ref[...], preferred_element_type=F32)
    merged = gate[:, :d] * ya + gate[:, d:] * yb
    o = jnp.dot(merged.astype(BF16), wout_ref[...], preferred_element_type=F32)
    x1 = x + mod(2) * o.reshape(nb, tt, d)

    h2 = _norm_modulate(x1, g2_ref[...], mod(3), mod(4)).reshape(rows, d).astype(BF16)
    d_ff = wg_ref.shape[1]
    unit = MXU_TILE if d_ff % MXU_TILE == 0 else d_ff // n_chunks
    edges = [len(g) for g in _spread(list(range(d_ff // unit)), n_chunks)]
    acc = jnp.zeros((rows, d), F32)
    start = 0
    for width in edges:
        sl = slice(start * unit, (start + width) * unit)
        start += width
        gl = jnp.dot(h2, wg_ref[:, sl], preferred_element_type=F32)
        up = jnp.dot(h2, wu_ref[:, sl], preferred_element_type=F32)
        act = (gl * jax.nn.sigmoid(gl) * up).astype(BF16)
        acc = acc + jnp.dot(act, wd_ref[sl, :], preferred_element_type=F32)
    x2 = x1 + mod(5) * acc.reshape(nb, tt, d)
    ms = jnp.mean(x2 * x2, axis=-1, keepdims=True)
    o_ref[...] = x2 * lax.rsqrt(ms + EPS) * fg_ref[...]


def _mixer(x, mod3, norm1_g, norm2_g, final_g, perm_t, za_tm, zb, w_gate, w_a, w_b, w_out, w_g, w_u, w_d):
    bsz, seq, d = x.shape
    d_s5 = w_a.shape[0]
    d_hy = w_b.shape[0]
    d_ff = w_g.shape[1]
    tt = _time_tile(bsz, seq)
    n_chunks = 2 if d_ff % (2 * LANE) == 0 else 1
    consts = (mod3, norm1_g, norm2_g, final_g, perm_t)
    weights = (w_gate, w_a, w_b, w_out, w_g, w_u, w_d)
    return pl.pallas_call(
        functools.partial(_mixer_kernel, n_chunks=n_chunks),
        out_shape=jax.ShapeDtypeStruct(x.shape, F32),
        grid=(seq // tt,),
        in_specs=[pl.BlockSpec((bsz, tt, d), lambda j: (0, j, 0))]
        + [_const_spec(a.shape) for a in consts]
        + [pl.BlockSpec((tt * bsz, d_s5), lambda j: (j, 0)),
           pl.BlockSpec((bsz, tt, d_hy), lambda j: (0, j, 0))]
        + [_const_spec(a.shape) for a in weights],
        out_specs=pl.BlockSpec((bsz, tt, d), lambda j: (0, j, 0)),
        compiler_params=_cparams(1),
        name="merge_swiglu_final_norm",
    )(x, *consts, za_tm, zb, *weights)


def kernel(x, c, ada_w, ada_b, norm1_g, norm2_g, w_in, s5_lam_re, s5_lam_im, s5_log_step, s5_b_re, s5_b_im, s5_c_re, s5_c_im, s5_d, s5_glu_w, s5_glu_b, hy_conv_w, hy_conv_b, hy_ffn_w1, hy_ffn_b1, hy_ffn_w2, hy_ffn_b2, hy_ffn_w3, hy_ffn_b3, hy_freq, hy_decay, hy_bias, w_branch_a, w_branch_b, w_out, ffn_w_gu, ffn_w_down, final_g):
    bsz, seq, d = x.shape
    depth = ada_w.shape[0]
    if depth != 1:
        raise NotImplementedError("the final RMSNorm is fused into the (single) layer's channel mixer")
    d_s5 = s5_d.shape[-1]
    n_order, d_hy = hy_bias.shape[1:]
    d_uh = d_s5 + (n_order + 1) * d_hy
    d_ff = ffn_w_down.shape[1]
    i = 0
    perm = _row_permutation(bsz, _time_tile(bsz, seq))
    mod = _modulation(c, ada_w[i], ada_b[i]).reshape(bsz, 6, d)
    w_in_b = w_in[i].astype(BF16)
    u_s5, u_hy = _in_projection(x, mod, norm1_g[i].reshape(1, d), w_in_b[:, :d_uh], perm, d_s5)
    z_a = _s5_branch(u_s5, bsz, s5_lam_re[i], s5_lam_im[i], s5_log_step[i], s5_b_re[i], s5_b_im[i],
                     s5_c_re[i], s5_c_im[i], s5_d[i], s5_glu_w[i], s5_glu_b[i])
    z_b = _hyena_branch(u_hy.reshape(bsz * seq, -1), bsz, seq, hy_conv_w[i], hy_conv_b[i], hy_ffn_w1[i],
                        hy_ffn_b1[i], hy_ffn_w2[i], hy_ffn_b2[i], hy_ffn_w3[i], hy_ffn_b3[i], hy_freq[i],
                        hy_decay[i], hy_bias[i])
    w_gu = ffn_w_gu[i].astype(BF16)
    return _mixer(x, mod, norm1_g[i].reshape(1, d), norm2_g[i].reshape(1, d), final_g.reshape(1, d),
                  perm.T, z_a, z_b.reshape(bsz, seq, d_hy), w_in_b[:, d_uh:],
                  w_branch_a[i].astype(BF16), w_branch_b[i].astype(BF16), w_out[i].astype(BF16),
                  w_gu[:, :d_ff], w_gu[:, d_ff:], ffn_w_down[i].astype(BF16))
```

```python
import functools
import math

import jax
import jax.numpy as jnp
from jax import lax
from jax.experimental import pallas as pl
from jax.experimental.pallas import tpu as pltpu

F32 = jnp.float32
BF16 = jnp.bfloat16
EPS = 1e-6
HIGHEST = lax.Precision.HIGHEST

V7X_VMEM_BYTES = 64 * 1024 * 1024
VMEM_LIMIT_BYTES = 56 * 1024 * 1024
LANE = 128
SUBLANE = 8
ROW_TILE = 512
S5_TIME_CHUNK = 64
S5_LANE_CHUNK = 512
S5_CH_SPLIT = 256
S5_STEPS_PER_GROUP = 8
MXU_TILE = 256
HY_CH_TILE = 256
HY_BATCH_GROUP = 2
DFT_ROW_CHUNK = 512


def _cparams(n_axes):
    return pltpu.CompilerParams(
        dimension_semantics=("arbitrary",) * n_axes,
        vmem_limit_bytes=VMEM_LIMIT_BYTES,
    )


def _const_spec(shape):
    nd = len(shape)
    return pl.BlockSpec(shape, lambda *_: (0,) * nd, pipeline_mode=pl.Buffered(1))


def _gelu_tanh(x):
    return 0.5 * x * (1.0 + jnp.tanh(math.sqrt(2.0 / math.pi) * (x + 0.044715 * (x * x * x))))


def _norm_modulate(x, g, shift, scale):
    ms = jnp.mean(x * x, axis=-1, keepdims=True)
    r = x * lax.rsqrt(ms + EPS) * g
    return r * (1.0 + scale) + shift


def _mod_kernel(c_ref, w_ref, b_ref, o_ref):
    c = c_ref[...]
    ca = c * jax.nn.sigmoid(c)
    o_ref[...] = jnp.dot(ca, w_ref[...], precision=HIGHEST, preferred_element_type=F32) + b_ref[...]


def _modulation(c, ada_w, ada_b):
    bsz, d = c.shape
    n = ada_w.shape[1]
    tn = 512
    return pl.pallas_call(
        _mod_kernel,
        out_shape=jax.ShapeDtypeStruct((bsz, n), F32),
        grid=(n // tn,),
        in_specs=[
            pl.BlockSpec((bsz, d), lambda j: (0, 0)),
            pl.BlockSpec((d, tn), lambda j: (0, j)),
            pl.BlockSpec((1, tn), lambda j: (0, j)),
        ],
        out_specs=pl.BlockSpec((bsz, tn), lambda j: (0, j)),
        compiler_params=_cparams(1),
        name="adaln_mod",
    )(c, ada_w, ada_b.reshape(1, n))


def _time_tile(bsz, seq):
    return max(min(ROW_TILE // bsz, seq), 1)


def _row_permutation(bsz, tt):
    r = jnp.arange(bsz * tt, dtype=jnp.int32)
    src = (r % bsz) * tt + r // bsz
    return (src[:, None] == r[None, :]).astype(BF16)


def _inproj_kernel(x_ref, mod_ref, g_ref, w_ref, perm_ref, us5_ref, uhy_ref, *, d_s5):
    nb, tt, d = x_ref.shape
    h = _norm_modulate(x_ref[...], g_ref[...], mod_ref[:, 0:1, :], mod_ref[:, 1:2, :])
    hb = h.reshape(nb * tt, d).astype(BF16)
    p = jnp.dot(hb, w_ref[...], preferred_element_type=F32)
    us5_ref[...] = jnp.dot(perm_ref[...], p[:, :d_s5].astype(BF16),
                           preferred_element_type=F32).astype(us5_ref.dtype)
    uhy_ref[...] = p[:, d_s5:].astype(BF16).reshape(uhy_ref.shape)


def _in_projection(x, mod3, norm_g, w_uh, perm, d_s5):
    bsz, seq, d = x.shape
    n = w_uh.shape[1]
    tt = _time_tile(bsz, seq)
    return pl.pallas_call(
        functools.partial(_inproj_kernel, d_s5=d_s5),
        out_shape=(
            jax.ShapeDtypeStruct((seq * bsz, d_s5), BF16),
            jax.ShapeDtypeStruct((bsz, seq, n - d_s5), BF16),
        ),
        grid=(seq // tt,),
        in_specs=[
            pl.BlockSpec((bsz, tt, d), lambda j: (0, j, 0)),
            _const_spec(mod3.shape),
            _const_spec((1, d)),
            _const_spec((d, n)),
            _const_spec(perm.shape),
        ],
        out_specs=(
            pl.BlockSpec((tt * bsz, d_s5), lambda j: (j, 0)),
            pl.BlockSpec((bsz, tt, n - d_s5), lambda j: (0, j, 0)),
        ),
        compiler_params=_cparams(1),
        name="in_proj",
    )(x, mod3, norm_g, w_uh, perm)


def _s5_prep_kernel(lre_ref, lim_ref, lstep_ref, bre_ref, bim_ref, cim_ref,
                    are_ref, aim_ref, bbre_ref, bbim_ref, ncim_ref):
    step = jnp.exp(lstep_ref[...])
    lr = lre_ref[...]
    li = lim_ref[...]
    mag = jnp.exp(lr * step)
    ar = mag * jnp.cos(li * step)
    ai = mag * jnp.sin(li * step)
    num = ar - 1.0
    den = lr * lr + li * li
    cr = (num * lr + ai * li) / den
    ci = (ai * lr - num * li) / den
    are_ref[...] = ar
    aim_ref[...] = ai
    for d in range(lre_ref.shape[0]):
        br = bre_ref[d]
        bi = bim_ref[d]
        bbre_ref[d] = cr[d:d + 1, :] * br - ci[d:d + 1, :] * bi
        bbim_ref[d] = cr[d:d + 1, :] * bi + ci[d:d + 1, :] * br
    ncim_ref[...] = -cim_ref[...]


def _s5_prepare(lam_re, lam_im, log_step, b_re, b_im, c_im):
    outs = (
        jax.ShapeDtypeStruct(lam_re.shape, F32),
        jax.ShapeDtypeStruct(lam_re.shape, F32),
        jax.ShapeDtypeStruct(b_re.shape, F32),
        jax.ShapeDtypeStruct(b_re.shape, F32),
        jax.ShapeDtypeStruct(c_im.shape, F32),
    )
    return pl.pallas_call(
        _s5_prep_kernel,
        out_shape=outs,
        compiler_params=pltpu.CompilerParams(vmem_limit_bytes=VMEM_LIMIT_BYTES),
        name="s5_discretize",
    )(lam_re, lam_im, log_step, b_re, b_im, c_im)


def _block_diag(m):
    g, r, c = m.shape
    eye = jnp.eye(g, dtype=m.dtype)
    return (m[:, :, None, :] * eye[:, None, :, None]).reshape(g * r, g * c)


def _spread(items, n):
    return [items[(len(items) * i) // n:(len(items) * (i + 1)) // n] for i in range(n)]


def _s5_scan_kernel(*refs, reverse, final, tc, nb):
    if final:
        (u_ref, bd_ref, cd_ref, are_ref, aim_ref, yprev_ref, gluw_ref, glub_ref,
         out_ref, bu_ref, sb_ref, y_ref, s_ref) = refs
    else:
        (u_ref, bd_ref, cd_ref, are_ref, aim_ref, d_ref, out_ref, bu_ref, sb_ref, y_ref, s_ref) = refs
    rows = tc * nb
    nsplit, ks, two_nss = bd_ref.shape
    nss = two_nss // 2
    lc = min(S5_LANE_CHUNK, nss)
    tw = min(MXU_TILE, lc)
    rh = rows // 2 if rows % (2 * SUBLANE * 2) == 0 else rows
    row_halves = [slice(r, r + rh) for r in range(0, rows, rh)]

    @pl.when(pl.program_id(0) == 0)
    def _():
        s_ref[...] = jnp.zeros_like(s_ref)

    units = [(h, q) for h in range(nsplit) for q in range(nss // lc)]

    def lanes(unit, part, off=0, width=None):
        h, q = unit
        start = h * two_nss + part * nss + q * lc + off
        return slice(start, start + (lc if width is None else width))

    def expand_pieces(unit):
        h, q = unit
        out = []
        for part in range(2):
            for off in range(0, lc, tw):
                for rs in row_halves:
                    def piece(part=part, off=off, rs=rs):
                        col = part * nss + q * lc + off
                        bu_ref[rs, lanes(unit, part, off, tw)] = jnp.dot(
                            u_ref[rs, h * ks:(h + 1) * ks], bd_ref[h, :, col:col + tw],
                            preferred_element_type=F32)
                    out.append(piece)
        return out

    def readout_pieces(unit):
        h, q = unit
        out = []
        for part in range(2):
            for off in range(0, lc, tw):
                for rs in row_halves:
                    first = q == 0 and part == 0 and off == 0
                    def piece(part=part, off=off, rs=rs, first=first):
                        k0 = part * nss + q * lc + off
                        v = jnp.dot(sb_ref[rs, lanes(unit, part, off, tw)], cd_ref[h, k0:k0 + tw, :],
                                    preferred_element_type=F32)
                        if first:
                            y_ref[rs, h * ks:(h + 1) * ks] = v
                        else:
                            y_ref[rs, h * ks:(h + 1) * ks] += v
                    out.append(piece)
        return out

    step_groups = _spread(list(range(tc)), max(tc // S5_STEPS_PER_GROUP, 1))
    for p in expand_pieces(units[0]):
        p()
    for c, unit in enumerate(units):
        mxu_work = (expand_pieces(units[c + 1]) if c + 1 < len(units) else []) + \
                   (readout_pieces(units[c - 1]) if c > 0 else [])
        mxu_groups = _spread(mxu_work, len(step_groups))
        a_sl = slice(unit[0] * nss + unit[1] * lc, unit[0] * nss + (unit[1] + 1) * lc)
        a_re = are_ref[:, a_sl]
        a_im = aim_ref[:, a_sl]
        re_sl, im_sl = lanes(unit, 0), lanes(unit, 1)
        sr = s_ref[:, re_sl]
        si = s_ref[:, im_sl]
        for steps, pieces in zip(step_groups, mxu_groups):
            for p in pieces:
                p()
            for k in steps:
                t = (tc - 1 - k) if reverse else k
                row = slice(t * nb, (t + 1) * nb)
                nr = a_re * sr - a_im * si + bu_ref[row, re_sl]
                ni = a_re * si + a_im * sr + bu_ref[row, im_sl]
                sb_ref[row, re_sl] = nr.astype(BF16)
                sb_ref[row, im_sl] = ni.astype(BF16)
                sr, si = nr, ni
        s_ref[:, re_sl] = sr
        s_ref[:, im_sl] = si
    for p in readout_pieces(units[-1]):
        p()

    rb = min(256, rows)
    for r in range(rows // rb):
        rs = slice(r * rb, (r + 1) * rb)
        if final:
            z = _gelu_tanh(yprev_ref[rs, :] + y_ref[rs, :])
            gate = jnp.dot(z.astype(BF16), gluw_ref[...], preferred_element_type=F32) + glub_ref[...]
            out_ref[rs, :] = (z * jax.nn.sigmoid(gate)).astype(out_ref.dtype)
        else:
            out_ref[rs, :] = y_ref[rs, :] + u_ref[rs, :] * d_ref[...]


def _s5_scan(u_tm, bd, cd, a_re, a_im, extra, *, reverse, final, nb, out_dtype):
    rows_total, d_s5 = u_tm.shape
    seq = rows_total // nb
    two_ns = bd.shape[0] * bd.shape[2]
    tc = min(S5_TIME_CHUNK, seq)
    nchunk = seq // tc
    rows = tc * nb
    if reverse:
        cidx = lambda i: (nchunk - 1 - i, 0)
    else:
        cidx = lambda i: (i, 0)
    in_specs = [
        pl.BlockSpec((rows, d_s5), cidx),
        _const_spec(bd.shape),
        _const_spec(cd.shape),
        _const_spec(a_re.shape),
        _const_spec(a_im.shape),
    ]
    if final:
        yprev, glu_w, glu_b = extra
        in_specs += [pl.BlockSpec((rows, d_s5), cidx), _const_spec(glu_w.shape), _const_spec(glu_b.shape)]
    else:
        in_specs += [_const_spec(extra[0].shape)]
    return pl.pallas_call(
        functools.partial(_s5_scan_kernel, reverse=reverse, final=final, tc=tc, nb=nb),
        out_shape=jax.ShapeDtypeStruct((rows_total, d_s5), out_dtype),
        grid=(nchunk,),
        in_specs=in_specs,
        out_specs=pl.BlockSpec((rows, d_s5), cidx),
        scratch_shapes=[pltpu.VMEM((rows, two_ns), F32), pltpu.VMEM((rows, two_ns), BF16),
                        pltpu.VMEM((rows, d_s5), F32), pltpu.VMEM((nb, two_ns), F32)],
        compiler_params=_cparams(1),
        name="s5_scan_bwd_glu" if final else "s5_scan_fwd",
    )(u_tm, bd, cd, a_re, a_im, *extra)


def _s5_branch(u_tm, nb, lam_re, lam_im, log_step, b_re, b_im, c_re, c_im, d, glu_w, glu_b):
    ndir, g, p = lam_re.shape
    grp = b_re.shape[-1]
    ns = g * p
    d_s5 = g * grp
    ks = S5_CH_SPLIT if d_s5 % S5_CH_SPLIT == 0 else d_s5
    nsplit = d_s5 // ks
    gs = g // nsplit
    flat = lambda a: a.reshape(ndir, ns)
    to_lanes = lambda a: jnp.transpose(a, (0, 3, 1, 2)).reshape(ndir, grp, ns)
    a_re, a_im, bb_re, bb_im, ncim = _s5_prepare(
        flat(lam_re), flat(lam_im), jnp.repeat(log_step, p, axis=-1),
        to_lanes(b_re), to_lanes(b_im),
        jnp.transpose(c_im, (0, 2, 1, 3)).reshape(ndir, grp, ns))

    def in_blocks(a):
        a = jnp.transpose(a.reshape(grp, nsplit, gs, p), (1, 2, 0, 3))
        return jnp.stack([_block_diag(a[h]) for h in range(nsplit)])

    def out_blocks(a):
        a = jnp.transpose(a.reshape(nsplit, gs, grp, p), (0, 1, 3, 2))
        return jnp.stack([_block_diag(a[h]) for h in range(nsplit)])

    y = None
    for direction in range(ndir):
        bd = jnp.concatenate([in_blocks(bb_re[direction]), in_blocks(bb_im[direction])], axis=2).astype(BF16)
        ncim_g = jnp.transpose(ncim[direction].reshape(grp, g, p), (1, 0, 2))
        cd = jnp.concatenate([out_blocks(c_re[direction]), out_blocks(ncim_g)], axis=1).astype(BF16)
        are = jnp.broadcast_to(a_re[direction][None, :], (nb, ns))
        aim = jnp.broadcast_to(a_im[direction][None, :], (nb, ns))
        if direction == 0:
            if ndir != 2:
                raise NotImplementedError("S5 branch expects forward and backward directions")
            y = _s5_scan(u_tm, bd, cd, are, aim, (d.reshape(1, -1),), reverse=False, final=False,
                         nb=nb, out_dtype=F32)
        else:
            extra = (y, glu_w.astype(BF16), glu_b.reshape(1, -1))
            y = _s5_scan(u_tm, bd, cd, are, aim, extra, reverse=True, final=True, nb=nb, out_dtype=BF16)
    return y


RSQRT2 = math.sqrt(0.5)


def _dft_table(nf, t, n):
    f_lo = 64 if nf % 64 == 0 else 1
    f_hi = nf // f_lo
    t = t[:, None, :]
    ka = (jnp.arange(f_hi, dtype=jnp.int32)[None, :, None] * f_lo * t) % n
    kb = (jnp.arange(f_lo, dtype=jnp.int32)[None, :, None] * t) % n
    w = 2.0 * math.pi / n
    aa = ka.astype(F32) * w
    ab = kb.astype(F32) * w
    ca, sa, cb, sb = jnp.cos(aa), jnp.sin(aa), jnp.cos(ab), jnp.sin(ab)
    cm = ca[:, :, None, :] * cb[:, None, :, :] - sa[:, :, None, :] * sb[:, None, :, :]
    sm = sa[:, :, None, :] * cb[:, None, :, :] + ca[:, :, None, :] * sb[:, None, :, :]
    shape = (t.shape[0], nf, t.shape[-1])
    return cm.reshape(shape), sm.reshape(shape)


def _hyena_tables(seq):
    q = seq // 4
    times = 4 * jnp.arange(q, dtype=jnp.int32)[None, :] + jnp.arange(4, dtype=jnp.int32)[:, None]
    cf, sf = _dft_table(q, times, 2 * seq)
    alt = (1 - 2 * (jnp.arange(q, dtype=jnp.int32) & 1)).astype(F32)
    f_is_0 = jnp.arange(q, dtype=jnp.int32)[None, :, None] == 0
    sf = jnp.where(f_is_0, alt[None, None, :], sf)
    return dict(c_fwd=cf.astype(BF16), s_fwd=sf.astype(BF16),
                c_inv=jnp.transpose(cf, (0, 2, 1)).astype(BF16), s_inv=jnp.transpose(sf, (0, 2, 1)).astype(BF16))


def _cmul(z, k):
    (zc, zs), (kc, ks) = z, k
    return zc * kc - zs * ks, zc * ks + zs * kc


def _forward_spectrum(x, a):
    (c0, s0), (c1, s1), (c2, s2), (c3, s3) = x
    pc, ps, mc, ms = c0 + c2, s0 + s2, c0 - c2, s0 - s2
    qc, qs, nc, ns = c1 + c3, s1 + s3, c1 - c3, s1 - s3
    za = (pc + qc, ps + qs)
    zb = (pc - qc, qs - ps)
    zc = (mc + ns, nc - ms)
    zd = (mc - ns, ms + nc)
    e = (a[1] - a[3]) * RSQRT2
    o = (a[1] + a[3]) * RSQRT2
    return [za, zb, zc, zd], [(a[0] + e, a[2] + o), (a[0] - e, o - a[2])]


def _inverse_spectrum(y, ymid):
    (ac, as_), (bc, bs), (cc, cs), (dc, ds) = y
    upc, ups, umc, ums = ac + bc, as_ - bs, ac - bc, as_ + bs
    vpc, vps, vmc, vms = dc + cc, ds - cs, dc - cc, ds + cs
    t = [(upc + vpc, ups + vps), (umc + vms, ums - vmc), (upc - vpc, ups - vps), (umc - vms, ums + vmc)]
    (y1c, y1s), (y3c, y3s) = ymid
    sp = [y1c + y3c, ((y1c + y1s) + (y3s - y3c)) * RSQRT2, y1s - y3s, ((y1s - y1c) + (y3c + y3s)) * RSQRT2]
    return t, sp


def _hy_filter_kernel(feat_ref, w1_ref, b1_ref, w2_ref, b2_ref, freq_ref,
                      w3f_ref, b3f_ref, decf_ref, w3b_ref, b3b_ref, decb_ref,
                      cf_ref, sf_ref,
                      kac_ref, kas_ref, kbc_ref, kbs_ref, kcc_ref, kcs_ref, kdc_ref, kds_ref, km_ref,
                      taps_s, rhs_s, h_s):
    seq = feat_ref.shape[0]
    q = seq // 4
    n = 2 * seq
    ct = kac_ref.shape[1]
    nsl = ct // LANE

    @pl.when((pl.program_id(0) == 0) & (pl.program_id(1) == 0))
    def _():
        f = freq_ref[...]
        h1 = jnp.sin(f * (jnp.dot(feat_ref[...], w1_ref[...], precision=HIGHEST,
                                  preferred_element_type=F32) + b1_ref[...]))
        h_s[...] = jnp.sin(f * (jnp.dot(h1, w2_ref[...], precision=HIGHEST,
                                        preferred_element_type=F32) + b2_ref[...]))

    h = h_s[...]
    t01 = feat_ref[:, 0:1]
    row = lax.broadcasted_iota(jnp.int32, (seq, 1), 0)

    def taps(w3_ref, b3_ref, dec_ref):
        v = jnp.dot(h, w3_ref[...], precision=HIGHEST, preferred_element_type=F32) + b3_ref[...]
        return v * jnp.exp(-t01 * jnp.abs(dec_ref[...]))

    fwd = taps(w3f_ref, b3f_ref, decf_ref)
    bwd = jnp.where(row == 0, 0.0, taps(w3b_ref, b3b_ref, decb_ref))
    for d, x in enumerate((fwd, bwd)):
        for j in range(nsl):
            taps_s[d * nsl + j] = x[:, j * LANE:(j + 1) * LANE]
        for r in range(4):
            for j in range(nsl):
                c0 = (4 * d + r) * ct + j * LANE
                rhs_s[:, c0:c0 + LANE] = taps_s[d * nsl + j, pl.ds(r, q, stride=4), :].astype(BF16)
    row0 = lax.broadcasted_iota(jnp.int32, (q, 1), 0) == 0

    def spectrum(d):
        x, a = [], []
        for r in range(4):
            rhs = rhs_s[:, (4 * d + r) * ct:(4 * d + r + 1) * ct]
            c = jnp.dot(cf_ref[r], rhs, preferred_element_type=F32)
            sn = jnp.dot(sf_ref[r], rhs, preferred_element_type=F32)
            a.append(sn[0:1, :])
            x.append((c, jnp.where(row0, 0.0, sn)))
        return _forward_spectrum(x, a)

    fz, fm = spectrum(0)
    bz, bm = spectrum(1)
    scale = jnp.where(row0, 1.0 / n, 2.0 / n)
    outs = ((kac_ref, kas_ref), (kbc_ref, kbs_ref), (kcc_ref, kcs_ref), (kdc_ref, kds_ref))
    for (oc, os_), (fc, fs), (bc, bs) in zip(outs, fz, bz):
        oc[...] = (fc + bc) * scale
        os_[...] = (fs - bs) * scale
    km_ref[...] = jnp.zeros_like(km_ref)
    for i, ((fc, fs), (bc, bs)) in enumerate(zip(fm, bm)):
        km_ref[2 * i:2 * i + 1, :] = (fc + bc) * (2.0 / n)
        km_ref[2 * i + 1:2 * i + 2, :] = (fs - bs) * (2.0 / n)


def _hyena_filters(seq, w1, b1, w2, b2, w3, b3, freq, decay, tables, n_order, n_dirs, d_hy):
    q = seq // 4
    emb, hid = w1.shape
    bands = (emb - 1) // 2
    t = jnp.arange(seq, dtype=F32)
    t01 = t / max(seq - 1, 1)
    band = jnp.linspace(1e-4, bands - 1, bands, dtype=F32)
    ang = (2.0 * math.pi) * t[:, None] * band[None, :] / seq
    feats = jnp.concatenate([t01[:, None], jnp.cos(ang), jnp.sin(ang)], axis=-1)
    kpad = LANE
    feats = jnp.pad(feats, ((0, 0), (0, kpad - emb)))
    w1p = jnp.pad(w1, ((0, kpad - emb), (0, 0)))
    tabs = [tables["c_fwd"], tables["s_fwd"]]
    ct = min(HY_CH_TILE, d_hy)
    nct = d_hy // ct
    ncol = n_order * d_hy
    b3r = b3.reshape(1, -1)
    decr = decay.reshape(1, -1)
    fcol = lambda o, c: (0, (o * n_dirs + 0) * nct + c)
    bcol = lambda o, c: (0, (o * n_dirs + 1) * nct + c)
    ocol = lambda o, c: (0, o * nct + c)
    full = lambda a: pl.BlockSpec(a.shape, lambda o, c: (0,) * a.ndim)
    spec = jax.ShapeDtypeStruct((q, ncol), F32)
    return pl.pallas_call(
        _hy_filter_kernel,
        out_shape=(spec,) * 8 + (jax.ShapeDtypeStruct((8, ncol), F32),),
        grid=(n_order, nct),
        in_specs=[
            full(feats), full(w1p), full(b1.reshape(1, -1)), full(w2), full(b2.reshape(1, -1)),
            full(freq.reshape(1, -1)),
            pl.BlockSpec((hid, ct), fcol), pl.BlockSpec((1, ct), fcol), pl.BlockSpec((1, ct), fcol),
            pl.BlockSpec((hid, ct), bcol), pl.BlockSpec((1, ct), bcol), pl.BlockSpec((1, ct), bcol),
        ] + [_const_spec(a.shape) for a in tabs],
        out_specs=tuple(pl.BlockSpec((q, ct), ocol) for _ in range(8)) + (pl.BlockSpec((8, ct), ocol),),
        scratch_shapes=[pltpu.VMEM((2 * ct // LANE, seq, LANE), F32), pltpu.VMEM((q, 8 * ct), BF16),
                        pltpu.VMEM((seq, hid), F32)],
        compiler_params=_cparams(2),
        name="hyena_filter_spectra",
    )(feats, w1p, b1.reshape(1, -1), w2, b2.reshape(1, -1), freq.reshape(1, -1),
      w3, b3r, decr, w3, b3r, decr, *tabs)


def _hy_conv_kernel(zin_ref, gin_ref, wz_ref, bz_ref, wg_ref, bg_ref,
                    kac_ref, kas_ref, kbc_ref, kbs_ref, kcc_ref, kcs_ref, kdc_ref, kds_ref, km_ref, bias_ref,
                    cf_ref, sf_ref, ci_ref, si_ref, out_ref,
                    zraw_s, graw_s, o_s, z_s, g_s, rhs_s, ac_s, as_s, *, conv_on_z, seq):
    ct = zin_ref.shape[1]
    nbat = zin_ref.shape[0] // seq
    q = seq // 4
    nsl = ct // LANE
    pad = zraw_s.shape[2] - seq
    top = pad // 2
    rc = min(DFT_ROW_CHUNK, q)
    k_refs = ((kac_ref, kas_ref), (kbc_ref, kbs_ref), (kcc_ref, kcs_ref), (kdc_ref, kds_ref))

    def stage(raw_s, src_ref, k):
        for j in range(nsl):
            raw_s[k, j, 0:top, :] = jnp.zeros((top, LANE), F32)
            raw_s[k, j, top + seq:, :] = jnp.zeros((pad - top, LANE), F32)
            raw_s[k, j, top:top + seq, :] = src_ref[k * seq:(k + 1) * seq, j * LANE:(j + 1) * LANE].astype(F32)

    def split_rows(raw_s, dst_ref, k, w_ref, b_ref):
        for r in range(4):
            for j in range(nsl):
                ls = slice(j * LANE, (j + 1) * LANE)
                tap = lambda d: raw_s[k, j, pl.ds(top + r + d, q, stride=4), :]
                if w_ref is None:
                    v = tap(0)
                else:
                    v = (b_ref[:, ls] + tap(-1) * w_ref[0:1, ls] + tap(0) * w_ref[1:2, ls]
                         + tap(1) * w_ref[2:3, ls])
                dst_ref[k, :, r * ct + j * LANE:r * ct + (j + 1) * LANE] = v

    def spectrum_chunk(k, i):
        r = slice(i * rc, (i + 1) * rc)
        row0 = lax.broadcasted_iota(jnp.int32, (rc, 1), 0) == 0
        x, a = [], []
        for p in range(4):
            rhs = rhs_s[k, :, p * ct:(p + 1) * ct]
            c = jnp.dot(cf_ref[p, r, :], rhs, preferred_element_type=F32)
            sn = jnp.dot(sf_ref[p, r, :], rhs, preferred_element_type=F32)
            if i == 0:
                a.append(sn[0:1, :])
                sn = jnp.where(row0, 0.0, sn)
            else:
                a.append(jnp.zeros((1, ct), F32))
            x.append((c, sn))
        z, zmid = _forward_spectrum(x, a)
        y = [_cmul(zx, (kc[r, :], ks[r, :])) for zx, (kc, ks) in zip(z, k_refs)]
        ymid = [_cmul(zmid[m], (km_ref[2 * m:2 * m + 1, :], km_ref[2 * m + 1:2 * m + 2, :])) for m in range(2)]
        t, sp = _inverse_spectrum(y, ymid)
        for p, ((tc, ts), spp) in enumerate(zip(t, sp)):
            if i == 0:
                ts = jnp.where(row0, spp, ts)
            ac_s[k, r, p * ct:(p + 1) * ct] = tc.astype(BF16)
            as_s[k, r, p * ct:(p + 1) * ct] = ts.astype(BF16)

    def output_chunk(k, i):
        r = slice(i * rc, (i + 1) * rc)
        for p in range(4):
            ps = slice(p * ct, (p + 1) * ct)
            y = jnp.dot(ci_ref[p, r, :], ac_s[k, :, ps], preferred_element_type=F32)
            y = y + jnp.dot(si_ref[p, r, :], as_s[k, :, ps], preferred_element_type=F32)
            rows = pl.ds(4 * i * rc + p, rc, stride=4)
            for j in range(nsl):
                ls = slice(p * ct + j * LANE, p * ct + (j + 1) * LANE)
                jl = slice(j * LANE, (j + 1) * LANE)
                o_s[k, j, rows, :] = g_s[k, r, ls] * (y[:, jl] + z_s[k, r, ls] * bias_ref[:, jl])

    for k in range(nbat):
        stage(zraw_s, zin_ref, k)
        stage(graw_s, gin_ref, k)
        split_rows(zraw_s, z_s, k, wz_ref if conv_on_z else None, bz_ref)
        split_rows(graw_s, g_s, k, wg_ref, bg_ref)
        rhs_s[k] = z_s[k].astype(BF16)
    for i in range(q // rc):
        for k in range(nbat):
            spectrum_chunk(k, i)
    for i in range(q // rc):
        for k in range(nbat):
            output_chunk(k, i)
    for k in range(nbat):
        for j in range(nsl):
            out_ref[k * seq:(k + 1) * seq, j * LANE:(j + 1) * LANE] = o_s[k, j].astype(out_ref.dtype)


def _hy_conv(zin, zcol0, gcol0, u_hy, conv_w, conv_b, spectra, bias_row, tables,
             *, order, conv_on_z, bsz, seq, d_hy):
    tabs = [tables[k] for k in ("c_fwd", "s_fwd", "c_inv", "s_inv")]
    q = seq // 4
    ct = min(HY_CH_TILE, d_hy)
    nsl = ct // LANE
    nct = d_hy // ct
    nbat = HY_BATCH_GROUP if bsz % HY_BATCH_GROUP == 0 else 1
    zc0 = zcol0 // ct
    gc0 = gcol0 // ct
    zw0 = zc0 if conv_on_z else 0
    kspec = lambda rows: pl.BlockSpec((rows, ct), lambda c, b: (0, order * nct + c),
                                      pipeline_mode=pl.Buffered(1))
    scratch = lambda dt: pltpu.VMEM((nbat, q, 4 * ct), dt)
    slabs = lambda rows: pltpu.VMEM((nbat, nsl, rows, LANE), F32)
    return pl.pallas_call(
        functools.partial(_hy_conv_kernel, conv_on_z=conv_on_z, seq=seq),
        out_shape=jax.ShapeDtypeStruct((bsz * seq, d_hy), BF16),
        grid=(nct, bsz // nbat),
        in_specs=[
            pl.BlockSpec((nbat * seq, ct), lambda c, b: (b, zc0 + c)),
            pl.BlockSpec((nbat * seq, ct), lambda c, b: (b, gc0 + c)),
            pl.BlockSpec((conv_w.shape[0], ct), lambda c, b: (0, zw0 + c)),
            pl.BlockSpec((1, ct), lambda c, b: (0, zw0 + c)),
            pl.BlockSpec((conv_w.shape[0], ct), lambda c, b: (0, gc0 + c)),
            pl.BlockSpec((1, ct), lambda c, b: (0, gc0 + c)),
        ] + [kspec(q)] * 8 + [kspec(8), kspec(1)] + [_const_spec(a.shape) for a in tabs],
        out_specs=pl.BlockSpec((nbat * seq, ct), lambda c, b: (b, c)),
        scratch_shapes=[
            slabs(seq + 2 * SUBLANE), slabs(seq + 2 * SUBLANE), slabs(seq),
            scratch(F32), scratch(F32), scratch(BF16), scratch(BF16), scratch(BF16),
        ],
        compiler_params=_cparams(2),
        name=f"hyena_conv_order{order}",
    )(zin, u_hy, conv_w, conv_b, conv_w, conv_b, *spectra, bias_row, *tabs)


def _hyena_branch(u_hy, bsz, seq, conv_w, conv_b, w1, b1, w2, b2, w3, b3, freq, decay, bias):
    n_order, d_hy = bias.shape
    n_dirs = w3.shape[1] // (n_order * d_hy)
    if n_order != 2 or seq % 8 != 0 or d_hy % LANE != 0:
        raise NotImplementedError("Hyena branch: two long convolutions, L % 8 == 0, 128-lane channel tiles")
    tables = _hyena_tables(seq)
    spectra = _hyena_filters(seq, w1, b1, w2, b2, w3, b3, freq, decay, tables, n_order, n_dirs, d_hy)
    cb = conv_b.reshape(1, -1)
    bias_row = bias.reshape(1, -1)
    common = dict(bsz=bsz, seq=seq, d_hy=d_hy)
    z1 = _hy_conv(u_hy, 0, d_hy, u_hy, conv_w, cb, spectra, bias_row, tables,
                  order=0, conv_on_z=True, **common)
    return _hy_conv(z1, 0, 2 * d_hy, u_hy, conv_w, cb, spectra, bias_row, tables,
                    order=1, conv_on_z=False, **common)


def _mixer_kernel(x_ref, mod_ref, g1_ref, g2_ref, fg_ref, permt_ref, za_ref, zb_ref,
                  wgate_ref, wa_ref, wb_ref, wout_ref, wg_ref, wu_ref, wd_ref, o_ref, *, n_chunks):
    nb, tt, d = x_ref.shape
    rows = nb * tt
    mod = lambda k: mod_ref[:, k:k + 1, :]
    x = x_ref[...]
    h = _norm_modulate(x, g1_ref[...], mod(0), mod(1)).reshape(rows, d).astype(BF16)
    gate = jax.nn.sigmoid(jnp.dot(h, wgate_ref[...], preferred_element_type=F32))
    za = jnp.dot(permt_ref[...], za_ref[...], preferred_element_type=F32).astype(BF16)
    ya = jnp.dot(za, wa_ref[...], preferred_element_type=F32)
    yb = jnp.dot(zb_ref[...].reshape(rows, zb_ref.shape[-1]), wb_ref[...], preferred_element_type=F32)
    merged = gate[:, :d] * ya + gate[:, d:] * yb
    o = jnp.dot(merged.astype(BF16), wout_ref[...], preferred_element_type=F32)
    x1 = x + mod(2) * o.reshape(nb, tt, d)

    h2 = _norm_modulate(x1, g2_ref[...], mod(3), mod(4)).reshape(rows, d).astype(BF16)
    d_ff = wg_ref.shape[1]
    unit = MXU_TILE if d_ff % MXU_TILE == 0 else d_ff // n_chunks
    edges = [len(g) for g in _spread(list(range(d_ff // unit)), n_chunks)]
    acc = jnp.zeros((rows, d), F32)
    start = 0
    for width in edges:
        sl = slice(start * unit, (start + width) * unit)
        start += width
        gl = jnp.dot(h2, wg_ref[:, sl], preferred_element_type=F32)
        up = jnp.dot(h2, wu_ref[:, sl], preferred_element_type=F32)
        act = (gl * jax.nn.sigmoid(gl) * up).astype(BF16)
        acc = acc + jnp.dot(act, wd_ref[sl, :], preferred_element_type=F32)
    x2 = x1 + mod(5) * acc.reshape(nb, tt, d)
    ms = jnp.mean(x2 * x2, axis=-1, keepdims=True)
    o_ref[...] = x2 * lax.rsqrt(ms + EPS) * fg_ref[...]


def _mixer(x, mod3, norm1_g, norm2_g, final_g, perm_t, za_tm, zb, w_gate, w_a, w_b, w_out, w_g, w_u, w_d):
    bsz, seq, d = x.shape
    d_s5 = w_a.shape[0]
    d_hy = w_b.shape[0]
    d_ff = w_g.shape[1]
    tt = _time_tile(bsz, seq)
    n_chunks = 2 if d_ff % (2 * LANE) == 0 else 1
    consts = (mod3, norm1_g, norm2_g, final_g, perm_t)
    weights = (w_gate, w_a, w_b, w_out, w_g, w_u, w_d)
    return pl.pallas_call(
        functools.partial(_mixer_kernel, n_chunks=n_chunks),
        out_shape=jax.ShapeDtypeStruct(x.shape, F32),
        grid=(seq // tt,),
        in_specs=[pl.BlockSpec((bsz, tt, d), lambda j: (0, j, 0))]
        + [_const_spec(a.shape) for a in consts]
        + [pl.BlockSpec((tt * bsz, d_s5), lambda j: (j, 0)),
           pl.BlockSpec((bsz, tt, d_hy), lambda j: (0, j, 0))]
        + [_const_spec(a.shape) for a in weights],
        out_specs=pl.BlockSpec((bsz, tt, d), lambda j: (0, j, 0)),
        compiler_params=_cparams(1),
        name="merge_swiglu_final_norm",
    )(x, *consts, za_tm, zb, *weights)


def kernel(x, c, ada_w, ada_b, norm1_g, norm2_g, w_in, s5_lam_re, s5_lam_im, s5_log_step, s5_b_re, s5_b_im, s5_c_re, s5_c_im, s5_d, s5_glu_w, s5_glu_b, hy_conv_w, hy_conv_b, hy_ffn_w1, hy_ffn_b1, hy_ffn_w2, hy_ffn_b2, hy_ffn_w3, hy_ffn_b3, hy_freq, hy_decay, hy_bias, w_branch_a, w_branch_b, w_out, ffn_w_gu, ffn_w_down, final_g):
    bsz, seq, d = x.shape
    depth = ada_w.shape[0]
    if depth != 1:
        raise NotImplementedError("the final RMSNorm is fused into the (single) layer's channel mixer")
    d_s5 = s5_d.shape[-1]
    n_order, d_hy = hy_bias.shape[1:]
    d_uh = d_s5 + (n_order + 1) * d_hy
    d_ff = ffn_w_down.shape[1]
    i = 0
    perm = _row_permutation(bsz, _time_tile(bsz, seq))
    mod = _modulation(c, ada_w[i], ada_b[i]).reshape(bsz, 6, d)
    w_in_b = w_in[i].astype(BF16)
    u_s5, u_hy = _in_projection(x, mod, norm1_g[i].reshape(1, d), w_in_b[:, :d_uh], perm, d_s5)
    z_a = _s5_branch(u_s5, bsz, s5_lam_re[i], s5_lam_im[i], s5_log_step[i], s5_b_re[i], s5_b_im[i],
                     s5_c_re[i], s5_c_im[i], s5_d[i], s5_glu_w[i], s5_glu_b[i])
    z_b = _hyena_branch(u_hy.reshape(bsz * seq, -1), bsz, seq, hy_conv_w[i], hy_conv_b[i], hy_ffn_w1[i],
                        hy_ffn_b1[i], hy_ffn_w2[i], hy_ffn_b2[i], hy_ffn_w3[i], hy_ffn_b3[i], hy_freq[i],
                        hy_decay[i], hy_bias[i])
    w_gu = ffn_w_gu[i].astype(BF16)
    return _mixer(x, mod, norm1_g[i].reshape(1, d), norm2_g[i].reshape(1, d), final_g.reshape(1, d),
                  perm.T, z_a, z_b.reshape(bsz, seq, d_hy), w_in_b[:, d_uh:],
                  w_branch_a[i].astype(BF16), w_branch_b[i].astype(BF16), w_out[i].astype(BF16),
                  w_gu[:, :d_ff], w_gu[:, d_ff:], ffn_w_down[i].astype(BF16))
```

```python
import functools
import math

import jax
import jax.numpy as jnp
from jax import lax
from jax.experimental import pallas as pl
from jax.experimental.pallas import tpu as pltpu

F32 = jnp.float32
BF16 = jnp.bfloat16
EPS = 1e-6
HIGHEST = lax.Precision.HIGHEST

V7X_VMEM_BYTES = 64 * 1024 * 1024
VMEM_LIMIT_BYTES = 56 * 1024 * 1024
LANE = 128
SUBLANE = 8
ROW_TILE = 512
S5_TIME_CHUNK = 64
S5_LANE_CHUNK = 512
S5_CH_SPLIT = 256
S5_STEPS_PER_GROUP = 8
MXU_TILE = 256
HY_CH_TILE = 256
HY_BATCH_GROUP = 2
DFT_ROW_CHUNK = 512


def _cparams(n_axes):
    return pltpu.CompilerParams(
        dimension_semantics=("arbitrary",) * n_axes,
        vmem_limit_bytes=VMEM_LIMIT_BYTES,
    )


def _const_spec(shape):
    nd = len(shape)
    return pl.BlockSpec(shape, lambda *_: (0,) * nd, pipeline_mode=pl.Buffered(1))


def _gelu_tanh(x):
    return 0.5 * x * (1.0 + jnp.tanh(math.sqrt(2.0 / math.pi) * (x + 0.044715 * (x * x * x))))


def _norm_modulate(x, g, shift, scale):
    ms = jnp.mean(x * x, axis=-1, keepdims=True)
    r = x * lax.rsqrt(ms + EPS) * g
    return r * (1.0 + scale) + shift


def _split_bf16(x):
    hi = x.astype(BF16)
    return hi, (x - hi.astype(F32)).astype(BF16)


def _mod_kernel(c_ref, w_ref, b_ref, o_ref):
    c = c_ref[...]
    c_hi, c_lo = _split_bf16(c * jax.nn.sigmoid(c))
    w_hi, w_lo = _split_bf16(w_ref[...])
    dot = lambda a, b: jnp.dot(a, b, preferred_element_type=F32)
    o_ref[...] = dot(c_hi, w_hi) + (dot(c_hi, w_lo) + dot(c_lo, w_hi)) + b_ref[...]


def _modulation(c, ada_w, ada_b):
    bsz, d = c.shape
    n = ada_w.shape[1]
    tn = 512
    return pl.pallas_call(
        _mod_kernel,
        out_shape=jax.ShapeDtypeStruct((bsz, n), F32),
        grid=(n // tn,),
        in_specs=[
            pl.BlockSpec((bsz, d), lambda j: (0, 0)),
            pl.BlockSpec((d, tn), lambda j: (0, j)),
            pl.BlockSpec((1, tn), lambda j: (0, j)),
        ],
        out_specs=pl.BlockSpec((bsz, tn), lambda j: (0, j)),
        compiler_params=_cparams(1),
        name="adaln_mod",
    )(c, ada_w, ada_b.reshape(1, n))


def _time_tile(bsz, seq):
    return max(min(ROW_TILE // bsz, seq), 1)


def _row_permutation(bsz, tt):
    r = jnp.arange(bsz * tt, dtype=jnp.int32)
    src = (r % bsz) * tt + r // bsz
    return (src[:, None] == r[None, :]).astype(BF16)


def _inproj_kernel(x_ref, mod_ref, g_ref, w_ref, perm_ref, us5_ref, uhy_ref, *, d_s5):
    nb, tt, d = x_ref.shape
    h = _norm_modulate(x_ref[...], g_ref[...], mod_ref[:, 0:1, :], mod_ref[:, 1:2, :])
    hb = h.reshape(nb * tt, d).astype(BF16)
    p = jnp.dot(hb, w_ref[...], preferred_element_type=F32)
    us5_ref[...] = jnp.dot(perm_ref[...], p[:, :d_s5].astype(BF16),
                           preferred_element_type=F32).astype(us5_ref.dtype)
    uhy_ref[...] = p[:, d_s5:].astype(BF16).reshape(uhy_ref.shape)


def _in_projection(x, mod3, norm_g, w_uh, perm, d_s5):
    bsz, seq, d = x.shape
    n = w_uh.shape[1]
    tt = _time_tile(bsz, seq)
    return pl.pallas_call(
        functools.partial(_inproj_kernel, d_s5=d_s5),
        out_shape=(
            jax.ShapeDtypeStruct((seq * bsz, d_s5), BF16),
            jax.ShapeDtypeStruct((bsz, seq, n - d_s5), BF16),
        ),
        grid=(seq // tt,),
        in_specs=[
            pl.BlockSpec((bsz, tt, d), lambda j: (0, j, 0)),
            _const_spec(mod3.shape),
            _const_spec((1, d)),
            _const_spec((d, n)),
            _const_spec(perm.shape),
        ],
        out_specs=(
            pl.BlockSpec((tt * bsz, d_s5), lambda j: (j, 0)),
            pl.BlockSpec((bsz, tt, n - d_s5), lambda j: (0, j, 0)),
        ),
        compiler_params=_cparams(1),
        name="in_proj",
    )(x, mod3, norm_g, w_uh, perm)


def _s5_prep_kernel(lre_ref, lim_ref, lstep_ref, bre_ref, bim_ref, cim_ref,
                    are_ref, aim_ref, bbre_ref, bbim_ref, ncim_ref):
    step = jnp.exp(lstep_ref[...])
    lr = lre_ref[...]
    li = lim_ref[...]
    mag = jnp.exp(lr * step)
    ar = mag * jnp.cos(li * step)
    ai = mag * jnp.sin(li * step)
    num = ar - 1.0
    den = lr * lr + li * li
    cr = (num * lr + ai * li) / den
    ci = (ai * lr - num * li) / den
    are_ref[...] = ar
    aim_ref[...] = ai
    for d in range(lre_ref.shape[0]):
        br = bre_ref[d]
        bi = bim_ref[d]
        bbre_ref[d] = cr[d:d + 1, :] * br - ci[d:d + 1, :] * bi
        bbim_ref[d] = cr[d:d + 1, :] * bi + ci[d:d + 1, :] * br
    ncim_ref[...] = -cim_ref[...]


def _s5_prepare(lam_re, lam_im, log_step, b_re, b_im, c_im):
    outs = (
        jax.ShapeDtypeStruct(lam_re.shape, F32),
        jax.ShapeDtypeStruct(lam_re.shape, F32),
        jax.ShapeDtypeStruct(b_re.shape, F32),
        jax.ShapeDtypeStruct(b_re.shape, F32),
        jax.ShapeDtypeStruct(c_im.shape, F32),
    )
    return pl.pallas_call(
        _s5_prep_kernel,
        out_shape=outs,
        compiler_params=pltpu.CompilerParams(vmem_limit_bytes=VMEM_LIMIT_BYTES),
        name="s5_discretize",
    )(lam_re, lam_im, log_step, b_re, b_im, c_im)


def _block_diag(m):
    g, r, c = m.shape
    eye = jnp.eye(g, dtype=m.dtype)
    return (m[:, :, None, :] * eye[:, None, :, None]).reshape(g * r, g * c)


def _spread(items, n):
    return [items[(len(items) * i) // n:(len(items) * (i + 1)) // n] for i in range(n)]


def _s5_scan_kernel(*refs, reverse, final, tc, nb):
    if final:
        (u_ref, bd_ref, cd_ref, are_ref, aim_ref, yprev_ref, gluw_ref, glub_ref,
         out_ref, bu_ref, sb_ref, y_ref, s_ref) = refs
    else:
        (u_ref, bd_ref, cd_ref, are_ref, aim_ref, d_ref, out_ref, bu_ref, sb_ref, y_ref, s_ref) = refs
    rows = tc * nb
    nsplit, ks, two_nss = bd_ref.shape
    nss = two_nss // 2
    lc = min(S5_LANE_CHUNK, nss)
    tw = min(MXU_TILE, lc)
    rh = rows // 2 if rows % (2 * SUBLANE * 2) == 0 else rows
    row_halves = [slice(r, r + rh) for r in range(0, rows, rh)]

    @pl.when(pl.program_id(0) == 0)
    def _():
        s_ref[...] = jnp.zeros_like(s_ref)

    units = [(h, q) for h in range(nsplit) for q in range(nss // lc)]

    def lanes(unit, part, off=0, width=None):
        h, q = unit
        start = h * two_nss + part * nss + q * lc + off
        return slice(start, start + (lc if width is None else width))

    def expand_pieces(unit):
        h, q = unit
        out = []
        for part in range(2):
            for off in range(0, lc, tw):
                for rs in row_halves:
                    def piece(part=part, off=off, rs=rs):
                        col = part * nss + q * lc + off
                        bu_ref[rs, lanes(unit, part, off, tw)] = jnp.dot(
                            u_ref[rs, h * ks:(h + 1) * ks], bd_ref[h, :, col:col + tw],
                            preferred_element_type=F32)
                    out.append(piece)
        return out

    def readout_pieces(unit):
        h, q = unit
        out = []
        for rs in row_halves:
            def piece(rs=rs):
                v = None
                for part in range(2):
                    k0 = part * nss + q * lc
                    d = jnp.dot(sb_ref[rs, lanes(unit, part)], cd_ref[h, k0:k0 + lc, :],
                                preferred_element_type=F32)
                    v = d if v is None else v + d
                if q == 0:
                    y_ref[rs, h * ks:(h + 1) * ks] = v
                else:
                    y_ref[rs, h * ks:(h + 1) * ks] += v
            out.append(piece)
        return out

    step_groups = _spread(list(range(tc)), max(tc // S5_STEPS_PER_GROUP, 1))
    for p in expand_pieces(units[0]):
        p()
    for c, unit in enumerate(units):
        mxu_work = (expand_pieces(units[c + 1]) if c + 1 < len(units) else []) + \
                   (readout_pieces(units[c - 1]) if c > 0 else [])
        mxu_groups = _spread(mxu_work, len(step_groups))
        a_sl = slice(unit[0] * nss + unit[1] * lc, unit[0] * nss + (unit[1] + 1) * lc)
        a_re = are_ref[:, a_sl]
        a_im = aim_ref[:, a_sl]
        re_sl, im_sl = lanes(unit, 0), lanes(unit, 1)
        sr = s_ref[:, re_sl]
        si = s_ref[:, im_sl]
        for steps, pieces in zip(step_groups, mxu_groups):
            for p in pieces:
                p()
            for k in steps:
                t = (tc - 1 - k) if reverse else k
                row = slice(t * nb, (t + 1) * nb)
                nr = a_re * sr - a_im * si + bu_ref[row, re_sl]
                ni = a_re * si + a_im * sr + bu_ref[row, im_sl]
                sb_ref[row, re_sl] = nr.astype(BF16)
                sb_ref[row, im_sl] = ni.astype(BF16)
                sr, si = nr, ni
        s_ref[:, re_sl] = sr
        s_ref[:, im_sl] = si
    for p in readout_pieces(units[-1]):
        p()

    rb = min(256, rows)
    for r in range(rows // rb):
        rs = slice(r * rb, (r + 1) * rb)
        if final:
            z = _gelu_tanh(yprev_ref[rs, :] + y_ref[rs, :])
            gate = jnp.dot(z.astype(BF16), gluw_ref[...], preferred_element_type=F32) + glub_ref[...]
            out_ref[rs, :] = (z * jax.nn.sigmoid(gate)).astype(out_ref.dtype)
        else:
            out_ref[rs, :] = y_ref[rs, :] + u_ref[rs, :] * d_ref[...]


def _s5_scan(u_tm, bd, cd, a_re, a_im, extra, *, reverse, final, nb, out_dtype):
    rows_total, d_s5 = u_tm.shape
    seq = rows_total // nb
    two_ns = bd.shape[0] * bd.shape[2]
    tc = min(S5_TIME_CHUNK, seq)
    nchunk = seq // tc
    rows = tc * nb
    if reverse:
        cidx = lambda i: (nchunk - 1 - i, 0)
    else:
        cidx = lambda i: (i, 0)
    in_specs = [
        pl.BlockSpec((rows, d_s5), cidx),
        _const_spec(bd.shape),
        _const_spec(cd.shape),
        _const_spec(a_re.shape),
        _const_spec(a_im.shape),
    ]
    if final:
        yprev, glu_w, glu_b = extra
        in_specs += [pl.BlockSpec((rows, d_s5), cidx), _const_spec(glu_w.shape), _const_spec(glu_b.shape)]
    else:
        in_specs += [_const_spec(extra[0].shape)]
    return pl.pallas_call(
        functools.partial(_s5_scan_kernel, reverse=reverse, final=final, tc=tc, nb=nb),
        out_shape=jax.ShapeDtypeStruct((rows_total, d_s5), out_dtype),
        grid=(nchunk,),
        in_specs=in_specs,
        out_specs=pl.BlockSpec((rows, d_s5), cidx),
        scratch_shapes=[pltpu.VMEM((rows, two_ns), F32), pltpu.VMEM((rows, two_ns), BF16),
                        pltpu.VMEM((rows, d_s5), F32), pltpu.VMEM((nb, two_ns), F32)],
        compiler_params=_cparams(1),
        name="s5_scan_bwd_glu" if final else "s5_scan_fwd",
    )(u_tm, bd, cd, a_re, a_im, *extra)


def _s5_branch(u_tm, nb, lam_re, lam_im, log_step, b_re, b_im, c_re, c_im, d, glu_w, glu_b):
    ndir, g, p = lam_re.shape
    grp = b_re.shape[-1]
    ns = g * p
    d_s5 = g * grp
    ks = S5_CH_SPLIT if d_s5 % S5_CH_SPLIT == 0 else d_s5
    nsplit = d_s5 // ks
    gs = g // nsplit
    flat = lambda a: a.reshape(ndir, ns)
    to_lanes = lambda a: jnp.transpose(a, (0, 3, 1, 2)).reshape(ndir, grp, ns)
    a_re, a_im, bb_re, bb_im, ncim = _s5_prepare(
        flat(lam_re), flat(lam_im), jnp.repeat(log_step, p, axis=-1),
        to_lanes(b_re), to_lanes(b_im),
        jnp.transpose(c_im, (0, 2, 1, 3)).reshape(ndir, grp, ns))

    def in_blocks(a):
        a = jnp.transpose(a.reshape(grp, nsplit, gs, p), (1, 2, 0, 3))
        return jnp.stack([_block_diag(a[h]) for h in range(nsplit)])

    def out_blocks(a):
        a = jnp.transpose(a.reshape(nsplit, gs, grp, p), (0, 1, 3, 2))
        return jnp.stack([_block_diag(a[h]) for h in range(nsplit)])

    y = None
    for direction in range(ndir):
        bd = jnp.concatenate([in_blocks(bb_re[direction]), in_blocks(bb_im[direction])], axis=2).astype(BF16)
        ncim_g = jnp.transpose(ncim[direction].reshape(grp, g, p), (1, 0, 2))
        cd = jnp.concatenate([out_blocks(c_re[direction]), out_blocks(ncim_g)], axis=1).astype(BF16)
        are = jnp.broadcast_to(a_re[direction][None, :], (nb, ns))
        aim = jnp.broadcast_to(a_im[direction][None, :], (nb, ns))
        if direction == 0:
            if ndir != 2:
                raise NotImplementedError("S5 branch expects forward and backward directions")
            y = _s5_scan(u_tm, bd, cd, are, aim, (d.reshape(1, -1),), reverse=False, final=False,
                         nb=nb, out_dtype=F32)
        else:
            extra = (y, glu_w.astype(BF16), glu_b.reshape(1, -1))
            y = _s5_scan(u_tm, bd, cd, are, aim, extra, reverse=True, final=True, nb=nb, out_dtype=BF16)
    return y


RSQRT2 = math.sqrt(0.5)


def _dft_table(nf, t, n):
    f_lo = 64 if nf % 64 == 0 else 1
    f_hi = nf // f_lo
    t = t[:, None, :]
    ka = (jnp.arange(f_hi, dtype=jnp.int32)[None, :, None] * f_lo * t) % n
    kb = (jnp.arange(f_lo, dtype=jnp.int32)[None, :, None] * t) % n
    w = 2.0 * math.pi / n
    aa = ka.astype(F32) * w
    ab = kb.astype(F32) * w
    ca, sa, cb, sb = jnp.cos(aa), jnp.sin(aa), jnp.cos(ab), jnp.sin(ab)
    cm = ca[:, :, None, :] * cb[:, None, :, :] - sa[:, :, None, :] * sb[:, None, :, :]
    sm = sa[:, :, None, :] * cb[:, None, :, :] + ca[:, :, None, :] * sb[:, None, :, :]
    shape = (t.shape[0], nf, t.shape[-1])
    return cm.reshape(shape), sm.reshape(shape)


def _hyena_tables(seq):
    q = seq // 4
    times = 4 * jnp.arange(q, dtype=jnp.int32)[None, :] + jnp.arange(4, dtype=jnp.int32)[:, None]
    cf, sf = _dft_table(q, times, 2 * seq)
    alt = (1 - 2 * (jnp.arange(q, dtype=jnp.int32) & 1)).astype(F32)
    f_is_0 = jnp.arange(q, dtype=jnp.int32)[None, :, None] == 0
    sf = jnp.where(f_is_0, alt[None, None, :], sf)
    return dict(c_fwd=cf.astype(BF16), s_fwd=sf.astype(BF16),
                c_inv=jnp.transpose(cf, (0, 2, 1)).astype(BF16), s_inv=jnp.transpose(sf, (0, 2, 1)).astype(BF16))


def _cmul(z, k):
    (zc, zs), (kc, ks) = z, k
    return zc * kc - zs * ks, zc * ks + zs * kc


def _forward_spectrum(x, a):
    (c0, s0), (c1, s1), (c2, s2), (c3, s3) = x
    pc, ps, mc, ms = c0 + c2, s0 + s2, c0 - c2, s0 - s2
    qc, qs, nc, ns = c1 + c3, s1 + s3, c1 - c3, s1 - s3
    za = (pc + qc, ps + qs)
    zb = (pc - qc, qs - ps)
    zc = (mc + ns, nc - ms)
    zd = (mc - ns, ms + nc)
    e = (a[1] - a[3]) * RSQRT2
    o = (a[1] + a[3]) * RSQRT2
    return [za, zb, zc, zd], [(a[0] + e, a[2] + o), (a[0] - e, o - a[2])]


def _inverse_spectrum(y, ymid):
    (ac, as_), (bc, bs), (cc, cs), (dc, ds) = y
    upc, ups, umc, ums = ac + bc, as_ - bs, ac - bc, as_ + bs
    vpc, vps, vmc, vms = dc + cc, ds - cs, dc - cc, ds + cs
    t = [(upc + vpc, ups + vps), (umc + vms, ums - vmc), (upc - vpc, ups - vps), (umc - vms, ums + vmc)]
    (y1c, y1s), (y3c, y3s) = ymid
    sp = [y1c + y3c, ((y1c + y1s) + (y3s - y3c)) * RSQRT2, y1s - y3s, ((y1s - y1c) + (y3c + y3s)) * RSQRT2]
    return t, sp


def _hy_filter_kernel(feat_ref, w1_ref, b1_ref, w2_ref, b2_ref, freq_ref,
                      w3f_ref, b3f_ref, decf_ref, w3b_ref, b3b_ref, decb_ref,
                      cf_ref, sf_ref,
                      kac_ref, kas_ref, kbc_ref, kbs_ref, kcc_ref, kcs_ref, kdc_ref, kds_ref, km_ref,
                      taps_s, rhs_s, h_s):
    seq = feat_ref.shape[0]
    q = seq // 4
    n = 2 * seq
    ct = kac_ref.shape[1]
    nsl = ct // LANE

    @pl.when((pl.program_id(0) == 0) & (pl.program_id(1) == 0))
    def _():
        f = freq_ref[...]
        h1 = jnp.sin(f * (jnp.dot(feat_ref[...], w1_ref[...], precision=HIGHEST,
                                  preferred_element_type=F32) + b1_ref[...]))
        h_s[...] = jnp.sin(f * (jnp.dot(h1, w2_ref[...], precision=HIGHEST,
                                        preferred_element_type=F32) + b2_ref[...]))

    h = h_s[...]
    t01 = feat_ref[:, 0:1]
    row = lax.broadcasted_iota(jnp.int32, (seq, 1), 0)

    def taps(w3_ref, b3_ref, dec_ref):
        v = jnp.dot(h, w3_ref[...], precision=HIGHEST, preferred_element_type=F32) + b3_ref[...]
        return v * jnp.exp(-t01 * jnp.abs(dec_ref[...]))

    fwd = taps(w3f_ref, b3f_ref, decf_ref)
    bwd = jnp.where(row == 0, 0.0, taps(w3b_ref, b3b_ref, decb_ref))
    for d, x in enumerate((fwd, bwd)):
        for j in range(nsl):
            taps_s[d * nsl + j] = x[:, j * LANE:(j + 1) * LANE]
        for r in range(4):
            for j in range(nsl):
                c0 = (4 * d + r) * ct + j * LANE
                rhs_s[:, c0:c0 + LANE] = taps_s[d * nsl + j, pl.ds(r, q, stride=4), :].astype(BF16)
    row0 = lax.broadcasted_iota(jnp.int32, (q, 1), 0) == 0

    def spectrum(d):
        x, a = [], []
        for r in range(4):
            rhs = rhs_s[:, (4 * d + r) * ct:(4 * d + r + 1) * ct]
            c = jnp.dot(cf_ref[r], rhs, preferred_element_type=F32)
            sn = jnp.dot(sf_ref[r], rhs, preferred_element_type=F32)
            a.append(sn[0:1, :])
            x.append((c, jnp.where(row0, 0.0, sn)))
        return _forward_spectrum(x, a)

    fz, fm = spectrum(0)
    bz, bm = spectrum(1)
    scale = jnp.where(row0, 1.0 / n, 2.0 / n)
    outs = ((kac_ref, kas_ref), (kbc_ref, kbs_ref), (kcc_ref, kcs_ref), (kdc_ref, kds_ref))
    for (oc, os_), (fc, fs), (bc, bs) in zip(outs, fz, bz):
        oc[...] = (fc + bc) * scale
        os_[...] = (fs - bs) * scale
    km_ref[...] = jnp.zeros_like(km_ref)
    for i, ((fc, fs), (bc, bs)) in enumerate(zip(fm, bm)):
        km_ref[2 * i:2 * i + 1, :] = (fc + bc) * (2.0 / n)
        km_ref[2 * i + 1:2 * i + 2, :] = (fs - bs) * (2.0 / n)


def _hyena_filters(seq, w1, b1, w2, b2, w3, b3, freq, decay, tables, n_order, n_dirs, d_hy):
    q = seq // 4
    emb, hid = w1.shape
    bands = (emb - 1) // 2
    t = jnp.arange(seq, dtype=F32)
    t01 = t / max(seq - 1, 1)
    band = jnp.linspace(1e-4, bands - 1, bands, dtype=F32)
    ang = (2.0 * math.pi) * t[:, None] * band[None, :] / seq
    feats = jnp.concatenate([t01[:, None], jnp.cos(ang), jnp.sin(ang)], axis=-1)
    kpad = LANE
    feats = jnp.pad(feats, ((0, 0), (0, kpad - emb)))
    w1p = jnp.pad(w1, ((0, kpad - emb), (0, 0)))
    tabs = [tables["c_fwd"], tables["s_fwd"]]
    ct = min(HY_CH_TILE, d_hy)
    nct = d_hy // ct
    ncol = n_order * d_hy
    b3r = b3.reshape(1, -1)
    decr = decay.reshape(1, -1)
    fcol = lambda o, c: (0, (o * n_dirs + 0) * nct + c)
    bcol = lambda o, c: (0, (o * n_dirs + 1) * nct + c)
    ocol = lambda o, c: (0, o * nct + c)
    full = lambda a: pl.BlockSpec(a.shape, lambda o, c: (0,) * a.ndim)
    spec = jax.ShapeDtypeStruct((q, ncol), F32)
    return pl.pallas_call(
        _hy_filter_kernel,
        out_shape=(spec,) * 8 + (jax.ShapeDtypeStruct((8, ncol), F32),),
        grid=(n_order, nct),
        in_specs=[
            full(feats), full(w1p), full(b1.reshape(1, -1)), full(w2), full(b2.reshape(1, -1)),
            full(freq.reshape(1, -1)),
            pl.BlockSpec((hid, ct), fcol), pl.BlockSpec((1, ct), fcol), pl.BlockSpec((1, ct), fcol),
            pl.BlockSpec((hid, ct), bcol), pl.BlockSpec((1, ct), bcol), pl.BlockSpec((1, ct), bcol),
        ] + [_const_spec(a.shape) for a in tabs],
        out_specs=tuple(pl.BlockSpec((q, ct), ocol) for _ in range(8)) + (pl.BlockSpec((8, ct), ocol),),
        scratch_shapes=[pltpu.VMEM((2 * ct // LANE, seq, LANE), F32), pltpu.VMEM((q, 8 * ct), BF16),
                        pltpu.VMEM((seq, hid), F32)],
        compiler_params=_cparams(2),
        name="hyena_filter_spectra",
    )(feats, w1p, b1.reshape(1, -1), w2, b2.reshape(1, -1), freq.reshape(1, -1),
      w3, b3r, decr, w3, b3r, decr, *tabs)


def _hy_conv_kernel(zin_ref, gin_ref, wz_ref, bz_ref, wg_ref, bg_ref,
                    kac_ref, kas_ref, kbc_ref, kbs_ref, kcc_ref, kcs_ref, kdc_ref, kds_ref, km_ref, bias_ref,
                    cf_ref, sf_ref, ci_ref, si_ref, out_ref,
                    zraw_s, graw_s, o_s, z_s, g_s, rhs_s, ac_s, as_s, *, conv_on_z, seq):
    ct = zin_ref.shape[1]
    nbat = zin_ref.shape[0] // seq
    q = seq // 4
    nsl = ct // LANE
    pad = zraw_s.shape[2] - seq
    top = pad // 2
    rc = min(DFT_ROW_CHUNK, q)
    k_refs = ((kac_ref, kas_ref), (kbc_ref, kbs_ref), (kcc_ref, kcs_ref), (kdc_ref, kds_ref))

    def stage(raw_s, src_ref, k):
        for j in range(nsl):
            raw_s[k, j, 0:top, :] = jnp.zeros((top, LANE), F32)
            raw_s[k, j, top + seq:, :] = jnp.zeros((pad - top, LANE), F32)
            raw_s[k, j, top:top + seq, :] = src_ref[k * seq:(k + 1) * seq, j * LANE:(j + 1) * LANE].astype(F32)

    def split_rows(raw_s, dst_ref, k, w_ref, b_ref):
        for r in range(4):
            for j in range(nsl):
                ls = slice(j * LANE, (j + 1) * LANE)
                tap = lambda d: raw_s[k, j, pl.ds(top + r + d, q, stride=4), :]
                if w_ref is None:
                    v = tap(0)
                else:
                    v = (b_ref[:, ls] + tap(-1) * w_ref[0:1, ls] + tap(0) * w_ref[1:2, ls]
                         + tap(1) * w_ref[2:3, ls])
                dst_ref[k, :, r * ct + j * LANE:r * ct + (j + 1) * LANE] = v

    def spectrum_chunk(k, i):
        r = slice(i * rc, (i + 1) * rc)
        row0 = lax.broadcasted_iota(jnp.int32, (rc, 1), 0) == 0
        x, a = [], []
        for p in range(4):
            rhs = rhs_s[k, :, p * ct:(p + 1) * ct]
            c = jnp.dot(cf_ref[p, r, :], rhs, preferred_element_type=F32)
            sn = jnp.dot(sf_ref[p, r, :], rhs, preferred_element_type=F32)
            if i == 0:
                a.append(sn[0:1, :])
                sn = jnp.where(row0, 0.0, sn)
            else:
                a.append(jnp.zeros((1, ct), F32))
            x.append((c, sn))
        z, zmid = _forward_spectrum(x, a)
        y = [_cmul(zx, (kc[r, :], ks[r, :])) for zx, (kc, ks) in zip(z, k_refs)]
        ymid = [_cmul(zmid[m], (km_ref[2 * m:2 * m + 1, :], km_ref[2 * m + 1:2 * m + 2, :])) for m in range(2)]
        t, sp = _inverse_spectrum(y, ymid)
        for p, ((tc, ts), spp) in enumerate(zip(t, sp)):
            if i == 0:
                ts = jnp.where(row0, spp, ts)
            ac_s[k, r, p * ct:(p + 1) * ct] = tc.astype(BF16)
            as_s[k, r, p * ct:(p + 1) * ct] = ts.astype(BF16)

    def output_chunk(k, i):
        r = slice(i * rc, (i + 1) * rc)
        for p in range(4):
            ps = slice(p * ct, (p + 1) * ct)
            y = jnp.dot(ci_ref[p, r, :], ac_s[k, :, ps], preferred_element_type=F32)
            y = y + jnp.dot(si_ref[p, r, :], as_s[k, :, ps], preferred_element_type=F32)
            rows = pl.ds(4 * i * rc + p, rc, stride=4)
            for j in range(nsl):
                ls = slice(p * ct + j * LANE, p * ct + (j + 1) * LANE)
                jl = slice(j * LANE, (j + 1) * LANE)
                o_s[k, j, rows, :] = g_s[k, r, ls] * (y[:, jl] + z_s[k, r, ls] * bias_ref[:, jl])

    for k in range(nbat):
        stage(zraw_s, zin_ref, k)
        stage(graw_s, gin_ref, k)
        split_rows(zraw_s, z_s, k, wz_ref if conv_on_z else None, bz_ref)
        split_rows(graw_s, g_s, k, wg_ref, bg_ref)
        rhs_s[k] = z_s[k].astype(BF16)
    for i in range(q // rc):
        for k in range(nbat):
            spectrum_chunk(k, i)
    for i in range(q // rc):
        for k in range(nbat):
            output_chunk(k, i)
    for k in range(nbat):
        for j in range(nsl):
            out_ref[k * seq:(k + 1) * seq, j * LANE:(j + 1) * LANE] = o_s[k, j].astype(out_ref.dtype)


def _hy_conv(zin, zcol0, gcol0, u_hy, conv_w, conv_b, spectra, bias_row, tables,
             *, order, conv_on_z, bsz, seq, d_hy):
    tabs = [tables[k] for k in ("c_fwd", "s_fwd", "c_inv", "s_inv")]
    q = seq // 4
    ct = min(HY_CH_TILE, d_hy)
    nsl = ct // LANE
    nct = d_hy // ct
    nbat = HY_BATCH_GROUP if bsz % HY_BATCH_GROUP == 0 else 1
    zc0 = zcol0 // ct
    gc0 = gcol0 // ct
    zw0 = zc0 if conv_on_z else 0
    kspec = lambda rows: pl.BlockSpec((rows, ct), lambda c, b: (0, order * nct + c),
                                      pipeline_mode=pl.Buffered(1))
    scratch = lambda dt: pltpu.VMEM((nbat, q, 4 * ct), dt)
    slabs = lambda rows: pltpu.VMEM((nbat, nsl, rows, LANE), F32)
    return pl.pallas_call(
        functools.partial(_hy_conv_kernel, conv_on_z=conv_on_z, seq=seq),
        out_shape=jax.ShapeDtypeStruct((bsz * seq, d_hy), BF16),
        grid=(nct, bsz // nbat),
        in_specs=[
            pl.BlockSpec((nbat * seq, ct), lambda c, b: (b, zc0 + c)),
            pl.BlockSpec((nbat * seq, ct), lambda c, b: (b, gc0 + c)),
            pl.BlockSpec((conv_w.shape[0], ct), lambda c, b: (0, zw0 + c)),
            pl.BlockSpec((1, ct), lambda c, b: (0, zw0 + c)),
            pl.BlockSpec((conv_w.shape[0], ct), lambda c, b: (0, gc0 + c)),
            pl.BlockSpec((1, ct), lambda c, b: (0, gc0 + c)),
        ] + [kspec(q)] * 8 + [kspec(8), kspec(1)] + [_const_spec(a.shape) for a in tabs],
        out_specs=pl.BlockSpec((nbat * seq, ct), lambda c, b: (b, c)),
        scratch_shapes=[
            slabs(seq + 2 * SUBLANE), slabs(seq + 2 * SUBLANE), slabs(seq),
            scratch(F32), scratch(F32), scratch(BF16), scratch(BF16), scratch(BF16),
        ],
        compiler_params=_cparams(2),
        name=f"hyena_conv_order{order}",
    )(zin, u_hy, conv_w, conv_b, conv_w, conv_b, *spectra, bias_row, *tabs)


def _hyena_branch(u_hy, bsz, seq, conv_w, conv_b, w1, b1, w2, b2, w3, b3, freq, decay, bias):
    n_order, d_hy = bias.shape
    n_dirs = w3.shape[1] // (n_order * d_hy)
    if n_order != 2 or seq % 8 != 0 or d_hy % LANE != 0:
        raise NotImplementedError("Hyena branch: two long convolutions, L % 8 == 0, 128-lane channel tiles")
    tables = _hyena_tables(seq)
    spectra = _hyena_filters(seq, w1, b1, w2, b2, w3, b3, freq, decay, tables, n_order, n_dirs, d_hy)
    cb = conv_b.reshape(1, -1)
    bias_row = bias.reshape(1, -1)
    common = dict(bsz=bsz, seq=seq, d_hy=d_hy)
    z1 = _hy_conv(u_hy, 0, d_hy, u_hy, conv_w, cb, spectra, bias_row, tables,
                  order=0, conv_on_z=True, **common)
    return _hy_conv(z1, 0, 2 * d_hy, u_hy, conv_w, cb, spectra, bias_row, tables,
                    order=1, conv_on_z=False, **common)


def _mixer_kernel(x_ref, mod_ref, g1_ref, g2_ref, fg_ref, permt_ref, za_ref, zb_ref,
                  wgate_ref, wa_ref, wb_ref, wout_ref, wg_ref, wu_ref, wd_ref, o_ref, *, n_chunks):
    nb, tt, d = x_ref.shape
    rows = nb * tt
    mod = lambda k: mod_ref[:, k:k + 1, :]
    x = x_ref[...]
    h = _norm_modulate(x, g1_ref[...], mod(0), mod(1)).reshape(rows, d).astype(BF16)
    gate = jax.nn.sigmoid(jnp.dot(h, wgate_ref[...], preferred_element_type=F32))
    za = jnp.dot(permt_ref[...], za_ref[...], preferred_element_type=F32).astype(BF16)
    ya = jnp.dot(za, wa_ref[...], preferred_element_type=F32)
    yb = jnp.dot(zb_ref[...].reshape(rows, zb_ref.shape[-1]), wb_ref[...], preferred_element_type=F32)
    merged = gate[:, :d] * ya + gate[:, d:] * yb
    o = jnp.dot(merged.astype(BF16), wout_ref[...], preferred_element_type=F32)
    x1 = x + mod(2) * o.reshape(nb, tt, d)

    h2 = _norm_modulate(x1, g2_ref[...], mod(3), mod(4)).reshape(rows, d).astype(BF16)
    d_ff = wg_ref.shape[1]
    unit = MXU_TILE if d_ff % MXU_TILE == 0 else d_ff // n_chunks
    edges = [len(g) for g in _spread(list(range(d_ff // unit)), n_chunks)]
    acc = jnp.zeros((rows, d), F32)
    start = 0
    for width in edges:
        sl = slice(start * unit, (start + width) * unit)
        start += width
        gl = jnp.dot(h2, wg_ref[:, sl], preferred_element_type=F32)
        up = jnp.dot(h2, wu_ref[:, sl], preferred_element_type=F32)
        act = (gl * jax.nn.sigmoid(gl) * up).astype(BF16)
        acc = acc + jnp.dot(act, wd_ref[sl, :], preferred_element_type=F32)
    x2 = x1 + mod(5) * acc.reshape(nb, tt, d)
    ms = jnp.mean(x2 * x2, axis=-1, keepdims=True)
    o_ref[...] = x2 * lax.rsqrt(ms + EPS) * fg_ref[...]


def _mixer(x, mod3, norm1_g, norm2_g, final_g, perm_t, za_tm, zb, w_gate, w_a, w_b, w_out, w_g, w_u, w_d):
    bsz, seq, d = x.shape
    d_s5 = w_a.shape[0]
    d_hy = w_b.shape[0]
    d_ff = w_g.shape[1]
    tt = _time_tile(bsz, seq)
    n_chunks = 2 if d_ff % (2 * LANE) == 0 else 1
    consts = (mod3, norm1_g, norm2_g, final_g, perm_t)
    weights = (w_gate, w_a, w_b, w_out, w_g, w_u, w_d)
    return pl.pallas_call(
        functools.partial(_mixer_kernel, n_chunks=n_chunks),
        out_shape=jax.ShapeDtypeStruct(x.shape, F32),
        grid=(seq // tt,),
        in_specs=[pl.BlockSpec((bsz, tt, d), lambda j: (0, j, 0))]
        + [_const_spec(a.shape) for a in consts]
        + [pl.BlockSpec((tt * bsz, d_s5), lambda j: (j, 0)),
           pl.BlockSpec((bsz, tt, d_hy), lambda j: (0, j, 0))]
        + [_const_spec(a.shape) for a in weights],
        out_specs=pl.BlockSpec((bsz, tt, d), lambda j: (0, j, 0)),
        compiler_params=_cparams(1),
        name="merge_swiglu_final_norm",
    )(x, *consts, za_tm, zb, *weights)


def kernel(x, c, ada_w, ada_b, norm1_g, norm2_g, w_in, s5_lam_re, s5_lam_im, s5_log_step, s5_b_re, s5_b_im, s5_c_re, s5_c_im, s5_d, s5_glu_w, s5_glu_b, hy_conv_w, hy_conv_b, hy_ffn_w1, hy_ffn_b1, hy_ffn_w2, hy_ffn_b2, hy_ffn_w3, hy_ffn_b3, hy_freq, hy_decay, hy_bias, w_branch_a, w_branch_b, w_out, ffn_w_gu, ffn_w_down, final_g):
    bsz, seq, d = x.shape
    depth = ada_w.shape[0]
    if depth != 1:
        raise NotImplementedError("the final RMSNorm is fused into the (single) layer's channel mixer")
    d_s5 = s5_d.shape[-1]
    n_order, d_hy = hy_bias.shape[1:]
    d_uh = d_s5 + (n_order + 1) * d_hy
    d_ff = ffn_w_down.shape[1]
    i = 0
    perm = _row_permutation(bsz, _time_tile(bsz, seq))
    mod = _modulation(c, ada_w[i], ada_b[i]).reshape(bsz, 6, d)
    w_in_b = w_in[i].astype(BF16)
    u_s5, u_hy = _in_projection(x, mod, norm1_g[i].reshape(1, d), w_in_b[:, :d_uh], perm, d_s5)
    z_a = _s5_branch(u_s5, bsz, s5_lam_re[i], s5_lam_im[i], s5_log_step[i], s5_b_re[i], s5_b_im[i],
                     s5_c_re[i], s5_c_im[i], s5_d[i], s5_glu_w[i], s5_glu_b[i])
    z_b = _hyena_branch(u_hy.reshape(bsz * seq, -1), bsz, seq, hy_conv_w[i], hy_conv_b[i], hy_ffn_w1[i],
                        hy_ffn_b1[i], hy_ffn_w2[i], hy_ffn_b2[i], hy_ffn_w3[i], hy_ffn_b3[i], hy_freq[i],
                        hy_decay[i], hy_bias[i])
    w_gu = ffn_w_gu[i].astype(BF16)
    return _mixer(x, mod, norm1_g[i].reshape(1, d), norm2_g[i].reshape(1, d), final_g.reshape(1, d),
                  perm.T, z_a, z_b.reshape(bsz, seq, d_hy), w_in_b[:, d_uh:],
                  w_branch_a[i].astype(BF16), w_branch_b[i].astype(BF16), w_out[i].astype(BF16),
                  w_gu[:, :d_ff], w_gu[:, d_ff:], ffn_w_down[i].astype(BF16))
```

```python
import functools
import math

import jax
import jax.numpy as jnp
from jax import lax
from jax.experimental import pallas as pl
from jax.experimental.pallas import tpu as pltpu

F32 = jnp.float32
BF16 = jnp.bfloat16
EPS = 1e-6
HIGHEST = lax.Precision.HIGHEST

V7X_VMEM_BYTES = 64 * 1024 * 1024
VMEM_LIMIT_BYTES = 56 * 1024 * 1024
LANE = 128
SUBLANE = 8
ROW_TILE = 512
S5_TIME_CHUNK = 64
S5_LANE_CHUNK = 512
S5_CH_SPLIT = 256
S5_STEPS_PER_GROUP = 8
MXU_TILE = 256
HY_CH_TILE = 256
HY_BATCH_GROUP = 2
DFT_ROW_CHUNK = 512


def _cparams(n_axes):
    return pltpu.CompilerParams(
        dimension_semantics=("arbitrary",) * n_axes,
        vmem_limit_bytes=VMEM_LIMIT_BYTES,
    )


def _const_spec(shape):
    nd = len(shape)
    return pl.BlockSpec(shape, lambda *_: (0,) * nd, pipeline_mode=pl.Buffered(1))


def _gelu_tanh(x):
    return 0.5 * x * (1.0 + jnp.tanh(math.sqrt(2.0 / math.pi) * (x + 0.044715 * (x * x * x))))


def _norm_modulate(x, g, shift, scale):
    ms = jnp.mean(x * x, axis=-1, keepdims=True)
    r = x * lax.rsqrt(ms + EPS) * g
    return r * (1.0 + scale) + shift


def _split_bf16(x):
    hi = x.astype(BF16)
    return hi, (x - hi.astype(F32)).astype(BF16)


def _mod_kernel(c_ref, w_ref, b_ref, o_ref):
    c = c_ref[...]
    c_hi, c_lo = _split_bf16(c * jax.nn.sigmoid(c))
    w_hi, w_lo = _split_bf16(w_ref[...])
    dot = lambda a, b: jnp.dot(a, b, preferred_element_type=F32)
    o_ref[...] = dot(c_hi, w_hi) + (dot(c_hi, w_lo) + dot(c_lo, w_hi)) + b_ref[...]


def _modulation(c, ada_w, ada_b):
    bsz, d = c.shape
    n = ada_w.shape[1]
    tn = 512
    return pl.pallas_call(
        _mod_kernel,
        out_shape=jax.ShapeDtypeStruct((bsz, n), F32),
        grid=(n // tn,),
        in_specs=[
            pl.BlockSpec((bsz, d), lambda j: (0, 0)),
            pl.BlockSpec((d, tn), lambda j: (0, j)),
            pl.BlockSpec((1, tn), lambda j: (0, j)),
        ],
        out_specs=pl.BlockSpec((bsz, tn), lambda j: (0, j)),
        compiler_params=_cparams(1),
        name="adaln_mod",
    )(c, ada_w, ada_b.reshape(1, n))


def _time_tile(bsz, seq):
    return max(min(ROW_TILE // bsz, seq), 1)


def _row_permutation(bsz, tt):
    r = jnp.arange(bsz * tt, dtype=jnp.int32)
    src = (r % bsz) * tt + r // bsz
    return (src[:, None] == r[None, :]).astype(BF16)


def _inproj_kernel(x_ref, mod_ref, g_ref, w_ref, perm_ref, us5_ref, uhy_ref, *, d_s5):
    nb, tt, d = x_ref.shape
    h = _norm_modulate(x_ref[...], g_ref[...], mod_ref[:, 0:1, :], mod_ref[:, 1:2, :])
    hb = h.reshape(nb * tt, d).astype(BF16)
    p = jnp.dot(hb, w_ref[...], preferred_element_type=F32)
    us5_ref[...] = jnp.dot(perm_ref[...], p[:, :d_s5].astype(BF16),
                           preferred_element_type=F32).astype(us5_ref.dtype)
    uhy_ref[...] = p[:, d_s5:].astype(BF16).reshape(uhy_ref.shape)


def _in_projection(x, mod3, norm_g, w_uh, perm, d_s5):
    bsz, seq, d = x.shape
    n = w_uh.shape[1]
    tt = _time_tile(bsz, seq)
    return pl.pallas_call(
        functools.partial(_inproj_kernel, d_s5=d_s5),
        out_shape=(
            jax.ShapeDtypeStruct((seq * bsz, d_s5), BF16),
            jax.ShapeDtypeStruct((bsz, seq, n - d_s5), BF16),
        ),
        grid=(seq // tt,),
        in_specs=[
            pl.BlockSpec((bsz, tt, d), lambda j: (0, j, 0)),
            _const_spec(mod3.shape),
            _const_spec((1, d)),
            _const_spec((d, n)),
            _const_spec(perm.shape),
        ],
        out_specs=(
            pl.BlockSpec((tt * bsz, d_s5), lambda j: (j, 0)),
            pl.BlockSpec((bsz, tt, n - d_s5), lambda j: (0, j, 0)),
        ),
        compiler_params=_cparams(1),
        name="in_proj",
    )(x, mod3, norm_g, w_uh, perm)


def _s5_prep_kernel(lre_ref, lim_ref, lstep_ref, bre_ref, bim_ref, cim_ref,
                    are_ref, aim_ref, bbre_ref, bbim_ref, ncim_ref):
    step = jnp.exp(lstep_ref[...])
    lr = lre_ref[...]
    li = lim_ref[...]
    mag = jnp.exp(lr * step)
    ar = mag * jnp.cos(li * step)
    ai = mag * jnp.sin(li * step)
    num = ar - 1.0
    den = lr * lr + li * li
    cr = (num * lr + ai * li) / den
    ci = (ai * lr - num * li) / den
    are_ref[...] = ar
    aim_ref[...] = ai
    for d in range(lre_ref.shape[0]):
        br = bre_ref[d]
        bi = bim_ref[d]
        bbre_ref[d] = cr[d:d + 1, :] * br - ci[d:d + 1, :] * bi
        bbim_ref[d] = cr[d:d + 1, :] * bi + ci[d:d + 1, :] * br
    ncim_ref[...] = -cim_ref[...]


def _s5_prepare(lam_re, lam_im, log_step, b_re, b_im, c_im):
    outs = (
        jax.ShapeDtypeStruct(lam_re.shape, F32),
        jax.ShapeDtypeStruct(lam_re.shape, F32),
        jax.ShapeDtypeStruct(b_re.shape, F32),
        jax.ShapeDtypeStruct(b_re.shape, F32),
        jax.ShapeDtypeStruct(c_im.shape, F32),
    )
    return pl.pallas_call(
        _s5_prep_kernel,
        out_shape=outs,
        compiler_params=pltpu.CompilerParams(vmem_limit_bytes=VMEM_LIMIT_BYTES),
        name="s5_discretize",
    )(lam_re, lam_im, log_step, b_re, b_im, c_im)


def _block_diag(m):
    g, r, c = m.shape
    eye = jnp.eye(g, dtype=m.dtype)
    return (m[:, :, None, :] * eye[:, None, :, None]).reshape(g * r, g * c)


def _spread(items, n):
    return [items[(len(items) * i) // n:(len(items) * (i + 1)) // n] for i in range(n)]


def _s5_scan_kernel(*refs, reverse, final, tc, nb):
    if final:
        (u_ref, bd_ref, cd_ref, are_ref, aim_ref, yprev_ref, gluw_ref, glub_ref,
         out_ref, bu_ref, sb_ref, y_ref, s_ref) = refs
    else:
        (u_ref, bd_ref, cd_ref, are_ref, aim_ref, d_ref, out_ref, bu_ref, sb_ref, y_ref, s_ref) = refs
    rows = tc * nb
    nsplit, ks, two_nss = bd_ref.shape
    nss = two_nss // 2
    lc = min(S5_LANE_CHUNK, nss)
    tw = min(MXU_TILE, lc)
    rh = rows // 2 if rows % (2 * SUBLANE * 2) == 0 else rows
    row_halves = [slice(r, r + rh) for r in range(0, rows, rh)]

    @pl.when(pl.program_id(0) == 0)
    def _():
        s_ref[...] = jnp.zeros_like(s_ref)

    units = [(h, q) for h in range(nsplit) for q in range(nss // lc)]

    def lanes(unit, part, off=0, width=None):
        h, q = unit
        start = h * two_nss + part * nss + q * lc + off
        return slice(start, start + (lc if width is None else width))

    def expand_pieces(unit):
        h, q = unit
        out = []
        for part in range(2):
            for off in range(0, lc, tw):
                for rs in row_halves:
                    def piece(part=part, off=off, rs=rs):
                        col = part * nss + q * lc + off
                        bu_ref[rs, lanes(unit, part, off, tw)] = jnp.dot(
                            u_ref[rs, h * ks:(h + 1) * ks], bd_ref[h, :, col:col + tw],
                            preferred_element_type=F32)
                    out.append(piece)
        return out

    def readout_pieces(unit):
        h, q = unit
        out = []
        for rs in row_halves:
            def piece(rs=rs):
                v = None
                for part in range(2):
                    k0 = part * nss + q * lc
                    d = jnp.dot(sb_ref[rs, lanes(unit, part)], cd_ref[h, k0:k0 + lc, :],
                                preferred_element_type=F32)
                    v = d if v is None else v + d
                if q == 0:
                    y_ref[rs, h * ks:(h + 1) * ks] = v
                else:
                    y_ref[rs, h * ks:(h + 1) * ks] += v
            out.append(piece)
        return out

    step_groups = _spread(list(range(tc)), max(tc // S5_STEPS_PER_GROUP, 1))
    for p in expand_pieces(units[0]):
        p()
    for c, unit in enumerate(units):
        mxu_work = (expand_pieces(units[c + 1]) if c + 1 < len(units) else []) + \
                   (readout_pieces(units[c - 1]) if c > 0 else [])
        mxu_groups = _spread(mxu_work, len(step_groups))
        a_sl = slice(unit[0] * nss + unit[1] * lc, unit[0] * nss + (unit[1] + 1) * lc)
        a_re = are_ref[:, a_sl]
        a_im = aim_ref[:, a_sl]
        re_sl, im_sl = lanes(unit, 0), lanes(unit, 1)
        sr = s_ref[:, re_sl]
        si = s_ref[:, im_sl]
        for steps, pieces in zip(step_groups, mxu_groups):
            for p in pieces:
                p()
            for k in steps:
                t = (tc - 1 - k) if reverse else k
                row = slice(t * nb, (t + 1) * nb)
                nr = a_re * sr - a_im * si + bu_ref[row, re_sl]
                ni = a_re * si + a_im * sr + bu_ref[row, im_sl]
                sb_ref[row, re_sl] = nr.astype(BF16)
                sb_ref[row, im_sl] = ni.astype(BF16)
                sr, si = nr, ni
        s_ref[:, re_sl] = sr
        s_ref[:, im_sl] = si
    for p in readout_pieces(units[-1]):
        p()

    rb = min(256, rows)
    for r in range(rows // rb):
        rs = slice(r * rb, (r + 1) * rb)
        if final:
            z = _gelu_tanh(yprev_ref[rs, :] + y_ref[rs, :])
            gate = jnp.dot(z.astype(BF16), gluw_ref[...], preferred_element_type=F32) + glub_ref[...]
            out_ref[rs, :] = (z * jax.nn.sigmoid(gate)).astype(out_ref.dtype)
        else:
            out_ref[rs, :] = y_ref[rs, :] + u_ref[rs, :] * d_ref[...]


def _s5_scan(u_tm, bd, cd, a_re, a_im, extra, *, reverse, final, nb, out_dtype):
    rows_total, d_s5 = u_tm.shape
    seq = rows_total // nb
    two_ns = bd.shape[0] * bd.shape[2]
    tc = min(S5_TIME_CHUNK, seq)
    nchunk = seq // tc
    rows = tc * nb
    if reverse:
        cidx = lambda i: (nchunk - 1 - i, 0)
    else:
        cidx = lambda i: (i, 0)
    in_specs = [
        pl.BlockSpec((rows, d_s5), cidx),
        _const_spec(bd.shape),
        _const_spec(cd.shape),
        _const_spec(a_re.shape),
        _const_spec(a_im.shape),
    ]
    if final:
        yprev, glu_w, glu_b = extra
        in_specs += [pl.BlockSpec((rows, d_s5), cidx), _const_spec(glu_w.shape), _const_spec(glu_b.shape)]
    else:
        in_specs += [_const_spec(extra[0].shape)]
    return pl.pallas_call(
        functools.partial(_s5_scan_kernel, reverse=reverse, final=final, tc=tc, nb=nb),
        out_shape=jax.ShapeDtypeStruct((rows_total, d_s5), out_dtype),
        grid=(nchunk,),
        in_specs=in_specs,
        out_specs=pl.BlockSpec((rows, d_s5), cidx),
        scratch_shapes=[pltpu.VMEM((rows, two_ns), F32), pltpu.VMEM((rows, two_ns), BF16),
                        pltpu.VMEM((rows, d_s5), F32), pltpu.VMEM((nb, two_ns), F32)],
        compiler_params=_cparams(1),
        name="s5_scan_bwd_glu" if final else "s5_scan_fwd",
    )(u_tm, bd, cd, a_re, a_im, *extra)


def _s5_branch(u_tm, nb, lam_re, lam_im, log_step, b_re, b_im, c_re, c_im, d, glu_w, glu_b):
    ndir, g, p = lam_re.shape
    grp = b_re.shape[-1]
    ns = g * p
    d_s5 = g * grp
    ks = S5_CH_SPLIT if d_s5 % S5_CH_SPLIT == 0 else d_s5
    nsplit = d_s5 // ks
    gs = g // nsplit
    flat = lambda a: a.reshape(ndir, ns)
    to_lanes = lambda a: jnp.transpose(a, (0, 3, 1, 2)).reshape(ndir, grp, ns)
    a_re, a_im, bb_re, bb_im, ncim = _s5_prepare(
        flat(lam_re), flat(lam_im), jnp.repeat(log_step, p, axis=-1),
        to_lanes(b_re), to_lanes(b_im),
        jnp.transpose(c_im, (0, 2, 1, 3)).reshape(ndir, grp, ns))

    def in_blocks(a):
        a = jnp.transpose(a.reshape(grp, nsplit, gs, p), (1, 2, 0, 3))
        return jnp.stack([_block_diag(a[h]) for h in range(nsplit)])

    def out_blocks(a):
        a = jnp.transpose(a.reshape(nsplit, gs, grp, p), (0, 1, 3, 2))
        return jnp.stack([_block_diag(a[h]) for h in range(nsplit)])

    y = None
    for direction in range(ndir):
        bd = jnp.concatenate([in_blocks(bb_re[direction]), in_blocks(bb_im[direction])], axis=2).astype(BF16)
        ncim_g = jnp.transpose(ncim[direction].reshape(grp, g, p), (1, 0, 2))
        cd = jnp.concatenate([out_blocks(c_re[direction]), out_blocks(ncim_g)], axis=1).astype(BF16)
        are = jnp.broadcast_to(a_re[direction][None, :], (nb, ns))
        aim = jnp.broadcast_to(a_im[direction][None, :], (nb, ns))
        if direction == 0:
            if ndir != 2:
                raise NotImplementedError("S5 branch expects forward and backward directions")
            y = _s5_scan(u_tm, bd, cd, are, aim, (d.reshape(1, -1),), reverse=False, final=False,
                         nb=nb, out_dtype=F32)
        else:
            extra = (y, glu_w.astype(BF16), glu_b.reshape(1, -1))
            y = _s5_scan(u_tm, bd, cd, are, aim, extra, reverse=True, final=True, nb=nb, out_dtype=BF16)
    return y


RSQRT2 = math.sqrt(0.5)


def _dft_table(nf, t, n):
    f_lo = 64 if nf % 64 == 0 else 1
    f_hi = nf // f_lo
    t = t[:, None, :]
    ka = (jnp.arange(f_hi, dtype=jnp.int32)[None, :, None] * f_lo * t) % n
    kb = (jnp.arange(f_lo, dtype=jnp.int32)[None, :, None] * t) % n
    w = 2.0 * math.pi / n
    aa = ka.astype(F32) * w
    ab = kb.astype(F32) * w
    ca, sa, cb, sb = jnp.cos(aa), jnp.sin(aa), jnp.cos(ab), jnp.sin(ab)
    cm = ca[:, :, None, :] * cb[:, None, :, :] - sa[:, :, None, :] * sb[:, None, :, :]
    sm = sa[:, :, None, :] * cb[:, None, :, :] + ca[:, :, None, :] * sb[:, None, :, :]
    shape = (t.shape[0], nf, t.shape[-1])
    return cm.reshape(shape), sm.reshape(shape)


def _hyena_tables(seq):
    q = seq // 4
    times = 4 * jnp.arange(q, dtype=jnp.int32)[None, :] + jnp.arange(4, dtype=jnp.int32)[:, None]
    cf, sf = _dft_table(q, times, 2 * seq)
    alt = (1 - 2 * (jnp.arange(q, dtype=jnp.int32) & 1)).astype(F32)
    f_is_0 = jnp.arange(q, dtype=jnp.int32)[None, :, None] == 0
    sf = jnp.where(f_is_0, alt[None, None, :], sf)
    return dict(c_fwd=cf.astype(BF16), s_fwd=sf.astype(BF16),
                c_inv=jnp.transpose(cf, (0, 2, 1)).astype(BF16), s_inv=jnp.transpose(sf, (0, 2, 1)).astype(BF16))


def _cmul(z, k):
    (zc, zs), (kc, ks) = z, k
    return zc * kc - zs * ks, zc * ks + zs * kc


def _forward_spectrum(x, a):
    (c0, s0), (c1, s1), (c2, s2), (c3, s3) = x
    pc, ps, mc, ms = c0 + c2, s0 + s2, c0 - c2, s0 - s2
    qc, qs, nc, ns = c1 + c3, s1 + s3, c1 - c3, s1 - s3
    za = (pc + qc, ps + qs)
    zb = (pc - qc, qs - ps)
    zc = (mc + ns, nc - ms)
    zd = (mc - ns, ms + nc)
    e = (a[1] - a[3]) * RSQRT2
    o = (a[1] + a[3]) * RSQRT2
    return [za, zb, zc, zd], [(a[0] + e, a[2] + o), (a[0] - e, o - a[2])]


def _inverse_spectrum(y, ymid):
    (ac, as_), (bc, bs), (cc, cs), (dc, ds) = y
    upc, ups, umc, ums = ac + bc, as_ - bs, ac - bc, as_ + bs
    vpc, vps, vmc, vms = dc + cc, ds - cs, dc - cc, ds + cs
    t = [(upc + vpc, ups + vps), (umc + vms, ums - vmc), (upc - vpc, ups - vps), (umc - vms, ums + vmc)]
    (y1c, y1s), (y3c, y3s) = ymid
    sp = [y1c + y3c, ((y1c + y1s) + (y3s - y3c)) * RSQRT2, y1s - y3s, ((y1s - y1c) + (y3c + y3s)) * RSQRT2]
    return t, sp


def _hy_filter_kernel(feat_ref, w1_ref, b1_ref, w2_ref, b2_ref, freq_ref,
                      w3f_ref, b3f_ref, decf_ref, w3b_ref, b3b_ref, decb_ref,
                      cf_ref, sf_ref,
                      kac_ref, kas_ref, kbc_ref, kbs_ref, kcc_ref, kcs_ref, kdc_ref, kds_ref, km_ref,
                      taps_s, rhs_s, h_s):
    seq = feat_ref.shape[0]
    q = seq // 4
    n = 2 * seq
    ct = kac_ref.shape[1]
    nsl = ct // LANE

    @pl.when((pl.program_id(0) == 0) & (pl.program_id(1) == 0))
    def _():
        f = freq_ref[...]
        h1 = jnp.sin(f * (jnp.dot(feat_ref[...], w1_ref[...], precision=HIGHEST,
                                  preferred_element_type=F32) + b1_ref[...]))
        h_s[...] = jnp.sin(f * (jnp.dot(h1, w2_ref[...], precision=HIGHEST,
                                        preferred_element_type=F32) + b2_ref[...]))

    h = h_s[...]
    t01 = feat_ref[:, 0:1]
    row = lax.broadcasted_iota(jnp.int32, (seq, 1), 0)

    def taps(w3_ref, b3_ref, dec_ref):
        v = jnp.dot(h, w3_ref[...], precision=HIGHEST, preferred_element_type=F32) + b3_ref[...]
        return v * jnp.exp(-t01 * jnp.abs(dec_ref[...]))

    fwd = taps(w3f_ref, b3f_ref, decf_ref)
    bwd = jnp.where(row == 0, 0.0, taps(w3b_ref, b3b_ref, decb_ref))
    for d, x in enumerate((fwd, bwd)):
        for j in range(nsl):
            taps_s[d * nsl + j] = x[:, j * LANE:(j + 1) * LANE]
        for r in range(4):
            for j in range(nsl):
                c0 = (4 * d + r) * ct + j * LANE
                rhs_s[:, c0:c0 + LANE] = taps_s[d * nsl + j, pl.ds(r, q, stride=4), :].astype(BF16)
    row0 = lax.broadcasted_iota(jnp.int32, (q, 1), 0) == 0

    def spectrum(d):
        x, a = [], []
        for r in range(4):
            rhs = rhs_s[:, (4 * d + r) * ct:(4 * d + r + 1) * ct]
            c = jnp.dot(cf_ref[r], rhs, preferred_element_type=F32)
            sn = jnp.dot(sf_ref[r], rhs, preferred_element_type=F32)
            a.append(sn[0:1, :])
            x.append((c, jnp.where(row0, 0.0, sn)))
        return _forward_spectrum(x, a)

    fz, fm = spectrum(0)
    bz, bm = spectrum(1)
    scale = jnp.where(row0, 1.0 / n, 2.0 / n)
    outs = ((kac_ref, kas_ref), (kbc_ref, kbs_ref), (kcc_ref, kcs_ref), (kdc_ref, kds_ref))
    for (oc, os_), (fc, fs), (bc, bs) in zip(outs, fz, bz):
        oc[...] = (fc + bc) * scale
        os_[...] = (fs - bs) * scale
    km_ref[...] = jnp.zeros_like(km_ref)
    for i, ((fc, fs), (bc, bs)) in enumerate(zip(fm, bm)):
        km_ref[2 * i:2 * i + 1, :] = (fc + bc) * (2.0 / n)
        km_ref[2 * i + 1:2 * i + 2, :] = (fs - bs) * (2.0 / n)


def _hyena_filters(seq, w1, b1, w2, b2, w3, b3, freq, decay, tables, n_order, n_dirs, d_hy):
    q = seq // 4
    emb, hid = w1.shape
    bands = (emb - 1) // 2
    t = jnp.arange(seq, dtype=F32)
    t01 = t / max(seq - 1, 1)
    band = jnp.linspace(1e-4, bands - 1, bands, dtype=F32)
    ang = (2.0 * math.pi) * t[:, None] * band[None, :] / seq
    feats = jnp.concatenate([t01[:, None], jnp.cos(ang), jnp.sin(ang)], axis=-1)
    kpad = LANE
    feats = jnp.pad(feats, ((0, 0), (0, kpad - emb)))
    w1p = jnp.pad(w1, ((0, kpad - emb), (0, 0)))
    tabs = [tables["c_fwd"], tables["s_fwd"]]
    ct = min(HY_CH_TILE, d_hy)
    nct = d_hy // ct
    ncol = n_order * d_hy
    b3r = b3.reshape(1, -1)
    decr = decay.reshape(1, -1)
    fcol = lambda o, c: (0, (o * n_dirs + 0) * nct + c)
    bcol = lambda o, c: (0, (o * n_dirs + 1) * nct + c)
    ocol = lambda o, c: (0, o * nct + c)
    full = lambda a: pl.BlockSpec(a.shape, lambda o, c: (0,) * a.ndim)
    spec = jax.ShapeDtypeStruct((q, ncol), F32)
    return pl.pallas_call(
        _hy_filter_kernel,
        out_shape=(spec,) * 8 + (jax.ShapeDtypeStruct((8, ncol), F32),),
        grid=(n_order, nct),
        in_specs=[
            full(feats), full(w1p), full(b1.reshape(1, -1)), full(w2), full(b2.reshape(1, -1)),
            full(freq.reshape(1, -1)),
            pl.BlockSpec((hid, ct), fcol), pl.BlockSpec((1, ct), fcol), pl.BlockSpec((1, ct), fcol),
            pl.BlockSpec((hid, ct), bcol), pl.BlockSpec((1, ct), bcol), pl.BlockSpec((1, ct), bcol),
        ] + [_const_spec(a.shape) for a in tabs],
        out_specs=tuple(pl.BlockSpec((q, ct), ocol) for _ in range(8)) + (pl.BlockSpec((8, ct), ocol),),
        scratch_shapes=[pltpu.VMEM((2 * ct // LANE, seq, LANE), F32), pltpu.VMEM((q, 8 * ct), BF16),
                        pltpu.VMEM((seq, hid), F32)],
        compiler_params=_cparams(2),
        name="hyena_filter_spectra",
    )(feats, w1p, b1.reshape(1, -1), w2, b2.reshape(1, -1), freq.reshape(1, -1),
      w3, b3r, decr, w3, b3r, decr, *tabs)


def _hy_conv_kernel(zin_ref, gin_ref, wz_ref, bz_ref, wg_ref, bg_ref,
                    kac_ref, kas_ref, kbc_ref, kbs_ref, kcc_ref, kcs_ref, kdc_ref, kds_ref, km_ref, bias_ref,
                    cf_ref, sf_ref, ci_ref, si_ref, out_ref,
                    zraw_s, graw_s, o_s, z_s, g_s, rhs_s, ac_s, as_s, *, conv_on_z, seq):
    ct = gin_ref.shape[1]
    nbat = gin_ref.shape[0] // seq
    q = seq // 4
    nsl = ct // LANE
    pad = zraw_s.shape[2] - seq
    top = pad // 2
    rc = min(DFT_ROW_CHUNK, q)
    k_refs = ((kac_ref, kas_ref), (kbc_ref, kbs_ref), (kcc_ref, kcs_ref), (kdc_ref, kds_ref))

    def stage(raw_s, src_ref, k):
        for j in range(nsl):
            raw_s[k, j, 0:top, :] = jnp.zeros((top, LANE), F32)
            raw_s[k, j, top + seq:, :] = jnp.zeros((pad - top, LANE), F32)
            raw_s[k, j, top:top + seq, :] = src_ref[k * seq:(k + 1) * seq, j * LANE:(j + 1) * LANE].astype(F32)

    def split_rows(raw_s, dst_ref, k, w_ref, b_ref):
        for r in range(4):
            for j in range(nsl):
                ls = slice(j * LANE, (j + 1) * LANE)
                tap = lambda d: raw_s[k, j, pl.ds(top + r + d, q, stride=4), :]
                dst_ref[k, :, r * ct + j * LANE:r * ct + (j + 1) * LANE] = (
                    b_ref[:, ls] + tap(-1) * w_ref[0:1, ls] + tap(0) * w_ref[1:2, ls] + tap(1) * w_ref[2:3, ls])

    def spectrum_chunk(k, i):
        r = slice(i * rc, (i + 1) * rc)
        row0 = lax.broadcasted_iota(jnp.int32, (rc, 1), 0) == 0
        x, a = [], []
        for p in range(4):
            rhs = rhs_s[k, :, p * ct:(p + 1) * ct]
            c = jnp.dot(cf_ref[p, r, :], rhs, preferred_element_type=F32)
            sn = jnp.dot(sf_ref[p, r, :], rhs, preferred_element_type=F32)
            if i == 0:
                a.append(sn[0:1, :])
                sn = jnp.where(row0, 0.0, sn)
            else:
                a.append(jnp.zeros((1, ct), F32))
            x.append((c, sn))
        z, zmid = _forward_spectrum(x, a)
        y = [_cmul(zx, (kc[r, :], ks[r, :])) for zx, (kc, ks) in zip(z, k_refs)]
        ymid = [_cmul(zmid[m], (km_ref[2 * m:2 * m + 1, :], km_ref[2 * m + 1:2 * m + 2, :])) for m in range(2)]
        t, sp = _inverse_spectrum(y, ymid)
        for p, ((tc, ts), spp) in enumerate(zip(t, sp)):
            if i == 0:
                ts = jnp.where(row0, spp, ts)
            ac_s[k, r, p * ct:(p + 1) * ct] = tc.astype(BF16)
            as_s[k, r, p * ct:(p + 1) * ct] = ts.astype(BF16)

    def output_chunk(k, i):
        r = slice(i * rc, (i + 1) * rc)
        for p in range(4):
            ps = slice(p * ct, (p + 1) * ct)
            y = jnp.dot(ci_ref[p, r, :], ac_s[k, :, ps], preferred_element_type=F32)
            y = y + jnp.dot(si_ref[p, r, :], as_s[k, :, ps], preferred_element_type=F32)
            rows = pl.ds(4 * i * rc + p, rc, stride=4)
            if conv_on_z:
                out_ref[k * q + i * rc:k * q + (i + 1) * rc, ps] = (
                    g_s[k, r, ps] * (y + z_s[k, r, ps] * bias_ref[...])).astype(out_ref.dtype)
                continue
            for j in range(nsl):
                ls = slice(p * ct + j * LANE, p * ct + (j + 1) * LANE)
                jl = slice(j * LANE, (j + 1) * LANE)
                o_s[k, j, rows, :] = g_s[k, r, ls] * (y[:, jl] + z_s[k, r, ls] * bias_ref[:, jl])

    for k in range(nbat):
        stage(graw_s, gin_ref, k)
        split_rows(graw_s, g_s, k, wg_ref, bg_ref)
        if conv_on_z:
            stage(zraw_s, zin_ref, k)
            split_rows(zraw_s, z_s, k, wz_ref, bz_ref)
            rhs_s[k] = z_s[k].astype(BF16)
        else:
            rhs_s[k] = zin_ref[k * q:(k + 1) * q, :]
            z_s[k] = zin_ref[k * q:(k + 1) * q, :].astype(F32)
    for i in range(q // rc):
        for k in range(nbat):
            spectrum_chunk(k, i)
    for i in range(q // rc):
        for k in range(nbat):
            output_chunk(k, i)
    if not conv_on_z:
        for k in range(nbat):
            for j in range(nsl):
                out_ref[k * seq:(k + 1) * seq, j * LANE:(j + 1) * LANE] = o_s[k, j].astype(out_ref.dtype)


def _hy_conv(zin, zcol0, gcol0, u_hy, conv_w, conv_b, spectra, bias_row, tables,
             *, order, conv_on_z, bsz, seq, d_hy):
    tabs = [tables[k] for k in ("c_fwd", "s_fwd", "c_inv", "s_inv")]
    q = seq // 4
    ct = min(HY_CH_TILE, d_hy)
    nsl = ct // LANE
    nct = d_hy // ct
    nbat = HY_BATCH_GROUP if bsz % HY_BATCH_GROUP == 0 else 1
    zc0 = zcol0 // ct
    gc0 = gcol0 // ct
    zw0 = zc0 if conv_on_z else 0
    kspec = lambda rows: pl.BlockSpec((rows, ct), lambda c, b: (0, order * nct + c),
                                      pipeline_mode=pl.Buffered(1))
    scratch = lambda dt: pltpu.VMEM((nbat, q, 4 * ct), dt)
    slabs = lambda rows: pltpu.VMEM((nbat, nsl, rows, LANE), F32)
    time_block = pl.BlockSpec((nbat * seq, ct), lambda c, b: (b, c))
    split_block = pl.BlockSpec((nbat * q, 4 * ct), lambda c, b: (b, c))
    return pl.pallas_call(
        functools.partial(_hy_conv_kernel, conv_on_z=conv_on_z, seq=seq),
        out_shape=jax.ShapeDtypeStruct((bsz * q, 4 * d_hy) if conv_on_z else (bsz * seq, d_hy), BF16),
        grid=(nct, bsz // nbat),
        in_specs=[
            pl.BlockSpec((nbat * seq, ct), lambda c, b: (b, zc0 + c)) if conv_on_z else split_block,
            pl.BlockSpec((nbat * seq, ct), lambda c, b: (b, gc0 + c)),
            pl.BlockSpec((conv_w.shape[0], ct), lambda c, b: (0, zw0 + c)),
            pl.BlockSpec((1, ct), lambda c, b: (0, zw0 + c)),
            pl.BlockSpec((conv_w.shape[0], ct), lambda c, b: (0, gc0 + c)),
            pl.BlockSpec((1, ct), lambda c, b: (0, gc0 + c)),
        ] + [kspec(q)] * 8 + [kspec(8), kspec(1)] + [_const_spec(a.shape) for a in tabs],
        out_specs=split_block if conv_on_z else time_block,
        scratch_shapes=[
            slabs(seq + 2 * SUBLANE), slabs(seq + 2 * SUBLANE), slabs(seq),
            scratch(F32), scratch(F32), scratch(BF16), scratch(BF16), scratch(BF16),
        ],
        compiler_params=_cparams(2),
        name=f"hyena_conv_order{order}",
    )(zin, u_hy, conv_w, conv_b, conv_w, conv_b, *spectra, bias_row, *tabs)


def _hyena_branch(u_hy, bsz, seq, conv_w, conv_b, w1, b1, w2, b2, w3, b3, freq, decay, bias):
    n_order, d_hy = bias.shape
    n_dirs = w3.shape[1] // (n_order * d_hy)
    if n_order != 2 or seq % 8 != 0 or d_hy % LANE != 0:
        raise NotImplementedError("Hyena branch: two long convolutions, L % 8 == 0, 128-lane channel tiles")
    tables = _hyena_tables(seq)
    spectra = _hyena_filters(seq, w1, b1, w2, b2, w3, b3, freq, decay, tables, n_order, n_dirs, d_hy)
    cb = conv_b.reshape(1, -1)
    bias_row = bias.reshape(1, -1)
    common = dict(bsz=bsz, seq=seq, d_hy=d_hy)
    z1 = _hy_conv(u_hy, 0, d_hy, u_hy, conv_w, cb, spectra, bias_row, tables,
                  order=0, conv_on_z=True, **common)
    return _hy_conv(z1, 0, 2 * d_hy, u_hy, conv_w, cb, spectra, bias_row, tables,
                    order=1, conv_on_z=False, **common)


def _mixer_kernel(x_ref, mod_ref, g1_ref, g2_ref, fg_ref, permt_ref, za_ref, zb_ref,
                  wgate_ref, wa_ref, wb_ref, wout_ref, wg_ref, wu_ref, wd_ref, o_ref, *, n_chunks):
    nb, tt, d = x_ref.shape
    rows = nb * tt
    mod = lambda k: mod_ref[:, k:k + 1, :]
    x = x_ref[...]
    h = _norm_modulate(x, g1_ref[...], mod(0), mod(1)).reshape(rows, d).astype(BF16)
    gate = jax.nn.sigmoid(jnp.dot(h, wgate_ref[...], preferred_element_type=F32))
    za = jnp.dot(permt_ref[...], za_ref[...], preferred_element_type=F32).astype(BF16)
    ya = jnp.dot(za, wa_ref[...], preferred_element_type=F32)
    yb = jnp.dot(zb_ref[...].reshape(rows, zb_ref.shape[-1]), wb_ref[...], preferred_element_type=F32)
    merged = gate[:, :d] * ya + gate[:, d:] * yb
    o = jnp.dot(merged.astype(BF16), wout_ref[...], preferred_element_type=F32)
    x1 = x + mod(2) * o.reshape(nb, tt, d)

    h2 = _norm_modulate(x1, g2_ref[...], mod(3), mod(4)).reshape(rows, d).astype(BF16)
    d_ff = wg_ref.shape[1]
    unit = MXU_TILE if d_ff % MXU_TILE == 0 else d_ff // n_chunks
    edges = [len(g) for g in _spread(list(range(d_ff // unit)), n_chunks)]
    acc = jnp.zeros((rows, d), F32)
    start = 0
    for width in edges:
        sl = slice(start * unit, (start + width) * unit)
        start += width
        gl = jnp.dot(h2, wg_ref[:, sl], preferred_element_type=F32)
        up = jnp.dot(h2, wu_ref[:, sl], preferred_element_type=F32)
        act = (gl * jax.nn.sigmoid(gl) * up).astype(BF16)
        acc = acc + jnp.dot(act, wd_ref[sl, :], preferred_element_type=F32)
    x2 = x1 + mod(5) * acc.reshape(nb, tt, d)
    ms = jnp.mean(x2 * x2, axis=-1, keepdims=True)
    o_ref[...] = x2 * lax.rsqrt(ms + EPS) * fg_ref[...]


def _mixer(x, mod3, norm1_g, norm2_g, final_g, perm_t, za_tm, zb, w_gate, w_a, w_b, w_out, w_g, w_u, w_d):
    bsz, seq, d = x.shape
    d_s5 = w_a.shape[0]
    d_hy = w_b.shape[0]
    d_ff = w_g.shape[1]
    tt = _time_tile(bsz, seq)
    n_chunks = 2 if d_ff % (2 * LANE) == 0 else 1
    consts = (mod3, norm1_g, norm2_g, final_g, perm_t)
    weights = (w_gate, w_a, w_b, w_out, w_g, w_u, w_d)
    return pl.pallas_call(
        functools.partial(_mixer_kernel, n_chunks=n_chunks),
        out_shape=jax.ShapeDtypeStruct(x.shape, F32),
        grid=(seq // tt,),
        in_specs=[pl.BlockSpec((bsz, tt, d), lambda j: (0, j, 0))]
        + [_const_spec(a.shape) for a in consts]
        + [pl.BlockSpec((tt * bsz, d_s5), lambda j: (j, 0)),
           pl.BlockSpec((bsz, tt, d_hy), lambda j: (0, j, 0))]
        + [_const_spec(a.shape) for a in weights],
        out_specs=pl.BlockSpec((bsz, tt, d), lambda j: (0, j, 0)),
        compiler_params=_cparams(1),
        name="merge_swiglu_final_norm",
    )(x, *consts, za_tm, zb, *weights)


def kernel(x, c, ada_w, ada_b, norm1_g, norm2_g, w_in, s5_lam_re, s5_lam_im, s5_log_step, s5_b_re, s5_b_im, s5_c_re, s5_c_im, s5_d, s5_glu_w, s5_glu_b, hy_conv_w, hy_conv_b, hy_ffn_w1, hy_ffn_b1, hy_ffn_w2, hy_ffn_b2, hy_ffn_w3, hy_ffn_b3, hy_freq, hy_decay, hy_bias, w_branch_a, w_branch_b, w_out, ffn_w_gu, ffn_w_down, final_g):
    bsz, seq, d = x.shape
    depth = ada_w.shape[0]
    if depth != 1:
        raise NotImplementedError("the final RMSNorm is fused into the (single) layer's channel mixer")
    d_s5 = s5_d.shape[-1]
    n_order, d_hy = hy_bias.shape[1:]
    d_uh = d_s5 + (n_order + 1) * d_hy
    d_ff = ffn_w_down.shape[1]
    i = 0
    perm = _row_permutation(bsz, _time_tile(bsz, seq))
    mod = _modulation(c, ada_w[i], ada_b[i]).reshape(bsz, 6, d)
    w_in_b = w_in[i].astype(BF16)
    u_s5, u_hy = _in_projection(x, mod, norm1_g[i].reshape(1, d), w_in_b[:, :d_uh], perm, d_s5)
    z_a = _s5_branch(u_s5, bsz, s5_lam_re[i], s5_lam_im[i], s5_log_step[i], s5_b_re[i], s5_b_im[i],
                     s5_c_re[i], s5_c_im[i], s5_d[i], s5_glu_w[i], s5_glu_b[i])
    z_b = _hyena_branch(u_hy.reshape(bsz * seq, -1), bsz, seq, hy_conv_w[i], hy_conv_b[i], hy_ffn_w1[i],
                        hy_ffn_b1[i], hy_ffn_w2[i], hy_ffn_b2[i], hy_ffn_w3[i], hy_ffn_b3[i], hy_freq[i],
                        hy_decay[i], hy_bias[i])
    w_gu = ffn_w_gu[i].astype(BF16)
    return _mixer(x, mod, norm1_g[i].reshape(1, d), norm2_g[i].reshape(1, d), final_g.reshape(1, d),
                  perm.T, z_a, z_b.reshape(bsz, seq, d_hy), w_in_b[:, d_uh:],
                  w_branch_a[i].astype(BF16), w_branch_b[i].astype(BF16), w_out[i].astype(BF16),
                  w_gu[:, :d_ff], w_gu[:, d_ff:], ffn_w_down[i].astype(BF16))
```

```python
import functools
import math

import jax
import jax.numpy as jnp
from jax import lax
from jax.experimental import pallas as pl
from jax.experimental.pallas import tpu as pltpu

F32 = jnp.float32
BF16 = jnp.bfloat16
EPS = 1e-6
HIGHEST = lax.Precision.HIGHEST

V7X_VMEM_BYTES = 64 * 1024 * 1024
VMEM_LIMIT_BYTES = 56 * 1024 * 1024
LANE = 128
SUBLANE = 8
ROW_TILE = 512
S5_TIME_CHUNK = 64
S5_LANE_CHUNK = 512
S5_CH_SPLIT = 256
S5_STEPS_PER_GROUP = 8
MXU_TILE = 256
HY_CH_TILE = 256
HY_BATCH_GROUP = 2
DFT_ROW_CHUNK = 512


def _cparams(n_axes):
    return pltpu.CompilerParams(
        dimension_semantics=("arbitrary",) * n_axes,
        vmem_limit_bytes=VMEM_LIMIT_BYTES,
    )


def _const_spec(shape):
    nd = len(shape)
    return pl.BlockSpec(shape, lambda *_: (0,) * nd, pipeline_mode=pl.Buffered(1))


def _gelu_tanh(x):
    return 0.5 * x * (1.0 + jnp.tanh(math.sqrt(2.0 / math.pi) * (x + 0.044715 * (x * x * x))))


def _norm_modulate(x, g, shift, scale):
    ms = jnp.mean(x * x, axis=-1, keepdims=True)
    r = x * lax.rsqrt(ms + EPS) * g
    return r * (1.0 + scale) + shift


def _split_bf16(x):
    hi = x.astype(BF16)
    return hi, (x - hi.astype(F32)).astype(BF16)


def _mod_kernel(c_ref, w_ref, b_ref, o_ref):
    c = c_ref[...]
    c_hi, c_lo = _split_bf16(c * jax.nn.sigmoid(c))
    w_hi, w_lo = _split_bf16(w_ref[...])
    dot = lambda a, b: jnp.dot(a, b, preferred_element_type=F32)
    o_ref[...] = dot(c_hi, w_hi) + (dot(c_hi, w_lo) + dot(c_lo, w_hi)) + b_ref[...]


def _modulation(c, ada_w, ada_b):
    bsz, d = c.shape
    n = ada_w.shape[1]
    tn = 512
    return pl.pallas_call(
        _mod_kernel,
        out_shape=jax.ShapeDtypeStruct((bsz, n), F32),
        grid=(n // tn,),
        in_specs=[
            pl.BlockSpec((bsz, d), lambda j: (0, 0)),
            pl.BlockSpec((d, tn), lambda j: (0, j)),
            pl.BlockSpec((1, tn), lambda j: (0, j)),
        ],
        out_specs=pl.BlockSpec((bsz, tn), lambda j: (0, j)),
        compiler_params=_cparams(1),
        name="adaln_mod",
    )(c, ada_w, ada_b.reshape(1, n))


def _time_tile(bsz, seq):
    return max(min(ROW_TILE // bsz, seq), 1)


def _row_permutation(bsz, tt):
    r = jnp.arange(bsz * tt, dtype=jnp.int32)
    src = (r % bsz) * tt + r // bsz
    return (src[:, None] == r[None, :]).astype(BF16)


def _inproj_kernel(x_ref, mod_ref, g_ref, w_ref, perm_ref, us5_ref, uhy_ref, *, d_s5):
    nb, tt, d = x_ref.shape
    h = _norm_modulate(x_ref[...], g_ref[...], mod_ref[:, 0:1, :], mod_ref[:, 1:2, :])
    hb = h.reshape(nb * tt, d).astype(BF16)
    p = jnp.dot(hb, w_ref[...], preferred_element_type=F32)
    us5_ref[...] = jnp.dot(perm_ref[...], p[:, :d_s5].astype(BF16),
                           preferred_element_type=F32).astype(us5_ref.dtype)
    uhy_ref[...] = p[:, d_s5:].astype(BF16).reshape(uhy_ref.shape)


def _in_projection(x, mod3, norm_g, w_uh, perm, d_s5):
    bsz, seq, d = x.shape
    n = w_uh.shape[1]
    tt = _time_tile(bsz, seq)
    return pl.pallas_call(
        functools.partial(_inproj_kernel, d_s5=d_s5),
        out_shape=(
            jax.ShapeDtypeStruct((seq * bsz, d_s5), BF16),
            jax.ShapeDtypeStruct((bsz, seq, n - d_s5), BF16),
        ),
        grid=(seq // tt,),
        in_specs=[
            pl.BlockSpec((bsz, tt, d), lambda j: (0, j, 0)),
            _const_spec(mod3.shape),
            _const_spec((1, d)),
            _const_spec((d, n)),
            _const_spec(perm.shape),
        ],
        out_specs=(
            pl.BlockSpec((tt * bsz, d_s5), lambda j: (j, 0)),
            pl.BlockSpec((bsz, tt, n - d_s5), lambda j: (0, j, 0)),
        ),
        compiler_params=_cparams(1),
        name="in_proj",
    )(x, mod3, norm_g, w_uh, perm)


def _s5_prep_kernel(lre_ref, lim_ref, lstep_ref, bre_ref, bim_ref, cim_ref,
                    are_ref, aim_ref, bbre_ref, bbim_ref, ncim_ref):
    step = jnp.exp(lstep_ref[...])
    lr = lre_ref[...]
    li = lim_ref[...]
    mag = jnp.exp(lr * step)
    ar = mag * jnp.cos(li * step)
    ai = mag * jnp.sin(li * step)
    num = ar - 1.0
    den = lr * lr + li * li
    cr = (num * lr + ai * li) / den
    ci = (ai * lr - num * li) / den
    are_ref[...] = ar
    aim_ref[...] = ai
    for d in range(lre_ref.shape[0]):
        br = bre_ref[d]
        bi = bim_ref[d]
        bbre_ref[d] = cr[d:d + 1, :] * br - ci[d:d + 1, :] * bi
        bbim_ref[d] = cr[d:d + 1, :] * bi + ci[d:d + 1, :] * br
    ncim_ref[...] = -cim_ref[...]


def _s5_prepare(lam_re, lam_im, log_step, b_re, b_im, c_im):
    outs = (
        jax.ShapeDtypeStruct(lam_re.shape, F32),
        jax.ShapeDtypeStruct(lam_re.shape, F32),
        jax.ShapeDtypeStruct(b_re.shape, F32),
        jax.ShapeDtypeStruct(b_re.shape, F32),
        jax.ShapeDtypeStruct(c_im.shape, F32),
    )
    return pl.pallas_call(
        _s5_prep_kernel,
        out_shape=outs,
        compiler_params=pltpu.CompilerParams(vmem_limit_bytes=VMEM_LIMIT_BYTES),
        name="s5_discretize",
    )(lam_re, lam_im, log_step, b_re, b_im, c_im)


def _spread(items, n):
    return [items[(len(items) * i) // n:(len(items) * (i + 1)) // n] for i in range(n)]


def _s5_scan_kernel(*refs, reverse, final, tc, nb):
    if final:
        (u_ref, bd_ref, cd_ref, are_ref, aim_ref, yprev_ref, gluw_ref, glub_ref,
         out_ref, bu_ref, sb_ref, y_ref, s_ref) = refs
    else:
        (u_ref, bd_ref, cd_ref, are_ref, aim_ref, d_ref, out_ref, bu_ref, sb_ref, y_ref, s_ref) = refs
    rows = tc * nb
    _, nsplit, ks, two_nss = bd_ref.shape
    nss = two_nss // 2
    lc = min(S5_LANE_CHUNK, nss)
    tw = min(MXU_TILE, lc)
    rh = rows // 2 if rows % (2 * SUBLANE * 2) == 0 else rows
    row_halves = [slice(r, r + rh) for r in range(0, rows, rh)]

    @pl.when(pl.program_id(0) == 0)
    def _():
        s_ref[...] = jnp.zeros_like(s_ref)

    units = [(h, q) for h in range(nsplit) for q in range(nss // lc)]

    def lanes(unit, part, off=0, width=None):
        h, q = unit
        start = h * two_nss + part * nss + q * lc + off
        return slice(start, start + (lc if width is None else width))

    def expand_pieces(unit):
        h, q = unit
        out = []
        for part in range(2):
            for off in range(0, lc, tw):
                for rs in row_halves:
                    def piece(part=part, off=off, rs=rs):
                        col = part * nss + q * lc + off
                        bu_ref[rs, lanes(unit, part, off, tw)] = jnp.dot(
                            u_ref[rs, h * ks:(h + 1) * ks], bd_ref[0, h, :, col:col + tw],
                            preferred_element_type=F32)
                    out.append(piece)
        return out

    def readout_pieces(unit):
        h, q = unit
        out = []
        for rs in row_halves:
            def piece(rs=rs):
                v = None
                for part in range(2):
                    k0 = part * nss + q * lc
                    d = jnp.dot(sb_ref[rs, lanes(unit, part)], cd_ref[0, h, k0:k0 + lc, :],
                                preferred_element_type=F32)
                    v = d if v is None else v + d
                if q == 0:
                    y_ref[rs, h * ks:(h + 1) * ks] = v
                else:
                    y_ref[rs, h * ks:(h + 1) * ks] += v
            out.append(piece)
        return out

    step_groups = _spread(list(range(tc)), max(tc // S5_STEPS_PER_GROUP, 1))
    for p in expand_pieces(units[0]):
        p()
    for c, unit in enumerate(units):
        mxu_work = (expand_pieces(units[c + 1]) if c + 1 < len(units) else []) + \
                   (readout_pieces(units[c - 1]) if c > 0 else [])
        mxu_groups = _spread(mxu_work, len(step_groups))
        a_sl = slice(unit[0] * nss + unit[1] * lc, unit[0] * nss + (unit[1] + 1) * lc)
        a_re = jnp.broadcast_to(are_ref[0, :, a_sl], (nb, lc))
        a_im = jnp.broadcast_to(aim_ref[0, :, a_sl], (nb, lc))
        re_sl, im_sl = lanes(unit, 0), lanes(unit, 1)
        sr = s_ref[:, re_sl]
        si = s_ref[:, im_sl]
        for steps, pieces in zip(step_groups, mxu_groups):
            for p in pieces:
                p()
            for k in steps:
                t = (tc - 1 - k) if reverse else k
                row = slice(t * nb, (t + 1) * nb)
                nr = a_re * sr - a_im * si + bu_ref[row, re_sl]
                ni = a_re * si + a_im * sr + bu_ref[row, im_sl]
                sb_ref[row, re_sl] = nr.astype(BF16)
                sb_ref[row, im_sl] = ni.astype(BF16)
                sr, si = nr, ni
        s_ref[:, re_sl] = sr
        s_ref[:, im_sl] = si
    for p in readout_pieces(units[-1]):
        p()

    rb = min(256, rows)
    for r in range(rows // rb):
        rs = slice(r * rb, (r + 1) * rb)
        if final:
            z = _gelu_tanh(yprev_ref[rs, :] + y_ref[rs, :])
            gate = jnp.dot(z.astype(BF16), gluw_ref[...], preferred_element_type=F32) + glub_ref[...]
            out_ref[rs, :] = (z * jax.nn.sigmoid(gate)).astype(out_ref.dtype)
        else:
            out_ref[rs, :] = y_ref[rs, :] + u_ref[rs, :] * d_ref[...]


def _s5_scan(u_tm, bd, cd, a_re, a_im, extra, *, direction, final, nb, out_dtype):
    rows_total, d_s5 = u_tm.shape
    seq = rows_total // nb
    two_ns = bd.shape[1] * bd.shape[3]
    tc = min(S5_TIME_CHUNK, seq)
    nchunk = seq // tc
    rows = tc * nb
    reverse = direction == 1
    if reverse:
        cidx = lambda i: (nchunk - 1 - i, 0)
    else:
        cidx = lambda i: (i, 0)

    def dir_spec(a):
        nd = a.ndim
        return pl.BlockSpec((1,) + a.shape[1:], lambda i: (direction,) + (0,) * (nd - 1),
                            pipeline_mode=pl.Buffered(1))

    in_specs = [pl.BlockSpec((rows, d_s5), cidx), dir_spec(bd), dir_spec(cd), dir_spec(a_re), dir_spec(a_im)]
    if final:
        yprev, glu_w, glu_b = extra
        in_specs += [pl.BlockSpec((rows, d_s5), cidx), _const_spec(glu_w.shape), _const_spec(glu_b.shape)]
    else:
        in_specs += [_const_spec(extra[0].shape)]
    return pl.pallas_call(
        functools.partial(_s5_scan_kernel, reverse=reverse, final=final, tc=tc, nb=nb),
        out_shape=jax.ShapeDtypeStruct((rows_total, d_s5), out_dtype),
        grid=(nchunk,),
        in_specs=in_specs,
        out_specs=pl.BlockSpec((rows, d_s5), cidx),
        scratch_shapes=[pltpu.VMEM((rows, two_ns), F32), pltpu.VMEM((rows, two_ns), BF16),
                        pltpu.VMEM((rows, d_s5), F32), pltpu.VMEM((nb, two_ns), F32)],
        compiler_params=_cparams(1),
        name="s5_scan_bwd_glu" if final else "s5_scan_fwd",
    )(u_tm, bd, cd, a_re, a_im, *extra)


def _s5_branch(u_tm, nb, lam_re, lam_im, log_step, b_re, b_im, c_re, c_im, d, glu_w, glu_b):
    ndir, g, p = lam_re.shape
    if ndir != 2:
        raise NotImplementedError("S5 branch expects forward and backward directions")
    grp = b_re.shape[-1]
    ns = g * p
    d_s5 = g * grp
    ks = S5_CH_SPLIT if d_s5 % S5_CH_SPLIT == 0 else d_s5
    nsplit = d_s5 // ks
    gs = g // nsplit
    flat = lambda a: a.reshape(ndir, ns)
    to_lanes = lambda a: jnp.transpose(a, (0, 3, 1, 2)).reshape(ndir, grp, ns)
    a_re, a_im, bb_re, bb_im, ncim = _s5_prepare(
        flat(lam_re), flat(lam_im), jnp.repeat(log_step, p, axis=-1),
        to_lanes(b_re), to_lanes(b_im),
        jnp.transpose(c_im, (0, 2, 1, 3)).reshape(ndir, grp, ns))

    eye = jnp.eye(gs, dtype=F32)
    bb = jnp.stack([bb_re, bb_im], axis=1).reshape(ndir, 2, grp, nsplit, gs, p)
    bb = jnp.transpose(bb, (0, 3, 4, 2, 1, 5))
    bd = bb[:, :, :, :, :, None, :] * eye[None, None, :, None, None, :, None]
    bd = bd.reshape(ndir, nsplit, ks, 2 * gs * p).astype(BF16)
    ncim_g = jnp.transpose(ncim.reshape(ndir, grp, g, p), (0, 2, 1, 3))
    cc = jnp.stack([c_re, ncim_g], axis=1).reshape(ndir, 2, nsplit, gs, grp, p)
    cc = jnp.transpose(cc, (0, 2, 1, 3, 5, 4))
    cd = cc[:, :, :, :, :, None, :] * eye[None, None, None, :, None, :, None]
    cd = cd.reshape(ndir, nsplit, 2 * gs * p, ks).astype(BF16)
    a_re = a_re.reshape(ndir, 1, ns)
    a_im = a_im.reshape(ndir, 1, ns)
    y = _s5_scan(u_tm, bd, cd, a_re, a_im, (d.reshape(1, -1),), direction=0, final=False, nb=nb, out_dtype=F32)
    extra = (y, glu_w.astype(BF16), glu_b.reshape(1, -1))
    return _s5_scan(u_tm, bd, cd, a_re, a_im, extra, direction=1, final=True, nb=nb, out_dtype=BF16)


RSQRT2 = math.sqrt(0.5)


def _dft_table(nf, t, n):
    f_lo = 64 if nf % 64 == 0 else 1
    f_hi = nf // f_lo
    t = t[:, None, :]
    ka = (jnp.arange(f_hi, dtype=jnp.int32)[None, :, None] * f_lo * t) % n
    kb = (jnp.arange(f_lo, dtype=jnp.int32)[None, :, None] * t) % n
    w = 2.0 * math.pi / n
    aa = ka.astype(F32) * w
    ab = kb.astype(F32) * w
    ca, sa, cb, sb = jnp.cos(aa), jnp.sin(aa), jnp.cos(ab), jnp.sin(ab)
    cm = ca[:, :, None, :] * cb[:, None, :, :] - sa[:, :, None, :] * sb[:, None, :, :]
    sm = sa[:, :, None, :] * cb[:, None, :, :] + ca[:, :, None, :] * sb[:, None, :, :]
    shape = (t.shape[0], nf, t.shape[-1])
    return cm.reshape(shape), sm.reshape(shape)


def _hyena_tables(seq):
    q = seq // 4
    times = 4 * jnp.arange(q, dtype=jnp.int32)[None, :] + jnp.arange(4, dtype=jnp.int32)[:, None]
    cf, sf = _dft_table(q, times, 2 * seq)
    alt = (1 - 2 * (jnp.arange(q, dtype=jnp.int32) & 1)).astype(F32)
    f_is_0 = jnp.arange(q, dtype=jnp.int32)[None, :, None] == 0
    sf = jnp.where(f_is_0, alt[None, None, :], sf)
    return dict(c_fwd=cf.astype(BF16), s_fwd=sf.astype(BF16),
                c_inv=jnp.transpose(cf, (0, 2, 1)).astype(BF16), s_inv=jnp.transpose(sf, (0, 2, 1)).astype(BF16))


def _cmul(z, k):
    (zc, zs), (kc, ks) = z, k
    return zc * kc - zs * ks, zc * ks + zs * kc


def _forward_spectrum(x, a):
    (c0, s0), (c1, s1), (c2, s2), (c3, s3) = x
    pc, ps, mc, ms = c0 + c2, s0 + s2, c0 - c2, s0 - s2
    qc, qs, nc, ns = c1 + c3, s1 + s3, c1 - c3, s1 - s3
    za = (pc + qc, ps + qs)
    zb = (pc - qc, qs - ps)
    zc = (mc + ns, nc - ms)
    zd = (mc - ns, ms + nc)
    e = (a[1] - a[3]) * RSQRT2
    o = (a[1] + a[3]) * RSQRT2
    return [za, zb, zc, zd], [(a[0] + e, a[2] + o), (a[0] - e, o - a[2])]


def _inverse_spectrum(y, ymid):
    (ac, as_), (bc, bs), (cc, cs), (dc, ds) = y
    upc, ups, umc, ums = ac + bc, as_ - bs, ac - bc, as_ + bs
    vpc, vps, vmc, vms = dc + cc, ds - cs, dc - cc, ds + cs
    t = [(upc + vpc, ups + vps), (umc + vms, ums - vmc), (upc - vpc, ups - vps), (umc - vms, ums + vmc)]
    (y1c, y1s), (y3c, y3s) = ymid
    sp = [y1c + y3c, ((y1c + y1s) + (y3s - y3c)) * RSQRT2, y1s - y3s, ((y1s - y1c) + (y3c + y3s)) * RSQRT2]
    return t, sp


def _hy_filter_kernel(feat_ref, w1_ref, b1_ref, w2_ref, b2_ref, freq_ref,
                      w3f_ref, b3f_ref, decf_ref, w3b_ref, b3b_ref, decb_ref,
                      cf_ref, sf_ref,
                      kac_ref, kas_ref, kbc_ref, kbs_ref, kcc_ref, kcs_ref, kdc_ref, kds_ref, km_ref,
                      taps_s, rhs_s, h_s):
    seq = feat_ref.shape[0]
    q = seq // 4
    n = 2 * seq
    ct = kac_ref.shape[1]
    nsl = ct // LANE

    @pl.when((pl.program_id(0) == 0) & (pl.program_id(1) == 0))
    def _():
        f = freq_ref[...]
        h1 = jnp.sin(f * (jnp.dot(feat_ref[...], w1_ref[...], precision=HIGHEST,
                                  preferred_element_type=F32) + b1_ref[...]))
        h_s[...] = jnp.sin(f * (jnp.dot(h1, w2_ref[...], precision=HIGHEST,
                                        preferred_element_type=F32) + b2_ref[...]))

    h = h_s[...]
    t01 = feat_ref[:, 0:1]
    row = lax.broadcasted_iota(jnp.int32, (seq, 1), 0)

    def taps(w3_ref, b3_ref, dec_ref):
        v = jnp.dot(h, w3_ref[...], precision=HIGHEST, preferred_element_type=F32) + b3_ref[...]
        return v * jnp.exp(-t01 * jnp.abs(dec_ref[...]))

    fwd = taps(w3f_ref, b3f_ref, decf_ref)
    bwd = jnp.where(row == 0, 0.0, taps(w3b_ref, b3b_ref, decb_ref))
    for d, x in enumerate((fwd, bwd)):
        for j in range(nsl):
            taps_s[d * nsl + j] = x[:, j * LANE:(j + 1) * LANE]
        for r in range(4):
            for j in range(nsl):
                c0 = (4 * d + r) * ct + j * LANE
                rhs_s[:, c0:c0 + LANE] = taps_s[d * nsl + j, pl.ds(r, q, stride=4), :].astype(BF16)
    row0 = lax.broadcasted_iota(jnp.int32, (q, 1), 0) == 0

    def spectrum(d):
        x, a = [], []
        for r in range(4):
            rhs = rhs_s[:, (4 * d + r) * ct:(4 * d + r + 1) * ct]
            c = jnp.dot(cf_ref[r], rhs, preferred_element_type=F32)
            sn = jnp.dot(sf_ref[r], rhs, preferred_element_type=F32)
            a.append(sn[0:1, :])
            x.append((c, jnp.where(row0, 0.0, sn)))
        return _forward_spectrum(x, a)

    fz, fm = spectrum(0)
    bz, bm = spectrum(1)
    scale = jnp.where(row0, 1.0 / n, 2.0 / n)
    outs = ((kac_ref, kas_ref), (kbc_ref, kbs_ref), (kcc_ref, kcs_ref), (kdc_ref, kds_ref))
    for (oc, os_), (fc, fs), (bc, bs) in zip(outs, fz, bz):
        oc[...] = (fc + bc) * scale
        os_[...] = (fs - bs) * scale
    km_ref[...] = jnp.zeros_like(km_ref)
    for i, ((fc, fs), (bc, bs)) in enumerate(zip(fm, bm)):
        km_ref[2 * i:2 * i + 1, :] = (fc + bc) * (2.0 / n)
        km_ref[2 * i + 1:2 * i + 2, :] = (fs - bs) * (2.0 / n)


def _hyena_filters(seq, w1, b1, w2, b2, w3, b3, freq, decay, tables, n_order, n_dirs, d_hy):
    q = seq // 4
    emb, hid = w1.shape
    bands = (emb - 1) // 2
    t = jnp.arange(seq, dtype=F32)
    t01 = t / max(seq - 1, 1)
    band = jnp.linspace(1e-4, bands - 1, bands, dtype=F32)
    ang = (2.0 * math.pi) * t[:, None] * band[None, :] / seq
    feats = jnp.concatenate([t01[:, None], jnp.cos(ang), jnp.sin(ang)], axis=-1)
    kpad = LANE
    feats = jnp.pad(feats, ((0, 0), (0, kpad - emb)))
    w1p = jnp.pad(w1, ((0, kpad - emb), (0, 0)))
    tabs = [tables["c_fwd"], tables["s_fwd"]]
    ct = min(HY_CH_TILE, d_hy)
    nct = d_hy // ct
    ncol = n_order * d_hy
    b3r = b3.reshape(1, -1)
    decr = decay.reshape(1, -1)
    fcol = lambda o, c: (0, (o * n_dirs + 0) * nct + c)
    bcol = lambda o, c: (0, (o * n_dirs + 1) * nct + c)
    ocol = lambda o, c: (0, o * nct + c)
    full = lambda a: pl.BlockSpec(a.shape, lambda o, c: (0,) * a.ndim)
    spec = jax.ShapeDtypeStruct((q, ncol), F32)
    return pl.pallas_call(
        _hy_filter_kernel,
        out_shape=(spec,) * 8 + (jax.ShapeDtypeStruct((8, ncol), F32),),
        grid=(n_order, nct),
        in_specs=[
            full(feats), full(w1p), full(b1.reshape(1, -1)), full(w2), full(b2.reshape(1, -1)),
            full(freq.reshape(1, -1)),
            pl.BlockSpec((hid, ct), fcol), pl.BlockSpec((1, ct), fcol), pl.BlockSpec((1, ct), fcol),
            pl.BlockSpec((hid, ct), bcol), pl.BlockSpec((1, ct), bcol), pl.BlockSpec((1, ct), bcol),
        ] + [_const_spec(a.shape) for a in tabs],
        out_specs=tuple(pl.BlockSpec((q, ct), ocol) for _ in range(8)) + (pl.BlockSpec((8, ct), ocol),),
        scratch_shapes=[pltpu.VMEM((2 * ct // LANE, seq, LANE), F32), pltpu.VMEM((q, 8 * ct), BF16),
                        pltpu.VMEM((seq, hid), F32)],
        compiler_params=_cparams(2),
        name="hyena_filter_spectra",
    )(feats, w1p, b1.reshape(1, -1), w2, b2.reshape(1, -1), freq.reshape(1, -1),
      w3, b3r, decr, w3, b3r, decr, *tabs)


def _hy_conv_kernel(zin_ref, gin_ref, wz_ref, bz_ref, wg_ref, bg_ref,
                    kac_ref, kas_ref, kbc_ref, kbs_ref, kcc_ref, kcs_ref, kdc_ref, kds_ref, km_ref, bias_ref,
                    cf_ref, sf_ref, ci_ref, si_ref, out_ref,
                    zraw_s, graw_s, o_s, z_s, g_s, rhs_s, ac_s, as_s, *, conv_on_z, seq):
    ct = gin_ref.shape[1]
    nbat = gin_ref.shape[0] // seq
    q = seq // 4
    nsl = ct // LANE
    pad = zraw_s.shape[2] - seq
    top = pad // 2
    rc = min(DFT_ROW_CHUNK, q)
    k_refs = ((kac_ref, kas_ref), (kbc_ref, kbs_ref), (kcc_ref, kcs_ref), (kdc_ref, kds_ref))

    def stage(raw_s, src_ref, k):
        for j in range(nsl):
            raw_s[k, j, 0:top, :] = jnp.zeros((top, LANE), F32)
            raw_s[k, j, top + seq:, :] = jnp.zeros((pad - top, LANE), F32)
            raw_s[k, j, top:top + seq, :] = src_ref[k * seq:(k + 1) * seq, j * LANE:(j + 1) * LANE].astype(F32)

    def split_rows(raw_s, dst_ref, k, w_ref, b_ref):
        for r in range(4):
            for j in range(nsl):
                ls = slice(j * LANE, (j + 1) * LANE)
                tap = lambda d: raw_s[k, j, pl.ds(top + r + d, q, stride=4), :]
                dst_ref[k, :, r * ct + j * LANE:r * ct + (j + 1) * LANE] = (
                    b_ref[:, ls] + tap(-1) * w_ref[0:1, ls] + tap(0) * w_ref[1:2, ls] + tap(1) * w_ref[2:3, ls])

    def spectrum_chunk(k, i):
        r = slice(i * rc, (i + 1) * rc)
        row0 = lax.broadcasted_iota(jnp.int32, (rc, 1), 0) == 0
        x, a = [], []
        for p in range(4):
            rhs = rhs_s[k, :, p * ct:(p + 1) * ct]
            c = jnp.dot(cf_ref[p, r, :], rhs, preferred_element_type=F32)
            sn = jnp.dot(sf_ref[p, r, :], rhs, preferred_element_type=F32)
            if i == 0:
                a.append(sn[0:1, :])
                sn = jnp.where(row0, 0.0, sn)
            else:
                a.append(jnp.zeros((1, ct), F32))
            x.append((c, sn))
        z, zmid = _forward_spectrum(x, a)
        y = [_cmul(zx, (kc[r, :], ks[r, :])) for zx, (kc, ks) in zip(z, k_refs)]
        ymid = [_cmul(zmid[m], (km_ref[2 * m:2 * m + 1, :], km_ref[2 * m + 1:2 * m + 2, :])) for m in range(2)]
        t, sp = _inverse_spectrum(y, ymid)
        for p, ((tc, ts), spp) in enumerate(zip(t, sp)):
            if i == 0:
                ts = jnp.where(row0, spp, ts)
            ac_s[k, r, p * ct:(p + 1) * ct] = tc.astype(BF16)
            as_s[k, r, p * ct:(p + 1) * ct] = ts.astype(BF16)

    def output_chunk(k, i):
        r = slice(i * rc, (i + 1) * rc)
        for p in range(4):
            ps = slice(p * ct, (p + 1) * ct)
            y = jnp.dot(ci_ref[p, r, :], ac_s[k, :, ps], preferred_element_type=F32)
            y = y + jnp.dot(si_ref[p, r, :], as_s[k, :, ps], preferred_element_type=F32)
            rows = pl.ds(4 * i * rc + p, rc, stride=4)
            if conv_on_z:
                out_ref[k * q + i * rc:k * q + (i + 1) * rc, ps] = (
                    g_s[k, r, ps] * (y + z_s[k, r, ps] * bias_ref[...])).astype(out_ref.dtype)
                continue
            for j in range(nsl):
                ls = slice(p * ct + j * LANE, p * ct + (j + 1) * LANE)
                jl = slice(j * LANE, (j + 1) * LANE)
                o_s[k, j, rows, :] = g_s[k, r, ls] * (y[:, jl] + z_s[k, r, ls] * bias_ref[:, jl])

    for k in range(nbat):
        stage(graw_s, gin_ref, k)
        split_rows(graw_s, g_s, k, wg_ref, bg_ref)
        if conv_on_z:
            stage(zraw_s, zin_ref, k)
            split_rows(zraw_s, z_s, k, wz_ref, bz_ref)
            rhs_s[k] = z_s[k].astype(BF16)
        else:
            rhs_s[k] = zin_ref[k * q:(k + 1) * q, :]
            z_s[k] = zin_ref[k * q:(k + 1) * q, :].astype(F32)
    for i in range(q // rc):
        for k in range(nbat):
            spectrum_chunk(k, i)
    for i in range(q // rc):
        for k in range(nbat):
            output_chunk(k, i)
    if not conv_on_z:
        for k in range(nbat):
            for j in range(nsl):
                out_ref[k * seq:(k + 1) * seq, j * LANE:(j + 1) * LANE] = o_s[k, j].astype(out_ref.dtype)


def _hy_conv(zin, zcol0, gcol0, u_hy, conv_w, conv_b, spectra, bias_row, tables,
             *, order, conv_on_z, bsz, seq, d_hy):
    tabs = [tables[k] for k in ("c_fwd", "s_fwd", "c_inv", "s_inv")]
    q = seq // 4
    ct = min(HY_CH_TILE, d_hy)
    nsl = ct // LANE
    nct = d_hy // ct
    nbat = HY_BATCH_GROUP if bsz % HY_BATCH_GROUP == 0 else 1
    zc0 = zcol0 // ct
    gc0 = gcol0 // ct
    zw0 = zc0 if conv_on_z else 0
    kspec = lambda rows: pl.BlockSpec((rows, ct), lambda c, b: (0, order * nct + c),
                                      pipeline_mode=pl.Buffered(1))
    scratch = lambda dt: pltpu.VMEM((nbat, q, 4 * ct), dt)
    slabs = lambda rows: pltpu.VMEM((nbat, nsl, rows, LANE), F32)
    time_block = pl.BlockSpec((nbat * seq, ct), lambda c, b: (b, c))
    split_block = pl.BlockSpec((nbat * q, 4 * ct), lambda c, b: (b, c))
    return pl.pallas_call(
        functools.partial(_hy_conv_kernel, conv_on_z=conv_on_z, seq=seq),
        out_shape=jax.ShapeDtypeStruct((bsz * q, 4 * d_hy) if conv_on_z else (bsz * seq, d_hy), BF16),
        grid=(nct, bsz // nbat),
        in_specs=[
            pl.BlockSpec((nbat * seq, ct), lambda c, b: (b, zc0 + c)) if conv_on_z else split_block,
            pl.BlockSpec((nbat * seq, ct), lambda c, b: (b, gc0 + c)),
            pl.BlockSpec((conv_w.shape[0], ct), lambda c, b: (0, zw0 + c)),
            pl.BlockSpec((1, ct), lambda c, b: (0, zw0 + c)),
            pl.BlockSpec((conv_w.shape[0], ct), lambda c, b: (0, gc0 + c)),
            pl.BlockSpec((1, ct), lambda c, b: (0, gc0 + c)),
        ] + [kspec(q)] * 8 + [kspec(8), kspec(1)] + [_const_spec(a.shape) for a in tabs],
        out_specs=split_block if conv_on_z else time_block,
        scratch_shapes=[
            slabs(seq + 2 * SUBLANE), slabs(seq + 2 * SUBLANE), slabs(seq),
            scratch(F32), scratch(F32), scratch(BF16), scratch(BF16), scratch(BF16),
        ],
        compiler_params=_cparams(2),
        name=f"hyena_conv_order{order}",
    )(zin, u_hy, conv_w, conv_b, conv_w, conv_b, *spectra, bias_row, *tabs)


def _hyena_branch(u_hy, bsz, seq, conv_w, conv_b, w1, b1, w2, b2, w3, b3, freq, decay, bias):
    n_order, d_hy = bias.shape
    n_dirs = w3.shape[1] // (n_order * d_hy)
    if n_order != 2 or seq % 8 != 0 or d_hy % LANE != 0:
        raise NotImplementedError("Hyena branch: two long convolutions, L % 8 == 0, 128-lane channel tiles")
    tables = _hyena_tables(seq)
    spectra = _hyena_filters(seq, w1, b1, w2, b2, w3, b3, freq, decay, tables, n_order, n_dirs, d_hy)
    cb = conv_b.reshape(1, -1)
    bias_row = bias.reshape(1, -1)
    common = dict(bsz=bsz, seq=seq, d_hy=d_hy)
    z1 = _hy_conv(u_hy, 0, d_hy, u_hy, conv_w, cb, spectra, bias_row, tables,
                  order=0, conv_on_z=True, **common)
    return _hy_conv(z1, 0, 2 * d_hy, u_hy, conv_w, cb, spectra, bias_row, tables,
                    order=1, conv_on_z=False, **common)


def _mixer_kernel(x_ref, mod_ref, g1_ref, g2_ref, fg_ref, permt_ref, za_ref, zb_ref,
                  wgate_ref, wa_ref, wb_ref, wout_ref, wg_ref, wu_ref, wd_ref, o_ref, *, n_chunks):
    nb, tt, d = x_ref.shape
    rows = nb * tt
    mod = lambda k: mod_ref[:, k:k + 1, :]
    x = x_ref[...]
    h = _norm_modulate(x, g1_ref[...], mod(0), mod(1)).reshape(rows, d).astype(BF16)
    gate = jax.nn.sigmoid(jnp.dot(h, wgate_ref[...], preferred_element_type=F32))
    za = jnp.dot(permt_ref[...], za_ref[...], preferred_element_type=F32).astype(BF16)
    ya = jnp.dot(za, wa_ref[...], preferred_element_type=F32)
    yb = jnp.dot(zb_ref[...].reshape(rows, zb_ref.shape[-1]), wb_ref[...], preferred_element_type=F32)
    merged = gate[:, :d] * ya + gate[:, d:] * yb
    o = jnp.dot(merged.astype(BF16), wout_ref[...], preferred_element_type=F32)
    x1 = x + mod(2) * o.reshape(nb, tt, d)

    h2 = _norm_modulate(x1, g2_ref[...], mod(3), mod(4)).reshape(rows, d).astype(BF16)
    d_ff = wg_ref.shape[1]
    unit = MXU_TILE if d_ff % MXU_TILE == 0 else d_ff // n_chunks
    edges = [len(g) for g in _spread(list(range(d_ff // unit)), n_chunks)]
    acc = jnp.zeros((rows, d), F32)
    start = 0
    for width in edges:
        sl = slice(start * unit, (start + width) * unit)
        start += width
        gl = jnp.dot(h2, wg_ref[:, sl], preferred_element_type=F32)
        up = jnp.dot(h2, wu_ref[:, sl], preferred_element_type=F32)
        act = (gl * jax.nn.sigmoid(gl) * up).astype(BF16)
        acc = acc + jnp.dot(act, wd_ref[sl, :], preferred_element_type=F32)
    x2 = x1 + mod(5) * acc.reshape(nb, tt, d)
    ms = jnp.mean(x2 * x2, axis=-1, keepdims=True)
    o_ref[...] = x2 * lax.rsqrt(ms + EPS) * fg_ref[...]


def _mixer(x, mod3, norm1_g, norm2_g, final_g, perm_t, za_tm, zb, w_gate, w_a, w_b, w_out, w_g, w_u, w_d):
    bsz, seq, d = x.shape
    d_s5 = w_a.shape[0]
    d_hy = w_b.shape[0]
    d_ff = w_g.shape[1]
    tt = _time_tile(bsz, seq)
    n_chunks = 2 if d_ff % (2 * LANE) == 0 else 1
    consts = (mod3, norm1_g, norm2_g, final_g, perm_t)
    weights = (w_gate, w_a, w_b, w_out, w_g, w_u, w_d)
    return pl.pallas_call(
        functools.partial(_mixer_kernel, n_chunks=n_chunks),
        out_shape=jax.ShapeDtypeStruct(x.shape, F32),
        grid=(seq // tt,),
        in_specs=[pl.BlockSpec((bsz, tt, d), lambda j: (0, j, 0))]
        + [_const_spec(a.shape) for a in consts]
        + [pl.BlockSpec((tt * bsz, d_s5), lambda j: (j, 0)),
           pl.BlockSpec((bsz, tt, d_hy), lambda j: (0, j, 0))]
        + [_const_spec(a.shape) for a in weights],
        out_specs=pl.BlockSpec((bsz, tt, d), lambda j: (0, j, 0)),
        compiler_params=_cparams(1),
        name="merge_swiglu_final_norm",
    )(x, *consts, za_tm, zb, *weights)


def kernel(x, c, ada_w, ada_b, norm1_g, norm2_g, w_in, s5_lam_re, s5_lam_im, s5_log_step, s5_b_re, s5_b_im, s5_c_re, s5_c_im, s5_d, s5_glu_w, s5_glu_b, hy_conv_w, hy_conv_b, hy_ffn_w1, hy_ffn_b1, hy_ffn_w2, hy_ffn_b2, hy_ffn_w3, hy_ffn_b3, hy_freq, hy_decay, hy_bias, w_branch_a, w_branch_b, w_out, ffn_w_gu, ffn_w_down, final_g):
    bsz, seq, d = x.shape
    depth = ada_w.shape[0]
    if depth != 1:
        raise NotImplementedError("the final RMSNorm is fused into the (single) layer's channel mixer")
    d_s5 = s5_d.shape[-1]
    n_order, d_hy = hy_bias.shape[1:]
    d_uh = d_s5 + (n_order + 1) * d_hy
    d_ff = ffn_w_down.shape[1]
    i = 0
    perm = _row_permutation(bsz, _time_tile(bsz, seq))
    mod = _modulation(c, ada_w[i], ada_b[i]).reshape(bsz, 6, d)
    w_in_b = w_in[i].astype(BF16)
    u_s5, u_hy = _in_projection(x, mod, norm1_g[i].reshape(1, d), w_in_b[:, :d_uh], perm, d_s5)
    z_a = _s5_branch(u_s5, bsz, s5_lam_re[i], s5_lam_im[i], s5_log_step[i], s5_b_re[i], s5_b_im[i],
                     s5_c_re[i], s5_c_im[i], s5_d[i], s5_glu_w[i], s5_glu_b[i])
    z_b = _hyena_branch(u_hy.reshape(bsz * seq, -1), bsz, seq, hy_conv_w[i], hy_conv_b[i], hy_ffn_w1[i],
                        hy_ffn_b1[i], hy_ffn_w2[i], hy_ffn_b2[i], hy_ffn_w3[i], hy_ffn_b3[i], hy_freq[i],
                        hy_decay[i], hy_bias[i])
    w_gu = ffn_w_gu[i].astype(BF16)
    return _mixer(x, mod, norm1_g[i].reshape(1, d), norm2_g[i].reshape(1, d), final_g.reshape(1, d),
                  perm.T, z_a, z_b.reshape(bsz, seq, d_hy), w_in_b[:, d_uh:],
                  w_branch_a[i].astype(BF16), w_branch_b[i].astype(BF16), w_out[i].astype(BF16),
                  w_gu[:, :d_ff], w_gu[:, d_ff:], ffn_w_down[i].astype(BF16))
```

```python
import functools
import math

import jax
import jax.numpy as jnp
from jax import lax
from jax.experimental import pallas as pl
from jax.experimental.pallas import tpu as pltpu

F32 = jnp.float32
BF16 = jnp.bfloat16
EPS = 1e-6
HIGHEST = lax.Precision.HIGHEST

V7X_VMEM_BYTES = 64 * 1024 * 1024
VMEM_LIMIT_BYTES = 56 * 1024 * 1024
LANE = 128
SUBLANE = 8
ROW_TILE = 512
S5_TIME_CHUNK = 64
S5_LANE_CHUNK = 512
S5_CH_SPLIT = 256
S5_STEPS_PER_GROUP = 8
MXU_TILE = 256
HY_CH_TILE = 256
HY_BATCH_GROUP = 2
DFT_ROW_CHUNK = 512


def _cparams(n_axes):
    return pltpu.CompilerParams(
        dimension_semantics=("arbitrary",) * n_axes,
        vmem_limit_bytes=VMEM_LIMIT_BYTES,
    )


def _const_spec(shape):
    nd = len(shape)
    return pl.BlockSpec(shape, lambda *_: (0,) * nd, pipeline_mode=pl.Buffered(1))


def _gelu_tanh(x):
    return 0.5 * x * (1.0 + jnp.tanh(math.sqrt(2.0 / math.pi) * (x + 0.044715 * (x * x * x))))


def _norm_modulate(x, g, shift, scale):
    ms = jnp.mean(x * x, axis=-1, keepdims=True)
    r = x * lax.rsqrt(ms + EPS) * g
    return r * (1.0 + scale) + shift


def _split_bf16(x):
    hi = x.astype(BF16)
    return hi, (x - hi.astype(F32)).astype(BF16)


def _mod_kernel(c_ref, w_ref, b_ref, o_ref):
    c = c_ref[...]
    c_hi, c_lo = _split_bf16(c * jax.nn.sigmoid(c))
    w_hi, w_lo = _split_bf16(w_ref[...])
    dot = lambda a, b: jnp.dot(a, b, preferred_element_type=F32)
    o_ref[...] = dot(c_hi, w_hi) + (dot(c_hi, w_lo) + dot(c_lo, w_hi)) + b_ref[...]


def _modulation(c, ada_w, ada_b):
    bsz, d = c.shape
    n = ada_w.shape[1]
    tn = 512
    return pl.pallas_call(
        _mod_kernel,
        out_shape=jax.ShapeDtypeStruct((bsz, n), F32),
        grid=(n // tn,),
        in_specs=[
            pl.BlockSpec((bsz, d), lambda j: (0, 0)),
            pl.BlockSpec((d, tn), lambda j: (0, j)),
            pl.BlockSpec((1, tn), lambda j: (0, j)),
        ],
        out_specs=pl.BlockSpec((bsz, tn), lambda j: (0, j)),
        compiler_params=_cparams(1),
        name="adaln_mod",
    )(c, ada_w, ada_b.reshape(1, n))


def _time_tile(bsz, seq):
    return max(min(ROW_TILE // bsz, seq), 1)


def _row_permutation(bsz, tt):
    r = jnp.arange(bsz * tt, dtype=jnp.int32)
    src = (r % bsz) * tt + r // bsz
    return (src[:, None] == r[None, :]).astype(BF16)


def _inproj_kernel(x_ref, mod_ref, g_ref, w_ref, perm_ref, us5_ref, uhy_ref, *, d_s5):
    nb, tt, d = x_ref.shape
    h = _norm_modulate(x_ref[...], g_ref[...], mod_ref[:, 0:1, :], mod_ref[:, 1:2, :])
    hb = h.reshape(nb * tt, d).astype(BF16)
    p = jnp.dot(hb, w_ref[...], preferred_element_type=F32)
    us5_ref[...] = jnp.dot(perm_ref[...], p[:, :d_s5].astype(BF16),
                           preferred_element_type=F32).astype(us5_ref.dtype)
    uhy_ref[...] = p[:, d_s5:].astype(BF16).reshape(uhy_ref.shape)


def _in_projection(x, mod3, norm_g, w_uh, perm, d_s5):
    bsz, seq, d = x.shape
    n = w_uh.shape[1]
    tt = _time_tile(bsz, seq)
    return pl.pallas_call(
        functools.partial(_inproj_kernel, d_s5=d_s5),
        out_shape=(
            jax.ShapeDtypeStruct((seq * bsz, d_s5), BF16),
            jax.ShapeDtypeStruct((bsz, seq, n - d_s5), BF16),
        ),
        grid=(seq // tt,),
        in_specs=[
            pl.BlockSpec((bsz, tt, d), lambda j: (0, j, 0)),
            _const_spec(mod3.shape),
            _const_spec((1, d)),
            _const_spec((d, n)),
            _const_spec(perm.shape),
        ],
        out_specs=(
            pl.BlockSpec((tt * bsz, d_s5), lambda j: (j, 0)),
            pl.BlockSpec((bsz, tt, n - d_s5), lambda j: (0, j, 0)),
        ),
        compiler_params=_cparams(1),
        name="in_proj",
    )(x, mod3, norm_g, w_uh, perm)


def _group_of(idx, size):
    if size & (size - 1) == 0:
        return lax.shift_right_logical(idx, jnp.int32(size.bit_length() - 1))
    return idx // size


def _s5_prep_kernel(lre_ref, lim_ref, lstep_ref, bre_ref, bim_ref, cre_ref, cim_ref,
                    are_ref, aim_ref, bd_ref, cd_ref):
    ndir, nsplit, ks, two_nss = bd_ref.shape
    nss = two_nss // 2
    grp = bre_ref.shape[1]
    p = nss // (ks // grp)
    nb = are_ref.shape[1]
    step = jnp.exp(lstep_ref[...])
    lr = lre_ref[...]
    li = lim_ref[...]
    mag = jnp.exp(lr * step)
    ar = mag * jnp.cos(li * step)
    ai = mag * jnp.sin(li * step)
    num = ar - 1.0
    den = lr * lr + li * li
    cr = (num * lr + ai * li) / den
    ci = (ai * lr - num * li) / den

    iota = lambda shape, axis: lax.broadcasted_iota(jnp.int32, shape, axis)
    ch_of_row = iota((ks, grp), 0) - _group_of(iota((ks, grp), 0), grp) * grp
    rep_rows = (ch_of_row == iota((ks, grp), 1)).astype(F32)
    ch_of_lane = iota((grp, ks), 1) - _group_of(iota((grp, ks), 1), grp) * grp
    rep_lanes = (ch_of_lane == iota((grp, ks), 0)).astype(F32)
    in_mask = _group_of(iota((ks, 1), 0), grp) == _group_of(iota((1, nss), 1), p)
    out_mask = _group_of(iota((nss, 1), 0), p) == _group_of(iota((1, ks), 1), grp)
    rep = lambda a, b: jnp.dot(a, b, precision=HIGHEST, preferred_element_type=F32)

    for d in range(ndir):
        are_ref[d] = jnp.broadcast_to(ar[d:d + 1, :], (nb, ar.shape[1]))
        aim_ref[d] = jnp.broadcast_to(ai[d:d + 1, :], (nb, ai.shape[1]))
        br = bre_ref[d]
        bi = bim_ref[d]
        bb = (cr[d:d + 1, :] * br - ci[d:d + 1, :] * bi, cr[d:d + 1, :] * bi + ci[d:d + 1, :] * br)
        cc = (cre_ref[d], -cim_ref[d])
        for h in range(nsplit):
            st = slice(h * nss, (h + 1) * nss)
            for part in range(2):
                lanes = slice(part * nss, (part + 1) * nss)
                bd_ref[d, h, :, lanes] = jnp.where(in_mask, rep(rep_rows, bb[part][:, st]), 0.0).astype(BF16)
                cd_ref[d, h, lanes, :] = jnp.where(out_mask, rep(cc[part][st, :], rep_lanes), 0.0).astype(BF16)


def _s5_prepare(lam_re, lam_im, log_step, b_re, b_im, c_re, c_im, *, nb, nsplit, p):
    ndir, grp, ns = b_re.shape
    nss = ns // nsplit
    ks = (nss // p) * grp
    outs = (
        jax.ShapeDtypeStruct((ndir, nb, ns), F32),
        jax.ShapeDtypeStruct((ndir, nb, ns), F32),
        jax.ShapeDtypeStruct((ndir, nsplit, ks, 2 * nss), BF16),
        jax.ShapeDtypeStruct((ndir, nsplit, 2 * nss, ks), BF16),
    )
    return pl.pallas_call(
        _s5_prep_kernel,
        out_shape=outs,
        compiler_params=pltpu.CompilerParams(vmem_limit_bytes=VMEM_LIMIT_BYTES),
        name="s5_discretize",
    )(lam_re, lam_im, log_step, b_re, b_im, c_re, c_im)


def _spread(items, n):
    return [items[(len(items) * i) // n:(len(items) * (i + 1)) // n] for i in range(n)]


def _s5_scan_kernel(*refs, reverse, final, tc, nb):
    if final:
        (u_ref, bd_ref, cd_ref, are_ref, aim_ref, yprev_ref, gluw_ref, glub_ref,
         out_ref, bu_ref, sb_ref, y_ref, s_ref) = refs
    else:
        (u_ref, bd_ref, cd_ref, are_ref, aim_ref, d_ref, out_ref, bu_ref, sb_ref, y_ref, s_ref) = refs
    rows = tc * nb
    _, nsplit, ks, two_nss = bd_ref.shape
    nss = two_nss // 2
    lc = min(S5_LANE_CHUNK, nss)
    tw = min(MXU_TILE, lc)
    rh = rows // 2 if rows % (2 * SUBLANE * 2) == 0 else rows
    row_halves = [slice(r, r + rh) for r in range(0, rows, rh)]

    @pl.when(pl.program_id(0) == 0)
    def _():
        s_ref[...] = jnp.zeros_like(s_ref)

    units = [(h, q) for h in range(nsplit) for q in range(nss // lc)]

    def lanes(unit, part, off=0, width=None):
        h, q = unit
        start = h * two_nss + part * nss + q * lc + off
        return slice(start, start + (lc if width is None else width))

    def expand_pieces(unit):
        h, q = unit
        out = []
        for part in range(2):
            for off in range(0, lc, tw):
                for rs in row_halves:
                    def piece(part=part, off=off, rs=rs):
                        col = part * nss + q * lc + off
                        bu_ref[rs, lanes(unit, part, off, tw)] = jnp.dot(
                            u_ref[rs, h * ks:(h + 1) * ks], bd_ref[0, h, :, col:col + tw],
                            preferred_element_type=F32)
                    out.append(piece)
        return out

    def readout_pieces(unit):
        h, q = unit
        out = []
        for rs in row_halves:
            def piece(rs=rs):
                v = None
                for part in range(2):
                    k0 = part * nss + q * lc
                    d = jnp.dot(sb_ref[rs, lanes(unit, part)], cd_ref[0, h, k0:k0 + lc, :],
                                preferred_element_type=F32)
                    v = d if v is None else v + d
                if q == 0:
                    y_ref[rs, h * ks:(h + 1) * ks] = v
                else:
                    y_ref[rs, h * ks:(h + 1) * ks] += v
            out.append(piece)
        return out

    step_groups = _spread(list(range(tc)), max(tc // S5_STEPS_PER_GROUP, 1))
    for p in expand_pieces(units[0]):
        p()
    for c, unit in enumerate(units):
        mxu_work = (expand_pieces(units[c + 1]) if c + 1 < len(units) else []) + \
                   (readout_pieces(units[c - 1]) if c > 0 else [])
        mxu_groups = _spread(mxu_work, len(step_groups))
        a_sl = slice(unit[0] * nss + unit[1] * lc, unit[0] * nss + (unit[1] + 1) * lc)
        a_re = are_ref[0, :, a_sl]
        a_im = aim_ref[0, :, a_sl]
        re_sl, im_sl = lanes(unit, 0), lanes(unit, 1)
        sr = s_ref[:, re_sl]
        si = s_ref[:, im_sl]
        for steps, pieces in zip(step_groups, mxu_groups):
            for p in pieces:
                p()
            for k in steps:
                t = (tc - 1 - k) if reverse else k
                row = slice(t * nb, (t + 1) * nb)
                nr = a_re * sr - a_im * si + bu_ref[row, re_sl]
                ni = a_re * si + a_im * sr + bu_ref[row, im_sl]
                sb_ref[row, re_sl] = nr.astype(BF16)
                sb_ref[row, im_sl] = ni.astype(BF16)
                sr, si = nr, ni
        s_ref[:, re_sl] = sr
        s_ref[:, im_sl] = si
    for p in readout_pieces(units[-1]):
        p()

    rb = min(256, rows)
    for r in range(rows // rb):
        rs = slice(r * rb, (r + 1) * rb)
        if final:
            z = _gelu_tanh(yprev_ref[rs, :] + y_ref[rs, :])
            gate = jnp.dot(z.astype(BF16), gluw_ref[...], preferred_element_type=F32) + glub_ref[...]
            out_ref[rs, :] = (z * jax.nn.sigmoid(gate)).astype(out_ref.dtype)
        else:
            out_ref[rs, :] = y_ref[rs, :] + u_ref[rs, :] * d_ref[...]


def _s5_scan(u_tm, bd, cd, a_re, a_im, extra, *, direction, final, nb, out_dtype):
    rows_total, d_s5 = u_tm.shape
    seq = rows_total // nb
    two_ns = bd.shape[1] * bd.shape[3]
    tc = min(S5_TIME_CHUNK, seq)
    nchunk = seq // tc
    rows = tc * nb
    reverse = direction == 1
    if reverse:
        cidx = lambda i: (nchunk - 1 - i, 0)
    else:
        cidx = lambda i: (i, 0)

    def dir_spec(a):
        nd = a.ndim
        return pl.BlockSpec((1,) + a.shape[1:], lambda i: (direction,) + (0,) * (nd - 1),
                            pipeline_mode=pl.Buffered(1))

    in_specs = [pl.BlockSpec((rows, d_s5), cidx), dir_spec(bd), dir_spec(cd), dir_spec(a_re), dir_spec(a_im)]
    if final:
        yprev, glu_w, glu_b = extra
        in_specs += [pl.BlockSpec((rows, d_s5), cidx), _const_spec(glu_w.shape), _const_spec(glu_b.shape)]
    else:
        in_specs += [_const_spec(extra[0].shape)]
    return pl.pallas_call(
        functools.partial(_s5_scan_kernel, reverse=reverse, final=final, tc=tc, nb=nb),
        out_shape=jax.ShapeDtypeStruct((rows_total, d_s5), out_dtype),
        grid=(nchunk,),
        in_specs=in_specs,
        out_specs=pl.BlockSpec((rows, d_s5), cidx),
        scratch_shapes=[pltpu.VMEM((rows, two_ns), F32), pltpu.VMEM((rows, two_ns), BF16),
                        pltpu.VMEM((rows, d_s5), F32), pltpu.VMEM((nb, two_ns), F32)],
        compiler_params=_cparams(1),
        name="s5_scan_bwd_glu" if final else "s5_scan_fwd",
    )(u_tm, bd, cd, a_re, a_im, *extra)


def _s5_branch(u_tm, nb, lam_re, lam_im, log_step, b_re, b_im, c_re, c_im, d, glu_w, glu_b):
    ndir, g, p = lam_re.shape
    if ndir != 2:
        raise NotImplementedError("S5 branch expects forward and backward directions")
    grp = b_re.shape[-1]
    ns = g * p
    d_s5 = g * grp
    ks = S5_CH_SPLIT if d_s5 % S5_CH_SPLIT == 0 else d_s5
    flat = lambda a: a.reshape(ndir, ns)
    to_lanes = lambda a: jnp.transpose(a, (0, 3, 1, 2)).reshape(ndir, grp, ns)
    to_rows = lambda a: jnp.transpose(a, (0, 1, 3, 2)).reshape(ndir, ns, grp)
    a_re, a_im, bd, cd = _s5_prepare(
        flat(lam_re), flat(lam_im), jnp.repeat(log_step, p, axis=-1), to_lanes(b_re), to_lanes(b_im),
        to_rows(c_re), to_rows(c_im), nb=nb, nsplit=d_s5 // ks, p=p)
    y = _s5_scan(u_tm, bd, cd, a_re, a_im, (d.reshape(1, -1),), direction=0, final=False, nb=nb, out_dtype=F32)
    extra = (y, glu_w.astype(BF16), glu_b.reshape(1, -1))
    return _s5_scan(u_tm, bd, cd, a_re, a_im, extra, direction=1, final=True, nb=nb, out_dtype=BF16)


RSQRT2 = math.sqrt(0.5)


def _dft_table(nf, t, n):
    f_lo = 64 if nf % 64 == 0 else 1
    f_hi = nf // f_lo
    t = t[:, None, :]
    ka = (jnp.arange(f_hi, dtype=jnp.int32)[None, :, None] * f_lo * t) % n
    kb = (jnp.arange(f_lo, dtype=jnp.int32)[None, :, None] * t) % n
    w = 2.0 * math.pi / n
    aa = ka.astype(F32) * w
    ab = kb.astype(F32) * w
    ca, sa, cb, sb = jnp.cos(aa), jnp.sin(aa), jnp.cos(ab), jnp.sin(ab)
    cm = ca[:, :, None, :] * cb[:, None, :, :] - sa[:, :, None, :] * sb[:, None, :, :]
    sm = sa[:, :, None, :] * cb[:, None, :, :] + ca[:, :, None, :] * sb[:, None, :, :]
    shape = (t.shape[0], nf, t.shape[-1])
    return cm.reshape(shape), sm.reshape(shape)


def _hyena_tables(seq):
    q = seq // 4
    times = 4 * jnp.arange(q, dtype=jnp.int32)[None, :] + jnp.arange(4, dtype=jnp.int32)[:, None]
    cf, sf = _dft_table(q, times, 2 * seq)
    alt = (1 - 2 * (jnp.arange(q, dtype=jnp.int32) & 1)).astype(F32)
    f_is_0 = jnp.arange(q, dtype=jnp.int32)[None, :, None] == 0
    sf = jnp.where(f_is_0, alt[None, None, :], sf)
    return dict(c_fwd=cf.astype(BF16), s_fwd=sf.astype(BF16),
                c_inv=jnp.transpose(cf, (0, 2, 1)).astype(BF16), s_inv=jnp.transpose(sf, (0, 2, 1)).astype(BF16))


def _cmul(z, k):
    (zc, zs), (kc, ks) = z, k
    return zc * kc - zs * ks, zc * ks + zs * kc


def _forward_spectrum(x, a):
    (c0, s0), (c1, s1), (c2, s2), (c3, s3) = x
    pc, ps, mc, ms = c0 + c2, s0 + s2, c0 - c2, s0 - s2
    qc, qs, nc, ns = c1 + c3, s1 + s3, c1 - c3, s1 - s3
    za = (pc + qc, ps + qs)
    zb = (pc - qc, qs - ps)
    zc = (mc + ns, nc - ms)
    zd = (mc - ns, ms + nc)
    e = (a[1] - a[3]) * RSQRT2
    o = (a[1] + a[3]) * RSQRT2
    return [za, zb, zc, zd], [(a[0] + e, a[2] + o), (a[0] - e, o - a[2])]


def _inverse_spectrum(y, ymid):
    (ac, as_), (bc, bs), (cc, cs), (dc, ds) = y
    upc, ups, umc, ums = ac + bc, as_ - bs, ac - bc, as_ + bs
    vpc, vps, vmc, vms = dc + cc, ds - cs, dc - cc, ds + cs
    t = [(upc + vpc, ups + vps), (umc + vms, ums - vmc), (upc - vpc, ups - vps), (umc - vms, ums + vmc)]
    (y1c, y1s), (y3c, y3s) = ymid
    sp = [y1c + y3c, ((y1c + y1s) + (y3s - y3c)) * RSQRT2, y1s - y3s, ((y1s - y1c) + (y3c + y3s)) * RSQRT2]
    return t, sp


def _hy_filter_kernel(feat_ref, w1_ref, b1_ref, w2_ref, b2_ref, freq_ref,
                      w3f_ref, b3f_ref, decf_ref, w3b_ref, b3b_ref, decb_ref,
                      cf_ref, sf_ref,
                      kac_ref, kas_ref, kbc_ref, kbs_ref, kcc_ref, kcs_ref, kdc_ref, kds_ref, km_ref,
                      taps_s, rhs_s, h_s):
    seq = feat_ref.shape[0]
    q = seq // 4
    n = 2 * seq
    ct = kac_ref.shape[1]
    nsl = ct // LANE

    @pl.when((pl.program_id(0) == 0) & (pl.program_id(1) == 0))
    def _():
        f = freq_ref[...]
        h1 = jnp.sin(f * (jnp.dot(feat_ref[...], w1_ref[...], precision=HIGHEST,
                                  preferred_element_type=F32) + b1_ref[...]))
        h_s[...] = jnp.sin(f * (jnp.dot(h1, w2_ref[...], precision=HIGHEST,
                                        preferred_element_type=F32) + b2_ref[...]))

    h = h_s[...]
    t01 = feat_ref[:, 0:1]
    row = lax.broadcasted_iota(jnp.int32, (seq, 1), 0)

    def taps(w3_ref, b3_ref, dec_ref):
        v = jnp.dot(h, w3_ref[...], precision=HIGHEST, preferred_element_type=F32) + b3_ref[...]
        return v * jnp.exp(-t01 * jnp.abs(dec_ref[...]))

    fwd = taps(w3f_ref, b3f_ref, decf_ref)
    bwd = jnp.where(row == 0, 0.0, taps(w3b_ref, b3b_ref, decb_ref))
    for d, x in enumerate((fwd, bwd)):
        for j in range(nsl):
            taps_s[d * nsl + j] = x[:, j * LANE:(j + 1) * LANE]
        for r in range(4):
            for j in range(nsl):
                c0 = (4 * d + r) * ct + j * LANE
                rhs_s[:, c0:c0 + LANE] = taps_s[d * nsl + j, pl.ds(r, q, stride=4), :].astype(BF16)
    row0 = lax.broadcasted_iota(jnp.int32, (q, 1), 0) == 0

    def spectrum(d):
        x, a = [], []
        for r in range(4):
            rhs = rhs_s[:, (4 * d + r) * ct:(4 * d + r + 1) * ct]
            c = jnp.dot(cf_ref[r], rhs, preferred_element_type=F32)
            sn = jnp.dot(sf_ref[r], rhs, preferred_element_type=F32)
            a.append(sn[0:1, :])
            x.append((c, jnp.where(row0, 0.0, sn)))
        return _forward_spectrum(x, a)

    fz, fm = spectrum(0)
    bz, bm = spectrum(1)
    scale = jnp.where(row0, 1.0 / n, 2.0 / n)
    outs = ((kac_ref, kas_ref), (kbc_ref, kbs_ref), (kcc_ref, kcs_ref), (kdc_ref, kds_ref))
    for (oc, os_), (fc, fs), (bc, bs) in zip(outs, fz, bz):
        oc[...] = (fc + bc) * scale
        os_[...] = (fs - bs) * scale
    km_ref[...] = jnp.zeros_like(km_ref)
    for i, ((fc, fs), (bc, bs)) in enumerate(zip(fm, bm)):
        km_ref[2 * i:2 * i + 1, :] = (fc + bc) * (2.0 / n)
        km_ref[2 * i + 1:2 * i + 2, :] = (fs - bs) * (2.0 / n)


def _hyena_filters(seq, w1, b1, w2, b2, w3, b3, freq, decay, tables, n_order, n_dirs, d_hy):
    q = seq // 4
    emb, hid = w1.shape
    bands = (emb - 1) // 2
    t = jnp.arange(seq, dtype=F32)
    t01 = t / max(seq - 1, 1)
    band = jnp.linspace(1e-4, bands - 1, bands, dtype=F32)
    ang = (2.0 * math.pi) * t[:, None] * band[None, :] / seq
    feats = jnp.concatenate([t01[:, None], jnp.cos(ang), jnp.sin(ang)], axis=-1)
    kpad = LANE
    feats = jnp.pad(feats, ((0, 0), (0, kpad - emb)))
    w1p = jnp.pad(w1, ((0, kpad - emb), (0, 0)))
    tabs = [tables["c_fwd"], tables["s_fwd"]]
    ct = min(HY_CH_TILE, d_hy)
    nct = d_hy // ct
    ncol = n_order * d_hy
    b3r = b3.reshape(1, -1)
    decr = decay.reshape(1, -1)
    fcol = lambda o, c: (0, (o * n_dirs + 0) * nct + c)
    bcol = lambda o, c: (0, (o * n_dirs + 1) * nct + c)
    ocol = lambda o, c: (0, o * nct + c)
    full = lambda a: pl.BlockSpec(a.shape, lambda o, c: (0,) * a.ndim)
    spec = jax.ShapeDtypeStruct((q, ncol), F32)
    return pl.pallas_call(
        _hy_filter_kernel,
        out_shape=(spec,) * 8 + (jax.ShapeDtypeStruct((8, ncol), F32),),
        grid=(n_order, nct),
        in_specs=[
            full(feats), full(w1p), full(b1.reshape(1, -1)), full(w2), full(b2.reshape(1, -1)),
            full(freq.reshape(1, -1)),
            pl.BlockSpec((hid, ct), fcol), pl.BlockSpec((1, ct), fcol), pl.BlockSpec((1, ct), fcol),
            pl.BlockSpec((hid, ct), bcol), pl.BlockSpec((1, ct), bcol), pl.BlockSpec((1, ct), bcol),
        ] + [_const_spec(a.shape) for a in tabs],
        out_specs=tuple(pl.BlockSpec((q, ct), ocol) for _ in range(8)) + (pl.BlockSpec((8, ct), ocol),),
        scratch_shapes=[pltpu.VMEM((2 * ct // LANE, seq, LANE), F32), pltpu.VMEM((q, 8 * ct), BF16),
                        pltpu.VMEM((seq, hid), F32)],
        compiler_params=_cparams(2),
        name="hyena_filter_spectra",
    )(feats, w1p, b1.reshape(1, -1), w2, b2.reshape(1, -1), freq.reshape(1, -1),
      w3, b3r, decr, w3, b3r, decr, *tabs)


def _hy_conv_kernel(zin_ref, gin_ref, wz_ref, bz_ref, wg_ref, bg_ref,
                    kac_ref, kas_ref, kbc_ref, kbs_ref, kcc_ref, kcs_ref, kdc_ref, kds_ref, km_ref, bias_ref,
                    cf_ref, sf_ref, ci_ref, si_ref, out_ref,
                    zraw_s, graw_s, o_s, z_s, g_s, rhs_s, ac_s, as_s, *, conv_on_z, seq):
    ct = gin_ref.shape[1]
    nbat = gin_ref.shape[0] // seq
    q = seq // 4
    nsl = ct // LANE
    pad = zraw_s.shape[2] - seq
    top = pad // 2
    rc = min(DFT_ROW_CHUNK, q)
    k_refs = ((kac_ref, kas_ref), (kbc_ref, kbs_ref), (kcc_ref, kcs_ref), (kdc_ref, kds_ref))

    def stage(raw_s, src_ref, k):
        for j in range(nsl):
            raw_s[k, j, 0:top, :] = jnp.zeros((top, LANE), F32)
            raw_s[k, j, top + seq:, :] = jnp.zeros((pad - top, LANE), F32)
            raw_s[k, j, top:top + seq, :] = src_ref[k * seq:(k + 1) * seq, j * LANE:(j + 1) * LANE].astype(F32)

    def split_rows(raw_s, dst_ref, k, w_ref, b_ref):
        for r in range(4):
            for j in range(nsl):
                ls = slice(j * LANE, (j + 1) * LANE)
                tap = lambda d: raw_s[k, j, pl.ds(top + r + d, q, stride=4), :]
                dst_ref[k, :, r * ct + j * LANE:r * ct + (j + 1) * LANE] = (
                    b_ref[:, ls] + tap(-1) * w_ref[0:1, ls] + tap(0) * w_ref[1:2, ls] + tap(1) * w_ref[2:3, ls])

    def spectrum_chunk(k, i):
        r = slice(i * rc, (i + 1) * rc)
        row0 = lax.broadcasted_iota(jnp.int32, (rc, 1), 0) == 0
        x, a = [], []
        for p in range(4):
            rhs = rhs_s[k, :, p * ct:(p + 1) * ct]
            c = jnp.dot(cf_ref[p, r, :], rhs, preferred_element_type=F32)
            sn = jnp.dot(sf_ref[p, r, :], rhs, preferred_element_type=F32)
            if i == 0:
                a.append(sn[0:1, :])
                sn = jnp.where(row0, 0.0, sn)
            else:
                a.append(jnp.zeros((1, ct), F32))
            x.append((c, sn))
        z, zmid = _forward_spectrum(x, a)
        y = [_cmul(zx, (kc[r, :], ks[r, :])) for zx, (kc, ks) in zip(z, k_refs)]
        ymid = [_cmul(zmid[m], (km_ref[2 * m:2 * m + 1, :], km_ref[2 * m + 1:2 * m + 2, :])) for m in range(2)]
        t, sp = _inverse_spectrum(y, ymid)
        for p, ((tc, ts), spp) in enumerate(zip(t, sp)):
            if i == 0:
                ts = jnp.where(row0, spp, ts)
            ac_s[k, r, p * ct:(p + 1) * ct] = tc.astype(BF16)
            as_s[k, r, p * ct:(p + 1) * ct] = ts.astype(BF16)

    def output_chunk(k, i):
        r = slice(i * rc, (i + 1) * rc)
        for p in range(4):
            ps = slice(p * ct, (p + 1) * ct)
            y = jnp.dot(ci_ref[p, r, :], ac_s[k, :, ps], preferred_element_type=F32)
            y = y + jnp.dot(si_ref[p, r, :], as_s[k, :, ps], preferred_element_type=F32)
            rows = pl.ds(4 * i * rc + p, rc, stride=4)
            if conv_on_z:
                out_ref[k * q + i * rc:k * q + (i + 1) * rc, ps] = (
                    g_s[k, r, ps] * (y + z_s[k, r, ps] * bias_ref[...])).astype(out_ref.dtype)
                continue
            for j in range(nsl):
                ls = slice(p * ct + j * LANE, p * ct + (j + 1) * LANE)
                jl = slice(j * LANE, (j + 1) * LANE)
                o_s[k, j, rows, :] = g_s[k, r, ls] * (y[:, jl] + z_s[k, r, ls] * bias_ref[:, jl])

    for k in range(nbat):
        stage(graw_s, gin_ref, k)
        split_rows(graw_s, g_s, k, wg_ref, bg_ref)
        if conv_on_z:
            stage(zraw_s, zin_ref, k)
            split_rows(zraw_s, z_s, k, wz_ref, bz_ref)
            rhs_s[k] = z_s[k].astype(BF16)
        else:
            rhs_s[k] = zin_ref[k * q:(k + 1) * q, :]
            z_s[k] = zin_ref[k * q:(k + 1) * q, :].astype(F32)
    for i in range(q // rc):
        for k in range(nbat):
            spectrum_chunk(k, i)
    for i in range(q // rc):
        for k in range(nbat):
            output_chunk(k, i)
    if not conv_on_z:
        for k in range(nbat):
            for j in range(nsl):
                out_ref[k * seq:(k + 1) * seq, j * LANE:(j + 1) * LANE] = o_s[k, j].astype(out_ref.dtype)


def _hy_conv(zin, zcol0, gcol0, u_hy, conv_w, conv_b, spectra, bias_row, tables,
             *, order, conv_on_z, bsz, seq, d_hy):
    tabs = [tables[k] for k in ("c_fwd", "s_fwd", "c_inv", "s_inv")]
    q = seq // 4
    ct = min(HY_CH_TILE, d_hy)
    nsl = ct // LANE
    nct = d_hy // ct
    nbat = HY_BATCH_GROUP if bsz % HY_BATCH_GROUP == 0 else 1
    zc0 = zcol0 // ct
    gc0 = gcol0 // ct
    zw0 = zc0 if conv_on_z else 0
    kspec = lambda rows: pl.BlockSpec((rows, ct), lambda c, b: (0, order * nct + c),
                                      pipeline_mode=pl.Buffered(1))
    scratch = lambda dt: pltpu.VMEM((nbat, q, 4 * ct), dt)
    slabs = lambda rows: pltpu.VMEM((nbat, nsl, rows, LANE), F32)
    time_block = pl.BlockSpec((nbat * seq, ct), lambda c, b: (b, c))
    split_block = pl.BlockSpec((nbat * q, 4 * ct), lambda c, b: (b, c))
    return pl.pallas_call(
        functools.partial(_hy_conv_kernel, conv_on_z=conv_on_z, seq=seq),
        out_shape=jax.ShapeDtypeStruct((bsz * q, 4 * d_hy) if conv_on_z else (bsz * seq, d_hy), BF16),
        grid=(nct, bsz // nbat),
        in_specs=[
            pl.BlockSpec((nbat * seq, ct), lambda c, b: (b, zc0 + c)) if conv_on_z else split_block,
            pl.BlockSpec((nbat * seq, ct), lambda c, b: (b, gc0 + c)),
            pl.BlockSpec((conv_w.shape[0], ct), lambda c, b: (0, zw0 + c)),
            pl.BlockSpec((1, ct), lambda c, b: (0, zw0 + c)),
            pl.BlockSpec((conv_w.shape[0], ct), lambda c, b: (0, gc0 + c)),
            pl.BlockSpec((1, ct), lambda c, b: (0, gc0 + c)),
        ] + [kspec(q)] * 8 + [kspec(8), kspec(1)] + [_const_spec(a.shape) for a in tabs],
        out_specs=split_block if conv_on_z else time_block,
        scratch_shapes=[
            slabs(seq + 2 * SUBLANE), slabs(seq + 2 * SUBLANE), slabs(seq),
            scratch(F32), scratch(F32), scratch(BF16), scratch(BF16), scratch(BF16),
        ],
        compiler_params=_cparams(2),
        name=f"hyena_conv_order{order}",
    )(zin, u_hy, conv_w, conv_b, conv_w, conv_b, *spectra, bias_row, *tabs)


def _hyena_branch(u_hy, bsz, seq, conv_w, conv_b, w1, b1, w2, b2, w3, b3, freq, decay, bias):
    n_order, d_hy = bias.shape
    n_dirs = w3.shape[1] // (n_order * d_hy)
    if n_order != 2 or seq % 8 != 0 or d_hy % LANE != 0:
        raise NotImplementedError("Hyena branch: two long convolutions, L % 8 == 0, 128-lane channel tiles")
    tables = _hyena_tables(seq)
    spectra = _hyena_filters(seq, w1, b1, w2, b2, w3, b3, freq, decay, tables, n_order, n_dirs, d_hy)
    cb = conv_b.reshape(1, -1)
    bias_row = bias.reshape(1, -1)
    common = dict(bsz=bsz, seq=seq, d_hy=d_hy)
    z1 = _hy_conv(u_hy, 0, d_hy, u_hy, conv_w, cb, spectra, bias_row, tables,
                  order=0, conv_on_z=True, **common)
    return _hy_conv(z1, 0, 2 * d_hy, u_hy, conv_w, cb, spectra, bias_row, tables,
                    order=1, conv_on_z=False, **common)


def _mixer_kernel(x_ref, mod_ref, g1_ref, g2_ref, fg_ref, permt_ref, za_ref, zb_ref,
                  wgate_ref, wa_ref, wb_ref, wout_ref, wg_ref, wu_ref, wd_ref, o_ref, *, n_chunks):
    nb, tt, d = x_ref.shape
    rows = nb * tt
    mod = lambda k: mod_ref[:, k:k + 1, :]
    x = x_ref[...]
    h = _norm_modulate(x, g1_ref[...], mod(0), mod(1)).reshape(rows, d).astype(BF16)
    gate = jax.nn.sigmoid(jnp.dot(h, wgate_ref[...], preferred_element_type=F32))
    za = jnp.dot(permt_ref[...], za_ref[...], preferred_element_type=F32).astype(BF16)
    ya = jnp.dot(za, wa_ref[...], preferred_element_type=F32)
    yb = jnp.dot(zb_ref[...].reshape(rows, zb_ref.shape[-1]), wb_ref[...], preferred_element_type=F32)
    merged = gate[:, :d] * ya + gate[:, d:] * yb
    o = jnp.dot(merged.astype(BF16), wout_ref[...], preferred_element_type=F32)
    x1 = x + mod(2) * o.reshape(nb, tt, d)

    h2 = _norm_modulate(x1, g2_ref[...], mod(3), mod(4)).reshape(rows, d).astype(BF16)
    d_ff = wg_ref.shape[1]
    unit = MXU_TILE if d_ff % MXU_TILE == 0 else d_ff // n_chunks
    edges = [len(g) for g in _spread(list(range(d_ff // unit)), n_chunks)]
    acc = jnp.zeros((rows, d), F32)
    start = 0
    for width in edges:
        sl = slice(start * unit, (start + width) * unit)
        start += width
        gl = jnp.dot(h2, wg_ref[:, sl], preferred_element_type=F32)
        up = jnp.dot(h2, wu_ref[:, sl], preferred_element_type=F32)
        act = (gl * jax.nn.sigmoid(gl) * up).astype(BF16)
        acc = acc + jnp.dot(act, wd_ref[sl, :], preferred_element_type=F32)
    x2 = x1 + mod(5) * acc.reshape(nb, tt, d)
    ms = jnp.mean(x2 * x2, axis=-1, keepdims=True)
    o_ref[...] = x2 * lax.rsqrt(ms + EPS) * fg_ref[...]


def _mixer(x, mod3, norm1_g, norm2_g, final_g, perm_t, za_tm, zb, w_gate, w_a, w_b, w_out, w_g, w_u, w_d):
    bsz, seq, d = x.shape
    d_s5 = w_a.shape[0]
    d_hy = w_b.shape[0]
    d_ff = w_g.shape[1]
    tt = _time_tile(bsz, seq)
    n_chunks = 2 if d_ff % (2 * LANE) == 0 else 1
    consts = (mod3, norm1_g, norm2_g, final_g, perm_t)
    weights = (w_gate, w_a, w_b, w_out, w_g, w_u, w_d)
    return pl.pallas_call(
        functools.partial(_mixer_kernel, n_chunks=n_chunks),
        out_shape=jax.ShapeDtypeStruct(x.shape, F32),
        grid=(seq // tt,),
        in_specs=[pl.BlockSpec((bsz, tt, d), lambda j: (0, j, 0))]
        + [_const_spec(a.shape) for a in consts]
        + [pl.BlockSpec((tt * bsz, d_s5), lambda j: (j, 0)),
           pl.BlockSpec((bsz, tt, d_hy), lambda j: (0, j, 0))]
        + [_const_spec(a.shape) for a in weights],
        out_specs=pl.BlockSpec((bsz, tt, d), lambda j: (0, j, 0)),
        compiler_params=_cparams(1),
        name="merge_swiglu_final_norm",
    )(x, *consts, za_tm, zb, *weights)


def kernel(x, c, ada_w, ada_b, norm1_g, norm2_g, w_in, s5_lam_re, s5_lam_im, s5_log_step, s5_b_re, s5_b_im, s5_c_re, s5_c_im, s5_d, s5_glu_w, s5_glu_b, hy_conv_w, hy_conv_b, hy_ffn_w1, hy_ffn_b1, hy_ffn_w2, hy_ffn_b2, hy_ffn_w3, hy_ffn_b3, hy_freq, hy_decay, hy_bias, w_branch_a, w_branch_b, w_out, ffn_w_gu, ffn_w_down, final_g):
    bsz, seq, d = x.shape
    depth = ada_w.shape[0]
    if depth != 1:
        raise NotImplementedError("the final RMSNorm is fused into the (single) layer's channel mixer")
    d_s5 = s5_d.shape[-1]
    n_order, d_hy = hy_bias.shape[1:]
    d_uh = d_s5 + (n_order + 1) * d_hy
    d_ff = ffn_w_down.shape[1]
    i = 0
    perm = _row_permutation(bsz, _time_tile(bsz, seq))
    mod = _modulation(c, ada_w[i], ada_b[i]).reshape(bsz, 6, d)
    w_in_b = w_in[i].astype(BF16)
    u_s5, u_hy = _in_projection(x, mod, norm1_g[i].reshape(1, d), w_in_b[:, :d_uh], perm, d_s5)
    z_a = _s5_branch(u_s5, bsz, s5_lam_re[i], s5_lam_im[i], s5_log_step[i], s5_b_re[i], s5_b_im[i],
                     s5_c_re[i], s5_c_im[i], s5_d[i], s5_glu_w[i], s5_glu_b[i])
    z_b = _hyena_branch(u_hy.reshape(bsz * seq, -1), bsz, seq, hy_conv_w[i], hy_conv_b[i], hy_ffn_w1[i],
                        hy_ffn_b1[i], hy_ffn_w2[i], hy_ffn_b2[i], hy_ffn_w3[i], hy_ffn_b3[i], hy_freq[i],
                        hy_decay[i], hy_bias[i])
    w_gu = ffn_w_gu[i].astype(BF16)
    return _mixer(x, mod, norm1_g[i].reshape(1, d), norm2_g[i].reshape(1, d), final_g.reshape(1, d),
                  perm.T, z_a, z_b.reshape(bsz, seq, d_hy), w_in_b[:, d_uh:],
                  w_branch_a[i].astype(BF16), w_branch_b[i].astype(BF16), w_out[i].astype(BF16),
                  w_gu[:, :d_ff], w_gu[:, d_ff:], ffn_w_down[i].astype(BF16))
```

```python
import functools
import math

import jax
import jax.numpy as jnp
from jax import lax
from jax.experimental import pallas as pl
from jax.experimental.pallas import tpu as pltpu

F32 = jnp.float32
BF16 = jnp.bfloat16
EPS = 1e-6
HIGHEST = lax.Precision.HIGHEST

V7X_VMEM_BYTES = 64 * 1024 * 1024
VMEM_LIMIT_BYTES = 56 * 1024 * 1024
LANE = 128
SUBLANE = 8
ROW_TILE = 512
S5_TIME_CHUNK = 64
S5_LANE_CHUNK = 512
S5_CH_SPLIT = 256
S5_STEPS_PER_GROUP = 8
MXU_TILE = 256
HY_CH_TILE = 256
HY_BATCH_GROUP = 2
DFT_ROW_CHUNK = 512


def _cparams(n_axes):
    return pltpu.CompilerParams(
        dimension_semantics=("arbitrary",) * n_axes,
        vmem_limit_bytes=VMEM_LIMIT_BYTES,
    )


def _const_spec(shape):
    nd = len(shape)
    return pl.BlockSpec(shape, lambda *_: (0,) * nd, pipeline_mode=pl.Buffered(1))


def _gelu_tanh(x):
    return 0.5 * x * (1.0 + jnp.tanh(math.sqrt(2.0 / math.pi) * (x + 0.044715 * (x * x * x))))


def _norm_modulate(x, g, shift, scale):
    ms = jnp.mean(x * x, axis=-1, keepdims=True)
    r = x * lax.rsqrt(ms + EPS) * g
    return r * (1.0 + scale) + shift


def _split_bf16(x):
    hi = x.astype(BF16)
    return hi, (x - hi.astype(F32)).astype(BF16)


def _mod_kernel(c_ref, w_ref, b_ref, o_ref):
    c = c_ref[...]
    c_hi, c_lo = _split_bf16(c * jax.nn.sigmoid(c))
    w_hi, w_lo = _split_bf16(w_ref[...])
    dot = lambda a, b: jnp.dot(a, b, preferred_element_type=F32)
    o_ref[...] = dot(c_hi, w_hi) + (dot(c_hi, w_lo) + dot(c_lo, w_hi)) + b_ref[...]


def _modulation(c, ada_w, ada_b):
    bsz, d = c.shape
    n = ada_w.shape[1]
    tn = 512
    return pl.pallas_call(
        _mod_kernel,
        out_shape=jax.ShapeDtypeStruct((bsz, n), F32),
        grid=(n // tn,),
        in_specs=[
            pl.BlockSpec((bsz, d), lambda j: (0, 0)),
            pl.BlockSpec((d, tn), lambda j: (0, j)),
            pl.BlockSpec((1, tn), lambda j: (0, j)),
        ],
        out_specs=pl.BlockSpec((bsz, tn), lambda j: (0, j)),
        compiler_params=_cparams(1),
        name="adaln_mod",
    )(c, ada_w, ada_b.reshape(1, n))


def _time_tile(bsz, seq):
    return max(min(ROW_TILE // bsz, seq), 1)


def _row_permutation(bsz, tt):
    r = jnp.arange(bsz * tt, dtype=jnp.int32)
    src = (r % bsz) * tt + r // bsz
    return (src[:, None] == r[None, :]).astype(BF16)


def _inproj_kernel(x_ref, mod_ref, g_ref, w_ref, perm_ref, us5_ref, uhy_ref, *, d_s5):
    nb, tt, d = x_ref.shape
    h = _norm_modulate(x_ref[...], g_ref[...], mod_ref[:, 0:1, :], mod_ref[:, 1:2, :])
    hb = h.reshape(nb * tt, d).astype(BF16)
    p = jnp.dot(hb, w_ref[...], preferred_element_type=F32)
    us5_ref[...] = jnp.dot(perm_ref[...], p[:, :d_s5].astype(BF16),
                           preferred_element_type=F32).astype(us5_ref.dtype)
    uhy_ref[...] = p[:, d_s5:].astype(BF16).reshape(uhy_ref.shape)


def _in_projection(x, mod3, norm_g, w_uh, perm, d_s5):
    bsz, seq, d = x.shape
    n = w_uh.shape[1]
    tt = _time_tile(bsz, seq)
    return pl.pallas_call(
        functools.partial(_inproj_kernel, d_s5=d_s5),
        out_shape=(
            jax.ShapeDtypeStruct((seq * bsz, d_s5), BF16),
            jax.ShapeDtypeStruct((bsz, seq, n - d_s5), BF16),
        ),
        grid=(seq // tt,),
        in_specs=[
            pl.BlockSpec((bsz, tt, d), lambda j: (0, j, 0)),
            _const_spec(mod3.shape),
            _const_spec((1, d)),
            _const_spec((d, n)),
            _const_spec(perm.shape),
        ],
        out_specs=(
            pl.BlockSpec((tt * bsz, d_s5), lambda j: (j, 0)),
            pl.BlockSpec((bsz, tt, n - d_s5), lambda j: (0, j, 0)),
        ),
        compiler_params=_cparams(1),
        name="in_proj",
    )(x, mod3, norm_g, w_uh, perm)


def _group_of(idx, size):
    if size & (size - 1) == 0:
        return lax.shift_right_logical(idx, jnp.int32(size.bit_length() - 1))
    return idx // size


def _s5_prep_kernel(lre_ref, lim_ref, lstep_ref, bre_ref, bim_ref, cre_ref, cim_ref,
                    are_ref, aim_ref, bd_ref, cd_ref):
    ndir, nsplit, ks, two_nss = bd_ref.shape
    nss = two_nss // 2
    grp = bre_ref.shape[1]
    p = nss // (ks // grp)
    nb = are_ref.shape[1]
    step = jnp.exp(lstep_ref[...])
    lr = lre_ref[...]
    li = lim_ref[...]
    mag = jnp.exp(lr * step)
    ar = mag * jnp.cos(li * step)
    ai = mag * jnp.sin(li * step)
    num = ar - 1.0
    den = lr * lr + li * li
    cr = (num * lr + ai * li) / den
    ci = (ai * lr - num * li) / den

    iota = lambda shape, axis: lax.broadcasted_iota(jnp.int32, shape, axis)
    ch_of_row = iota((ks, grp), 0) - _group_of(iota((ks, grp), 0), grp) * grp
    rep_rows = (ch_of_row == iota((ks, grp), 1)).astype(F32)
    ch_of_lane = iota((grp, ks), 1) - _group_of(iota((grp, ks), 1), grp) * grp
    rep_lanes = (ch_of_lane == iota((grp, ks), 0)).astype(F32)
    in_mask = _group_of(iota((ks, 1), 0), grp) == _group_of(iota((1, nss), 1), p)
    out_mask = _group_of(iota((nss, 1), 0), p) == _group_of(iota((1, ks), 1), grp)
    rep = lambda a, b: jnp.dot(a.astype(BF16), b.astype(BF16), preferred_element_type=F32)

    for d in range(ndir):
        are_ref[d] = jnp.broadcast_to(ar[d:d + 1, :], (nb, ar.shape[1]))
        aim_ref[d] = jnp.broadcast_to(ai[d:d + 1, :], (nb, ai.shape[1]))
        br = bre_ref[d]
        bi = bim_ref[d]
        bb = (cr[d:d + 1, :] * br - ci[d:d + 1, :] * bi, cr[d:d + 1, :] * bi + ci[d:d + 1, :] * br)
        cc = (cre_ref[d], -cim_ref[d])
        for h in range(nsplit):
            st = slice(h * nss, (h + 1) * nss)
            for part in range(2):
                lanes = slice(part * nss, (part + 1) * nss)
                bd_ref[d, h, :, lanes] = jnp.where(in_mask, rep(rep_rows, bb[part][:, st]), 0.0).astype(BF16)
                cd_ref[d, h, lanes, :] = jnp.where(out_mask, rep(cc[part][st, :], rep_lanes), 0.0).astype(BF16)


def _s5_prepare(lam_re, lam_im, log_step, b_re, b_im, c_re, c_im, *, nb, nsplit, p):
    ndir, grp, ns = b_re.shape
    nss = ns // nsplit
    ks = (nss // p) * grp
    outs = (
        jax.ShapeDtypeStruct((ndir, nb, ns), F32),
        jax.ShapeDtypeStruct((ndir, nb, ns), F32),
        jax.ShapeDtypeStruct((ndir, nsplit, ks, 2 * nss), BF16),
        jax.ShapeDtypeStruct((ndir, nsplit, 2 * nss, ks), BF16),
    )
    return pl.pallas_call(
        _s5_prep_kernel,
        out_shape=outs,
        compiler_params=pltpu.CompilerParams(vmem_limit_bytes=VMEM_LIMIT_BYTES),
        name="s5_discretize",
    )(lam_re, lam_im, log_step, b_re, b_im, c_re, c_im)


def _spread(items, n):
    return [items[(len(items) * i) // n:(len(items) * (i + 1)) // n] for i in range(n)]


def _s5_scan_kernel(*refs, reverse, final, tc, nb):
    if final:
        (u_ref, bd_ref, cd_ref, are_ref, aim_ref, yprev_ref, gluw_ref, glub_ref,
         out_ref, bu_ref, sb_ref, y_ref, s_ref) = refs
    else:
        (u_ref, bd_ref, cd_ref, are_ref, aim_ref, d_ref, out_ref, bu_ref, sb_ref, y_ref, s_ref) = refs
    rows = tc * nb
    _, nsplit, ks, two_nss = bd_ref.shape
    nss = two_nss // 2
    lc = min(S5_LANE_CHUNK, nss)
    tw = min(MXU_TILE, lc)
    rh = rows // 2 if rows % (2 * SUBLANE * 2) == 0 else rows
    row_halves = [slice(r, r + rh) for r in range(0, rows, rh)]

    @pl.when(pl.program_id(0) == 0)
    def _():
        s_ref[...] = jnp.zeros_like(s_ref)

    units = [(h, q) for h in range(nsplit) for q in range(nss // lc)]

    def lanes(unit, part, off=0, width=None):
        h, q = unit
        start = h * two_nss + part * nss + q * lc + off
        return slice(start, start + (lc if width is None else width))

    def expand_pieces(unit):
        h, q = unit
        out = []
        for part in range(2):
            for off in range(0, lc, tw):
                for rs in row_halves:
                    def piece(part=part, off=off, rs=rs):
                        col = part * nss + q * lc + off
                        bu_ref[rs, lanes(unit, part, off, tw)] = jnp.dot(
                            u_ref[rs, h * ks:(h + 1) * ks], bd_ref[0, h, :, col:col + tw],
                            preferred_element_type=F32)
                    out.append(piece)
        return out

    def readout_pieces(unit):
        h, q = unit
        out = []
        for rs in row_halves:
            def piece(rs=rs):
                v = None
                for part in range(2):
                    k0 = part * nss + q * lc
                    d = jnp.dot(sb_ref[rs, lanes(unit, part)], cd_ref[0, h, k0:k0 + lc, :],
                                preferred_element_type=F32)
                    v = d if v is None else v + d
                if q == 0:
                    y_ref[rs, h * ks:(h + 1) * ks] = v
                else:
                    y_ref[rs, h * ks:(h + 1) * ks] += v
            out.append(piece)
        return out

    step_groups = _spread(list(range(tc)), max(tc // S5_STEPS_PER_GROUP, 1))
    for p in expand_pieces(units[0]):
        p()
    for c, unit in enumerate(units):
        mxu_work = (expand_pieces(units[c + 1]) if c + 1 < len(units) else []) + \
                   (readout_pieces(units[c - 1]) if c > 0 else [])
        mxu_groups = _spread(mxu_work, len(step_groups))
        a_sl = slice(unit[0] * nss + unit[1] * lc, unit[0] * nss + (unit[1] + 1) * lc)
        a_re = are_ref[0, :, a_sl]
        a_im = aim_ref[0, :, a_sl]
        re_sl, im_sl = lanes(unit, 0), lanes(unit, 1)
        sr = s_ref[:, re_sl]
        si = s_ref[:, im_sl]
        for steps, pieces in zip(step_groups, mxu_groups):
            for p in pieces:
                p()
            for k in steps:
                t = (tc - 1 - k) if reverse else k
                row = slice(t * nb, (t + 1) * nb)
                nr = a_re * sr - a_im * si + bu_ref[row, re_sl]
                ni = a_re * si + a_im * sr + bu_ref[row, im_sl]
                sb_ref[row, re_sl] = nr.astype(BF16)
                sb_ref[row, im_sl] = ni.astype(BF16)
                sr, si = nr, ni
        s_ref[:, re_sl] = sr
        s_ref[:, im_sl] = si
    for p in readout_pieces(units[-1]):
        p()

    rb = min(256, rows)
    for r in range(rows // rb):
        rs = slice(r * rb, (r + 1) * rb)
        if final:
            z = _gelu_tanh(yprev_ref[rs, :] + y_ref[rs, :])
            gate = jnp.dot(z.astype(BF16), gluw_ref[...], preferred_element_type=F32) + glub_ref[...]
            out_ref[rs, :] = (z * jax.nn.sigmoid(gate)).astype(out_ref.dtype)
        else:
            out_ref[rs, :] = y_ref[rs, :] + u_ref[rs, :] * d_ref[...]


def _s5_scan(u_tm, bd, cd, a_re, a_im, extra, *, direction, final, nb, out_dtype):
    rows_total, d_s5 = u_tm.shape
    seq = rows_total // nb
    two_ns = bd.shape[1] * bd.shape[3]
    tc = min(S5_TIME_CHUNK, seq)
    nchunk = seq // tc
    rows = tc * nb
    reverse = direction == 1
    if reverse:
        cidx = lambda i: (nchunk - 1 - i, 0)
    else:
        cidx = lambda i: (i, 0)

    def dir_spec(a):
        nd = a.ndim
        return pl.BlockSpec((1,) + a.shape[1:], lambda i: (direction,) + (0,) * (nd - 1),
                            pipeline_mode=pl.Buffered(1))

    in_specs = [pl.BlockSpec((rows, d_s5), cidx), dir_spec(bd), dir_spec(cd), dir_spec(a_re), dir_spec(a_im)]
    if final:
        yprev, glu_w, glu_b = extra
        in_specs += [pl.BlockSpec((rows, d_s5), cidx), _const_spec(glu_w.shape), _const_spec(glu_b.shape)]
    else:
        in_specs += [_const_spec(extra[0].shape)]
    return pl.pallas_call(
        functools.partial(_s5_scan_kernel, reverse=reverse, final=final, tc=tc, nb=nb),
        out_shape=jax.ShapeDtypeStruct((rows_total, d_s5), out_dtype),
        grid=(nchunk,),
        in_specs=in_specs,
        out_specs=pl.BlockSpec((rows, d_s5), cidx),
        scratch_shapes=[pltpu.VMEM((rows, two_ns), F32), pltpu.VMEM((rows, two_ns), BF16),
                        pltpu.VMEM((rows, d_s5), F32), pltpu.VMEM((nb, two_ns), F32)],
        compiler_params=_cparams(1),
        name="s5_scan_bwd_glu" if final else "s5_scan_fwd",
    )(u_tm, bd, cd, a_re, a_im, *extra)


def _s5_branch(u_tm, nb, lam_re, lam_im, log_step, b_re, b_im, c_re, c_im, d, glu_w, glu_b):
    ndir, g, p = lam_re.shape
    if ndir != 2:
        raise NotImplementedError("S5 branch expects forward and backward directions")
    grp = b_re.shape[-1]
    ns = g * p
    d_s5 = g * grp
    ks = S5_CH_SPLIT if d_s5 % S5_CH_SPLIT == 0 else d_s5
    flat = lambda a: a.reshape(ndir, ns)
    to_lanes = lambda a: jnp.transpose(a, (0, 3, 1, 2)).reshape(ndir, grp, ns)
    to_rows = lambda a: jnp.transpose(a, (0, 1, 3, 2)).reshape(ndir, ns, grp)
    a_re, a_im, bd, cd = _s5_prepare(
        flat(lam_re), flat(lam_im), jnp.repeat(log_step, p, axis=-1), to_lanes(b_re), to_lanes(b_im),
        to_rows(c_re), to_rows(c_im), nb=nb, nsplit=d_s5 // ks, p=p)
    y = _s5_scan(u_tm, bd, cd, a_re, a_im, (d.reshape(1, -1),), direction=0, final=False, nb=nb, out_dtype=F32)
    extra = (y, glu_w.astype(BF16), glu_b.reshape(1, -1))
    return _s5_scan(u_tm, bd, cd, a_re, a_im, extra, direction=1, final=True, nb=nb, out_dtype=BF16)


RSQRT2 = math.sqrt(0.5)


def _dft_table(nf, t, n):
    f_lo = 64 if nf % 64 == 0 else 1
    f_hi = nf // f_lo
    t = t[:, None, :]
    ka = (jnp.arange(f_hi, dtype=jnp.int32)[None, :, None] * f_lo * t) % n
    kb = (jnp.arange(f_lo, dtype=jnp.int32)[None, :, None] * t) % n
    w = 2.0 * math.pi / n
    aa = ka.astype(F32) * w
    ab = kb.astype(F32) * w
    ca, sa, cb, sb = jnp.cos(aa), jnp.sin(aa), jnp.cos(ab), jnp.sin(ab)
    cm = ca[:, :, None, :] * cb[:, None, :, :] - sa[:, :, None, :] * sb[:, None, :, :]
    sm = sa[:, :, None, :] * cb[:, None, :, :] + ca[:, :, None, :] * sb[:, None, :, :]
    shape = (t.shape[0], nf, t.shape[-1])
    return cm.reshape(shape), sm.reshape(shape)


def _hyena_tables(seq):
    q = seq // 4
    times = 4 * jnp.arange(q, dtype=jnp.int32)[None, :] + jnp.arange(4, dtype=jnp.int32)[:, None]
    cf, sf = _dft_table(q, times, 2 * seq)
    alt = (1 - 2 * (jnp.arange(q, dtype=jnp.int32) & 1)).astype(F32)
    f_is_0 = jnp.arange(q, dtype=jnp.int32)[None, :, None] == 0
    sf = jnp.where(f_is_0, alt[None, None, :], sf)
    return dict(c_fwd=cf.astype(BF16), s_fwd=sf.astype(BF16),
                c_inv=jnp.transpose(cf, (0, 2, 1)).astype(BF16), s_inv=jnp.transpose(sf, (0, 2, 1)).astype(BF16))


def _cmul(z, k):
    (zc, zs), (kc, ks) = z, k
    return zc * kc - zs * ks, zc * ks + zs * kc


def _forward_spectrum(x, a):
    (c0, s0), (c1, s1), (c2, s2), (c3, s3) = x
    pc, ps, mc, ms = c0 + c2, s0 + s2, c0 - c2, s0 - s2
    qc, qs, nc, ns = c1 + c3, s1 + s3, c1 - c3, s1 - s3
    za = (pc + qc, ps + qs)
    zb = (pc - qc, qs - ps)
    zc = (mc + ns, nc - ms)
    zd = (mc - ns, ms + nc)
    e = (a[1] - a[3]) * RSQRT2
    o = (a[1] + a[3]) * RSQRT2
    return [za, zb, zc, zd], [(a[0] + e, a[2] + o), (a[0] - e, o - a[2])]


def _inverse_spectrum(y, ymid):
    (ac, as_), (bc, bs), (cc, cs), (dc, ds) = y
    upc, ups, umc, ums = ac + bc, as_ - bs, ac - bc, as_ + bs
    vpc, vps, vmc, vms = dc + cc, ds - cs, dc - cc, ds + cs
    t = [(upc + vpc, ups + vps), (umc + vms, ums - vmc), (upc - vpc, ups - vps), (umc - vms, ums + vmc)]
    (y1c, y1s), (y3c, y3s) = ymid
    sp = [y1c + y3c, ((y1c + y1s) + (y3s - y3c)) * RSQRT2, y1s - y3s, ((y1s - y1c) + (y3c + y3s)) * RSQRT2]
    return t, sp


def _hy_filter_kernel(feat_ref, w1_ref, b1_ref, w2_ref, b2_ref, freq_ref,
                      w3f_ref, b3f_ref, decf_ref, w3b_ref, b3b_ref, decb_ref,
                      cf_ref, sf_ref,
                      kac_ref, kas_ref, kbc_ref, kbs_ref, kcc_ref, kcs_ref, kdc_ref, kds_ref, km_ref,
                      taps_s, rhs_s, h_s):
    seq = feat_ref.shape[0]
    q = seq // 4
    n = 2 * seq
    ct = kac_ref.shape[1]
    nsl = ct // LANE

    @pl.when((pl.program_id(0) == 0) & (pl.program_id(1) == 0))
    def _():
        f = freq_ref[...]
        h1 = jnp.sin(f * (jnp.dot(feat_ref[...], w1_ref[...], precision=HIGHEST,
                                  preferred_element_type=F32) + b1_ref[...]))
        h_s[...] = jnp.sin(f * (jnp.dot(h1, w2_ref[...], precision=HIGHEST,
                                        preferred_element_type=F32) + b2_ref[...]))

    h_hi, h_lo = _split_bf16(h_s[...])
    t01 = feat_ref[:, 0:1]
    row = lax.broadcasted_iota(jnp.int32, (seq, 1), 0)
    dot = lambda a, b: jnp.dot(a, b, preferred_element_type=F32)

    def taps(w3_ref, b3_ref, dec_ref):
        w_hi, w_lo = _split_bf16(w3_ref[...])
        v = dot(h_hi, w_hi) + (dot(h_hi, w_lo) + dot(h_lo, w_hi)) + b3_ref[...]
        return v * jnp.exp(-t01 * jnp.abs(dec_ref[...]))

    fwd = taps(w3f_ref, b3f_ref, decf_ref)
    bwd = jnp.where(row == 0, 0.0, taps(w3b_ref, b3b_ref, decb_ref))
    for d, x in enumerate((fwd, bwd)):
        for j in range(nsl):
            taps_s[d * nsl + j] = x[:, j * LANE:(j + 1) * LANE]
        for r in range(4):
            for j in range(nsl):
                c0 = (4 * d + r) * ct + j * LANE
                rhs_s[:, c0:c0 + LANE] = taps_s[d * nsl + j, pl.ds(r, q, stride=4), :].astype(BF16)
    row0 = lax.broadcasted_iota(jnp.int32, (q, 1), 0) == 0

    def spectrum(d):
        x, a = [], []
        for r in range(4):
            rhs = rhs_s[:, (4 * d + r) * ct:(4 * d + r + 1) * ct]
            c = jnp.dot(cf_ref[r], rhs, preferred_element_type=F32)
            sn = jnp.dot(sf_ref[r], rhs, preferred_element_type=F32)
            a.append(sn[0:1, :])
            x.append((c, jnp.where(row0, 0.0, sn)))
        return _forward_spectrum(x, a)

    fz, fm = spectrum(0)
    bz, bm = spectrum(1)
    scale = jnp.where(row0, 1.0 / n, 2.0 / n)
    outs = ((kac_ref, kas_ref), (kbc_ref, kbs_ref), (kcc_ref, kcs_ref), (kdc_ref, kds_ref))
    for (oc, os_), (fc, fs), (bc, bs) in zip(outs, fz, bz):
        oc[...] = (fc + bc) * scale
        os_[...] = (fs - bs) * scale
    km_ref[...] = jnp.zeros_like(km_ref)
    for i, ((fc, fs), (bc, bs)) in enumerate(zip(fm, bm)):
        km_ref[2 * i:2 * i + 1, :] = (fc + bc) * (2.0 / n)
        km_ref[2 * i + 1:2 * i + 2, :] = (fs - bs) * (2.0 / n)


def _hyena_filters(seq, w1, b1, w2, b2, w3, b3, freq, decay, tables, n_order, n_dirs, d_hy):
    q = seq // 4
    emb, hid = w1.shape
    bands = (emb - 1) // 2
    t = jnp.arange(seq, dtype=F32)
    t01 = t / max(seq - 1, 1)
    band = jnp.linspace(1e-4, bands - 1, bands, dtype=F32)
    ang = (2.0 * math.pi) * t[:, None] * band[None, :] / seq
    feats = jnp.concatenate([t01[:, None], jnp.cos(ang), jnp.sin(ang)], axis=-1)
    kpad = LANE
    feats = jnp.pad(feats, ((0, 0), (0, kpad - emb)))
    w1p = jnp.pad(w1, ((0, kpad - emb), (0, 0)))
    tabs = [tables["c_fwd"], tables["s_fwd"]]
    ct = min(HY_CH_TILE, d_hy)
    nct = d_hy // ct
    ncol = n_order * d_hy
    b3r = b3.reshape(1, -1)
    decr = decay.reshape(1, -1)
    fcol = lambda o, c: (0, (o * n_dirs + 0) * nct + c)
    bcol = lambda o, c: (0, (o * n_dirs + 1) * nct + c)
    ocol = lambda o, c: (0, o * nct + c)
    full = lambda a: pl.BlockSpec(a.shape, lambda o, c: (0,) * a.ndim)
    spec = jax.ShapeDtypeStruct((q, ncol), F32)
    return pl.pallas_call(
        _hy_filter_kernel,
        out_shape=(spec,) * 8 + (jax.ShapeDtypeStruct((8, ncol), F32),),
        grid=(n_order, nct),
        in_specs=[
            full(feats), full(w1p), full(b1.reshape(1, -1)), full(w2), full(b2.reshape(1, -1)),
            full(freq.reshape(1, -1)),
            pl.BlockSpec((hid, ct), fcol), pl.BlockSpec((1, ct), fcol), pl.BlockSpec((1, ct), fcol),
            pl.BlockSpec((hid, ct), bcol), pl.BlockSpec((1, ct), bcol), pl.BlockSpec((1, ct), bcol),
        ] + [_const_spec(a.shape) for a in tabs],
        out_specs=tuple(pl.BlockSpec((q, ct), ocol) for _ in range(8)) + (pl.BlockSpec((8, ct), ocol),),
        scratch_shapes=[pltpu.VMEM((2 * ct // LANE, seq, LANE), F32), pltpu.VMEM((q, 8 * ct), BF16),
                        pltpu.VMEM((seq, hid), F32)],
        compiler_params=_cparams(2),
        name="hyena_filter_spectra",
    )(feats, w1p, b1.reshape(1, -1), w2, b2.reshape(1, -1), freq.reshape(1, -1),
      w3, b3r, decr, w3, b3r, decr, *tabs)


def _hy_conv_kernel(zin_ref, gin_ref, wz_ref, bz_ref, wg_ref, bg_ref,
                    kac_ref, kas_ref, kbc_ref, kbs_ref, kcc_ref, kcs_ref, kdc_ref, kds_ref, km_ref, bias_ref,
                    cf_ref, sf_ref, ci_ref, si_ref, out_ref,
                    zraw_s, graw_s, o_s, z_s, g_s, rhs_s, ac_s, as_s, *, conv_on_z, seq):
    ct = gin_ref.shape[1]
    nbat = gin_ref.shape[0] // seq
    q = seq // 4
    nsl = ct // LANE
    pad = zraw_s.shape[2] - seq
    top = pad // 2
    rc = min(DFT_ROW_CHUNK, q)
    k_refs = ((kac_ref, kas_ref), (kbc_ref, kbs_ref), (kcc_ref, kcs_ref), (kdc_ref, kds_ref))

    def stage(raw_s, src_ref, k):
        for j in range(nsl):
            raw_s[k, j, 0:top, :] = jnp.zeros((top, LANE), F32)
            raw_s[k, j, top + seq:, :] = jnp.zeros((pad - top, LANE), F32)
            raw_s[k, j, top:top + seq, :] = src_ref[k * seq:(k + 1) * seq, j * LANE:(j + 1) * LANE].astype(F32)

    def split_rows(raw_s, dst_ref, k, w_ref, b_ref):
        for r in range(4):
            for j in range(nsl):
                ls = slice(j * LANE, (j + 1) * LANE)
                tap = lambda d: raw_s[k, j, pl.ds(top + r + d, q, stride=4), :]
                dst_ref[k, :, r * ct + j * LANE:r * ct + (j + 1) * LANE] = (
                    b_ref[:, ls] + tap(-1) * w_ref[0:1, ls] + tap(0) * w_ref[1:2, ls] + tap(1) * w_ref[2:3, ls])

    def spectrum_chunk(k, i):
        r = slice(i * rc, (i + 1) * rc)
        row0 = lax.broadcasted_iota(jnp.int32, (rc, 1), 0) == 0
        x, a = [], []
        for p in range(4):
            rhs = rhs_s[k, :, p * ct:(p + 1) * ct]
            c = jnp.dot(cf_ref[p, r, :], rhs, preferred_element_type=F32)
            sn = jnp.dot(sf_ref[p, r, :], rhs, preferred_element_type=F32)
            if i == 0:
                a.append(sn[0:1, :])
                sn = jnp.where(row0, 0.0, sn)
            else:
                a.append(jnp.zeros((1, ct), F32))
            x.append((c, sn))
        z, zmid = _forward_spectrum(x, a)
        y = [_cmul(zx, (kc[r, :], ks[r, :])) for zx, (kc, ks) in zip(z, k_refs)]
        ymid = [_cmul(zmid[m], (km_ref[2 * m:2 * m + 1, :], km_ref[2 * m + 1:2 * m + 2, :])) for m in range(2)]
        t, sp = _inverse_spectrum(y, ymid)
        for p, ((tc, ts), spp) in enumerate(zip(t, sp)):
            if i == 0:
                ts = jnp.where(row0, spp, ts)
            ac_s[k, r, p * ct:(p + 1) * ct] = tc.astype(BF16)
            as_s[k, r, p * ct:(p + 1) * ct] = ts.astype(BF16)

    def output_chunk(k, i):
        r = slice(i * rc, (i + 1) * rc)
        for p in range(4):
            ps = slice(p * ct, (p + 1) * ct)
            y = jnp.dot(ci_ref[p, r, :], ac_s[k, :, ps], preferred_element_type=F32)
            y = y + jnp.dot(si_ref[p, r, :], as_s[k, :, ps], preferred_element_type=F32)
            rows = pl.ds(4 * i * rc + p, rc, stride=4)
            if conv_on_z:
                out_ref[k * q + i * rc:k * q + (i + 1) * rc, ps] = (
                    g_s[k, r, ps] * (y + z_s[k, r, ps] * bias_ref[...])).astype(out_ref.dtype)
                continue
            for j in range(nsl):
                ls = slice(p * ct + j * LANE, p * ct + (j + 1) * LANE)
                jl = slice(j * LANE, (j + 1) * LANE)
                o_s[k, j, rows, :] = g_s[k, r, ls] * (y[:, jl] + z_s[k, r, ls] * bias_ref[:, jl])

    for k in range(nbat):
        stage(graw_s, gin_ref, k)
        split_rows(graw_s, g_s, k, wg_ref, bg_ref)
        if conv_on_z:
            stage(zraw_s, zin_ref, k)
            split_rows(zraw_s, z_s, k, wz_ref, bz_ref)
            rhs_s[k] = z_s[k].astype(BF16)
        else:
            rhs_s[k] = zin_ref[k * q:(k + 1) * q, :]
            z_s[k] = zin_ref[k * q:(k + 1) * q, :].astype(F32)
    for i in range(q // rc):
        for k in range(nbat):
            spectrum_chunk(k, i)
    for i in range(q // rc):
        for k in range(nbat):
            output_chunk(k, i)
    if not conv_on_z:
        for k in range(nbat):
            for j in range(nsl):
                out_ref[k * seq:(k + 1) * seq, j * LANE:(j + 1) * LANE] = o_s[k, j].astype(out_ref.dtype)


def _hy_conv(zin, zcol0, gcol0, u_hy, conv_w, conv_b, spectra, bias_row, tables,
             *, order, conv_on_z, bsz, seq, d_hy):
    tabs = [tables[k] for k in ("c_fwd", "s_fwd", "c_inv", "s_inv")]
    q = seq // 4
    ct = min(HY_CH_TILE, d_hy)
    nsl = ct // LANE
    nct = d_hy // ct
    nbat = HY_BATCH_GROUP if bsz % HY_BATCH_GROUP == 0 else 1
    zc0 = zcol0 // ct
    gc0 = gcol0 // ct
    zw0 = zc0 if conv_on_z else 0
    kspec = lambda rows: pl.BlockSpec((rows, ct), lambda c, b: (0, order * nct + c),
                                      pipeline_mode=pl.Buffered(1))
    scratch = lambda dt: pltpu.VMEM((nbat, q, 4 * ct), dt)
    slabs = lambda rows: pltpu.VMEM((nbat, nsl, rows, LANE), F32)
    time_block = pl.BlockSpec((nbat * seq, ct), lambda c, b: (b, c))
    split_block = pl.BlockSpec((nbat * q, 4 * ct), lambda c, b: (b, c))
    return pl.pallas_call(
        functools.partial(_hy_conv_kernel, conv_on_z=conv_on_z, seq=seq),
        out_shape=jax.ShapeDtypeStruct((bsz * q, 4 * d_hy) if conv_on_z else (bsz * seq, d_hy), BF16),
        grid=(nct, bsz // nbat),
        in_specs=[
            pl.BlockSpec((nbat * seq, ct), lambda c, b: (b, zc0 + c)) if conv_on_z else split_block,
            pl.BlockSpec((nbat * seq, ct), lambda c, b: (b, gc0 + c)),
            pl.BlockSpec((conv_w.shape[0], ct), lambda c, b: (0, zw0 + c)),
            pl.BlockSpec((1, ct), lambda c, b: (0, zw0 + c)),
            pl.BlockSpec((conv_w.shape[0], ct), lambda c, b: (0, gc0 + c)),
            pl.BlockSpec((1, ct), lambda c, b: (0, gc0 + c)),
        ] + [kspec(q)] * 8 + [kspec(8), kspec(1)] + [_const_spec(a.shape) for a in tabs],
        out_specs=split_block if conv_on_z else time_block,
        scratch_shapes=[
            slabs(seq + 2 * SUBLANE), slabs(seq + 2 * SUBLANE), slabs(seq),
            scratch(F32), scratch(F32), scratch(BF16), scratch(BF16), scratch(BF16),
        ],
        compiler_params=_cparams(2),
        name=f"hyena_conv_order{order}",
    )(zin, u_hy, conv_w, conv_b, conv_w, conv_b, *spectra, bias_row, *tabs)


def _hyena_branch(u_hy, bsz, seq, conv_w, conv_b, w1, b1, w2, b2, w3, b3, freq, decay, bias):
    n_order, d_hy = bias.shape
    n_dirs = w3.shape[1] // (n_order * d_hy)
    if n_order != 2 or seq % 8 != 0 or d_hy % LANE != 0:
        raise NotImplementedError("Hyena branch: two long convolutions, L % 8 == 0, 128-lane channel tiles")
    tables = _hyena_tables(seq)
    spectra = _hyena_filters(seq, w1, b1, w2, b2, w3, b3, freq, decay, tables, n_order, n_dirs, d_hy)
    cb = conv_b.reshape(1, -1)
    bias_row = bias.reshape(1, -1)
    common = dict(bsz=bsz, seq=seq, d_hy=d_hy)
    z1 = _hy_conv(u_hy, 0, d_hy, u_hy, conv_w, cb, spectra, bias_row, tables,
                  order=0, conv_on_z=True, **common)
    return _hy_conv(z1, 0, 2 * d_hy, u_hy, conv_w, cb, spectra, bias_row, tables,
                    order=1, conv_on_z=False, **common)


def _mixer_kernel(x_ref, mod_ref, g1_ref, g2_ref, fg_ref, permt_ref, za_ref, zb_ref,
                  wgate_ref, wa_ref, wb_ref, wout_ref, wg_ref, wu_ref, wd_ref, o_ref, *, n_chunks):
    nb, tt, d = x_ref.shape
    rows = nb * tt
    mod = lambda k: mod_ref[:, k:k + 1, :]
    x = x_ref[...]
    h = _norm_modulate(x, g1_ref[...], mod(0), mod(1)).reshape(rows, d).astype(BF16)
    gate = jax.nn.sigmoid(jnp.dot(h, wgate_ref[...], preferred_element_type=F32))
    za = jnp.dot(permt_ref[...], za_ref[...], preferred_element_type=F32).astype(BF16)
    ya = jnp.dot(za, wa_ref[...], preferred_element_type=F32)
    yb = jnp.dot(zb_ref[...].reshape(rows, zb_ref.shape[-1]), wb_ref[...], preferred_element_type=F32)
    merged = gate[:, :d] * ya + gate[:, d:] * yb
    o = jnp.dot(merged.astype(BF16), wout_ref[...], preferred_element_type=F32)
    x1 = x + mod(2) * o.reshape(nb, tt, d)

    h2 = _norm_modulate(x1, g2_ref[...], mod(3), mod(4)).reshape(rows, d).astype(BF16)
    d_ff = wg_ref.shape[1]
    unit = MXU_TILE if d_ff % MXU_TILE == 0 else d_ff // n_chunks
    edges = [len(g) for g in _spread(list(range(d_ff // unit)), n_chunks)]
    acc = jnp.zeros((rows, d), F32)
    start = 0
    for width in edges:
        sl = slice(start * unit, (start + width) * unit)
        start += width
        gl = jnp.dot(h2, wg_ref[:, sl], preferred_element_type=F32)
        up = jnp.dot(h2, wu_ref[:, sl], preferred_element_type=F32)
        act = (gl * jax.nn.sigmoid(gl) * up).astype(BF16)
        acc = acc + jnp.dot(act, wd_ref[sl, :], preferred_element_type=F32)
    x2 = x1 + mod(5) * acc.reshape(nb, tt, d)
    ms = jnp.mean(x2 * x2, axis=-1, keepdims=True)
    o_ref[...] = x2 * lax.rsqrt(ms + EPS) * fg_ref[...]


def _mixer(x, mod3, norm1_g, norm2_g, final_g, perm_t, za_tm, zb, w_gate, w_a, w_b, w_out, w_g, w_u, w_d):
    bsz, seq, d = x.shape
    d_s5 = w_a.shape[0]
    d_hy = w_b.shape[0]
    d_ff = w_g.shape[1]
    tt = _time_tile(bsz, seq)
    n_chunks = 2 if d_ff % (2 * LANE) == 0 else 1
    consts = (mod3, norm1_g, norm2_g, final_g, perm_t)
    weights = (w_gate, w_a, w_b, w_out, w_g, w_u, w_d)
    return pl.pallas_call(
        functools.partial(_mixer_kernel, n_chunks=n_chunks),
        out_shape=jax.ShapeDtypeStruct(x.shape, F32),
        grid=(seq // tt,),
        in_specs=[pl.BlockSpec((bsz, tt, d), lambda j: (0, j, 0))]
        + [_const_spec(a.shape) for a in consts]
        + [pl.BlockSpec((tt * bsz, d_s5), lambda j: (j, 0)),
           pl.BlockSpec((bsz, tt, d_hy), lambda j: (0, j, 0))]
        + [_const_spec(a.shape) for a in weights],
        out_specs=pl.BlockSpec((bsz, tt, d), lambda j: (0, j, 0)),
        compiler_params=_cparams(1),
        name="merge_swiglu_final_norm",
    )(x, *consts, za_tm, zb, *weights)


def kernel(x, c, ada_w, ada_b, norm1_g, norm2_g, w_in, s5_lam_re, s5_lam_im, s5_log_step, s5_b_re, s5_b_im, s5_c_re, s5_c_im, s5_d, s5_glu_w, s5_glu_b, hy_conv_w, hy_conv_b, hy_ffn_w1, hy_ffn_b1, hy_ffn_w2, hy_ffn_b2, hy_ffn_w3, hy_ffn_b3, hy_freq, hy_decay, hy_bias, w_branch_a, w_branch_b, w_out, ffn_w_gu, ffn_w_down, final_g):
    bsz, seq, d = x.shape
    depth = ada_w.shape[0]
    if depth != 1:
        raise NotImplementedError("the final RMSNorm is fused into the (single) layer's channel mixer")
    d_s5 = s5_d.shape[-1]
    n_order, d_hy = hy_bias.shape[1:]
    d_uh = d_s5 + (n_order + 1) * d_hy
    d_ff = ffn_w_down.shape[1]
    i = 0
    perm = _row_permutation(bsz, _time_tile(bsz, seq))
    mod = _modulation(c, ada_w[i], ada_b[i]).reshape(bsz, 6, d)
    w_in_b = w_in[i].astype(BF16)
    u_s5, u_hy = _in_projection(x, mod, norm1_g[i].reshape(1, d), w_in_b[:, :d_uh], perm, d_s5)
    z_a = _s5_branch(u_s5, bsz, s5_lam_re[i], s5_lam_im[i], s5_log_step[i], s5_b_re[i], s5_b_im[i],
                     s5_c_re[i], s5_c_im[i], s5_d[i], s5_glu_w[i], s5_glu_b[i])
    z_b = _hyena_branch(u_hy.reshape(bsz * seq, -1), bsz, seq, hy_conv_w[i], hy_conv_b[i], hy_ffn_w1[i],
                        hy_ffn_b1[i], hy_ffn_w2[i], hy_ffn_b2[i], hy_ffn_w3[i], hy_ffn_b3[i], hy_freq[i],
                        hy_decay[i], hy_bias[i])
    w_gu = ffn_w_gu[i].astype(BF16)
    return _mixer(x, mod, norm1_g[i].reshape(1, d), norm2_g[i].reshape(1, d), final_g.reshape(1, d),
                  perm.T, z_a, z_b.reshape(bsz, seq, d_hy), w_in_b[:, d_uh:],
                  w_branch_a[i].astype(BF16), w_branch_b[i].astype(BF16), w_out[i].astype(BF16),
                  w_gu[:, :d_ff], w_gu[:, d_ff:], ffn_w_down[i].astype(BF16))
```

```python
import functools
import math

import jax
import jax.numpy as jnp
from jax import lax
from jax.experimental import pallas as pl
from jax.experimental.pallas import tpu as pltpu

F32 = jnp.float32
BF16 = jnp.bfloat16
EPS = 1e-6
HIGHEST = lax.Precision.HIGHEST

V7X_VMEM_BYTES = 64 * 1024 * 1024
VMEM_LIMIT_BYTES = 56 * 1024 * 1024
LANE = 128
SUBLANE = 8
ROW_TILE = 512
S5_TIME_CHUNK = 64
S5_LANE_CHUNK = 512
S5_CH_SPLIT = 256
S5_STEPS_PER_GROUP = 8
MXU_TILE = 256
HY_CH_TILE = 256
HY_BATCH_GROUP = 2
DFT_ROW_CHUNK = 512


def _cparams(n_axes):
    return pltpu.CompilerParams(
        dimension_semantics=("arbitrary",) * n_axes,
        vmem_limit_bytes=VMEM_LIMIT_BYTES,
    )


def _const_spec(shape):
    nd = len(shape)
    return pl.BlockSpec(shape, lambda *_: (0,) * nd, pipeline_mode=pl.Buffered(1))


def _gelu_tanh(x):
    return 0.5 * x * (1.0 + jnp.tanh(math.sqrt(2.0 / math.pi) * (x + 0.044715 * (x * x * x))))


def _norm_modulate(x, g, shift, scale):
    ms = jnp.mean(x * x, axis=-1, keepdims=True)
    r = x * lax.rsqrt(ms + EPS) * g
    return r * (1.0 + scale) + shift


def _split_bf16(x):
    hi = x.astype(BF16)
    return hi, (x - hi.astype(F32)).astype(BF16)


def _mod_kernel(c_ref, w_ref, b_ref, o_ref):
    c = c_ref[...]
    c_hi, c_lo = _split_bf16(c * jax.nn.sigmoid(c))
    w_hi, w_lo = _split_bf16(w_ref[...])
    dot = lambda a, b: jnp.dot(a, b, preferred_element_type=F32)
    o_ref[...] = dot(c_hi, w_hi) + (dot(c_hi, w_lo) + dot(c_lo, w_hi)) + b_ref[...]


def _modulation(c, ada_w, ada_b):
    bsz, d = c.shape
    n = ada_w.shape[1]
    tn = 1024 if n % 1024 == 0 else 512
    return pl.pallas_call(
        _mod_kernel,
        out_shape=jax.ShapeDtypeStruct((bsz, n), F32),
        grid=(n // tn,),
        in_specs=[
            pl.BlockSpec((bsz, d), lambda j: (0, 0)),
            pl.BlockSpec((d, tn), lambda j: (0, j)),
            pl.BlockSpec((1, tn), lambda j: (0, j)),
        ],
        out_specs=pl.BlockSpec((bsz, tn), lambda j: (0, j)),
        compiler_params=_cparams(1),
        name="adaln_mod",
    )(c, ada_w, ada_b.reshape(1, n))


def _time_tile(bsz, seq):
    return max(min(ROW_TILE // bsz, seq), 1)


def _row_permutation(bsz, tt):
    r = jnp.arange(bsz * tt, dtype=jnp.int32)
    src = (r % bsz) * tt + r // bsz
    return (src[:, None] == r[None, :]).astype(BF16)


def _inproj_kernel(x_ref, mod_ref, g_ref, w_ref, perm_ref, us5_ref, uhy_ref, *, d_s5):
    nb, tt, d = x_ref.shape
    h = _norm_modulate(x_ref[...], g_ref[...], mod_ref[:, 0:1, :], mod_ref[:, 1:2, :])
    hb = h.reshape(nb * tt, d).astype(BF16)
    p = jnp.dot(hb, w_ref[...], preferred_element_type=F32)
    us5_ref[...] = jnp.dot(perm_ref[...], p[:, :d_s5].astype(BF16),
                           preferred_element_type=F32).astype(us5_ref.dtype)
    uhy_ref[...] = p[:, d_s5:].astype(BF16).reshape(uhy_ref.shape)


def _col_block_spec(rows, width, index):
    return pl.BlockSpec((rows, width), lambda *_: (0, index), pipeline_mode=pl.Buffered(1))


def _in_projection(x, mod3, norm_g, w_in, perm, d_s5, n):
    bsz, seq, d = x.shape
    tt = _time_tile(bsz, seq)
    return pl.pallas_call(
        functools.partial(_inproj_kernel, d_s5=d_s5),
        out_shape=(
            jax.ShapeDtypeStruct((seq * bsz, d_s5), BF16),
            jax.ShapeDtypeStruct((bsz, seq, n - d_s5), BF16),
        ),
        grid=(seq // tt,),
        in_specs=[
            pl.BlockSpec((bsz, tt, d), lambda j: (0, j, 0)),
            _const_spec(mod3.shape),
            _const_spec((1, d)),
            _col_block_spec(d, n, 0),
            _const_spec(perm.shape),
        ],
        out_specs=(
            pl.BlockSpec((tt * bsz, d_s5), lambda j: (j, 0)),
            pl.BlockSpec((bsz, tt, n - d_s5), lambda j: (0, j, 0)),
        ),
        compiler_params=_cparams(1),
        name="in_proj",
    )(x, mod3, norm_g, w_in, perm)


def _group_of(idx, size):
    if size & (size - 1) == 0:
        return lax.shift_right_logical(idx, jnp.int32(size.bit_length() - 1))
    return idx // size


def _s5_prep_kernel(lre_ref, lim_ref, lstep_ref, bre_ref, bim_ref, cre_ref, cim_ref,
                    are_ref, aim_ref, bd_ref, cd_ref):
    ndir, nsplit, ks, two_nss = bd_ref.shape
    nss = two_nss // 2
    grp = bre_ref.shape[1]
    p = nss // (ks // grp)
    nb = are_ref.shape[1]
    step = jnp.exp(lstep_ref[...])
    lr = lre_ref[...]
    li = lim_ref[...]
    mag = jnp.exp(lr * step)
    ar = mag * jnp.cos(li * step)
    ai = mag * jnp.sin(li * step)
    num = ar - 1.0
    den = lr * lr + li * li
    cr = (num * lr + ai * li) / den
    ci = (ai * lr - num * li) / den

    iota = lambda shape, axis: lax.broadcasted_iota(jnp.int32, shape, axis)
    ch_of_row = iota((ks, grp), 0) - _group_of(iota((ks, grp), 0), grp) * grp
    rep_rows = (ch_of_row == iota((ks, grp), 1)).astype(F32)
    ch_of_lane = iota((grp, ks), 1) - _group_of(iota((grp, ks), 1), grp) * grp
    rep_lanes = (ch_of_lane == iota((grp, ks), 0)).astype(F32)
    in_mask = _group_of(iota((ks, 1), 0), grp) == _group_of(iota((1, nss), 1), p)
    out_mask = _group_of(iota((nss, 1), 0), p) == _group_of(iota((1, ks), 1), grp)
    rep = lambda a, b: jnp.dot(a.astype(BF16), b.astype(BF16), preferred_element_type=F32)

    for d in range(ndir):
        are_ref[d] = jnp.broadcast_to(ar[d:d + 1, :], (nb, ar.shape[1]))
        aim_ref[d] = jnp.broadcast_to(ai[d:d + 1, :], (nb, ai.shape[1]))
        br = bre_ref[d]
        bi = bim_ref[d]
        bb = (cr[d:d + 1, :] * br - ci[d:d + 1, :] * bi, cr[d:d + 1, :] * bi + ci[d:d + 1, :] * br)
        cc = (cre_ref[d], -cim_ref[d])
        for h in range(nsplit):
            st = slice(h * nss, (h + 1) * nss)
            for part in range(2):
                lanes = slice(part * nss, (part + 1) * nss)
                bd_ref[d, h, :, lanes] = jnp.where(in_mask, rep(rep_rows, bb[part][:, st]), 0.0).astype(BF16)
                cd_ref[d, h, lanes, :] = jnp.where(out_mask, rep(cc[part][st, :], rep_lanes), 0.0).astype(BF16)


def _s5_prepare(lam_re, lam_im, log_step, b_re, b_im, c_re, c_im, *, nb, nsplit, p):
    ndir, grp, ns = b_re.shape
    nss = ns // nsplit
    ks = (nss // p) * grp
    outs = (
        jax.ShapeDtypeStruct((ndir, nb, ns), F32),
        jax.ShapeDtypeStruct((ndir, nb, ns), F32),
        jax.ShapeDtypeStruct((ndir, nsplit, ks, 2 * nss), BF16),
        jax.ShapeDtypeStruct((ndir, nsplit, 2 * nss, ks), BF16),
    )
    return pl.pallas_call(
        _s5_prep_kernel,
        out_shape=outs,
        compiler_params=pltpu.CompilerParams(vmem_limit_bytes=VMEM_LIMIT_BYTES),
        name="s5_discretize",
    )(lam_re, lam_im, log_step, b_re, b_im, c_re, c_im)


def _spread(items, n):
    return [items[(len(items) * i) // n:(len(items) * (i + 1)) // n] for i in range(n)]


def _s5_scan_kernel(*refs, reverse, final, tc, nb):
    if final:
        (u_ref, bd_ref, cd_ref, are_ref, aim_ref, yprev_ref, gluw_ref, glub_ref,
         out_ref, bu_ref, sb_ref, y_ref, s_ref) = refs
    else:
        (u_ref, bd_ref, cd_ref, are_ref, aim_ref, d_ref, out_ref, bu_ref, sb_ref, y_ref, s_ref) = refs
    rows = tc * nb
    _, nsplit, ks, two_nss = bd_ref.shape
    nss = two_nss // 2
    lc = min(S5_LANE_CHUNK, nss)
    tw = min(MXU_TILE, lc)
    rh = rows // 2 if rows % (2 * SUBLANE * 2) == 0 else rows
    row_halves = [slice(r, r + rh) for r in range(0, rows, rh)]

    @pl.when(pl.program_id(0) == 0)
    def _():
        s_ref[...] = jnp.zeros_like(s_ref)

    units = [(h, q) for h in range(nsplit) for q in range(nss // lc)]

    def lanes(unit, part, off=0, width=None):
        h, q = unit
        start = h * two_nss + part * nss + q * lc + off
        return slice(start, start + (lc if width is None else width))

    def expand_pieces(unit):
        h, q = unit
        out = []
        for part in range(2):
            for off in range(0, lc, tw):
                for rs in row_halves:
                    def piece(part=part, off=off, rs=rs):
                        col = part * nss + q * lc + off
                        bu_ref[rs, lanes(unit, part, off, tw)] = jnp.dot(
                            u_ref[rs, h * ks:(h + 1) * ks], bd_ref[0, h, :, col:col + tw],
                            preferred_element_type=F32)
                    out.append(piece)
        return out

    def readout_pieces(unit):
        h, q = unit
        out = []
        for rs in row_halves:
            def piece(rs=rs):
                v = None
                for part in range(2):
                    k0 = part * nss + q * lc
                    d = jnp.dot(sb_ref[rs, lanes(unit, part)], cd_ref[0, h, k0:k0 + lc, :],
                                preferred_element_type=F32)
                    v = d if v is None else v + d
                if q == 0:
                    y_ref[rs, h * ks:(h + 1) * ks] = v
                else:
                    y_ref[rs, h * ks:(h + 1) * ks] += v
            out.append(piece)
        return out

    step_groups = _spread(list(range(tc)), max(tc // S5_STEPS_PER_GROUP, 1))
    for p in expand_pieces(units[0]):
        p()
    for c, unit in enumerate(units):
        mxu_work = (expand_pieces(units[c + 1]) if c + 1 < len(units) else []) + \
                   (readout_pieces(units[c - 1]) if c > 0 else [])
        mxu_groups = _spread(mxu_work, len(step_groups))
        a_sl = slice(unit[0] * nss + unit[1] * lc, unit[0] * nss + (unit[1] + 1) * lc)
        a_re = are_ref[0, :, a_sl]
        a_im = aim_ref[0, :, a_sl]
        re_sl, im_sl = lanes(unit, 0), lanes(unit, 1)
        sr = s_ref[:, re_sl]
        si = s_ref[:, im_sl]
        for steps, pieces in zip(step_groups, mxu_groups):
            for p in pieces:
                p()
            for k in steps:
                t = (tc - 1 - k) if reverse else k
                row = slice(t * nb, (t + 1) * nb)
                nr = a_re * sr - a_im * si + bu_ref[row, re_sl]
                ni = a_re * si + a_im * sr + bu_ref[row, im_sl]
                sb_ref[row, re_sl] = nr.astype(BF16)
                sb_ref[row, im_sl] = ni.astype(BF16)
                sr, si = nr, ni
        s_ref[:, re_sl] = sr
        s_ref[:, im_sl] = si
    for p in readout_pieces(units[-1]):
        p()

    rb = min(256, rows)
    for r in range(rows // rb):
        rs = slice(r * rb, (r + 1) * rb)
        if final:
            z = _gelu_tanh(yprev_ref[rs, :] + y_ref[rs, :])
            gate = jnp.dot(z.astype(BF16), gluw_ref[...], preferred_element_type=F32) + glub_ref[...]
            out_ref[rs, :] = (z * jax.nn.sigmoid(gate)).astype(out_ref.dtype)
        else:
            out_ref[rs, :] = y_ref[rs, :] + u_ref[rs, :] * d_ref[...]


def _s5_scan(u_tm, bd, cd, a_re, a_im, extra, *, direction, final, nb, out_dtype):
    rows_total, d_s5 = u_tm.shape
    seq = rows_total // nb
    two_ns = bd.shape[1] * bd.shape[3]
    tc = min(S5_TIME_CHUNK, seq)
    nchunk = seq // tc
    rows = tc * nb
    reverse = direction == 1
    if reverse:
        cidx = lambda i: (nchunk - 1 - i, 0)
    else:
        cidx = lambda i: (i, 0)

    def dir_spec(a):
        nd = a.ndim
        return pl.BlockSpec((1,) + a.shape[1:], lambda i: (direction,) + (0,) * (nd - 1),
                            pipeline_mode=pl.Buffered(1))

    in_specs = [pl.BlockSpec((rows, d_s5), cidx), dir_spec(bd), dir_spec(cd), dir_spec(a_re), dir_spec(a_im)]
    if final:
        yprev, glu_w, glu_b = extra
        in_specs += [pl.BlockSpec((rows, d_s5), cidx), _const_spec(glu_w.shape), _const_spec(glu_b.shape)]
    else:
        in_specs += [_const_spec(extra[0].shape)]
    return pl.pallas_call(
        functools.partial(_s5_scan_kernel, reverse=reverse, final=final, tc=tc, nb=nb),
        out_shape=jax.ShapeDtypeStruct((rows_total, d_s5), out_dtype),
        grid=(nchunk,),
        in_specs=in_specs,
        out_specs=pl.BlockSpec((rows, d_s5), cidx),
        scratch_shapes=[pltpu.VMEM((rows, two_ns), F32), pltpu.VMEM((rows, two_ns), BF16),
                        pltpu.VMEM((rows, d_s5), F32), pltpu.VMEM((nb, two_ns), F32)],
        compiler_params=_cparams(1),
        name="s5_scan_bwd_glu" if final else "s5_scan_fwd",
    )(u_tm, bd, cd, a_re, a_im, *extra)


def _s5_branch(u_tm, nb, lam_re, lam_im, log_step, b_re, b_im, c_re, c_im, d, glu_w, glu_b):
    ndir, g, p = lam_re.shape
    if ndir != 2:
        raise NotImplementedError("S5 branch expects forward and backward directions")
    grp = b_re.shape[-1]
    ns = g * p
    d_s5 = g * grp
    ks = S5_CH_SPLIT if d_s5 % S5_CH_SPLIT == 0 else d_s5
    flat = lambda a: a.reshape(ndir, ns)
    to_lanes = lambda a: jnp.transpose(a, (0, 3, 1, 2)).reshape(ndir, grp, ns)
    to_rows = lambda a: jnp.transpose(a, (0, 1, 3, 2)).reshape(ndir, ns, grp)
    a_re, a_im, bd, cd = _s5_prepare(
        flat(lam_re), flat(lam_im), jnp.repeat(log_step, p, axis=-1), to_lanes(b_re), to_lanes(b_im),
        to_rows(c_re), to_rows(c_im), nb=nb, nsplit=d_s5 // ks, p=p)
    y = _s5_scan(u_tm, bd, cd, a_re, a_im, (d.reshape(1, -1),), direction=0, final=False, nb=nb, out_dtype=F32)
    extra = (y, glu_w.astype(BF16), glu_b.reshape(1, -1))
    return _s5_scan(u_tm, bd, cd, a_re, a_im, extra, direction=1, final=True, nb=nb, out_dtype=BF16)


RSQRT2 = math.sqrt(0.5)


def _dft_table(nf, t, n):
    f_lo = 64 if nf % 64 == 0 else 1
    f_hi = nf // f_lo
    t = t[:, None, :]
    ka = (jnp.arange(f_hi, dtype=jnp.int32)[None, :, None] * f_lo * t) % n
    kb = (jnp.arange(f_lo, dtype=jnp.int32)[None, :, None] * t) % n
    w = 2.0 * math.pi / n
    aa = ka.astype(F32) * w
    ab = kb.astype(F32) * w
    ca, sa, cb, sb = jnp.cos(aa), jnp.sin(aa), jnp.cos(ab), jnp.sin(ab)
    cm = ca[:, :, None, :] * cb[:, None, :, :] - sa[:, :, None, :] * sb[:, None, :, :]
    sm = sa[:, :, None, :] * cb[:, None, :, :] + ca[:, :, None, :] * sb[:, None, :, :]
    shape = (t.shape[0], nf, t.shape[-1])
    return cm.reshape(shape), sm.reshape(shape)


def _hyena_tables(seq):
    q = seq // 4
    times = 4 * jnp.arange(q, dtype=jnp.int32)[None, :] + jnp.arange(4, dtype=jnp.int32)[:, None]
    cf, sf = _dft_table(q, times, 2 * seq)
    alt = (1 - 2 * (jnp.arange(q, dtype=jnp.int32) & 1)).astype(F32)
    f_is_0 = jnp.arange(q, dtype=jnp.int32)[None, :, None] == 0
    sf = jnp.where(f_is_0, alt[None, None, :], sf)
    return dict(c_fwd=cf.astype(BF16), s_fwd=sf.astype(BF16),
                c_inv=jnp.transpose(cf, (0, 2, 1)).astype(BF16), s_inv=jnp.transpose(sf, (0, 2, 1)).astype(BF16))


def _cmul(z, k):
    (zc, zs), (kc, ks) = z, k
    return zc * kc - zs * ks, zc * ks + zs * kc


def _forward_spectrum(x, a):
    (c0, s0), (c1, s1), (c2, s2), (c3, s3) = x
    pc, ps, mc, ms = c0 + c2, s0 + s2, c0 - c2, s0 - s2
    qc, qs, nc, ns = c1 + c3, s1 + s3, c1 - c3, s1 - s3
    za = (pc + qc, ps + qs)
    zb = (pc - qc, qs - ps)
    zc = (mc + ns, nc - ms)
    zd = (mc - ns, ms + nc)
    e = (a[1] - a[3]) * RSQRT2
    o = (a[1] + a[3]) * RSQRT2
    return [za, zb, zc, zd], [(a[0] + e, a[2] + o), (a[0] - e, o - a[2])]


def _inverse_spectrum(y, ymid):
    (ac, as_), (bc, bs), (cc, cs), (dc, ds) = y
    upc, ups, umc, ums = ac + bc, as_ - bs, ac - bc, as_ + bs
    vpc, vps, vmc, vms = dc + cc, ds - cs, dc - cc, ds + cs
    t = [(upc + vpc, ups + vps), (umc + vms, ums - vmc), (upc - vpc, ups - vps), (umc - vms, ums + vmc)]
    (y1c, y1s), (y3c, y3s) = ymid
    sp = [y1c + y3c, ((y1c + y1s) + (y3s - y3c)) * RSQRT2, y1s - y3s, ((y1s - y1c) + (y3c + y3s)) * RSQRT2]
    return t, sp


def _hy_filter_kernel(feat_ref, w1_ref, b1_ref, w2_ref, b2_ref, freq_ref,
                      w3f_ref, b3f_ref, decf_ref, w3b_ref, b3b_ref, decb_ref,
                      cf_ref, sf_ref,
                      kac_ref, kas_ref, kbc_ref, kbs_ref, kcc_ref, kcs_ref, kdc_ref, kds_ref, km_ref,
                      taps_s, rhs_s, h_s):
    seq = feat_ref.shape[0]
    q = seq // 4
    n = 2 * seq
    ct = kac_ref.shape[1]
    nsl = ct // LANE

    @pl.when((pl.program_id(0) == 0) & (pl.program_id(1) == 0))
    def _():
        f = freq_ref[...]
        h1 = jnp.sin(f * (jnp.dot(feat_ref[...], w1_ref[...], precision=HIGHEST,
                                  preferred_element_type=F32) + b1_ref[...]))
        h_s[...] = jnp.sin(f * (jnp.dot(h1, w2_ref[...], precision=HIGHEST,
                                        preferred_element_type=F32) + b2_ref[...]))

    h_hi, h_lo = _split_bf16(h_s[...])
    t01 = feat_ref[:, 0:1]
    row = lax.broadcasted_iota(jnp.int32, (seq, 1), 0)
    dot = lambda a, b: jnp.dot(a, b, preferred_element_type=F32)

    def taps(w3_ref, b3_ref, dec_ref):
        w_hi, w_lo = _split_bf16(w3_ref[...])
        v = dot(h_hi, w_hi) + (dot(h_hi, w_lo) + dot(h_lo, w_hi)) + b3_ref[...]
        return v * jnp.exp(-t01 * jnp.abs(dec_ref[...]))

    fwd = taps(w3f_ref, b3f_ref, decf_ref)
    bwd = jnp.where(row == 0, 0.0, taps(w3b_ref, b3b_ref, decb_ref))
    for d, x in enumerate((fwd, bwd)):
        for j in range(nsl):
            taps_s[d * nsl + j] = x[:, j * LANE:(j + 1) * LANE]
        for r in range(4):
            for j in range(nsl):
                c0 = (4 * d + r) * ct + j * LANE
                rhs_s[:, c0:c0 + LANE] = taps_s[d * nsl + j, pl.ds(r, q, stride=4), :].astype(BF16)
    row0 = lax.broadcasted_iota(jnp.int32, (q, 1), 0) == 0

    def spectrum(d):
        x, a = [], []
        for r in range(4):
            rhs = rhs_s[:, (4 * d + r) * ct:(4 * d + r + 1) * ct]
            c = jnp.dot(cf_ref[r], rhs, preferred_element_type=F32)
            sn = jnp.dot(sf_ref[r], rhs, preferred_element_type=F32)
            a.append(sn[0:1, :])
            x.append((c, jnp.where(row0, 0.0, sn)))
        return _forward_spectrum(x, a)

    fz, fm = spectrum(0)
    bz, bm = spectrum(1)
    scale = jnp.where(row0, 1.0 / n, 2.0 / n)
    outs = ((kac_ref, kas_ref), (kbc_ref, kbs_ref), (kcc_ref, kcs_ref), (kdc_ref, kds_ref))
    for (oc, os_), (fc, fs), (bc, bs) in zip(outs, fz, bz):
        oc[...] = (fc + bc) * scale
        os_[...] = (fs - bs) * scale
    km_ref[...] = jnp.zeros_like(km_ref)
    for i, ((fc, fs), (bc, bs)) in enumerate(zip(fm, bm)):
        km_ref[2 * i:2 * i + 1, :] = (fc + bc) * (2.0 / n)
        km_ref[2 * i + 1:2 * i + 2, :] = (fs - bs) * (2.0 / n)


def _hyena_filters(seq, w1, b1, w2, b2, w3, b3, freq, decay, tables, n_order, n_dirs, d_hy):
    q = seq // 4
    emb, hid = w1.shape
    bands = (emb - 1) // 2
    t = jnp.arange(seq, dtype=F32)
    t01 = t / max(seq - 1, 1)
    band = jnp.linspace(1e-4, bands - 1, bands, dtype=F32)
    ang = (2.0 * math.pi) * t[:, None] * band[None, :] / seq
    feats = jnp.concatenate([t01[:, None], jnp.cos(ang), jnp.sin(ang)], axis=-1)
    kpad = LANE
    feats = jnp.pad(feats, ((0, 0), (0, kpad - emb)))
    w1p = jnp.pad(w1, ((0, kpad - emb), (0, 0)))
    tabs = [tables["c_fwd"], tables["s_fwd"]]
    ct = min(HY_CH_TILE, d_hy)
    nct = d_hy // ct
    ncol = n_order * d_hy
    b3r = b3.reshape(1, -1)
    decr = decay.reshape(1, -1)
    fcol = lambda o, c: (0, (o * n_dirs + 0) * nct + c)
    bcol = lambda o, c: (0, (o * n_dirs + 1) * nct + c)
    ocol = lambda o, c: (0, o * nct + c)
    full = lambda a: pl.BlockSpec(a.shape, lambda o, c: (0,) * a.ndim)
    spec = jax.ShapeDtypeStruct((q, ncol), F32)
    return pl.pallas_call(
        _hy_filter_kernel,
        out_shape=(spec,) * 8 + (jax.ShapeDtypeStruct((8, ncol), F32),),
        grid=(n_order, nct),
        in_specs=[
            full(feats), full(w1p), full(b1.reshape(1, -1)), full(w2), full(b2.reshape(1, -1)),
            full(freq.reshape(1, -1)),
            pl.BlockSpec((hid, ct), fcol), pl.BlockSpec((1, ct), fcol), pl.BlockSpec((1, ct), fcol),
            pl.BlockSpec((hid, ct), bcol), pl.BlockSpec((1, ct), bcol), pl.BlockSpec((1, ct), bcol),
        ] + [_const_spec(a.shape) for a in tabs],
        out_specs=tuple(pl.BlockSpec((q, ct), ocol) for _ in range(8)) + (pl.BlockSpec((8, ct), ocol),),
        scratch_shapes=[pltpu.VMEM((2 * ct // LANE, seq, LANE), F32), pltpu.VMEM((q, 8 * ct), BF16),
                        pltpu.VMEM((seq, hid), F32)],
        compiler_params=_cparams(2),
        name="hyena_filter_spectra",
    )(feats, w1p, b1.reshape(1, -1), w2, b2.reshape(1, -1), freq.reshape(1, -1),
      w3, b3r, decr, w3, b3r, decr, *tabs)


def _hy_conv_kernel(zin_ref, gin_ref, wz_ref, bz_ref, wg_ref, bg_ref,
                    kac_ref, kas_ref, kbc_ref, kbs_ref, kcc_ref, kcs_ref, kdc_ref, kds_ref, km_ref, bias_ref,
                    cf_ref, sf_ref, ci_ref, si_ref, out_ref,
                    zraw_s, graw_s, o_s, z_s, g_s, rhs_s, ac_s, as_s, *, conv_on_z, seq):
    ct = gin_ref.shape[1]
    nbat = gin_ref.shape[0] // seq
    q = seq // 4
    nsl = ct // LANE
    pad = zraw_s.shape[2] - seq
    top = pad // 2
    rc = min(DFT_ROW_CHUNK, q)
    k_refs = ((kac_ref, kas_ref), (kbc_ref, kbs_ref), (kcc_ref, kcs_ref), (kdc_ref, kds_ref))

    def stage(raw_s, src_ref, k):
        for j in range(nsl):
            raw_s[k, j, 0:top, :] = jnp.zeros((top, LANE), F32)
            raw_s[k, j, top + seq:, :] = jnp.zeros((pad - top, LANE), F32)
            raw_s[k, j, top:top + seq, :] = src_ref[k * seq:(k + 1) * seq, j * LANE:(j + 1) * LANE].astype(F32)

    def split_rows(raw_s, dst_ref, k, w_ref, b_ref):
        for r in range(4):
            for j in range(nsl):
                ls = slice(j * LANE, (j + 1) * LANE)
                tap = lambda d: raw_s[k, j, pl.ds(top + r + d, q, stride=4), :]
                dst_ref[k, :, r * ct + j * LANE:r * ct + (j + 1) * LANE] = (
                    b_ref[:, ls] + tap(-1) * w_ref[0:1, ls] + tap(0) * w_ref[1:2, ls] + tap(1) * w_ref[2:3, ls])

    def spectrum_chunk(k, i):
        r = slice(i * rc, (i + 1) * rc)
        row0 = lax.broadcasted_iota(jnp.int32, (rc, 1), 0) == 0
        x, a = [], []
        for p in range(4):
            rhs = rhs_s[k, :, p * ct:(p + 1) * ct]
            c = jnp.dot(cf_ref[p, r, :], rhs, preferred_element_type=F32)
            sn = jnp.dot(sf_ref[p, r, :], rhs, preferred_element_type=F32)
            if i == 0:
                a.append(sn[0:1, :])
                sn = jnp.where(row0, 0.0, sn)
            else:
                a.append(jnp.zeros((1, ct), F32))
            x.append((c, sn))
        z, zmid = _forward_spectrum(x, a)
        y = [_cmul(zx, (kc[r, :], ks[r, :])) for zx, (kc, ks) in zip(z, k_refs)]
        ymid = [_cmul(zmid[m], (km_ref[2 * m:2 * m + 1, :], km_ref[2 * m + 1:2 * m + 2, :])) for m in range(2)]
        t, sp = _inverse_spectrum(y, ymid)
        for p, ((tc, ts), spp) in enumerate(zip(t, sp)):
            if i == 0:
                ts = jnp.where(row0, spp, ts)
            ac_s[k, r, p * ct:(p + 1) * ct] = tc.astype(BF16)
            as_s[k, r, p * ct:(p + 1) * ct] = ts.astype(BF16)

    def output_chunk(k, i):
        r = slice(i * rc, (i + 1) * rc)
        for p in range(4):
            ps = slice(p * ct, (p + 1) * ct)
            y = jnp.dot(ci_ref[p, r, :], ac_s[k, :, ps], preferred_element_type=F32)
            y = y + jnp.dot(si_ref[p, r, :], as_s[k, :, ps], preferred_element_type=F32)
            rows = pl.ds(4 * i * rc + p, rc, stride=4)
            if conv_on_z:
                out_ref[k * q + i * rc:k * q + (i + 1) * rc, ps] = (
                    g_s[k, r, ps] * (y + z_s[k, r, ps] * bias_ref[...])).astype(out_ref.dtype)
                continue
            for j in range(nsl):
                ls = slice(p * ct + j * LANE, p * ct + (j + 1) * LANE)
                jl = slice(j * LANE, (j + 1) * LANE)
                o_s[k, j, rows, :] = g_s[k, r, ls] * (y[:, jl] + z_s[k, r, ls] * bias_ref[:, jl])

    for k in range(nbat):
        stage(graw_s, gin_ref, k)
        split_rows(graw_s, g_s, k, wg_ref, bg_ref)
        if conv_on_z:
            stage(zraw_s, zin_ref, k)
            split_rows(zraw_s, z_s, k, wz_ref, bz_ref)
            rhs_s[k] = z_s[k].astype(BF16)
        else:
            rhs_s[k] = zin_ref[k * q:(k + 1) * q, :]
            z_s[k] = zin_ref[k * q:(k + 1) * q, :].astype(F32)
    for i in range(q // rc):
        for k in range(nbat):
            spectrum_chunk(k, i)
    for i in range(q // rc):
        for k in range(nbat):
            output_chunk(k, i)
    if not conv_on_z:
        for k in range(nbat):
            for j in range(nsl):
                out_ref[k * seq:(k + 1) * seq, j * LANE:(j + 1) * LANE] = o_s[k, j].astype(out_ref.dtype)


def _hy_conv(zin, zcol0, gcol0, u_hy, conv_w, conv_b, spectra, bias_row, tables,
             *, order, conv_on_z, bsz, seq, d_hy):
    tabs = [tables[k] for k in ("c_fwd", "s_fwd", "c_inv", "s_inv")]
    q = seq // 4
    ct = min(HY_CH_TILE, d_hy)
    nsl = ct // LANE
    nct = d_hy // ct
    nbat = HY_BATCH_GROUP if bsz % HY_BATCH_GROUP == 0 else 1
    zc0 = zcol0 // ct
    gc0 = gcol0 // ct
    zw0 = zc0 if conv_on_z else 0
    kspec = lambda rows: pl.BlockSpec((rows, ct), lambda c, b: (0, order * nct + c),
                                      pipeline_mode=pl.Buffered(1))
    scratch = lambda dt: pltpu.VMEM((nbat, q, 4 * ct), dt)
    slabs = lambda rows: pltpu.VMEM((nbat, nsl, rows, LANE), F32)
    time_block = pl.BlockSpec((nbat * seq, ct), lambda c, b: (b, c))
    split_block = pl.BlockSpec((nbat * q, 4 * ct), lambda c, b: (b, c))
    return pl.pallas_call(
        functools.partial(_hy_conv_kernel, conv_on_z=conv_on_z, seq=seq),
        out_shape=jax.ShapeDtypeStruct((bsz * q, 4 * d_hy) if conv_on_z else (bsz * seq, d_hy), BF16),
        grid=(nct, bsz // nbat),
        in_specs=[
            pl.BlockSpec((nbat * seq, ct), lambda c, b: (b, zc0 + c)) if conv_on_z else split_block,
            pl.BlockSpec((nbat * seq, ct), lambda c, b: (b, gc0 + c)),
            pl.BlockSpec((conv_w.shape[0], ct), lambda c, b: (0, zw0 + c)),
            pl.BlockSpec((1, ct), lambda c, b: (0, zw0 + c)),
            pl.BlockSpec((conv_w.shape[0], ct), lambda c, b: (0, gc0 + c)),
            pl.BlockSpec((1, ct), lambda c, b: (0, gc0 + c)),
        ] + [kspec(q)] * 8 + [kspec(8), kspec(1)] + [_const_spec(a.shape) for a in tabs],
        out_specs=split_block if conv_on_z else time_block,
        scratch_shapes=[
            slabs(seq + 2 * SUBLANE), slabs(seq + 2 * SUBLANE), slabs(seq),
            scratch(F32), scratch(F32), scratch(BF16), scratch(BF16), scratch(BF16),
        ],
        compiler_params=_cparams(2),
        name=f"hyena_conv_order{order}",
    )(zin, u_hy, conv_w, conv_b, conv_w, conv_b, *spectra, bias_row, *tabs)


def _hyena_branch(u_hy, bsz, seq, conv_w, conv_b, w1, b1, w2, b2, w3, b3, freq, decay, bias):
    n_order, d_hy = bias.shape
    n_dirs = w3.shape[1] // (n_order * d_hy)
    if n_order != 2 or seq % 8 != 0 or d_hy % LANE != 0:
        raise NotImplementedError("Hyena branch: two long convolutions, L % 8 == 0, 128-lane channel tiles")
    tables = _hyena_tables(seq)
    spectra = _hyena_filters(seq, w1, b1, w2, b2, w3, b3, freq, decay, tables, n_order, n_dirs, d_hy)
    cb = conv_b.reshape(1, -1)
    bias_row = bias.reshape(1, -1)
    common = dict(bsz=bsz, seq=seq, d_hy=d_hy)
    z1 = _hy_conv(u_hy, 0, d_hy, u_hy, conv_w, cb, spectra, bias_row, tables,
                  order=0, conv_on_z=True, **common)
    return _hy_conv(z1, 0, 2 * d_hy, u_hy, conv_w, cb, spectra, bias_row, tables,
                    order=1, conv_on_z=False, **common)


def _mixer_kernel(x_ref, mod_ref, g1_ref, g2_ref, fg_ref, permt_ref, za_ref, zb_ref,
                  wgate_ref, wa_ref, wb_ref, wout_ref, wg_ref, wu_ref, wd_ref, o_ref, *, n_chunks):
    nb, tt, d = x_ref.shape
    rows = nb * tt
    mod = lambda k: mod_ref[:, k:k + 1, :]
    x = x_ref[...]
    h = _norm_modulate(x, g1_ref[...], mod(0), mod(1)).reshape(rows, d).astype(BF16)
    gate = jax.nn.sigmoid(jnp.dot(h, wgate_ref[...], preferred_element_type=F32))
    za = jnp.dot(permt_ref[...], za_ref[...], preferred_element_type=F32).astype(BF16)
    ya = jnp.dot(za, wa_ref[...], preferred_element_type=F32)
    yb = jnp.dot(zb_ref[...].reshape(rows, zb_ref.shape[-1]), wb_ref[...], preferred_element_type=F32)
    merged = gate[:, :d] * ya + gate[:, d:] * yb
    o = jnp.dot(merged.astype(BF16), wout_ref[...], preferred_element_type=F32)
    x1 = x + mod(2) * o.reshape(nb, tt, d)

    h2 = _norm_modulate(x1, g2_ref[...], mod(3), mod(4)).reshape(rows, d).astype(BF16)
    d_ff = wg_ref.shape[1]
    unit = MXU_TILE if d_ff % MXU_TILE == 0 else d_ff // n_chunks
    edges = [len(g) for g in _spread(list(range(d_ff // unit)), n_chunks)]
    acc = jnp.zeros((rows, d), F32)
    start = 0
    for width in edges:
        sl = slice(start * unit, (start + width) * unit)
        start += width
        gl = jnp.dot(h2, wg_ref[:, sl], preferred_element_type=F32)
        up = jnp.dot(h2, wu_ref[:, sl], preferred_element_type=F32)
        act = (gl * jax.nn.sigmoid(gl) * up).astype(BF16)
        acc = acc + jnp.dot(act, wd_ref[sl, :], preferred_element_type=F32)
    x2 = x1 + mod(5) * acc.reshape(nb, tt, d)
    ms = jnp.mean(x2 * x2, axis=-1, keepdims=True)
    o_ref[...] = x2 * lax.rsqrt(ms + EPS) * fg_ref[...]


def _mixer(x, mod3, norm1_g, norm2_g, final_g, perm_t, za_tm, zb, w_in, gate_col, w_a, w_b, w_out, w_gu, w_d):
    bsz, seq, d = x.shape
    d_s5 = w_a.shape[0]
    d_hy = w_b.shape[0]
    d_ff = w_d.shape[0]
    tt = _time_tile(bsz, seq)
    n_chunks = 2 if d_ff % (2 * LANE) == 0 else 1
    consts = (mod3, norm1_g, norm2_g, final_g, perm_t)
    weights = (w_in, w_a, w_b, w_out, w_gu, w_gu, w_d)
    weight_specs = ([_col_block_spec(d, 2 * d, gate_col)] + [_const_spec(a.shape) for a in (w_a, w_b, w_out)]
                    + [_col_block_spec(d, d_ff, 0), _col_block_spec(d, d_ff, 1), _const_spec(w_d.shape)])
    return pl.pallas_call(
        functools.partial(_mixer_kernel, n_chunks=n_chunks),
        out_shape=jax.ShapeDtypeStruct(x.shape, F32),
        grid=(seq // tt,),
        in_specs=[pl.BlockSpec((bsz, tt, d), lambda j: (0, j, 0))]
        + [_const_spec(a.shape) for a in consts]
        + [pl.BlockSpec((tt * bsz, d_s5), lambda j: (j, 0)),
           pl.BlockSpec((bsz, tt, d_hy), lambda j: (0, j, 0))]
        + weight_specs,
        out_specs=pl.BlockSpec((bsz, tt, d), lambda j: (0, j, 0)),
        compiler_params=_cparams(1),
        name="merge_swiglu_final_norm",
    )(x, *consts, za_tm, zb, *weights)


def kernel(x, c, ada_w, ada_b, norm1_g, norm2_g, w_in, s5_lam_re, s5_lam_im, s5_log_step, s5_b_re, s5_b_im, s5_c_re, s5_c_im, s5_d, s5_glu_w, s5_glu_b, hy_conv_w, hy_conv_b, hy_ffn_w1, hy_ffn_b1, hy_ffn_w2, hy_ffn_b2, hy_ffn_w3, hy_ffn_b3, hy_freq, hy_decay, hy_bias, w_branch_a, w_branch_b, w_out, ffn_w_gu, ffn_w_down, final_g):
    bsz, seq, d = x.shape
    depth = ada_w.shape[0]
    if depth != 1:
        raise NotImplementedError("the final RMSNorm is fused into the (single) layer's channel mixer")
    d_s5 = s5_d.shape[-1]
    n_order, d_hy = hy_bias.shape[1:]
    d_uh = d_s5 + (n_order + 1) * d_hy
    d_ff = ffn_w_down.shape[1]
    i = 0
    perm = _row_permutation(bsz, _time_tile(bsz, seq))
    mod = _modulation(c, ada_w[i], ada_b[i]).reshape(bsz, 6, d)
    w_in_b = w_in[i].astype(BF16)
    if d_uh % (2 * d) != 0:
        raise NotImplementedError("gate columns of w_in must start on a 2*D column block")
    u_s5, u_hy = _in_projection(x, mod, norm1_g[i].reshape(1, d), w_in_b, perm, d_s5, d_uh)
    z_a = _s5_branch(u_s5, bsz, s5_lam_re[i], s5_lam_im[i], s5_log_step[i], s5_b_re[i], s5_b_im[i],
                     s5_c_re[i], s5_c_im[i], s5_d[i], s5_glu_w[i], s5_glu_b[i])
    z_b = _hyena_branch(u_hy.reshape(bsz * seq, -1), bsz, seq, hy_conv_w[i], hy_conv_b[i], hy_ffn_w1[i],
                        hy_ffn_b1[i], hy_ffn_w2[i], hy_ffn_b2[i], hy_ffn_w3[i], hy_ffn_b3[i], hy_freq[i],
                        hy_decay[i], hy_bias[i])
    w_gu = ffn_w_gu[i].astype(BF16)
    return _mixer(x, mod, norm1_g[i].reshape(1, d), norm2_g[i].reshape(1, d), final_g.reshape(1, d),
                  perm.T, z_a, z_b.reshape(bsz, seq, d_hy), w_in_b, d_uh // (2 * d),
                  w_branch_a[i].astype(BF16), w_branch_b[i].astype(BF16), w_out[i].astype(BF16),
                  w_gu, ffn_w_down[i].astype(BF16))
```

```python
import functools
import math

import jax
import jax.numpy as jnp
from jax import lax
from jax.experimental import pallas as pl
from jax.experimental.pallas import tpu as pltpu

F32 = jnp.float32
BF16 = jnp.bfloat16
EPS = 1e-6
HIGHEST = lax.Precision.HIGHEST

V7X_VMEM_BYTES = 64 * 1024 * 1024
VMEM_LIMIT_BYTES = 56 * 1024 * 1024
LANE = 128
SUBLANE = 8
ROW_TILE = 512
S5_TIME_CHUNK = 64
S5_LANE_CHUNK = 512
S5_CH_SPLIT = 256
S5_STEPS_PER_GROUP = 8
MXU_TILE = 256
HY_CH_TILE = 256
HY_BATCH_GROUP = 2
DFT_ROW_CHUNK = 512


def _cparams(n_axes, fuse_inputs=None):
    return pltpu.CompilerParams(
        dimension_semantics=("arbitrary",) * n_axes,
        vmem_limit_bytes=VMEM_LIMIT_BYTES,
        allow_input_fusion=fuse_inputs,
    )


def _const_spec(shape):
    nd = len(shape)
    return pl.BlockSpec(shape, lambda *_: (0,) * nd, pipeline_mode=pl.Buffered(1))


def _gelu_tanh(x):
    return 0.5 * x * (1.0 + jnp.tanh(math.sqrt(2.0 / math.pi) * (x + 0.044715 * (x * x * x))))


def _norm_modulate(x, g, shift, scale):
    ms = jnp.mean(x * x, axis=-1, keepdims=True)
    r = x * lax.rsqrt(ms + EPS) * g
    return r * (1.0 + scale) + shift


def _split_bf16(x):
    hi = x.astype(BF16)
    return hi, (x - hi.astype(F32)).astype(BF16)


def _mod_kernel(c_ref, w_ref, b_ref, o_ref):
    c = c_ref[...]
    c_hi, c_lo = _split_bf16(c * jax.nn.sigmoid(c))
    w_hi, w_lo = _split_bf16(w_ref[...])
    dot = lambda a, b: jnp.dot(a, b, preferred_element_type=F32)
    o_ref[...] = dot(c_hi, w_hi) + (dot(c_hi, w_lo) + dot(c_lo, w_hi)) + b_ref[...]


def _modulation(c, ada_w, ada_b):
    bsz, d = c.shape
    n = ada_w.shape[1]
    tn = 1024 if n % 1024 == 0 else 512
    return pl.pallas_call(
        _mod_kernel,
        out_shape=jax.ShapeDtypeStruct((bsz, n), F32),
        grid=(n // tn,),
        in_specs=[
            pl.BlockSpec((bsz, d), lambda j: (0, 0)),
            pl.BlockSpec((d, tn), lambda j: (0, j)),
            pl.BlockSpec((1, tn), lambda j: (0, j)),
        ],
        out_specs=pl.BlockSpec((bsz, tn), lambda j: (0, j)),
        compiler_params=_cparams(1),
        name="adaln_mod",
    )(c, ada_w, ada_b.reshape(1, n))


def _time_tile(bsz, seq):
    return max(min(ROW_TILE // bsz, seq), 1)


def _row_permutation(bsz, tt):
    r = jnp.arange(bsz * tt, dtype=jnp.int32)
    src = (r % bsz) * tt + r // bsz
    return (src[:, None] == r[None, :]).astype(BF16)


def _inproj_kernel(x_ref, mod_ref, g_ref, w_ref, perm_ref, us5_ref, uhy_ref, *, d_s5):
    nb, tt, d = x_ref.shape
    h = _norm_modulate(x_ref[...], g_ref[...], mod_ref[:, 0:1, :], mod_ref[:, 1:2, :])
    hb = h.reshape(nb * tt, d).astype(BF16)
    p = jnp.dot(hb, w_ref[...], preferred_element_type=F32)
    us5_ref[...] = jnp.dot(perm_ref[...], p[:, :d_s5].astype(BF16),
                           preferred_element_type=F32).astype(us5_ref.dtype)
    uhy_ref[...] = p[:, d_s5:].astype(BF16).reshape(uhy_ref.shape)


def _col_block_spec(rows, width, index):
    return pl.BlockSpec((rows, width), lambda *_: (0, index), pipeline_mode=pl.Buffered(1))


def _in_projection(x, mod3, norm_g, w_in, perm, d_s5, n):
    bsz, seq, d = x.shape
    tt = _time_tile(bsz, seq)
    return pl.pallas_call(
        functools.partial(_inproj_kernel, d_s5=d_s5),
        out_shape=(
            jax.ShapeDtypeStruct((seq * bsz, d_s5), BF16),
            jax.ShapeDtypeStruct((bsz, seq, n - d_s5), BF16),
        ),
        grid=(seq // tt,),
        in_specs=[
            pl.BlockSpec((bsz, tt, d), lambda j: (0, j, 0)),
            _const_spec(mod3.shape),
            _const_spec((1, d)),
            _col_block_spec(d, n, 0),
            _const_spec(perm.shape),
        ],
        out_specs=(
            pl.BlockSpec((tt * bsz, d_s5), lambda j: (j, 0)),
            pl.BlockSpec((bsz, tt, n - d_s5), lambda j: (0, j, 0)),
        ),
        compiler_params=_cparams(1, fuse_inputs=[False, False, False, True, False]),
        name="in_proj",
    )(x, mod3, norm_g, w_in, perm)


def _group_of(idx, size):
    if size & (size - 1) == 0:
        return lax.shift_right_logical(idx, jnp.int32(size.bit_length() - 1))
    return idx // size


def _s5_prep_kernel(lre_ref, lim_ref, lstep_ref, bre_ref, bim_ref, cre_ref, cim_ref,
                    are_ref, aim_ref, bd_ref, cd_ref):
    ndir, nsplit, ks, two_nss = bd_ref.shape
    nss = two_nss // 2
    grp = bre_ref.shape[1]
    p = nss // (ks // grp)
    nb = are_ref.shape[1]
    step = jnp.exp(lstep_ref[...])
    lr = lre_ref[...]
    li = lim_ref[...]
    mag = jnp.exp(lr * step)
    ar = mag * jnp.cos(li * step)
    ai = mag * jnp.sin(li * step)
    num = ar - 1.0
    den = lr * lr + li * li
    cr = (num * lr + ai * li) / den
    ci = (ai * lr - num * li) / den

    iota = lambda shape, axis: lax.broadcasted_iota(jnp.int32, shape, axis)
    ch_of_row = iota((ks, grp), 0) - _group_of(iota((ks, grp), 0), grp) * grp
    rep_rows = (ch_of_row == iota((ks, grp), 1)).astype(F32)
    ch_of_lane = iota((grp, ks), 1) - _group_of(iota((grp, ks), 1), grp) * grp
    rep_lanes = (ch_of_lane == iota((grp, ks), 0)).astype(F32)
    in_mask = _group_of(iota((ks, 1), 0), grp) == _group_of(iota((1, nss), 1), p)
    out_mask = _group_of(iota((nss, 1), 0), p) == _group_of(iota((1, ks), 1), grp)
    rep = lambda a, b: jnp.dot(a.astype(BF16), b.astype(BF16), preferred_element_type=F32)

    for d in range(ndir):
        are_ref[d] = jnp.broadcast_to(ar[d:d + 1, :], (nb, ar.shape[1]))
        aim_ref[d] = jnp.broadcast_to(ai[d:d + 1, :], (nb, ai.shape[1]))
        br = bre_ref[d]
        bi = bim_ref[d]
        bb = (cr[d:d + 1, :] * br - ci[d:d + 1, :] * bi, cr[d:d + 1, :] * bi + ci[d:d + 1, :] * br)
        cc = (cre_ref[d], -cim_ref[d])
        for h in range(nsplit):
            st = slice(h * nss, (h + 1) * nss)
            for part in range(2):
                lanes = slice(part * nss, (part + 1) * nss)
                bd_ref[d, h, :, lanes] = jnp.where(in_mask, rep(rep_rows, bb[part][:, st]), 0.0).astype(BF16)
                cd_ref[d, h, lanes, :] = jnp.where(out_mask, rep(cc[part][st, :], rep_lanes), 0.0).astype(BF16)


def _s5_prepare(lam_re, lam_im, log_step, b_re, b_im, c_re, c_im, *, nb, nsplit, p):
    ndir, grp, ns = b_re.shape
    nss = ns // nsplit
    ks = (nss // p) * grp
    outs = (
        jax.ShapeDtypeStruct((ndir, nb, ns), F32),
        jax.ShapeDtypeStruct((ndir, nb, ns), F32),
        jax.ShapeDtypeStruct((ndir, nsplit, ks, 2 * nss), BF16),
        jax.ShapeDtypeStruct((ndir, nsplit, 2 * nss, ks), BF16),
    )
    return pl.pallas_call(
        _s5_prep_kernel,
        out_shape=outs,
        compiler_params=pltpu.CompilerParams(vmem_limit_bytes=VMEM_LIMIT_BYTES),
        name="s5_discretize",
    )(lam_re, lam_im, log_step, b_re, b_im, c_re, c_im)


def _spread(items, n):
    return [items[(len(items) * i) // n:(len(items) * (i + 1)) // n] for i in range(n)]


def _s5_scan_kernel(*refs, reverse, final, tc, nb):
    if final:
        (u_ref, bd_ref, cd_ref, are_ref, aim_ref, yprev_ref, gluw_ref, glub_ref,
         out_ref, bu_ref, sb_ref, y_ref, s_ref) = refs
    else:
        (u_ref, bd_ref, cd_ref, are_ref, aim_ref, d_ref, out_ref, bu_ref, sb_ref, y_ref, s_ref) = refs
    rows = tc * nb
    _, nsplit, ks, two_nss = bd_ref.shape
    nss = two_nss // 2
    lc = min(S5_LANE_CHUNK, nss)
    tw = min(MXU_TILE, lc)
    rh = rows // 2 if rows % (2 * SUBLANE * 2) == 0 else rows
    row_halves = [slice(r, r + rh) for r in range(0, rows, rh)]

    @pl.when(pl.program_id(0) == 0)
    def _():
        s_ref[...] = jnp.zeros_like(s_ref)

    units = [(h, q) for h in range(nsplit) for q in range(nss // lc)]

    def lanes(unit, part, off=0, width=None):
        h, q = unit
        start = h * two_nss + part * nss + q * lc + off
        return slice(start, start + (lc if width is None else width))

    def expand_pieces(unit):
        h, q = unit
        out = []
        for part in range(2):
            for off in range(0, lc, tw):
                for rs in row_halves:
                    def piece(part=part, off=off, rs=rs):
                        col = part * nss + q * lc + off
                        bu_ref[rs, lanes(unit, part, off, tw)] = jnp.dot(
                            u_ref[rs, h * ks:(h + 1) * ks], bd_ref[0, h, :, col:col + tw],
                            preferred_element_type=F32)
                    out.append(piece)
        return out

    def readout_pieces(unit):
        h, q = unit
        out = []
        for rs in row_halves:
            def piece(rs=rs):
                v = None
                for part in range(2):
                    k0 = part * nss + q * lc
                    d = jnp.dot(sb_ref[rs, lanes(unit, part)], cd_ref[0, h, k0:k0 + lc, :],
                                preferred_element_type=F32)
                    v = d if v is None else v + d
                if q == 0:
                    y_ref[rs, h * ks:(h + 1) * ks] = v
                else:
                    y_ref[rs, h * ks:(h + 1) * ks] += v
            out.append(piece)
        return out

    step_groups = _spread(list(range(tc)), max(tc // S5_STEPS_PER_GROUP, 1))
    for p in expand_pieces(units[0]):
        p()
    for c, unit in enumerate(units):
        mxu_work = (expand_pieces(units[c + 1]) if c + 1 < len(units) else []) + \
                   (readout_pieces(units[c - 1]) if c > 0 else [])
        mxu_groups = _spread(mxu_work, len(step_groups))
        a_sl = slice(unit[0] * nss + unit[1] * lc, unit[0] * nss + (unit[1] + 1) * lc)
        a_re = are_ref[0, :, a_sl]
        a_im = aim_ref[0, :, a_sl]
        re_sl, im_sl = lanes(unit, 0), lanes(unit, 1)
        sr = s_ref[:, re_sl]
        si = s_ref[:, im_sl]
        for steps, pieces in zip(step_groups, mxu_groups):
            for p in pieces:
                p()
            for k in steps:
                t = (tc - 1 - k) if reverse else k
                row = slice(t * nb, (t + 1) * nb)
                nr = a_re * sr - a_im * si + bu_ref[row, re_sl]
                ni = a_re * si + a_im * sr + bu_ref[row, im_sl]
                sb_ref[row, re_sl] = nr.astype(BF16)
                sb_ref[row, im_sl] = ni.astype(BF16)
                sr, si = nr, ni
        s_ref[:, re_sl] = sr
        s_ref[:, im_sl] = si
    for p in readout_pieces(units[-1]):
        p()

    rb = min(256, rows)
    for r in range(rows // rb):
        rs = slice(r * rb, (r + 1) * rb)
        if final:
            z = _gelu_tanh(yprev_ref[rs, :] + y_ref[rs, :])
            gate = jnp.dot(z.astype(BF16), gluw_ref[...], preferred_element_type=F32) + glub_ref[...]
            out_ref[rs, :] = (z * jax.nn.sigmoid(gate)).astype(out_ref.dtype)
        else:
            out_ref[rs, :] = y_ref[rs, :] + u_ref[rs, :] * d_ref[...]


def _s5_scan(u_tm, bd, cd, a_re, a_im, extra, *, direction, final, nb, out_dtype):
    rows_total, d_s5 = u_tm.shape
    seq = rows_total // nb
    two_ns = bd.shape[1] * bd.shape[3]
    tc = min(S5_TIME_CHUNK, seq)
    nchunk = seq // tc
    rows = tc * nb
    reverse = direction == 1
    if reverse:
        cidx = lambda i: (nchunk - 1 - i, 0)
    else:
        cidx = lambda i: (i, 0)

    def dir_spec(a):
        nd = a.ndim
        return pl.BlockSpec((1,) + a.shape[1:], lambda i: (direction,) + (0,) * (nd - 1),
                            pipeline_mode=pl.Buffered(1))

    in_specs = [pl.BlockSpec((rows, d_s5), cidx), dir_spec(bd), dir_spec(cd), dir_spec(a_re), dir_spec(a_im)]
    if final:
        yprev, glu_w, glu_b = extra
        in_specs += [pl.BlockSpec((rows, d_s5), cidx), _const_spec(glu_w.shape), _const_spec(glu_b.shape)]
    else:
        in_specs += [_const_spec(extra[0].shape)]
    return pl.pallas_call(
        functools.partial(_s5_scan_kernel, reverse=reverse, final=final, tc=tc, nb=nb),
        out_shape=jax.ShapeDtypeStruct((rows_total, d_s5), out_dtype),
        grid=(nchunk,),
        in_specs=in_specs,
        out_specs=pl.BlockSpec((rows, d_s5), cidx),
        scratch_shapes=[pltpu.VMEM((rows, two_ns), F32), pltpu.VMEM((rows, two_ns), BF16),
                        pltpu.VMEM((rows, d_s5), F32), pltpu.VMEM((nb, two_ns), F32)],
        compiler_params=_cparams(1),
        name="s5_scan_bwd_glu" if final else "s5_scan_fwd",
    )(u_tm, bd, cd, a_re, a_im, *extra)


def _s5_branch(u_tm, nb, lam_re, lam_im, log_step, b_re, b_im, c_re, c_im, d, glu_w, glu_b):
    ndir, g, p = lam_re.shape
    if ndir != 2:
        raise NotImplementedError("S5 branch expects forward and backward directions")
    grp = b_re.shape[-1]
    ns = g * p
    d_s5 = g * grp
    ks = S5_CH_SPLIT if d_s5 % S5_CH_SPLIT == 0 else d_s5
    flat = lambda a: a.reshape(ndir, ns)
    to_lanes = lambda a: jnp.transpose(a, (0, 3, 1, 2)).reshape(ndir, grp, ns)
    to_rows = lambda a: jnp.transpose(a, (0, 1, 3, 2)).reshape(ndir, ns, grp)
    a_re, a_im, bd, cd = _s5_prepare(
        flat(lam_re), flat(lam_im), jnp.repeat(log_step, p, axis=-1), to_lanes(b_re), to_lanes(b_im),
        to_rows(c_re), to_rows(c_im), nb=nb, nsplit=d_s5 // ks, p=p)
    y = _s5_scan(u_tm, bd, cd, a_re, a_im, (d.reshape(1, -1),), direction=0, final=False, nb=nb, out_dtype=F32)
    extra = (y, glu_w.astype(BF16), glu_b.reshape(1, -1))
    return _s5_scan(u_tm, bd, cd, a_re, a_im, extra, direction=1, final=True, nb=nb, out_dtype=BF16)


RSQRT2 = math.sqrt(0.5)


def _dft_table(nf, t, n):
    f_lo = 64 if nf % 64 == 0 else 1
    f_hi = nf // f_lo
    t = t[:, None, :]
    ka = (jnp.arange(f_hi, dtype=jnp.int32)[None, :, None] * f_lo * t) % n
    kb = (jnp.arange(f_lo, dtype=jnp.int32)[None, :, None] * t) % n
    w = 2.0 * math.pi / n
    aa = ka.astype(F32) * w
    ab = kb.astype(F32) * w
    ca, sa, cb, sb = jnp.cos(aa), jnp.sin(aa), jnp.cos(ab), jnp.sin(ab)
    cm = ca[:, :, None, :] * cb[:, None, :, :] - sa[:, :, None, :] * sb[:, None, :, :]
    sm = sa[:, :, None, :] * cb[:, None, :, :] + ca[:, :, None, :] * sb[:, None, :, :]
    shape = (t.shape[0], nf, t.shape[-1])
    return cm.reshape(shape), sm.reshape(shape)


def _hyena_tables(seq):
    q = seq // 4
    times = 4 * jnp.arange(q, dtype=jnp.int32)[None, :] + jnp.arange(4, dtype=jnp.int32)[:, None]
    cf, sf = _dft_table(q, times, 2 * seq)
    alt = (1 - 2 * (jnp.arange(q, dtype=jnp.int32) & 1)).astype(F32)
    f_is_0 = jnp.arange(q, dtype=jnp.int32)[None, :, None] == 0
    sf = jnp.where(f_is_0, alt[None, None, :], sf)
    return dict(c_fwd=cf.astype(BF16), s_fwd=sf.astype(BF16),
                c_inv=jnp.transpose(cf, (0, 2, 1)).astype(BF16), s_inv=jnp.transpose(sf, (0, 2, 1)).astype(BF16))


def _cmul(z, k):
    (zc, zs), (kc, ks) = z, k
    return zc * kc - zs * ks, zc * ks + zs * kc


def _forward_spectrum(x, a):
    (c0, s0), (c1, s1), (c2, s2), (c3, s3) = x
    pc, ps, mc, ms = c0 + c2, s0 + s2, c0 - c2, s0 - s2
    qc, qs, nc, ns = c1 + c3, s1 + s3, c1 - c3, s1 - s3
    za = (pc + qc, ps + qs)
    zb = (pc - qc, qs - ps)
    zc = (mc + ns, nc - ms)
    zd = (mc - ns, ms + nc)
    e = (a[1] - a[3]) * RSQRT2
    o = (a[1] + a[3]) * RSQRT2
    return [za, zb, zc, zd], [(a[0] + e, a[2] + o), (a[0] - e, o - a[2])]


def _inverse_spectrum(y, ymid):
    (ac, as_), (bc, bs), (cc, cs), (dc, ds) = y
    upc, ups, umc, ums = ac + bc, as_ - bs, ac - bc, as_ + bs
    vpc, vps, vmc, vms = dc + cc, ds - cs, dc - cc, ds + cs
    t = [(upc + vpc, ups + vps), (umc + vms, ums - vmc), (upc - vpc, ups - vps), (umc - vms, ums + vmc)]
    (y1c, y1s), (y3c, y3s) = ymid
    sp = [y1c + y3c, ((y1c + y1s) + (y3s - y3c)) * RSQRT2, y1s - y3s, ((y1s - y1c) + (y3c + y3s)) * RSQRT2]
    return t, sp


def _hy_filter_kernel(feat_ref, w1_ref, b1_ref, w2_ref, b2_ref, freq_ref,
                      w3f_ref, b3f_ref, decf_ref, w3b_ref, b3b_ref, decb_ref,
                      cf_ref, sf_ref,
                      kac_ref, kas_ref, kbc_ref, kbs_ref, kcc_ref, kcs_ref, kdc_ref, kds_ref, km_ref,
                      taps_s, rhs_s, h_s):
    seq = feat_ref.shape[0]
    q = seq // 4
    n = 2 * seq
    ct = kac_ref.shape[1]
    nsl = ct // LANE

    @pl.when((pl.program_id(0) == 0) & (pl.program_id(1) == 0))
    def _():
        f = freq_ref[...]
        h1 = jnp.sin(f * (jnp.dot(feat_ref[...], w1_ref[...], precision=HIGHEST,
                                  preferred_element_type=F32) + b1_ref[...]))
        h_s[...] = jnp.sin(f * (jnp.dot(h1, w2_ref[...], precision=HIGHEST,
                                        preferred_element_type=F32) + b2_ref[...]))

    h_hi, h_lo = _split_bf16(h_s[...])
    t01 = feat_ref[:, 0:1]
    row = lax.broadcasted_iota(jnp.int32, (seq, 1), 0)
    dot = lambda a, b: jnp.dot(a, b, preferred_element_type=F32)

    def taps(w3_ref, b3_ref, dec_ref):
        w_hi, w_lo = _split_bf16(w3_ref[...])
        v = dot(h_hi, w_hi) + (dot(h_hi, w_lo) + dot(h_lo, w_hi)) + b3_ref[...]
        return v * jnp.exp(-t01 * jnp.abs(dec_ref[...]))

    fwd = taps(w3f_ref, b3f_ref, decf_ref)
    bwd = jnp.where(row == 0, 0.0, taps(w3b_ref, b3b_ref, decb_ref))
    for d, x in enumerate((fwd, bwd)):
        for j in range(nsl):
            taps_s[d * nsl + j] = x[:, j * LANE:(j + 1) * LANE]
        for r in range(4):
            for j in range(nsl):
                c0 = (4 * d + r) * ct + j * LANE
                rhs_s[:, c0:c0 + LANE] = taps_s[d * nsl + j, pl.ds(r, q, stride=4), :].astype(BF16)
    row0 = lax.broadcasted_iota(jnp.int32, (q, 1), 0) == 0

    def spectrum(d):
        x, a = [], []
        for r in range(4):
            rhs = rhs_s[:, (4 * d + r) * ct:(4 * d + r + 1) * ct]
            c = jnp.dot(cf_ref[r], rhs, preferred_element_type=F32)
            sn = jnp.dot(sf_ref[r], rhs, preferred_element_type=F32)
            a.append(sn[0:1, :])
            x.append((c, jnp.where(row0, 0.0, sn)))
        return _forward_spectrum(x, a)

    fz, fm = spectrum(0)
    bz, bm = spectrum(1)
    scale = jnp.where(row0, 1.0 / n, 2.0 / n)
    outs = ((kac_ref, kas_ref), (kbc_ref, kbs_ref), (kcc_ref, kcs_ref), (kdc_ref, kds_ref))
    for (oc, os_), (fc, fs), (bc, bs) in zip(outs, fz, bz):
        oc[...] = (fc + bc) * scale
        os_[...] = (fs - bs) * scale
    km_ref[...] = jnp.zeros_like(km_ref)
    for i, ((fc, fs), (bc, bs)) in enumerate(zip(fm, bm)):
        km_ref[2 * i:2 * i + 1, :] = (fc + bc) * (2.0 / n)
        km_ref[2 * i + 1:2 * i + 2, :] = (fs - bs) * (2.0 / n)


def _hyena_filters(seq, w1, b1, w2, b2, w3, b3, freq, decay, tables, n_order, n_dirs, d_hy):
    q = seq // 4
    emb, hid = w1.shape
    bands = (emb - 1) // 2
    t = jnp.arange(seq, dtype=F32)
    t01 = t / max(seq - 1, 1)
    band = jnp.linspace(1e-4, bands - 1, bands, dtype=F32)
    ang = (2.0 * math.pi) * t[:, None] * band[None, :] / seq
    feats = jnp.concatenate([t01[:, None], jnp.cos(ang), jnp.sin(ang)], axis=-1)
    kpad = LANE
    feats = jnp.pad(feats, ((0, 0), (0, kpad - emb)))
    w1p = jnp.pad(w1, ((0, kpad - emb), (0, 0)))
    tabs = [tables["c_fwd"], tables["s_fwd"]]
    ct = min(HY_CH_TILE, d_hy)
    nct = d_hy // ct
    ncol = n_order * d_hy
    b3r = b3.reshape(1, -1)
    decr = decay.reshape(1, -1)
    fcol = lambda o, c: (0, (o * n_dirs + 0) * nct + c)
    bcol = lambda o, c: (0, (o * n_dirs + 1) * nct + c)
    ocol = lambda o, c: (0, o * nct + c)
    full = lambda a: pl.BlockSpec(a.shape, lambda o, c: (0,) * a.ndim)
    spec = jax.ShapeDtypeStruct((q, ncol), F32)
    return pl.pallas_call(
        _hy_filter_kernel,
        out_shape=(spec,) * 8 + (jax.ShapeDtypeStruct((8, ncol), F32),),
        grid=(n_order, nct),
        in_specs=[
            full(feats), full(w1p), full(b1.reshape(1, -1)), full(w2), full(b2.reshape(1, -1)),
            full(freq.reshape(1, -1)),
            pl.BlockSpec((hid, ct), fcol), pl.BlockSpec((1, ct), fcol), pl.BlockSpec((1, ct), fcol),
            pl.BlockSpec((hid, ct), bcol), pl.BlockSpec((1, ct), bcol), pl.BlockSpec((1, ct), bcol),
        ] + [_const_spec(a.shape) for a in tabs],
        out_specs=tuple(pl.BlockSpec((q, ct), ocol) for _ in range(8)) + (pl.BlockSpec((8, ct), ocol),),
        scratch_shapes=[pltpu.VMEM((2 * ct // LANE, seq, LANE), F32), pltpu.VMEM((q, 8 * ct), BF16),
                        pltpu.VMEM((seq, hid), F32)],
        compiler_params=_cparams(2),
        name="hyena_filter_spectra",
    )(feats, w1p, b1.reshape(1, -1), w2, b2.reshape(1, -1), freq.reshape(1, -1),
      w3, b3r, decr, w3, b3r, decr, *tabs)


def _hy_conv_kernel(zin_ref, gin_ref, wz_ref, bz_ref, wg_ref, bg_ref,
                    kac_ref, kas_ref, kbc_ref, kbs_ref, kcc_ref, kcs_ref, kdc_ref, kds_ref, km_ref, bias_ref,
                    cf_ref, sf_ref, ci_ref, si_ref, out_ref,
                    zraw_s, graw_s, o_s, z_s, g_s, rhs_s, ac_s, as_s, *, conv_on_z, seq):
    ct = gin_ref.shape[1]
    nbat = gin_ref.shape[0] // seq
    q = seq // 4
    nsl = ct // LANE
    pad = zraw_s.shape[2] - seq
    top = pad // 2
    rc = min(DFT_ROW_CHUNK, q)
    k_refs = ((kac_ref, kas_ref), (kbc_ref, kbs_ref), (kcc_ref, kcs_ref), (kdc_ref, kds_ref))

    def stage(raw_s, src_ref, k):
        for j in range(nsl):
            raw_s[k, j, 0:top, :] = jnp.zeros((top, LANE), F32)
            raw_s[k, j, top + seq:, :] = jnp.zeros((pad - top, LANE), F32)
            raw_s[k, j, top:top + seq, :] = src_ref[k * seq:(k + 1) * seq, j * LANE:(j + 1) * LANE].astype(F32)

    def split_rows(raw_s, dst_ref, k, w_ref, b_ref):
        for r in range(4):
            for j in range(nsl):
                ls = slice(j * LANE, (j + 1) * LANE)
                tap = lambda d: raw_s[k, j, pl.ds(top + r + d, q, stride=4), :]
                dst_ref[k, :, r * ct + j * LANE:r * ct + (j + 1) * LANE] = (
                    b_ref[:, ls] + tap(-1) * w_ref[0:1, ls] + tap(0) * w_ref[1:2, ls] + tap(1) * w_ref[2:3, ls])

    def spectrum_chunk(k, i):
        r = slice(i * rc, (i + 1) * rc)
        row0 = lax.broadcasted_iota(jnp.int32, (rc, 1), 0) == 0
        x, a = [], []
        for p in range(4):
            rhs = rhs_s[k, :, p * ct:(p + 1) * ct]
            c = jnp.dot(cf_ref[p, r, :], rhs, preferred_element_type=F32)
            sn = jnp.dot(sf_ref[p, r, :], rhs, preferred_element_type=F32)
            if i == 0:
                a.append(sn[0:1, :])
                sn = jnp.where(row0, 0.0, sn)
            else:
                a.append(jnp.zeros((1, ct), F32))
            x.append((c, sn))
        z, zmid = _forward_spectrum(x, a)
        y = [_cmul(zx, (kc[r, :], ks[r, :])) for zx, (kc, ks) in zip(z, k_refs)]
        ymid = [_cmul(zmid[m], (km_ref[2 * m:2 * m + 1, :], km_ref[2 * m + 1:2 * m + 2, :])) for m in range(2)]
        t, sp = _inverse_spectrum(y, ymid)
        for p, ((tc, ts), spp) in enumerate(zip(t, sp)):
            if i == 0:
                ts = jnp.where(row0, spp, ts)
            ac_s[k, r, p * ct:(p + 1) * ct] = tc.astype(BF16)
            as_s[k, r, p * ct:(p + 1) * ct] = ts.astype(BF16)

    def output_chunk(k, i):
        r = slice(i * rc, (i + 1) * rc)
        for p in range(4):
            ps = slice(p * ct, (p + 1) * ct)
            y = jnp.dot(ci_ref[p, r, :], ac_s[k, :, ps], preferred_element_type=F32)
            y = y + jnp.dot(si_ref[p, r, :], as_s[k, :, ps], preferred_element_type=F32)
            rows = pl.ds(4 * i * rc + p, rc, stride=4)
            if conv_on_z:
                out_ref[k * q + i * rc:k * q + (i + 1) * rc, ps] = (
                    g_s[k, r, ps] * (y + z_s[k, r, ps] * bias_ref[...])).astype(out_ref.dtype)
                continue
            for j in range(nsl):
                ls = slice(p * ct + j * LANE, p * ct + (j + 1) * LANE)
                jl = slice(j * LANE, (j + 1) * LANE)
                o_s[k, j, rows, :] = g_s[k, r, ls] * (y[:, jl] + z_s[k, r, ls] * bias_ref[:, jl])

    for k in range(nbat):
        stage(graw_s, gin_ref, k)
        split_rows(graw_s, g_s, k, wg_ref, bg_ref)
        if conv_on_z:
            stage(zraw_s, zin_ref, k)
            split_rows(zraw_s, z_s, k, wz_ref, bz_ref)
            rhs_s[k] = z_s[k].astype(BF16)
        else:
            rhs_s[k] = zin_ref[k * q:(k + 1) * q, :]
            z_s[k] = zin_ref[k * q:(k + 1) * q, :].astype(F32)
    for i in range(q // rc):
        for k in range(nbat):
            spectrum_chunk(k, i)
    for i in range(q // rc):
        for k in range(nbat):
            output_chunk(k, i)
    if not conv_on_z:
        for k in range(nbat):
            for j in range(nsl):
                out_ref[k * seq:(k + 1) * seq, j * LANE:(j + 1) * LANE] = o_s[k, j].astype(out_ref.dtype)


def _hy_conv(zin, zcol0, gcol0, u_hy, conv_w, conv_b, spectra, bias_row, tables,
             *, order, conv_on_z, bsz, seq, d_hy):
    tabs = [tables[k] for k in ("c_fwd", "s_fwd", "c_inv", "s_inv")]
    q = seq // 4
    ct = min(HY_CH_TILE, d_hy)
    nsl = ct // LANE
    nct = d_hy // ct
    nbat = HY_BATCH_GROUP if bsz % HY_BATCH_GROUP == 0 else 1
    zc0 = zcol0 // ct
    gc0 = gcol0 // ct
    zw0 = zc0 if conv_on_z else 0
    kspec = lambda rows: pl.BlockSpec((rows, ct), lambda c, b: (0, order * nct + c),
                                      pipeline_mode=pl.Buffered(1))
    scratch = lambda dt: pltpu.VMEM((nbat, q, 4 * ct), dt)
    slabs = lambda rows: pltpu.VMEM((nbat, nsl, rows, LANE), F32)
    time_block = pl.BlockSpec((nbat * seq, ct), lambda c, b: (b, c))
    split_block = pl.BlockSpec((nbat * q, 4 * ct), lambda c, b: (b, c))
    return pl.pallas_call(
        functools.partial(_hy_conv_kernel, conv_on_z=conv_on_z, seq=seq),
        out_shape=jax.ShapeDtypeStruct((bsz * q, 4 * d_hy) if conv_on_z else (bsz * seq, d_hy), BF16),
        grid=(nct, bsz // nbat),
        in_specs=[
            pl.BlockSpec((nbat * seq, ct), lambda c, b: (b, zc0 + c)) if conv_on_z else split_block,
            pl.BlockSpec((nbat * seq, ct), lambda c, b: (b, gc0 + c)),
            pl.BlockSpec((conv_w.shape[0], ct), lambda c, b: (0, zw0 + c)),
            pl.BlockSpec((1, ct), lambda c, b: (0, zw0 + c)),
            pl.BlockSpec((conv_w.shape[0], ct), lambda c, b: (0, gc0 + c)),
            pl.BlockSpec((1, ct), lambda c, b: (0, gc0 + c)),
        ] + [kspec(q)] * 8 + [kspec(8), kspec(1)] + [_const_spec(a.shape) for a in tabs],
        out_specs=split_block if conv_on_z else time_block,
        scratch_shapes=[
            slabs(seq + 2 * SUBLANE), slabs(seq + 2 * SUBLANE), slabs(seq),
            scratch(F32), scratch(F32), scratch(BF16), scratch(BF16), scratch(BF16),
        ],
        compiler_params=_cparams(2),
        name=f"hyena_conv_order{order}",
    )(zin, u_hy, conv_w, conv_b, conv_w, conv_b, *spectra, bias_row, *tabs)


def _hyena_branch(u_hy, bsz, seq, conv_w, conv_b, w1, b1, w2, b2, w3, b3, freq, decay, bias):
    n_order, d_hy = bias.shape
    n_dirs = w3.shape[1] // (n_order * d_hy)
    if n_order != 2 or seq % 8 != 0 or d_hy % LANE != 0:
        raise NotImplementedError("Hyena branch: two long convolutions, L % 8 == 0, 128-lane channel tiles")
    tables = _hyena_tables(seq)
    spectra = _hyena_filters(seq, w1, b1, w2, b2, w3, b3, freq, decay, tables, n_order, n_dirs, d_hy)
    cb = conv_b.reshape(1, -1)
    bias_row = bias.reshape(1, -1)
    common = dict(bsz=bsz, seq=seq, d_hy=d_hy)
    z1 = _hy_conv(u_hy, 0, d_hy, u_hy, conv_w, cb, spectra, bias_row, tables,
                  order=0, conv_on_z=True, **common)
    return _hy_conv(z1, 0, 2 * d_hy, u_hy, conv_w, cb, spectra, bias_row, tables,
                    order=1, conv_on_z=False, **common)


def _mixer_kernel(x_ref, mod_ref, g1_ref, g2_ref, fg_ref, permt_ref, za_ref, zb_ref,
                  wgate_ref, wa_ref, wb_ref, wout_ref, wg_ref, wu_ref, wd_ref, o_ref, *, n_chunks):
    nb, tt, d = x_ref.shape
    rows = nb * tt
    mod = lambda k: mod_ref[:, k:k + 1, :]
    x = x_ref[...]
    h = _norm_modulate(x, g1_ref[...], mod(0), mod(1)).reshape(rows, d).astype(BF16)
    gate = jax.nn.sigmoid(jnp.dot(h, wgate_ref[...], preferred_element_type=F32))
    za = jnp.dot(permt_ref[...], za_ref[...], preferred_element_type=F32).astype(BF16)
    ya = jnp.dot(za, wa_ref[...], preferred_element_type=F32)
    yb = jnp.dot(zb_ref[...].reshape(rows, zb_ref.shape[-1]), wb_ref[...], preferred_element_type=F32)
    merged = gate[:, :d] * ya + gate[:, d:] * yb
    o = jnp.dot(merged.astype(BF16), wout_ref[...], preferred_element_type=F32)
    x1 = x + mod(2) * o.reshape(nb, tt, d)

    h2 = _norm_modulate(x1, g2_ref[...], mod(3), mod(4)).reshape(rows, d).astype(BF16)
    d_ff = wg_ref.shape[1]
    unit = MXU_TILE if d_ff % MXU_TILE == 0 else d_ff // n_chunks
    edges = [len(g) for g in _spread(list(range(d_ff // unit)), n_chunks)]
    acc = jnp.zeros((rows, d), F32)
    start = 0
    for width in edges:
        sl = slice(start * unit, (start + width) * unit)
        start += width
        gl = jnp.dot(h2, wg_ref[:, sl], preferred_element_type=F32)
        up = jnp.dot(h2, wu_ref[:, sl], preferred_element_type=F32)
        act = (gl * jax.nn.sigmoid(gl) * up).astype(BF16)
        acc = acc + jnp.dot(act, wd_ref[sl, :], preferred_element_type=F32)
    x2 = x1 + mod(5) * acc.reshape(nb, tt, d)
    ms = jnp.mean(x2 * x2, axis=-1, keepdims=True)
    o_ref[...] = x2 * lax.rsqrt(ms + EPS) * fg_ref[...]


def _mixer(x, mod3, norm1_g, norm2_g, final_g, perm_t, za_tm, zb, w_in, gate_col, w_a, w_b, w_out, w_gu, w_d):
    bsz, seq, d = x.shape
    d_s5 = w_a.shape[0]
    d_hy = w_b.shape[0]
    d_ff = w_d.shape[0]
    tt = _time_tile(bsz, seq)
    n_chunks = 2 if d_ff % (2 * LANE) == 0 else 1
    consts = (mod3, norm1_g, norm2_g, final_g, perm_t)
    weights = (w_in, w_a, w_b, w_out, w_gu, w_gu, w_d)
    weight_specs = ([_col_block_spec(d, 2 * d, gate_col)] + [_const_spec(a.shape) for a in (w_a, w_b, w_out)]
                    + [_col_block_spec(d, d_ff, 0), _col_block_spec(d, d_ff, 1), _const_spec(w_d.shape)])
    return pl.pallas_call(
        functools.partial(_mixer_kernel, n_chunks=n_chunks),
        out_shape=jax.ShapeDtypeStruct(x.shape, F32),
        grid=(seq // tt,),
        in_specs=[pl.BlockSpec((bsz, tt, d), lambda j: (0, j, 0))]
        + [_const_spec(a.shape) for a in consts]
        + [pl.BlockSpec((tt * bsz, d_s5), lambda j: (j, 0)),
           pl.BlockSpec((bsz, tt, d_hy), lambda j: (0, j, 0))]
        + weight_specs,
        out_specs=pl.BlockSpec((bsz, tt, d), lambda j: (0, j, 0)),
        compiler_params=_cparams(1, fuse_inputs=[False] * (1 + len(consts) + 2) + [True] * len(weights)),
        name="merge_swiglu_final_norm",
    )(x, *consts, za_tm, zb, *weights)


def kernel(x, c, ada_w, ada_b, norm1_g, norm2_g, w_in, s5_lam_re, s5_lam_im, s5_log_step, s5_b_re, s5_b_im, s5_c_re, s5_c_im, s5_d, s5_glu_w, s5_glu_b, hy_conv_w, hy_conv_b, hy_ffn_w1, hy_ffn_b1, hy_ffn_w2, hy_ffn_b2, hy_ffn_w3, hy_ffn_b3, hy_freq, hy_decay, hy_bias, w_branch_a, w_branch_b, w_out, ffn_w_gu, ffn_w_down, final_g):
    bsz, seq, d = x.shape
    depth = ada_w.shape[0]
    if depth != 1:
        raise NotImplementedError("the final RMSNorm is fused into the (single) layer's channel mixer")
    d_s5 = s5_d.shape[-1]
    n_order, d_hy = hy_bias.shape[1:]
    d_uh = d_s5 + (n_order + 1) * d_hy
    d_ff = ffn_w_down.shape[1]
    i = 0
    perm = _row_permutation(bsz, _time_tile(bsz, seq))
    mod = _modulation(c, ada_w[i], ada_b[i]).reshape(bsz, 6, d)
    w_in_b = w_in[i].astype(BF16)
    if d_uh % (2 * d) != 0:
        raise NotImplementedError("gate columns of w_in must start on a 2*D column block")
    u_s5, u_hy = _in_projection(x, mod, norm1_g[i].reshape(1, d), w_in_b, perm, d_s5, d_uh)
    z_a = _s5_branch(u_s5, bsz, s5_lam_re[i], s5_lam_im[i], s5_log_step[i], s5_b_re[i], s5_b_im[i],
                     s5_c_re[i], s5_c_im[i], s5_d[i], s5_glu_w[i], s5_glu_b[i])
    z_b = _hyena_branch(u_hy.reshape(bsz * seq, -1), bsz, seq, hy_conv_w[i], hy_conv_b[i], hy_ffn_w1[i],
                        hy_ffn_b1[i], hy_ffn_w2[i], hy_ffn_b2[i], hy_ffn_w3[i], hy_ffn_b3[i], hy_freq[i],
                        hy_decay[i], hy_bias[i])
    w_gu = ffn_w_gu[i].astype(BF16)
    return _mixer(x, mod, norm1_g[i].reshape(1, d), norm2_g[i].reshape(1, d), final_g.reshape(1, d),
                  perm.T, z_a, z_b.reshape(bsz, seq, d_hy), w_in_b, d_uh // (2 * d),
                  w_branch_a[i].astype(BF16), w_branch_b[i].astype(BF16), w_out[i].astype(BF16),
                  w_gu, ffn_w_down[i].astype(BF16))
```
